```python
import math
import jax, jax.numpy as jnp
from jax import lax
import numpy as np

D_MODEL = 1024
BATCH = 2
SEQ = 8192
DEPTH = 4

D_A = D_MODEL // 2
HEAD_A = 64
N_HEADS_A = D_A // HEAD_A
LORA_DECAY = 64
LORA_ICLR = 64
LORA_GATE = 128
N_DIR = 2
C_RWKV = 3 * D_A + N_DIR * LORA_DECAY + N_DIR * LORA_ICLR + LORA_GATE
RWKV_SPLITS = [D_A, 2 * D_A, 3 * D_A, 3 * D_A + N_DIR * LORA_DECAY,
               3 * D_A + N_DIR * LORA_DECAY + N_DIR * LORA_ICLR]
DECAY_SCALE = math.exp(-0.5)
GN_EPS = 64e-5
HEAD_B = 64
D_B = D_MODEL // 2
N_HEADS_B = D_B // (2 * HEAD_B)
C_ATTN = 3 * D_B
ALIBI_MAX_EXP = 8.0
Q_BLOCK = 128
N_BRANCH = 2
C_IN = C_RWKV + C_ATTN + N_BRANCH * D_MODEL
D_FF = ((8 * D_MODEL // 3 + 127) // 128) * 128
NORM_EPS = 1e-6

kernel_name = 'hybrid_rwkv7_diffattn_macaron_encoder'


def rms_norm(x, g, eps=NORM_EPS):
    xf = x.astype(jnp.float32)
    y = xf * lax.rsqrt(jnp.mean(xf * xf, axis=-1, keepdims=True) + eps)
    return (y * g.astype(jnp.float32)).astype(x.dtype)


def swiglu(h, w_in, w_out):
    gate, up = jnp.split(h @ w_in, 2, axis=-1)
    return (jax.nn.silu(gate) * up) @ w_out


def centred_shift(p):
    prev = jnp.pad(p[:, :-1], ((0, 0), (1, 0), (0, 0)))
    nxt = jnp.pad(p[:, 1:], ((0, 0), (0, 1), (0, 0)))
    return 0.5 * (prev + nxt)


def to_heads(t, n_heads):
    return t.reshape(t.shape[:-1] + (n_heads, t.shape[-1] // n_heads))


def rwkv7_scan(r, w, k, v, kk, a, reverse):
    b, s, h, n = r.shape
    xs = tuple(jnp.moveaxis(t.astype(jnp.float32), 1, 0) for t in (r, w, k, v, kk, a))

    def step(state, inp):
        r_t, w_t, k_t, v_t, kk_t, a_t = inp
        sa = jnp.einsum('bhvk,bhk->bhv', state, -kk_t)
        state = (state * w_t[:, :, None, :]
                 + sa[..., None] * (kk_t * a_t)[:, :, None, :]
                 + v_t[..., None] * k_t[:, :, None, :])
        y_t = jnp.einsum('bhvk,bhk->bhv', state, r_t)
        return state, y_t

    s0 = jnp.zeros((b, h, n, n), jnp.float32)
    _, ys = lax.scan(step, s0, xs, reverse=reverse)
    return jnp.moveaxis(ys, 0, 1)


def rwkv7_mixer(p, mu, w0, w2, a0, a2, g2, k_k, k_a, r_k, ln_g, ln_b):
    p = p + mu * (centred_shift(p) - p)
    r, k, v, dw, da, dg = jnp.split(p, RWKV_SPLITS, axis=-1)
    b, s, _ = r.shape
    dw = dw.reshape(b, s, N_DIR, LORA_DECAY)
    da = da.reshape(b, s, N_DIR, LORA_ICLR)
    decay = jnp.exp(-DECAY_SCALE * jax.nn.sigmoid(
        w0 + jnp.einsum('bsgr,grc->bsgc', jnp.tanh(dw), w2)))
    iclr = jax.nn.sigmoid(a0 + jnp.einsum('bsgr,grc->bsgc', da, a2))
    gate = jax.nn.sigmoid(dg) @ g2
    kk = to_heads(k * k_k, N_HEADS_A).astype(jnp.float32)
    kk = kk / jnp.maximum(jnp.sqrt(jnp.sum(kk * kk, axis=-1, keepdims=True)), 1e-12)
    k_dir = k[:, :, None, :] * (1.0 + (iclr - 1.0) * k_a)
    r_h = to_heads(r, N_HEADS_A)
    v_h = to_heads(v, N_HEADS_A)
    k_h = to_heads(k_dir, N_HEADS_A)
    w_h = to_heads(decay, N_HEADS_A)
    a_h = to_heads(iclr, N_HEADS_A)
    y = (rwkv7_scan(r_h, w_h[:, :, 0], k_h[:, :, 0], v_h, kk, a_h[:, :, 0], False)
         + rwkv7_scan(r_h, w_h[:, :, 1], k_h[:, :, 1], v_h, kk, a_h[:, :, 1], True))
    mean = jnp.mean(y, axis=-1, keepdims=True)
    var = jnp.mean(jnp.square(y - mean), axis=-1, keepdims=True)
    y = ((y - mean) * lax.rsqrt(var + GN_EPS)).reshape(b, s, D_A) * ln_g + ln_b
    bonus = jnp.einsum('bshn,bsghn,hn->bsh', r_h.astype(jnp.float32),
                       k_h.astype(jnp.float32), r_k.astype(jnp.float32))[..., None] * v_h
    return (y + bonus.reshape(b, s, D_A)) * gate


def diff_attention(p, q_gain, k_gain, lam_vecs, sub_g, lam_init):
    b, s, _ = p.shape
    q, k, v = jnp.split(p, 3, axis=-1)
    q = rms_norm(q.reshape(b, s, N_HEADS_B, 2, HEAD_B), q_gain)
    k = rms_norm(k.reshape(b, s, N_HEADS_B, 2, HEAD_B), k_gain)
    v = v.reshape(b, s, N_HEADS_B, 2 * HEAD_B)
    lv = lam_vecs.astype(jnp.float32)
    lam = jnp.exp(jnp.sum(lv[0] * lv[1])) - jnp.exp(jnp.sum(lv[2] * lv[3])) + lam_init
    n_blk = s // Q_BLOCK
    q_blk = q.reshape(b, n_blk, Q_BLOCK, N_HEADS_B, 2, HEAD_B).transpose(1, 0, 3, 4, 2, 5)
    k_t = k.transpose(0, 2, 3, 1, 4)
    v_t = v.transpose(0, 2, 1, 3)
    slopes = 2.0 ** (-ALIBI_MAX_EXP * jnp.arange(1, N_HEADS_B + 1, dtype=jnp.float32) / N_HEADS_B)
    k_pos = jnp.arange(s, dtype=jnp.float32)
    q_pos = k_pos.reshape(n_blk, Q_BLOCK)
    scale = HEAD_B ** -0.5

    def attend_block(args):
        qb, qp = args
        logits = jnp.einsum('bhmqd,bhmkd->bhmqk', qb, k_t,
                            preferred_element_type=jnp.float32) * scale
        bias = -slopes[:, None, None] * jnp.abs(qp[:, None] - k_pos[None, :])
        prob = jax.nn.softmax(logits + bias[None, :, None], axis=-1)
        attn = prob[:, :, 0] - lam * prob[:, :, 1]
        return jnp.einsum('bhqk,bhkc->bhqc', attn.astype(v_t.dtype), v_t)

    o = lax.map(attend_block, (q_blk, q_pos))
    o = o.transpose(1, 0, 3, 2, 4).reshape(b, s, N_HEADS_B, 2 * HEAD_B)
    o = rms_norm(o, sub_g) * (1.0 - lam_init)
    return o.reshape(b, s, D_B)


def setup_inputs(seed: int = 0) -> dict:
    key = jax.random.key(seed)
    ks = iter(jax.random.split(key, 32))
    L = DEPTH

    def nrm(shape, scale):
        return jax.random.normal(next(ks), shape, jnp.float32) * scale

    def gain(shape):
        return 1.0 + nrm(shape, 0.02)

    return {
        'x': nrm((BATCH, SEQ, D_MODEL), 1.0),
        'norm_ffn1': gain((L, D_MODEL)),
        'ffn1_in': nrm((L, D_MODEL, 2 * D_FF), D_MODEL ** -0.5),
        'ffn1_out': nrm((L, D_FF, D_MODEL), D_FF ** -0.5),
        'norm_mix': gain((L, D_MODEL)),
        'w_in': nrm((L, D_MODEL, C_IN), D_MODEL ** -0.5),
        'rwkv_mu': jax.random.uniform(next(ks), (L, C_RWKV), jnp.float32),
        'decay_w0': nrm((L, N_DIR, D_A), 1.0),
        'decay_w2': nrm((L, N_DIR, LORA_DECAY, D_A), 0.5 * LORA_DECAY ** -0.5),
        'iclr_a0': nrm((L, N_DIR, D_A), 0.5),
        'iclr_a2': nrm((L, N_DIR, LORA_ICLR, D_A), 0.5 * LORA_ICLR ** -0.5),
        'gate_g2': nrm((L, LORA_GATE, D_A), LORA_GATE ** -0.5),
        'k_k': 0.85 + nrm((L, D_A), 0.05),
        'k_a': 1.0 + nrm((L, D_A), 0.05),
        'r_k': nrm((L, N_HEADS_A, HEAD_A), 0.1),
        'ln_x_g': gain((L, D_A)),
        'ln_x_b': nrm((L, D_A), 0.02),
        'q_gain': gain((L, HEAD_B)),
        'k_gain': gain((L, HEAD_B)),
        'diff_lambda': nrm((L, 4, HEAD_B), 0.1),
        'subln_g': gain((L, 2 * HEAD_B)),
        'w_branch': nrm((L, N_BRANCH, D_A, D_MODEL), D_A ** -0.5),
        'w_out': nrm((L, D_MODEL, D_MODEL), D_MODEL ** -0.5),
        'norm_ffn2': gain((L, D_MODEL)),
        'ffn2_in': nrm((L, D_MODEL, 2 * D_FF), D_MODEL ** -0.5),
        'ffn2_out': nrm((L, D_FF, D_MODEL), D_FF ** -0.5),
    }


def reference(x, norm_ffn1, ffn1_in, ffn1_out, norm_mix, w_in, rwkv_mu, decay_w0, decay_w2,
              iclr_a0, iclr_a2, gate_g2, k_k, k_a, r_k, ln_x_g, ln_x_b, q_gain, k_gain,
              diff_lambda, subln_g, w_branch, w_out, norm_ffn2, ffn2_in, ffn2_out):
    b, s, d = x.shape
    for l in range(DEPTH):
        lam_init = 0.8 - 0.6 * math.exp(-0.3 * l)
        x = x + 0.5 * swiglu(rms_norm(x, norm_ffn1[l]), ffn1_in[l], ffn1_out[l])
        h = rms_norm(x, norm_mix[l])
        proj = h @ w_in[l]
        p_rwkv, p_attn, p_gate = jnp.split(proj, [C_RWKV, C_RWKV + C_ATTN], axis=-1)
        o_a = rwkv7_mixer(p_rwkv, rwkv_mu[l], decay_w0[l], decay_w2[l], iclr_a0[l],
                          iclr_a2[l], gate_g2[l], k_k[l], k_a[l], r_k[l],
                          ln_x_g[l], ln_x_b[l]).astype(x.dtype)
        o_b = diff_attention(p_attn, q_gain[l], k_gain[l], diff_lambda[l], subln_g[l],
                             lam_init).astype(x.dtype)
        branches = jnp.stack([o_a, o_b], axis=2)
        y = jnp.einsum('bsgc,gcd->bsgd', branches, w_branch[l])
        gates = jax.nn.sigmoid(p_gate.reshape(b, s, N_BRANCH, d))
        merged = jnp.sum(gates * y, axis=2)
        x = x + merged @ w_out[l]
        x = x + 0.5 * swiglu(rms_norm(x, norm_ffn2[l]), ffn2_in[l], ffn2_out[l])
    return x
```

```python
import functools
import math

import jax
import jax.numpy as jnp
from jax import lax
from jax.experimental import pallas as pl
from jax.experimental.pallas import tpu as pltpu

F32 = jnp.float32
BF16 = jnp.bfloat16

D_MODEL = 1024
D_A = 512
HEAD_A = 64
LORA = 64
LORA_GATE = 128
C_RWKV = 3 * D_A + 2 * LORA + 2 * LORA + LORA_GATE
D_B = 512
HEAD_B = 64
N_HEADS_B = D_B // (2 * HEAD_B)
D_FF = 2816
DECAY_SCALE = math.exp(-0.5)
GN_EPS = 64e-5
NORM_EPS = 1e-6
ALIBI_MAX_EXP = 8.0

LANES = 128
SUBLANES = 8
VMEM_LIMIT = 56 * 1024 * 1024

TOKEN_TILE = 512
FF_TILE = 1408
CHUNK = 64
ATTN_TQ = 256
ATTN_TK = 256

_NT = (((1,), (1,)), ((), ()))


def _mm(a, b):
    return jnp.dot(a.astype(BF16), b.astype(BF16), preferred_element_type=F32)


def _mm_nt(a, b):
    return lax.dot_general(a.astype(BF16), b.astype(BF16), _NT, preferred_element_type=F32)


def _split_bf16(x, n):
    parts = []
    for _ in range(n - 1):
        hi = x.astype(BF16)
        parts.append(hi)
        x = x - hi.astype(F32)
    parts.append(x.astype(BF16))
    return parts


def _mm_split_lhs(x, w, n=2):
    return sum(jnp.dot(p, w, preferred_element_type=F32) for p in _split_bf16(x, n))


def _rms_norm(x, g):
    return x * lax.rsqrt(jnp.mean(x * x, axis=-1, keepdims=True) + NORM_EPS) * g


def _const_spec(shape):
    return pl.BlockSpec(shape, lambda *_: (0,) * len(shape), pipeline_mode=pl.Buffered(1))


def _params(n_axes):
    return pltpu.CompilerParams(dimension_semantics=("arbitrary",) * n_axes,
                                vmem_limit_bytes=VMEM_LIMIT)


def _ffn_body(x_ref, g_ref, win_ref, wout_ref, o_ref):
    x = x_ref[...]
    h = _rms_norm(x, g_ref[...]).astype(BF16)
    acc = jnp.zeros(x.shape, F32)
    for j in range(D_FF // FF_TILE):
        lo, hi = j * FF_TILE, (j + 1) * FF_TILE
        gate = jnp.dot(h, win_ref[:, lo:hi], preferred_element_type=F32)
        up = jnp.dot(h, win_ref[:, D_FF + lo:D_FF + hi], preferred_element_type=F32)
        act = (gate * jax.nn.sigmoid(gate) * up).astype(BF16)
        acc = acc + jnp.dot(act, wout_ref[lo:hi, :], preferred_element_type=F32)
    o_ref[...] = x + 0.5 * acc


def _ffn(x2, g, w_in, w_out):
    t = x2.shape[0]
    tm = min(TOKEN_TILE, t)
    return pl.pallas_call(
        _ffn_body,
        grid=(t // tm,),
        in_specs=[pl.BlockSpec((tm, D_MODEL), lambda i: (i, 0)),
                  _const_spec((1, D_MODEL)),
                  _const_spec((D_MODEL, 2 * D_FF)),
                  _const_spec((D_FF, D_MODEL))],
        out_specs=pl.BlockSpec((tm, D_MODEL), lambda i: (i, 0)),
        out_shape=jax.ShapeDtypeStruct((t, D_MODEL), F32),
        compiler_params=_params(1),
        name="ffn",
    )(x2, g, w_in, w_out)


_Q0 = C_RWKV
_K0 = C_RWKV + D_B
_V0 = C_RWKV + 2 * D_B
_G0 = C_RWKV + 3 * D_B


def _proj_body(x_ref, g_ref, w_ref, qg_ref, kg_ref, bd_ref, prw_ref, q_ref, k_ref, v_ref, gt_ref):
    h = _rms_norm(x_ref[...], g_ref[...]).astype(BF16)
    prw_ref[...] = jnp.dot(h, w_ref[:, :C_RWKV], preferred_element_type=F32)
    bd = bd_ref[...]
    q = jnp.dot(h, w_ref[:, _Q0:_K0], preferred_element_type=F32)
    q_ref[...] = (q * lax.rsqrt(_mm_split_lhs(q * q, bd) + NORM_EPS) * qg_ref[...]).astype(BF16)
    k = jnp.dot(h, w_ref[:, _K0:_V0], preferred_element_type=F32)
    k_ref[...] = (k * lax.rsqrt(_mm_split_lhs(k * k, bd) + NORM_EPS) * kg_ref[...]).astype(BF16)
    v_ref[...] = jnp.dot(h, w_ref[:, _V0:_G0], preferred_element_type=F32).astype(BF16)
    gt_ref[...] = jax.nn.sigmoid(jnp.dot(h, w_ref[:, _G0:], preferred_element_type=F32)).astype(BF16)


def _proj(x2, g, w, qg, kg, bd_mean):
    t = x2.shape[0]
    tm = min(TOKEN_TILE, t)
    c_in = w.shape[1]
    row = lambda width: pl.BlockSpec((tm, width), lambda i: (i, 0))
    return pl.pallas_call(
        _proj_body,
        grid=(t // tm,),
        in_specs=[row(D_MODEL), _const_spec((1, D_MODEL)), _const_spec((D_MODEL, c_in)),
                  _const_spec((1, D_B)), _const_spec((1, D_B)), _const_spec((D_B, D_B))],
        out_specs=[row(C_RWKV), row(D_B), row(D_B), row(D_B), row(2 * D_MODEL)],
        out_shape=[jax.ShapeDtypeStruct((t, C_RWKV), F32),
                   jax.ShapeDtypeStruct((t, D_B), BF16),
                   jax.ShapeDtypeStruct((t, D_B), BF16),
                   jax.ShapeDtypeStruct((t, D_B), BF16),
                   jax.ShapeDtypeStruct((t, 2 * D_MODEL), BF16)],
        compiler_params=_params(1),
        name="proj",
    )(x2, g, w, qg, kg, bd_mean)


def _prep_body(p_ref, pp_ref, pn_ref, mu_ref, w0_ref, w2_ref, a0_ref, a2_ref, g2_ref, kk_ref, ka_ref,
               rk_ref, bd_ref,
               r_out, v_out, kk_out, lw0_out, k0_out, a0_out, lw1_out, k1_out, a1_out, gate_out,
               bonus_out, *, seq):
    i = pl.program_id(0)
    p = p_ref[...]
    tm = p.shape[0]
    row = lax.broadcasted_iota(jnp.int32, p.shape, 0)
    keep_prev = jnp.where((i * tm) % seq == 0, 0.0, 1.0)
    keep_next = jnp.where(((i + 1) * tm) % seq == 0, 0.0, 1.0)
    prev = jnp.where(row == 0, keep_prev * pp_ref[SUBLANES - 1:SUBLANES, :], pltpu.roll(p, 1, 0))
    nxt = jnp.where(row == tm - 1, keep_next * pn_ref[0:1, :], pltpu.roll(p, tm - 1, 0))
    p = p + mu_ref[...] * (0.5 * (prev + nxt) - p)

    r = p[:, :D_A]
    k = p[:, D_A:2 * D_A]
    v = p[:, 2 * D_A:3 * D_A]
    dw = p[:, 3 * D_A:3 * D_A + 2 * LORA]
    da = p[:, 3 * D_A + 2 * LORA:3 * D_A + 4 * LORA]
    dg = p[:, 3 * D_A + 4 * LORA:]
    bd = bd_ref[...]

    logw = -DECAY_SCALE * jax.nn.sigmoid(w0_ref[...] + _mm(jnp.tanh(dw), w2_ref[...]))
    iclr = jax.nn.sigmoid(a0_ref[...] + _mm(da, a2_ref[...]))
    gate_out[...] = _mm(jax.nn.sigmoid(dg), g2_ref[...])

    kk = k * kk_ref[...]
    norm = jnp.sqrt(_mm_split_lhs(kk * kk, bd))
    kk_out[...] = kk / jnp.maximum(norm, 1e-12)

    ka = ka_ref[...]
    a_f = iclr[:, :D_A]
    a_b = iclr[:, D_A:]
    k_f = k * (1.0 + (a_f - 1.0) * ka)
    k_b = k * (1.0 + (a_b - 1.0) * ka)
    bonus = _mm_split_lhs(r * rk_ref[...] * (k_f + k_b), bd)
    r_out[...] = r
    v_out[...] = v
    lw0_out[...] = logw[:, :D_A]
    k0_out[...] = k_f
    a0_out[...] = a_f
    lw1_out[...] = logw[:, D_A:]
    k1_out[...] = k_b
    a1_out[...] = a_b
    bonus_out[...] = bonus * v


def _prep(prw, seq, mu, w0, w2cat, a0, a2cat, g2, k_k, k_a, r_k, bd_sum):
    t = prw.shape[0]
    tm = min(TOKEN_TILE, seq)
    nb = tm // SUBLANES
    last = t // SUBLANES - 1
    row = lambda width: pl.BlockSpec((tm, width), lambda i: (i, 0))
    out = jax.ShapeDtypeStruct((t, D_A), F32)
    return pl.pallas_call(
        functools.partial(_prep_body, seq=seq),
        grid=(t // tm,),
        in_specs=[row(C_RWKV),
                  pl.BlockSpec((SUBLANES, C_RWKV), lambda i: (jnp.maximum(i * nb - 1, 0), 0)),
                  pl.BlockSpec((SUBLANES, C_RWKV), lambda i: (jnp.minimum((i + 1) * nb, last), 0)),
                  _const_spec((1, C_RWKV)),
                  _const_spec((1, 2 * D_A)), _const_spec((2 * LORA, 2 * D_A)),
                  _const_spec((1, 2 * D_A)), _const_spec((2 * LORA, 2 * D_A)),
                  _const_spec((LORA_GATE, D_A)),
                  _const_spec((1, D_A)), _const_spec((1, D_A)), _const_spec((1, D_A)),
                  _const_spec((D_A, D_A))],
        out_specs=[row(D_A)] * 11,
        out_shape=[out] * 11,
        compiler_params=_params(1),
        name="rwkv_prep",
    )(prw, prw, prw, mu, w0, w2cat, a0, a2cat, g2, k_k, k_a, r_k, bd_sum)


def _stack(x, lane_lo):
    return jnp.concatenate([jnp.where(lane_lo, x, 0.0), jnp.where(lane_lo, 0.0, x)], axis=0)


def _rwkv_chunk(r, v, kk, lw, k, a, h_ref, slot0, reverse):
    L = r.shape[0]
    ti = lax.broadcasted_iota(jnp.int32, (L, L), 0)
    tj = lax.broadcasted_iota(jnp.int32, (L, L), 1)
    tri = ((tj >= ti) if reverse else (tj <= ti)).astype(BF16)
    G = sum(jnp.dot(tri, part, preferred_element_type=F32) for part in _split_bf16(lw, 3))
    g_end = G[0:1] if reverse else G[L - 1:L]
    gam = jnp.exp(G)
    inv_gam = jnp.exp(-G)
    gam_end_rel = jnp.exp(g_end - G)
    gam_end = jnp.exp(g_end)
    b = kk * a
    At = -kk * jnp.exp(G - lw)
    Bt = b * inv_gam
    Kt = k * inv_gam
    Rt = r * gam
    Kh = k * gam_end_rel
    Bh = b * gam_end_rel

    pi = lax.broadcasted_iota(jnp.int32, (L, LANES), 0)
    lane = lax.broadcasted_iota(jnp.int32, (L, LANES), 1)
    pj = lane & (HEAD_A - 1)
    lane_lo = lane < HEAD_A
    strict = (pj > pi) if reverse else (pj < pi)
    incl = (pj >= pi) if reverse else (pj <= pi)
    eye = jnp.where(pj == pi, 1.0, 0.0)
    blk = lambda s: (pi // s) == (pj // s)
    row2 = lax.broadcasted_iota(jnp.int32, (2 * L, LANES), 0)
    col2 = lax.broadcasted_iota(jnp.int32, (2 * L, LANES), 1)
    same_head = (row2 // HEAD_A) == (col2 // HEAD_A)
    diag2 = row2 == col2
    zeros_l = jnp.zeros((L, LANES), F32)
    zeros_2l = jnp.zeros((2 * L, LANES), F32)

    def mmp(x, y):
        return _mm(x, _stack(y, lane_lo))

    ys = []
    for p in range(D_A // LANES):
        sl = slice(p * LANES, (p + 1) * LANES)
        At_p, Bt_p, Kt_p, Rt_p, Kh_p, Bh_p, v_p = (t[:, sl] for t in (At, Bt, Kt, Rt, Kh, Bh, v))
        a4 = _mm_nt(jnp.concatenate([At_p, Rt_p], axis=0),
                    jnp.concatenate([_stack(Bt_p, lane_lo), _stack(Kt_p, lane_lo)], axis=0))
        A_ab = jnp.where(strict, a4[:L, :LANES], 0.0)
        A_ak = jnp.where(strict, a4[:L, LANES:], 0.0)
        A_rb = jnp.where(incl, a4[L:, :LANES], 0.0)
        A_rk = jnp.where(incl, a4[L:, LANES:], 0.0)

        A0 = jnp.where(blk(8), A_ab, 0.0)
        A2 = mmp(A0, A0)
        A4 = mmp(A2, A2)
        T = eye + A0
        T = T + mmp(T, A2)
        T = T + mmp(T, A4)
        s = 8
        while s < L:
            off = jnp.where(blk(2 * s) & jnp.logical_not(blk(s)), A_ab, 0.0)
            T = T + mmp(mmp(T, off), T)
            s *= 2

        AkV = mmp(A_ak, v_p)
        WU = _mm(T, jnp.concatenate([_stack(At_p, lane_lo), _stack(AkV, lane_lo)], axis=1))
        W = WU[:, :LANES]
        U0 = WU[:, LANES:]
        kb_t = jnp.concatenate([Kh_p, Bh_p], axis=0).T
        MN = _mm(kb_t, jnp.concatenate([jnp.concatenate([zeros_l, v_p], axis=1),
                                        jnp.concatenate([W, U0], axis=1)], axis=0))
        M = jnp.where(same_head, MN[:, :LANES], 0.0) + jnp.where(diag2, gam_end[:, sl], 0.0)
        N = jnp.where(same_head, MN[:, LANES:], 0.0)
        QY = _mm(jnp.concatenate([A_rb, A_rk], axis=1),
                 jnp.concatenate([jnp.concatenate([_stack(W, lane_lo), _stack(U0, lane_lo)], axis=1),
                                  jnp.concatenate([zeros_2l, _stack(v_p, lane_lo)], axis=1)], axis=0))
        Q = Rt_p + QY[:, :LANES]
        H = h_ref[slot0 + p]
        ys.append(_mm(Q, H) + QY[:, LANES:])
        h_ref[slot0 + p] = _mm(M, H) + N
    return jnp.concatenate(ys, axis=1)


def _scan_body(rf, vf, kkf, lwf, kf, af, rb, vb, kkb, lwb, kb, ab, yf_ref, yb_ref, h_ref):
    @pl.when(pl.program_id(1) == 0)
    def _():
        h_ref[...] = jnp.zeros(h_ref.shape, F32)

    n_pairs = D_A // LANES
    yf_ref[...] = _rwkv_chunk(rf[...], vf[...], kkf[...], lwf[...], kf[...], af[...], h_ref, 0, False)
    yb_ref[...] = _rwkv_chunk(rb[...], vb[...], kkb[...], lwb[...], kb[...], ab[...], h_ref, n_pairs, True)


def _scan(r, v, kk, lw0, k0, a0, lw1, k1, a1, batch, seq):
    t = r.shape[0]
    nc = seq // CHUNK
    fwd = pl.BlockSpec((CHUNK, D_A), lambda b, c: (b * nc + c, 0))
    bwd = pl.BlockSpec((CHUNK, D_A), lambda b, c: (b * nc + nc - 1 - c, 0))
    out = jax.ShapeDtypeStruct((t, D_A), F32)
    return pl.pallas_call(
        _scan_body,
        grid=(batch, nc),
        in_specs=[fwd] * 6 + [bwd] * 6,
        out_specs=[fwd, bwd],
        out_shape=[out, out],
        scratch_shapes=[pltpu.VMEM((2 * (D_A // LANES), LANES, LANES), F32)],
        compiler_params=_params(2),
        name="rwkv_scan",
    )(r, v, kk, lw0, k0, a0, r, v, kk, lw1, k1, a1)


def _attn_body(q_ref, k_ref, v_ref, lv_ref, sg_ref, o_ref, qs_ref, m_ref, l_ref, acc_ref, src_ref,
               *, lam_init):
    tq = q_ref.shape[0]
    tk = tq
    seq = k_ref.shape[0]
    h = pl.program_id(1)
    i = pl.program_id(2)
    slope_bits = (127 - 2 * (h + 1)) << 23
    slope = lax.bitcast_convert_type(jnp.full((1, 1), slope_bits, jnp.int32), F32)

    @pl.when(i == 0)
    def _():
        rr = lax.broadcasted_iota(jnp.int32, (tq, tk), 0)
        cc = lax.broadcasted_iota(jnp.int32, (tq, tk), 1)
        src_ref[...] = slope * (rr - cc).astype(F32)

    q = q_ref[...]
    lane = lax.broadcasted_iota(jnp.int32, q.shape, 1)
    zero = jnp.zeros_like(q)
    qs_ref[:tq, :] = jnp.where(lane < HEAD_B, q, zero)
    qs_ref[tq:, :] = jnp.where(lane < HEAD_B, zero, q)
    m_ref[...] = jnp.full(m_ref.shape, -1e30, F32)
    l_ref[...] = jnp.zeros(l_ref.shape, F32)
    acc_ref[...] = jnp.zeros(acc_ref.shape, F32)

    def step(j, side):
        off = pl.multiple_of(j * tk, tk)
        kb = k_ref[pl.ds(off, tk), :]
        vb = v_ref[pl.ds(off, tk), :]
        s = lax.dot_general(qs_ref[...], kb, _NT, preferred_element_type=F32)
        src = src_ref[...]
        bias = -jnp.abs(src) if side == 0 else (src if side > 0 else -src)
        s = s + jnp.concatenate([bias, bias], axis=0)
        cst = -slope * (jnp.abs(i - j) * tq).astype(F32)
        m_old = m_ref[...]
        m_new = jnp.maximum(m_old, jnp.max(s, axis=-1, keepdims=True) + cst)
        alpha = jnp.exp(m_old - m_new)
        p = jnp.exp(s - (m_new - cst))
        l_ref[...] = alpha * l_ref[...] + jnp.sum(p, axis=-1, keepdims=True)
        acc_ref[...] = alpha * acc_ref[...] + jnp.dot(p.astype(BF16), vb, preferred_element_type=F32)
        m_ref[...] = m_new

    lax.fori_loop(0, i, lambda j, c: (step(j, -1), c)[1], 0)
    step(i, 0)
    lax.fori_loop(i + 1, seq // tk, lambda j, c: (step(j, 1), c)[1], 0)

    lv = lv_ref[...]
    lam = (jnp.exp(jnp.sum(lv[0:1] * lv[1:2], axis=-1, keepdims=True))
           - jnp.exp(jnp.sum(lv[2:3] * lv[3:4], axis=-1, keepdims=True)) + lam_init)
    o = acc_ref[:tq, :] / l_ref[:tq, :] - lam * (acc_ref[tq:, :] / l_ref[tq:, :])
    o_ref[...] = (_rms_norm(o, sg_ref[...]) * (1.0 - lam_init)).astype(o_ref.dtype)


def _attn(q, k, v, lam_vecs, sub_g, lam_init, batch, seq):
    t = q.shape[0]
    tq = min(ATTN_TQ, seq)
    nq = seq // tq
    hw = 2 * HEAD_B
    return pl.pallas_call(
        functools.partial(_attn_body, lam_init=lam_init),
        grid=(batch, N_HEADS_B, nq),
        in_specs=[pl.BlockSpec((tq, hw), lambda b, h, i: (b * nq + i, h)),
                  pl.BlockSpec((seq, hw), lambda b, h, i: (b, h)),
                  pl.BlockSpec((seq, hw), lambda b, h, i: (b, h)),
                  _const_spec((4, HEAD_B)), _const_spec((1, hw))],
        out_specs=pl.BlockSpec((tq, hw), lambda b, h, i: (b * nq + i, h)),
        out_shape=jax.ShapeDtypeStruct((t, D_B), BF16),
        scratch_shapes=[pltpu.VMEM((2 * tq, hw), BF16),
                        pltpu.VMEM((2 * tq, 1), F32),
                        pltpu.VMEM((2 * tq, 1), F32),
                        pltpu.VMEM((2 * tq, hw), F32),
                        pltpu.VMEM((tq, tq), F32)],
        compiler_params=_params(3),
        name="diff_attn",
    )(q, k, v, lam_vecs, sub_g)


def _merge_body(yf_ref, yb_ref, gate_ref, bonus_ref, ob_ref, gt_ref, x_ref, lng_ref, lnb_ref, bd_ref,
                wb0_ref, wb1_ref, wo_ref, o_ref):
    bd = bd_ref[...]
    y = yf_ref[...] + yb_ref[...]
    yc = y - _mm_split_lhs(y, bd)
    var = _mm_split_lhs(yc * yc, bd)
    yn = yc * lax.rsqrt(var + GN_EPS) * lng_ref[...] + lnb_ref[...]
    oa = (yn + bonus_ref[...]) * gate_ref[...]
    gts = gt_ref[...].astype(F32)
    merged = (gts[:, :D_MODEL] * _mm(oa, wb0_ref[...])
              + gts[:, D_MODEL:] * jnp.dot(ob_ref[...], wb1_ref[...], preferred_element_type=F32))
    o_ref[...] = x_ref[...] + _mm(merged, wo_ref[...])


def _merge(yf, yb, gate, bonus, ob, gts, x2, ln_g, ln_b, bd_mean, wb0, wb1, wo):
    t = x2.shape[0]
    tm = min(TOKEN_TILE, t)
    row = lambda width: pl.BlockSpec((tm, width), lambda i: (i, 0))
    return pl.pallas_call(
        _merge_body,
        grid=(t // tm,),
        in_specs=[row(D_A), row(D_A), row(D_A), row(D_A), row(D_B), row(2 * D_MODEL), row(D_MODEL),
                  _const_spec((1, D_A)), _const_spec((1, D_A)), _const_spec((D_A, D_A)),
                  _const_spec((D_A, D_MODEL)), _const_spec((D_B, D_MODEL)),
                  _const_spec((D_MODEL, D_MODEL))],
        out_specs=row(D_MODEL),
        out_shape=jax.ShapeDtypeStruct((t, D_MODEL), F32),
        compiler_params=_params(1),
        name="merge",
    )(yf, yb, gate, bonus, ob, gts, x2, ln_g, ln_b, bd_mean, wb0, wb1, wo)


def _block_diag2(w):
    z = jnp.zeros_like(w[0])
    return jnp.concatenate([jnp.concatenate([w[0], z], axis=1), jnp.concatenate([z, w[1]], axis=1)], axis=0)


def kernel(x, norm_ffn1, ffn1_in, ffn1_out, norm_mix, w_in, rwkv_mu, decay_w0, decay_w2, iclr_a0, iclr_a2,
           gate_g2, k_k, k_a, r_k, ln_x_g, ln_x_b, q_gain, k_gain, diff_lambda, subln_g, w_branch, w_out,
           norm_ffn2, ffn2_in, ffn2_out):
    batch, seq, d = x.shape
    depth = norm_ffn1.shape[0]
    x2 = x.reshape(batch * seq, d)
    head_id = jnp.arange(D_A) // HEAD_A
    same = head_id[:, None] == head_id[None, :]
    bd_sum = same.astype(BF16)
    bd_mean = (same.astype(F32) / HEAD_A).astype(BF16)
    n_qk = D_B // HEAD_B
    row = lambda a: a.reshape(1, -1)
    for l in range(depth):
        lam_init = 0.8 - 0.6 * math.exp(-0.3 * l)
        x2 = _ffn(x2, row(norm_ffn1[l]), ffn1_in[l].astype(BF16), ffn1_out[l].astype(BF16))
        prw, q, k, v, gts = _proj(x2, row(norm_mix[l]), w_in[l].astype(BF16),
                                  row(jnp.tile(q_gain[l], n_qk)) * HEAD_B ** -0.5,
                                  row(jnp.tile(k_gain[l], n_qk)), bd_mean)
        r, vv, kk, lw0, k0, a0, lw1, k1, a1, gate, bonus = _prep(
            prw, seq, row(rwkv_mu[l]), row(decay_w0[l]), _block_diag2(decay_w2[l]), row(iclr_a0[l]),
            _block_diag2(iclr_a2[l]), gate_g2[l], row(k_k[l]), row(k_a[l]), row(r_k[l]), bd_sum)
        yf, yb = _scan(r, vv, kk, lw0, k0, a0, lw1, k1, a1, batch, seq)
        ob = _attn(q, k, v, diff_lambda[l], row(subln_g[l]), lam_init, batch, seq)
        x2 = _merge(yf, yb, gate, bonus, ob, gts, x2, row(ln_x_g[l]), row(ln_x_b[l]), bd_mean,
                    w_branch[l, 0].astype(BF16), w_branch[l, 1].astype(BF16), w_out[l].astype(BF16))
        x2 = _ffn(x2, row(norm_ffn2[l]), ffn2_in[l].astype(BF16), ffn2_out[l].astype(BF16))
    return x2.reshape(batch, seq, d)
```

```python
import functools
import math

import jax
import jax.numpy as jnp
from jax import lax
from jax.experimental import pallas as pl
from jax.experimental.pallas import tpu as pltpu

F32 = jnp.float32
BF16 = jnp.bfloat16

D_MODEL = 1024
D_A = 512
HEAD_A = 64
LORA = 64
LORA_GATE = 128
C_RWKV = 3 * D_A + 2 * LORA + 2 * LORA + LORA_GATE
D_B = 512
HEAD_B = 64
N_HEADS_B = D_B // (2 * HEAD_B)
D_FF = 2816
DECAY_SCALE = math.exp(-0.5)
GN_EPS = 64e-5
NORM_EPS = 1e-6

LANES = 128
SUBLANES = 8
VMEM_LIMIT = 56 * 1024 * 1024

TOKEN_TILE = 512
FF_TILE = 1408
CHUNK = 64
ATTN_TQ = 256

_NT = (((1,), (1,)), ((), ()))


def _mm(a, b):
    return jnp.dot(a.astype(BF16), b.astype(BF16), preferred_element_type=F32)


def _mm_nt(a, b):
    return lax.dot_general(a.astype(BF16), b.astype(BF16), _NT, preferred_element_type=F32)


def _split_bf16(x, n):
    parts = []
    for _ in range(n - 1):
        hi = x.astype(BF16)
        parts.append(hi)
        x = x - hi.astype(F32)
    parts.append(x.astype(BF16))
    return parts


def _mm_split_lhs(x, w, n=2):
    return sum(jnp.dot(p, w, preferred_element_type=F32) for p in _split_bf16(x, n))


def _rms_norm(x, g):
    return x * lax.rsqrt(jnp.mean(x * x, axis=-1, keepdims=True) + NORM_EPS) * g


def _const_spec(shape):
    return pl.BlockSpec(shape, lambda *_: (0,) * len(shape), pipeline_mode=pl.Buffered(1))


def _params(n_axes):
    return pltpu.CompilerParams(dimension_semantics=("arbitrary",) * n_axes,
                                vmem_limit_bytes=VMEM_LIMIT)


def _ffn_body(x_ref, g_ref, win_ref, wout_ref, o_ref):
    x = x_ref[...]
    h = _rms_norm(x, g_ref[...]).astype(BF16)
    acc = jnp.zeros(x.shape, F32)
    for j in range(D_FF // FF_TILE):
        lo, hi = j * FF_TILE, (j + 1) * FF_TILE
        gate = jnp.dot(h, win_ref[:, lo:hi], preferred_element_type=F32)
        up = jnp.dot(h, win_ref[:, D_FF + lo:D_FF + hi], preferred_element_type=F32)
        act = (gate * jax.nn.sigmoid(gate) * up).astype(BF16)
        acc = acc + jnp.dot(act, wout_ref[lo:hi, :], preferred_element_type=F32)
    o_ref[...] = x + 0.5 * acc


def _ffn(x2, g, w_in, w_out):
    t = x2.shape[0]
    tm = min(TOKEN_TILE, t)
    return pl.pallas_call(
        _ffn_body,
        grid=(t // tm,),
        in_specs=[pl.BlockSpec((tm, D_MODEL), lambda i: (i, 0)),
                  _const_spec((1, D_MODEL)),
                  _const_spec((D_MODEL, 2 * D_FF)),
                  _const_spec((D_FF, D_MODEL))],
        out_specs=pl.BlockSpec((tm, D_MODEL), lambda i: (i, 0)),
        out_shape=jax.ShapeDtypeStruct((t, D_MODEL), F32),
        compiler_params=_params(1),
        name="ffn",
    )(x2, g, w_in, w_out)


_Q0 = C_RWKV
_K0 = C_RWKV + D_B
_V0 = C_RWKV + 2 * D_B
_G0 = C_RWKV + 3 * D_B


def _proj_body(x_ref, g_ref, w_ref, qg_ref, kg_ref, bd_ref, prw_ref, q_ref, k_ref, vt_ref, gt_ref):
    h = _rms_norm(x_ref[...], g_ref[...]).astype(BF16)
    prw_ref[...] = jnp.dot(h, w_ref[:, :C_RWKV], preferred_element_type=F32)
    bd = bd_ref[...]
    q = jnp.dot(h, w_ref[:, _Q0:_K0], preferred_element_type=F32)
    q_ref[...] = (q * lax.rsqrt(_mm_split_lhs(q * q, bd) + NORM_EPS) * qg_ref[...]).astype(BF16)
    k = jnp.dot(h, w_ref[:, _K0:_V0], preferred_element_type=F32)
    k_ref[...] = (k * lax.rsqrt(_mm_split_lhs(k * k, bd) + NORM_EPS) * kg_ref[...]).astype(BF16)
    vt_ref[0] = jnp.dot(h, w_ref[:, _V0:_G0], preferred_element_type=F32).T.astype(BF16)
    gt_ref[...] = jax.nn.sigmoid(jnp.dot(h, w_ref[:, _G0:], preferred_element_type=F32)).astype(BF16)


def _proj(x2, g, w, qg, kg, bd_mean):
    t = x2.shape[0]
    tm = TOKEN_TILE
    c_in = w.shape[1]
    row = lambda width: pl.BlockSpec((tm, width), lambda i: (i, 0))
    return pl.pallas_call(
        _proj_body,
        grid=(t // tm,),
        in_specs=[row(D_MODEL), _const_spec((1, D_MODEL)), _const_spec((D_MODEL, c_in)),
                  _const_spec((1, D_B)), _const_spec((1, D_B)), _const_spec((D_B, D_B))],
        out_specs=[row(C_RWKV), row(D_B), row(D_B),
                   pl.BlockSpec((1, D_B, tm), lambda i: (i, 0, 0)), row(2 * D_MODEL)],
        out_shape=[jax.ShapeDtypeStruct((t, C_RWKV), F32),
                   jax.ShapeDtypeStruct((t, D_B), BF16),
                   jax.ShapeDtypeStruct((t, D_B), BF16),
                   jax.ShapeDtypeStruct((t // tm, D_B, tm), BF16),
                   jax.ShapeDtypeStruct((t, 2 * D_MODEL), BF16)],
        compiler_params=_params(1),
        name="proj",
    )(x2, g, w, qg, kg, bd_mean)


def _prep_body(p_ref, pp_ref, pn_ref, mu_ref, w0_ref, w2_ref, a0_ref, a2_ref, g2_ref, kk_ref, ka_ref,
               rk_ref, bd_ref,
               r_out, v_out, kk_out, lw0_out, k0_out, a0_out, lw1_out, k1_out, a1_out, gate_out,
               bonus_out, *, seq):
    i = pl.program_id(0)
    p = p_ref[...]
    tm = p.shape[0]
    row = lax.broadcasted_iota(jnp.int32, p.shape, 0)
    keep_prev = jnp.where((i * tm) % seq == 0, 0.0, 1.0)
    keep_next = jnp.where(((i + 1) * tm) % seq == 0, 0.0, 1.0)
    prev = jnp.where(row == 0, keep_prev * pp_ref[SUBLANES - 1:SUBLANES, :], pltpu.roll(p, 1, 0))
    nxt = jnp.where(row == tm - 1, keep_next * pn_ref[0:1, :], pltpu.roll(p, tm - 1, 0))
    p = p + mu_ref[...] * (0.5 * (prev + nxt) - p)

    r = p[:, :D_A]
    k = p[:, D_A:2 * D_A]
    v = p[:, 2 * D_A:3 * D_A]
    dw = p[:, 3 * D_A:3 * D_A + 2 * LORA]
    da = p[:, 3 * D_A + 2 * LORA:3 * D_A + 4 * LORA]
    dg = p[:, 3 * D_A + 4 * LORA:]
    bd = bd_ref[...]

    logw = -DECAY_SCALE * jax.nn.sigmoid(w0_ref[...] + _mm(jnp.tanh(dw), w2_ref[...]))
    iclr = jax.nn.sigmoid(a0_ref[...] + _mm(da, a2_ref[...]))
    gate_out[...] = _mm(jax.nn.sigmoid(dg), g2_ref[...])

    kk = k * kk_ref[...]
    norm = jnp.sqrt(_mm_split_lhs(kk * kk, bd))
    kk_out[...] = kk / jnp.maximum(norm, 1e-12)

    ka = ka_ref[...]
    a_f = iclr[:, :D_A]
    a_b = iclr[:, D_A:]
    k_f = k * (1.0 + (a_f - 1.0) * ka)
    k_b = k * (1.0 + (a_b - 1.0) * ka)
    bonus = _mm_split_lhs(r * rk_ref[...] * (k_f + k_b), bd)
    r_out[...] = r
    v_out[...] = v
    lw0_out[...] = logw[:, :D_A]
    k0_out[...] = k_f
    a0_out[...] = a_f
    lw1_out[...] = logw[:, D_A:]
    k1_out[...] = k_b
    a1_out[...] = a_b
    bonus_out[...] = bonus * v


def _prep(prw, seq, mu, w0, w2cat, a0, a2cat, g2, k_k, k_a, r_k, bd_sum):
    t = prw.shape[0]
    tm = min(TOKEN_TILE, seq)
    nb = tm // SUBLANES
    last = t // SUBLANES - 1
    row = lambda width: pl.BlockSpec((tm, width), lambda i: (i, 0))
    out = jax.ShapeDtypeStruct((t, D_A), F32)
    return pl.pallas_call(
        functools.partial(_prep_body, seq=seq),
        grid=(t // tm,),
        in_specs=[row(C_RWKV),
                  pl.BlockSpec((SUBLANES, C_RWKV), lambda i: (jnp.maximum(i * nb - 1, 0), 0)),
                  pl.BlockSpec((SUBLANES, C_RWKV), lambda i: (jnp.minimum((i + 1) * nb, last), 0)),
                  _const_spec((1, C_RWKV)),
                  _const_spec((1, 2 * D_A)), _const_spec((2 * LORA, 2 * D_A)),
                  _const_spec((1, 2 * D_A)), _const_spec((2 * LORA, 2 * D_A)),
                  _const_spec((LORA_GATE, D_A)),
                  _const_spec((1, D_A)), _const_spec((1, D_A)), _const_spec((1, D_A)),
                  _const_spec((D_A, D_A))],
        out_specs=[row(D_A)] * 11,
        out_shape=[out] * 11,
        compiler_params=_params(1),
        name="rwkv_prep",
    )(prw, prw, prw, mu, w0, w2cat, a0, a2cat, g2, k_k, k_a, r_k, bd_sum)


def _scan_body(rf, vf, kkf, lwf, kf, af, rb, vb, kkb, lwb, kb, ab, yf_ref, yb_ref, h_ref):
    @pl.when(pl.program_id(1) == 0)
    def _():
        h_ref[...] = jnp.zeros(h_ref.shape, F32)

    L = rf.shape[0]
    n_pairs = D_A // LANES
    ti = lax.broadcasted_iota(jnp.int32, (L, L), 0)
    tj = lax.broadcasted_iota(jnp.int32, (L, L), 1)
    pi = lax.broadcasted_iota(jnp.int32, (L, LANES), 0)
    lane = lax.broadcasted_iota(jnp.int32, (L, LANES), 1)
    pj = lane & (HEAD_A - 1)
    lane_lo = lane < HEAD_A
    eye = jnp.where(pj == pi, 1.0, 0.0)
    blk = lambda s: (pi // s) == (pj // s)
    row2 = lax.broadcasted_iota(jnp.int32, (2 * L, LANES), 0)
    col2 = lax.broadcasted_iota(jnp.int32, (2 * L, LANES), 1)
    same_head = (row2 // HEAD_A) == (col2 // HEAD_A)
    diag2 = row2 == col2
    zeros_l = jnp.zeros((L, LANES), F32)
    zeros_2l = jnp.zeros((2 * L, LANES), F32)

    def stack(x):
        return jnp.concatenate([jnp.where(lane_lo, x, 0.0), jnp.where(lane_lo, 0.0, x)], axis=0)

    def mmp(x, y):
        return _mm(x, stack(y))

    cat = jnp.concatenate
    chains = []
    for reverse, (r_, v_, kk_, lw_, k_, a_) in ((False, (rf, vf, kkf, lwf, kf, af)),
                                               (True, (rb, vb, kkb, lwb, kb, ab))):
        r, v, kk, lw, k, a = r_[...], v_[...], kk_[...], lw_[...], k_[...], a_[...]
        tri = ((tj >= ti) if reverse else (tj <= ti)).astype(BF16)
        G = sum(jnp.dot(tri, part, preferred_element_type=F32) for part in _split_bf16(lw, 3))
        g_end = G[0:1] if reverse else G[L - 1:L]
        gam_end = jnp.exp(g_end)
        inv_gam = jnp.exp(-G)
        rel_end = jnp.exp(g_end - G)
        b = kk * a
        full = dict(At=-kk * jnp.exp(G - lw), Bt=b * inv_gam, Kt=k * inv_gam, Rt=r * jnp.exp(G),
                    Kh=k * rel_end, Bh=b * rel_end, v=v)
        strict = (pj > pi) if reverse else (pj < pi)
        incl = (pj >= pi) if reverse else (pj <= pi)
        for p in range(n_pairs):
            sl = slice(p * LANES, (p + 1) * LANES)
            c = {name: val[:, sl] for name, val in full.items()}
            c.update(strict=strict, incl=incl, gam_end=gam_end[:, sl])
            chains.append(c)

    for c in chains:
        a4 = _mm_nt(cat([c["At"], c["Rt"]], axis=0), cat([stack(c["Bt"]), stack(c["Kt"])], axis=0))
        c["A_ab"] = jnp.where(c["strict"], a4[:L, :LANES], 0.0)
        c["A_ak"] = jnp.where(c["strict"], a4[:L, LANES:], 0.0)
        c["A_rb"] = jnp.where(c["incl"], a4[L:, :LANES], 0.0)
        c["A_rk"] = jnp.where(c["incl"], a4[L:, LANES:], 0.0)
    for c in chains:
        c["A0"] = jnp.where(blk(8), c["A_ab"], 0.0)
        c["A2"] = mmp(c["A0"], c["A0"])
    for c in chains:
        c["A4"] = mmp(c["A2"], c["A2"])
        c["T"] = eye + c["A0"]
    for c in chains:
        c["T"] = c["T"] + mmp(c["T"], c["A2"])
    for c in chains:
        c["T"] = c["T"] + mmp(c["T"], c["A4"])
    s = 8
    while s < L:
        outer = blk(2 * s) & jnp.logical_not(blk(s))
        for c in chains:
            c["TO"] = mmp(c["T"], jnp.where(outer, c["A_ab"], 0.0))
        for c in chains:
            c["T"] = c["T"] + mmp(c["TO"], c["T"])
        s *= 2
    for c in chains:
        c["AkV"] = mmp(c["A_ak"], c["v"])
    for c in chains:
        wu = _mm(c["T"], cat([stack(c["At"]), stack(c["AkV"])], axis=1))
        c["W"], c["U0"] = wu[:, :LANES], wu[:, LANES:]
    for c in chains:
        kb_t = cat([c["Kh"], c["Bh"]], axis=0).T
        mn = _mm(kb_t, cat([cat([zeros_l, c["v"]], axis=1), cat([c["W"], c["U0"]], axis=1)], axis=0))
        c["M"] = jnp.where(same_head, mn[:, :LANES], 0.0) + jnp.where(diag2, c["gam_end"], 0.0)
        c["N"] = jnp.where(same_head, mn[:, LANES:], 0.0)
    for c in chains:
        qy = _mm(cat([c["A_rb"], c["A_rk"]], axis=1),
                 cat([cat([stack(c["W"]), stack(c["U0"])], axis=1),
                      cat([zeros_2l, stack(c["v"])], axis=1)], axis=0))
        c["Q"] = c["Rt"] + qy[:, :LANES]
        c["Y0"] = qy[:, LANES:]
    ys = []
    for slot, c in enumerate(chains):
        H = h_ref[slot]
        ys.append(_mm(c["Q"], H) + c["Y0"])
        h_ref[slot] = _mm(c["M"], H) + c["N"]
    yf_ref[...] = cat(ys[:n_pairs], axis=1)
    yb_ref[...] = cat(ys[n_pairs:], axis=1)


def _scan(r, v, kk, lw0, k0, a0, lw1, k1, a1, batch, seq):
    t = r.shape[0]
    nc = seq // CHUNK
    fwd = pl.BlockSpec((CHUNK, D_A), lambda b, c: (b * nc + c, 0))
    bwd = pl.BlockSpec((CHUNK, D_A), lambda b, c: (b * nc + nc - 1 - c, 0))
    out = jax.ShapeDtypeStruct((t, D_A), F32)
    return pl.pallas_call(
        _scan_body,
        grid=(batch, nc),
        in_specs=[fwd] * 6 + [bwd] * 6,
        out_specs=[fwd, bwd],
        out_shape=[out, out],
        scratch_shapes=[pltpu.VMEM((2 * (D_A // LANES), LANES, LANES), F32)],
        compiler_params=_params(2),
        name="rwkv_scan",
    )(r, v, kk, lw0, k0, a0, r, v, kk, lw1, k1, a1)


def _attn_body(q_ref, k_ref, vt_ref, lv_ref, sg_ref, o_ref, qs_ref, m_ref, l_ref, acc_ref, src_ref,
               *, lam_init):
    tq = q_ref.shape[0]
    n_kv, _, tk = vt_ref.shape
    h = pl.program_id(1)
    i = pl.program_id(2)
    slope_bits = (127 - 2 * (h + 1)) << 23
    slope = lax.bitcast_convert_type(jnp.full((1, 1), slope_bits, jnp.int32), jnp.float32).astype(F32)

    @pl.when(i == 0)
    def _():
        kv_pos = lax.broadcasted_iota(jnp.int32, (tk, tq), 0)
        q_pos = lax.broadcasted_iota(jnp.int32, (tk, tq), 1)
        src_ref[...] = slope * (q_pos - kv_pos).astype(F32)

    q = q_ref[...]
    lane = lax.broadcasted_iota(jnp.int32, q.shape, 1)
    zero = jnp.zeros_like(q)
    qs_ref[:tq, :] = jnp.where(lane < HEAD_B, q, zero)
    qs_ref[tq:, :] = jnp.where(lane < HEAD_B, zero, q)
    m_ref[...] = jnp.full(m_ref.shape, -1e30, F32)
    l_ref[...] = jnp.zeros(l_ref.shape, F32)
    acc_ref[...] = jnp.zeros(acc_ref.shape, F32)
    q0 = i * tq

    def step(j, side):
        k0 = j * tk
        kb = k_ref[pl.ds(pl.multiple_of(k0, tk), tk), :]
        s = lax.dot_general(kb, qs_ref[...], _NT, preferred_element_type=F32)
        src = src_ref[...]
        dist0 = (q0 - k0).astype(F32)
        if side == 0:
            bias = -jnp.abs(src + slope * dist0)
            cst = jnp.zeros((1, 1), F32)
        elif side < 0:
            bias = -src
            cst = -slope * dist0
        else:
            bias = src
            cst = slope * dist0
        s = s + jnp.concatenate([bias, bias], axis=1)
        m_old = m_ref[...]
        m_new = jnp.maximum(m_old, jnp.max(s, axis=0, keepdims=True) + cst)
        alpha = jnp.exp(m_old - m_new)
        p = jnp.exp(s - (m_new - cst))
        l_ref[...] = alpha * l_ref[...] + jnp.sum(p, axis=0, keepdims=True)
        acc_ref[...] = alpha * acc_ref[...] + jnp.dot(vt_ref[j], p.astype(BF16),
                                                      preferred_element_type=F32)
        m_ref[...] = m_new

    j_diag = q0 // tk
    lax.fori_loop(0, j_diag, lambda j, c: (step(j, -1), c)[1], 0)
    step(j_diag, 0)
    lax.fori_loop(j_diag + 1, n_kv, lambda j, c: (step(j, 1), c)[1], 0)

    lv = lv_ref[...]
    lam = (jnp.exp(jnp.sum(lv[0:1] * lv[1:2], axis=-1, keepdims=True))
           - jnp.exp(jnp.sum(lv[2:3] * lv[3:4], axis=-1, keepdims=True)) + lam_init)
    o_t = acc_ref[:, :tq] / l_ref[:, :tq] - lam * (acc_ref[:, tq:] / l_ref[:, tq:])
    o_t = o_t * lax.rsqrt(jnp.mean(o_t * o_t, axis=0, keepdims=True) + NORM_EPS)
    o_ref[...] = (o_t.T * sg_ref[...] * (1.0 - lam_init)).astype(o_ref.dtype)


def _attn(q, k, vt, lam_vecs, sub_g, lam_init, batch, seq):
    t = q.shape[0]
    tk = vt.shape[2]
    tq = min(ATTN_TQ, seq)
    nq = seq // tq
    n_kv = seq // tk
    hw = 2 * HEAD_B
    return pl.pallas_call(
        functools.partial(_attn_body, lam_init=lam_init),
        grid=(batch, N_HEADS_B, nq),
        in_specs=[pl.BlockSpec((tq, hw), lambda b, h, i: (b * nq + i, h)),
                  pl.BlockSpec((seq, hw), lambda b, h, i: (b, h)),
                  pl.BlockSpec((n_kv, hw, tk), lambda b, h, i: (b, h, 0)),
                  _const_spec((4, HEAD_B)), _const_spec((1, hw))],
        out_specs=pl.BlockSpec((tq, hw), lambda b, h, i: (b * nq + i, h)),
        out_shape=jax.ShapeDtypeStruct((t, D_B), BF16),
        scratch_shapes=[pltpu.VMEM((2 * tq, hw), BF16),
                        pltpu.VMEM((1, 2 * tq), F32),
                        pltpu.VMEM((1, 2 * tq), F32),
                        pltpu.VMEM((hw, 2 * tq), F32),
                        pltpu.VMEM((tk, tq), F32)],
        compiler_params=_params(3),
        name="diff_attn",
    )(q, k, vt, lam_vecs, sub_g)


def _merge_body(yf_ref, yb_ref, gate_ref, bonus_ref, ob_ref, gt_ref, x_ref, lng_ref, lnb_ref, bd_ref,
                wb0_ref, wb1_ref, wo_ref, o_ref):
    bd = bd_ref[...]
    y = yf_ref[...] + yb_ref[...]
    yc = y - _mm_split_lhs(y, bd)
    var = _mm_split_lhs(yc * yc, bd)
    yn = yc * lax.rsqrt(var + GN_EPS) * lng_ref[...] + lnb_ref[...]
    oa = (yn + bonus_ref[...]) * gate_ref[...]
    gts = gt_ref[...].astype(F32)
    merged = (gts[:, :D_MODEL] * _mm(oa, wb0_ref[...])
              + gts[:, D_MODEL:] * jnp.dot(ob_ref[...], wb1_ref[...], preferred_element_type=F32))
    o_ref[...] = x_ref[...] + _mm(merged, wo_ref[...])


def _merge(yf, yb, gate, bonus, ob, gts, x2, ln_g, ln_b, bd_mean, wb0, wb1, wo):
    t = x2.shape[0]
    tm = min(TOKEN_TILE, t)
    row = lambda width: pl.BlockSpec((tm, width), lambda i: (i, 0))
    return pl.pallas_call(
        _merge_body,
        grid=(t // tm,),
        in_specs=[row(D_A), row(D_A), row(D_A), row(D_A), row(D_B), row(2 * D_MODEL), row(D_MODEL),
                  _const_spec((1, D_A)), _const_spec((1, D_A)), _const_spec((D_A, D_A)),
                  _const_spec((D_A, D_MODEL)), _const_spec((D_B, D_MODEL)),
                  _const_spec((D_MODEL, D_MODEL))],
        out_specs=row(D_MODEL),
        out_shape=jax.ShapeDtypeStruct((t, D_MODEL), F32),
        compiler_params=_params(1),
        name="merge",
    )(yf, yb, gate, bonus, ob, gts, x2, ln_g, ln_b, bd_mean, wb0, wb1, wo)


def _block_diag2(w):
    z = jnp.zeros_like(w[0])
    return jnp.concatenate([jnp.concatenate([w[0], z], axis=1), jnp.concatenate([z, w[1]], axis=1)], axis=0)


def kernel(x, norm_ffn1, ffn1_in, ffn1_out, norm_mix, w_in, rwkv_mu, decay_w0, decay_w2, iclr_a0, iclr_a2,
           gate_g2, k_k, k_a, r_k, ln_x_g, ln_x_b, q_gain, k_gain, diff_lambda, subln_g, w_branch, w_out,
           norm_ffn2, ffn2_in, ffn2_out):
    batch, seq, d = x.shape
    depth = norm_ffn1.shape[0]
    assert seq % TOKEN_TILE == 0 and d == D_MODEL
    x2 = x.reshape(batch * seq, d)
    head_id = jnp.arange(D_A) // HEAD_A
    same = head_id[:, None] == head_id[None, :]
    bd_sum = same.astype(BF16)
    bd_mean = (same.astype(F32) / HEAD_A).astype(BF16)
    n_qk = D_B // HEAD_B
    row = lambda a: a.reshape(1, -1)
    for l in range(depth):
        lam_init = 0.8 - 0.6 * math.exp(-0.3 * l)
        x2 = _ffn(x2, row(norm_ffn1[l]), ffn1_in[l].astype(BF16), ffn1_out[l].astype(BF16))
        prw, q, k, vt, gts = _proj(x2, row(norm_mix[l]), w_in[l].astype(BF16),
                                   row(jnp.tile(q_gain[l], n_qk)) * HEAD_B ** -0.5,
                                   row(jnp.tile(k_gain[l], n_qk)), bd_mean)
        r, vv, kk, lw0, k0, a0, lw1, k1, a1, gate, bonus = _prep(
            prw, seq, row(rwkv_mu[l]), row(decay_w0[l]), _block_diag2(decay_w2[l]).astype(BF16),
            row(iclr_a0[l]), _block_diag2(iclr_a2[l]).astype(BF16), gate_g2[l].astype(BF16),
            row(k_k[l]), row(k_a[l]), row(r_k[l]), bd_sum)
        yf, yb = _scan(r, vv, kk, lw0, k0, a0, lw1, k1, a1, batch, seq)
        ob = _attn(q, k, vt, diff_lambda[l], row(subln_g[l]), lam_init, batch, seq)
        x2 = _merge(yf, yb, gate, bonus, ob, gts, x2, row(ln_x_g[l]), row(ln_x_b[l]), bd_mean,
                    w_branch[l, 0].astype(BF16), w_branch[l, 1].astype(BF16), w_out[l].astype(BF16))
        x2 = _ffn(x2, row(norm_ffn2[l]), ffn2_in[l].astype(BF16), ffn2_out[l].astype(BF16))
    return x2.reshape(batch, seq, d)
```

```python
import functools
import math

import jax
import jax.numpy as jnp
from jax import lax
from jax.experimental import pallas as pl
from jax.experimental.pallas import tpu as pltpu

F32 = jnp.float32
BF16 = jnp.bfloat16

D_MODEL = 1024
D_A = 512
HEAD_A = 64
LORA = 64
LORA_GATE = 128
C_RWKV = 3 * D_A + 2 * LORA + 2 * LORA + LORA_GATE
D_B = 512
HEAD_B = 64
N_HEADS_B = D_B // (2 * HEAD_B)
D_FF = 2816
DECAY_SCALE = math.exp(-0.5)
GN_EPS = 64e-5
NORM_EPS = 1e-6
LOG2E = math.log2(math.e)
F32_ZERO_EXP2 = 150.0
ROUNDING_MARGIN = 1.02

LANES = 128
SUBLANES = 8
VMEM_LIMIT = 56 * 1024 * 1024

TOKEN_TILE = 512
FF_TILE = 1408
CHUNK = 64
ATTN_TQ = 256

_NT = (((1,), (1,)), ((), ()))


def _mm(a, b):
    return jnp.dot(a.astype(BF16), b.astype(BF16), preferred_element_type=F32)


def _mm_nt(a, b):
    return lax.dot_general(a.astype(BF16), b.astype(BF16), _NT, preferred_element_type=F32)


def _split_bf16(x, n):
    parts = []
    for _ in range(n - 1):
        hi = x.astype(BF16)
        parts.append(hi)
        x = x - hi.astype(F32)
    parts.append(x.astype(BF16))
    return parts


def _mm_split_lhs(x, w, n=2):
    return sum(jnp.dot(p, w, preferred_element_type=F32) for p in _split_bf16(x, n))


def _rms_norm(x, g):
    return x * lax.rsqrt(jnp.mean(x * x, axis=-1, keepdims=True) + NORM_EPS) * g


def _const_spec(shape):
    return pl.BlockSpec(shape, lambda *_: (0,) * len(shape), pipeline_mode=pl.Buffered(1))


def _params(n_axes):
    return pltpu.CompilerParams(dimension_semantics=("arbitrary",) * n_axes,
                                vmem_limit_bytes=VMEM_LIMIT)


def _ffn_body(x_ref, g_ref, win_ref, wout_ref, o_ref):
    x = x_ref[...]
    h = _rms_norm(x, g_ref[...]).astype(BF16)
    acc = jnp.zeros(x.shape, F32)
    for j in range(D_FF // FF_TILE):
        lo, hi = j * FF_TILE, (j + 1) * FF_TILE
        gate = jnp.dot(h, win_ref[:, lo:hi], preferred_element_type=F32)
        up = jnp.dot(h, win_ref[:, D_FF + lo:D_FF + hi], preferred_element_type=F32)
        act = (gate * jax.nn.sigmoid(gate) * up).astype(BF16)
        acc = acc + jnp.dot(act, wout_ref[lo:hi, :], preferred_element_type=F32)
    o_ref[...] = x + 0.5 * acc


def _ffn(x2, g, w_in, w_out):
    t = x2.shape[0]
    tm = min(TOKEN_TILE, t)
    return pl.pallas_call(
        _ffn_body,
        grid=(t // tm,),
        in_specs=[pl.BlockSpec((tm, D_MODEL), lambda i: (i, 0)),
                  _const_spec((1, D_MODEL)),
                  _const_spec((D_MODEL, 2 * D_FF)),
                  _const_spec((D_FF, D_MODEL))],
        out_specs=pl.BlockSpec((tm, D_MODEL), lambda i: (i, 0)),
        out_shape=jax.ShapeDtypeStruct((t, D_MODEL), F32),
        compiler_params=_params(1),
        name="ffn",
    )(x2, g, w_in, w_out)


_Q0 = C_RWKV
_K0 = C_RWKV + D_B
_V0 = C_RWKV + 2 * D_B
_G0 = C_RWKV + 3 * D_B


def _proj_body(x_ref, g_ref, w_ref, qg_ref, kg_ref, bd_ref, prw_ref, q_ref, k_ref, vt_ref, gt_ref):
    h = _rms_norm(x_ref[...], g_ref[...]).astype(BF16)
    prw_ref[...] = jnp.dot(h, w_ref[:, :C_RWKV], preferred_element_type=F32)
    bd = bd_ref[...]
    q = jnp.dot(h, w_ref[:, _Q0:_K0], preferred_element_type=F32)
    q_ref[...] = (q * lax.rsqrt(_mm_split_lhs(q * q, bd) + NORM_EPS) * qg_ref[...]).astype(BF16)
    k = jnp.dot(h, w_ref[:, _K0:_V0], preferred_element_type=F32)
    k_ref[...] = (k * lax.rsqrt(_mm_split_lhs(k * k, bd) + NORM_EPS) * kg_ref[...]).astype(BF16)
    vt_ref[0] = jnp.dot(h, w_ref[:, _V0:_G0], preferred_element_type=F32).T.astype(BF16)
    gt_ref[...] = jax.nn.sigmoid(jnp.dot(h, w_ref[:, _G0:], preferred_element_type=F32)).astype(BF16)


def _proj(x2, g, w, qg, kg, bd_mean):
    t = x2.shape[0]
    tm = TOKEN_TILE
    c_in = w.shape[1]
    row = lambda width: pl.BlockSpec((tm, width), lambda i: (i, 0))
    return pl.pallas_call(
        _proj_body,
        grid=(t // tm,),
        in_specs=[row(D_MODEL), _const_spec((1, D_MODEL)), _const_spec((D_MODEL, c_in)),
                  _const_spec((1, D_B)), _const_spec((1, D_B)), _const_spec((D_B, D_B))],
        out_specs=[row(C_RWKV), row(D_B), row(D_B),
                   pl.BlockSpec((1, D_B, tm), lambda i: (i, 0, 0)), row(2 * D_MODEL)],
        out_shape=[jax.ShapeDtypeStruct((t, C_RWKV), F32),
                   jax.ShapeDtypeStruct((t, D_B), BF16),
                   jax.ShapeDtypeStruct((t, D_B), BF16),
                   jax.ShapeDtypeStruct((t // tm, D_B, tm), BF16),
                   jax.ShapeDtypeStruct((t, 2 * D_MODEL), BF16)],
        compiler_params=_params(1),
        name="proj",
    )(x2, g, w, qg, kg, bd_mean)


def _prep_body(p_ref, pp_ref, pn_ref, mu_ref, w0_ref, w2_ref, a0_ref, a2_ref, g2_ref, kk_ref, ka_ref,
               rk_ref, bd_ref,
               r_out, v_out, kk_out, lw0_out, k0_out, a0_out, lw1_out, k1_out, a1_out, gate_out,
               bonus_out, *, seq):
    i = pl.program_id(0)
    p = p_ref[...]
    tm = p.shape[0]
    row = lax.broadcasted_iota(jnp.int32, p.shape, 0)
    keep_prev = jnp.where((i * tm) % seq == 0, 0.0, 1.0)
    keep_next = jnp.where(((i + 1) * tm) % seq == 0, 0.0, 1.0)
    prev = jnp.where(row == 0, keep_prev * pp_ref[SUBLANES - 1:SUBLANES, :], pltpu.roll(p, 1, 0))
    nxt = jnp.where(row == tm - 1, keep_next * pn_ref[0:1, :], pltpu.roll(p, tm - 1, 0))
    p = p + mu_ref[...] * (0.5 * (prev + nxt) - p)

    r = p[:, :D_A]
    k = p[:, D_A:2 * D_A]
    v = p[:, 2 * D_A:3 * D_A]
    dw = p[:, 3 * D_A:3 * D_A + 2 * LORA]
    da = p[:, 3 * D_A + 2 * LORA:3 * D_A + 4 * LORA]
    dg = p[:, 3 * D_A + 4 * LORA:]
    bd = bd_ref[...]

    logw = -DECAY_SCALE * jax.nn.sigmoid(w0_ref[...] + _mm(jnp.tanh(dw), w2_ref[...]))
    iclr = jax.nn.sigmoid(a0_ref[...] + _mm(da, a2_ref[...]))
    gate_out[...] = _mm(jax.nn.sigmoid(dg), g2_ref[...])

    kk = k * kk_ref[...]
    norm = jnp.sqrt(_mm_split_lhs(kk * kk, bd))
    kk_out[...] = kk / jnp.maximum(norm, 1e-12)

    ka = ka_ref[...]
    a_f = iclr[:, :D_A]
    a_b = iclr[:, D_A:]
    k_f = k * (1.0 + (a_f - 1.0) * ka)
    k_b = k * (1.0 + (a_b - 1.0) * ka)
    bonus = _mm_split_lhs(r * rk_ref[...] * (k_f + k_b), bd)
    r_out[...] = r
    v_out[...] = v
    lw0_out[...] = logw[:, :D_A]
    k0_out[...] = k_f
    a0_out[...] = a_f
    lw1_out[...] = logw[:, D_A:]
    k1_out[...] = k_b
    a1_out[...] = a_b
    bonus_out[...] = bonus * v


def _prep(prw, seq, mu, w0, w2cat, a0, a2cat, g2, k_k, k_a, r_k, bd_sum):
    t = prw.shape[0]
    tm = min(TOKEN_TILE, seq)
    nb = tm // SUBLANES
    last = t // SUBLANES - 1
    row = lambda width: pl.BlockSpec((tm, width), lambda i: (i, 0))
    out = jax.ShapeDtypeStruct((t, D_A), F32)
    return pl.pallas_call(
        functools.partial(_prep_body, seq=seq),
        grid=(t // tm,),
        in_specs=[row(C_RWKV),
                  pl.BlockSpec((SUBLANES, C_RWKV), lambda i: (jnp.maximum(i * nb - 1, 0), 0)),
                  pl.BlockSpec((SUBLANES, C_RWKV), lambda i: (jnp.minimum((i + 1) * nb, last), 0)),
                  _const_spec((1, C_RWKV)),
                  _const_spec((1, 2 * D_A)), _const_spec((2 * LORA, 2 * D_A)),
                  _const_spec((1, 2 * D_A)), _const_spec((2 * LORA, 2 * D_A)),
                  _const_spec((LORA_GATE, D_A)),
                  _const_spec((1, D_A)), _const_spec((1, D_A)), _const_spec((1, D_A)),
                  _const_spec((D_A, D_A))],
        out_specs=[row(D_A)] * 11,
        out_shape=[out] * 11,
        compiler_params=_params(1),
        name="rwkv_prep",
    )(prw, prw, prw, mu, w0, w2cat, a0, a2cat, g2, k_k, k_a, r_k, bd_sum)


def _scan_body(rf, vf, kkf, lwf, kf, af, rb, vb, kkb, lwb, kb, ab, yf_ref, yb_ref, h_ref):
    @pl.when(pl.program_id(1) == 0)
    def _():
        h_ref[...] = jnp.zeros(h_ref.shape, F32)

    L = rf.shape[0]
    n_pairs = D_A // LANES
    ti = lax.broadcasted_iota(jnp.int32, (L, L), 0)
    tj = lax.broadcasted_iota(jnp.int32, (L, L), 1)
    pi = lax.broadcasted_iota(jnp.int32, (L, LANES), 0)
    lane = lax.broadcasted_iota(jnp.int32, (L, LANES), 1)
    pj = lane & (HEAD_A - 1)
    lane_lo = lane < HEAD_A
    eye = jnp.where(pj == pi, 1.0, 0.0)
    blk = lambda s: (pi // s) == (pj // s)
    row2 = lax.broadcasted_iota(jnp.int32, (2 * L, LANES), 0)
    col2 = lax.broadcasted_iota(jnp.int32, (2 * L, LANES), 1)
    same_head = (row2 // HEAD_A) == (col2 // HEAD_A)
    diag2 = row2 == col2
    zeros_l = jnp.zeros((L, LANES), F32)
    zeros_2l = jnp.zeros((2 * L, LANES), F32)

    def stack(x):
        return jnp.concatenate([jnp.where(lane_lo, x, 0.0), jnp.where(lane_lo, 0.0, x)], axis=0)

    def mmp(x, y):
        return _mm(x, stack(y))

    cat = jnp.concatenate
    chains = []
    for reverse, (r_, v_, kk_, lw_, k_, a_) in ((False, (rf, vf, kkf, lwf, kf, af)),
                                               (True, (rb, vb, kkb, lwb, kb, ab))):
        r, v, kk, lw, k, a = r_[...], v_[...], kk_[...], lw_[...], k_[...], a_[...]
        tri = ((tj >= ti) if reverse else (tj <= ti)).astype(BF16)
        G = sum(jnp.dot(tri, part, preferred_element_type=F32) for part in _split_bf16(lw, 3))
        g_end = G[0:1] if reverse else G[L - 1:L]
        gam_end = jnp.exp(g_end)
        inv_gam = jnp.exp(-G)
        rel_end = jnp.exp(g_end - G)
        b = kk * a
        full = dict(At=-kk * jnp.exp(G - lw), Bt=b * inv_gam, Kt=k * inv_gam, Rt=r * jnp.exp(G),
                    Kh=k * rel_end, Bh=b * rel_end, v=v)
        strict = (pj > pi) if reverse else (pj < pi)
        incl = (pj >= pi) if reverse else (pj <= pi)
        for p in range(n_pairs):
            sl = slice(p * LANES, (p + 1) * LANES)
            c = {name: val[:, sl] for name, val in full.items()}
            c.update(strict=strict, incl=incl, gam_end=gam_end[:, sl])
            chains.append(c)

    for c in chains:
        a4 = _mm_nt(cat([c["At"], c["Rt"]], axis=0), cat([stack(c["Bt"]), stack(c["Kt"])], axis=0))
        c["A_ab"] = jnp.where(c["strict"], a4[:L, :LANES], 0.0)
        c["A_ak"] = jnp.where(c["strict"], a4[:L, LANES:], 0.0)
        c["A_rb"] = jnp.where(c["incl"], a4[L:, :LANES], 0.0)
        c["A_rk"] = jnp.where(c["incl"], a4[L:, LANES:], 0.0)
    for c in chains:
        c["A0"] = jnp.where(blk(8), c["A_ab"], 0.0)
        c["A2"] = mmp(c["A0"], c["A0"])
    for c in chains:
        c["A4"] = mmp(c["A2"], c["A2"])
        c["T"] = eye + c["A0"]
    for c in chains:
        c["T"] = c["T"] + mmp(c["T"], c["A2"])
    for c in chains:
        c["T"] = c["T"] + mmp(c["T"], c["A4"])
    s = 8
    while s < L:
        outer = blk(2 * s) & jnp.logical_not(blk(s))
        for c in chains:
            c["TO"] = mmp(c["T"], jnp.where(outer, c["A_ab"], 0.0))
        for c in chains:
            c["T"] = c["T"] + mmp(c["TO"], c["T"])
        s *= 2
    for c in chains:
        c["AkV"] = mmp(c["A_ak"], c["v"])
    for c in chains:
        wu = _mm(c["T"], cat([stack(c["At"]), stack(c["AkV"])], axis=1))
        c["W"], c["U0"] = wu[:, :LANES], wu[:, LANES:]
    for c in chains:
        kb_t = cat([c["Kh"], c["Bh"]], axis=0).T
        mn = _mm(kb_t, cat([cat([zeros_l, c["v"]], axis=1), cat([c["W"], c["U0"]], axis=1)], axis=0))
        c["M"] = jnp.where(same_head, mn[:, :LANES], 0.0) + jnp.where(diag2, c["gam_end"], 0.0)
        c["N"] = jnp.where(same_head, mn[:, LANES:], 0.0)
    for c in chains:
        qy = _mm(cat([c["A_rb"], c["A_rk"]], axis=1),
                 cat([cat([stack(c["W"]), stack(c["U0"])], axis=1),
                      cat([zeros_2l, stack(c["v"])], axis=1)], axis=0))
        c["Q"] = c["Rt"] + qy[:, :LANES]
        c["Y0"] = qy[:, LANES:]
    ys = []
    for slot, c in enumerate(chains):
        H = h_ref[slot]
        ys.append(_mm(c["Q"], H) + c["Y0"])
        h_ref[slot] = _mm(c["M"], H) + c["N"]
    yf_ref[...] = cat(ys[:n_pairs], axis=1)
    yb_ref[...] = cat(ys[n_pairs:], axis=1)


def _scan(r, v, kk, lw0, k0, a0, lw1, k1, a1, batch, seq):
    t = r.shape[0]
    nc = seq // CHUNK
    fwd = pl.BlockSpec((CHUNK, D_A), lambda b, c: (b * nc + c, 0))
    bwd = pl.BlockSpec((CHUNK, D_A), lambda b, c: (b * nc + nc - 1 - c, 0))
    out = jax.ShapeDtypeStruct((t, D_A), F32)
    return pl.pallas_call(
        _scan_body,
        grid=(batch, nc),
        in_specs=[fwd] * 6 + [bwd] * 6,
        out_specs=[fwd, bwd],
        out_shape=[out, out],
        scratch_shapes=[pltpu.VMEM((2 * (D_A // LANES), LANES, LANES), F32)],
        compiler_params=_params(2),
        name="rwkv_scan",
    )(r, v, kk, lw0, k0, a0, r, v, kk, lw1, k1, a1)


def _attn_body(q_ref, k_ref, vt_ref, lv_ref, sg_ref, qg_ref, kg_ref, o_ref, qs_ref, m_ref, l_ref, acc_ref,
               src_ref, *, lam_init):
    tq = q_ref.shape[0]
    n_kv, _, tk = vt_ref.shape
    seq = n_kv * tk
    h = pl.program_id(1)
    i = pl.program_id(2)
    slope_bits = (127 - 2 * (h + 1)) << 23
    slope = (lax.bitcast_convert_type(jnp.full((1, 1), slope_bits, jnp.int32), jnp.float32).astype(F32)
             * LOG2E)
    bound = (HEAD_B * ROUNDING_MARGIN * jnp.max(jnp.abs(qg_ref[...]), axis=-1, keepdims=True)
             * jnp.max(jnp.abs(kg_ref[...]), axis=-1, keepdims=True))
    reach = jnp.minimum((F32_ZERO_EXP2 + 2.0 * bound) / slope + 1.0, float(seq)).astype(jnp.int32)[0, 0]

    @pl.when(i == 0)
    def _():
        kv_pos = lax.broadcasted_iota(jnp.int32, (tk, tq), 0)
        q_pos = lax.broadcasted_iota(jnp.int32, (tk, tq), 1)
        src_ref[...] = slope * (q_pos - kv_pos).astype(F32)

    q = q_ref[...]
    lane = lax.broadcasted_iota(jnp.int32, q.shape, 1)
    zero = jnp.zeros_like(q)
    qs_ref[:tq, :] = jnp.where(lane < HEAD_B, q, zero)
    qs_ref[tq:, :] = jnp.where(lane < HEAD_B, zero, q)
    m_ref[...] = jnp.full(m_ref.shape, -1e30, F32)
    l_ref[...] = jnp.zeros(l_ref.shape, F32)
    acc_ref[...] = jnp.zeros(acc_ref.shape, F32)
    q0 = i * tq

    def step(j, side):
        k0 = j * tk
        kb = k_ref[pl.ds(pl.multiple_of(k0, tk), tk), :]
        s = lax.dot_general(kb, qs_ref[...], _NT, preferred_element_type=F32)
        src = src_ref[...]
        dist0 = (q0 - k0).astype(F32)
        if side == 0:
            bias = -jnp.abs(src + slope * dist0)
            cst = jnp.zeros((1, 1), F32)
        elif side < 0:
            bias = -src
            cst = -slope * dist0
        else:
            bias = src
            cst = slope * dist0
        s = s + jnp.concatenate([bias, bias], axis=1)
        m_old = m_ref[...]
        m_new = jnp.maximum(m_old, jnp.max(s, axis=0, keepdims=True) + cst)
        alpha = jnp.exp2(m_old - m_new)
        p = jnp.exp2(s - (m_new - cst))
        l_ref[...] = alpha * l_ref[...] + jnp.sum(p, axis=0, keepdims=True)
        acc_ref[...] = alpha * acc_ref[...] + jnp.dot(vt_ref[j], p.astype(BF16),
                                                      preferred_element_type=F32)
        m_ref[...] = m_new

    j_diag = q0 // tk
    j_lo = jnp.maximum(q0 - reach, 0) // tk
    j_hi = jnp.minimum(q0 + tq - 1 + reach, seq - 1) // tk + 1
    lax.fori_loop(j_lo, j_diag, lambda j, c: (step(j, -1), c)[1], 0)
    step(j_diag, 0)
    lax.fori_loop(j_diag + 1, j_hi, lambda j, c: (step(j, 1), c)[1], 0)

    lv = lv_ref[...]
    lam = (jnp.exp(jnp.sum(lv[0:1] * lv[1:2], axis=-1, keepdims=True))
           - jnp.exp(jnp.sum(lv[2:3] * lv[3:4], axis=-1, keepdims=True)) + lam_init)
    o_t = acc_ref[:, :tq] / l_ref[:, :tq] - lam * (acc_ref[:, tq:] / l_ref[:, tq:])
    o_t = o_t * lax.rsqrt(jnp.mean(o_t * o_t, axis=0, keepdims=True) + NORM_EPS)
    o_ref[...] = (o_t.T * sg_ref[...] * (1.0 - lam_init)).astype(o_ref.dtype)


def _attn(q, k, vt, lam_vecs, sub_g, qg, kg, lam_init, batch, seq):
    t = q.shape[0]
    tk = vt.shape[2]
    tq = min(ATTN_TQ, seq)
    nq = seq // tq
    n_kv = seq // tk
    hw = 2 * HEAD_B
    return pl.pallas_call(
        functools.partial(_attn_body, lam_init=lam_init),
        grid=(batch, N_HEADS_B, nq),
        in_specs=[pl.BlockSpec((tq, hw), lambda b, h, i: (b * nq + i, h)),
                  pl.BlockSpec((seq, hw), lambda b, h, i: (b, h)),
                  pl.BlockSpec((n_kv, hw, tk), lambda b, h, i: (b, h, 0)),
                  _const_spec((4, HEAD_B)), _const_spec((1, hw)),
                  _const_spec((1, D_B)), _const_spec((1, D_B))],
        out_specs=pl.BlockSpec((tq, hw), lambda b, h, i: (b * nq + i, h)),
        out_shape=jax.ShapeDtypeStruct((t, D_B), BF16),
        scratch_shapes=[pltpu.VMEM((2 * tq, hw), BF16),
                        pltpu.VMEM((1, 2 * tq), F32),
                        pltpu.VMEM((1, 2 * tq), F32),
                        pltpu.VMEM((hw, 2 * tq), F32),
                        pltpu.VMEM((tk, tq), F32)],
        compiler_params=_params(3),
        name="diff_attn",
    )(q, k, vt, lam_vecs, sub_g, qg, kg)


def _merge_body(yf_ref, yb_ref, gate_ref, bonus_ref, ob_ref, gt_ref, x_ref, lng_ref, lnb_ref, bd_ref,
                wb0_ref, wb1_ref, wo_ref, o_ref):
    bd = bd_ref[...]
    y = yf_ref[...] + yb_ref[...]
    yc = y - _mm_split_lhs(y, bd)
    var = _mm_split_lhs(yc * yc, bd)
    yn = yc * lax.rsqrt(var + GN_EPS) * lng_ref[...] + lnb_ref[...]
    oa = (yn + bonus_ref[...]) * gate_ref[...]
    gts = gt_ref[...].astype(F32)
    merged = (gts[:, :D_MODEL] * _mm(oa, wb0_ref[...])
              + gts[:, D_MODEL:] * jnp.dot(ob_ref[...], wb1_ref[...], preferred_element_type=F32))
    o_ref[...] = x_ref[...] + _mm(merged, wo_ref[...])


def _merge(yf, yb, gate, bonus, ob, gts, x2, ln_g, ln_b, bd_mean, wb0, wb1, wo):
    t = x2.shape[0]
    tm = min(TOKEN_TILE, t)
    row = lambda width: pl.BlockSpec((tm, width), lambda i: (i, 0))
    return pl.pallas_call(
        _merge_body,
        grid=(t // tm,),
        in_specs=[row(D_A), row(D_A), row(D_A), row(D_A), row(D_B), row(2 * D_MODEL), row(D_MODEL),
                  _const_spec((1, D_A)), _const_spec((1, D_A)), _const_spec((D_A, D_A)),
                  _const_spec((D_A, D_MODEL)), _const_spec((D_B, D_MODEL)),
                  _const_spec((D_MODEL, D_MODEL))],
        out_specs=row(D_MODEL),
        out_shape=jax.ShapeDtypeStruct((t, D_MODEL), F32),
        compiler_params=_params(1),
        name="merge",
    )(yf, yb, gate, bonus, ob, gts, x2, ln_g, ln_b, bd_mean, wb0, wb1, wo)


def _block_diag2(w):
    z = jnp.zeros_like(w[0])
    return jnp.concatenate([jnp.concatenate([w[0], z], axis=1), jnp.concatenate([z, w[1]], axis=1)], axis=0)


def kernel(x, norm_ffn1, ffn1_in, ffn1_out, norm_mix, w_in, rwkv_mu, decay_w0, decay_w2, iclr_a0, iclr_a2,
           gate_g2, k_k, k_a, r_k, ln_x_g, ln_x_b, q_gain, k_gain, diff_lambda, subln_g, w_branch, w_out,
           norm_ffn2, ffn2_in, ffn2_out):
    batch, seq, d = x.shape
    depth = norm_ffn1.shape[0]
    assert seq % TOKEN_TILE == 0 and d == D_MODEL
    x2 = x.reshape(batch * seq, d)
    head_id = jnp.arange(D_A) // HEAD_A
    same = head_id[:, None] == head_id[None, :]
    bd_sum = same.astype(BF16)
    bd_mean = (same.astype(F32) / HEAD_A).astype(BF16)
    n_qk = D_B // HEAD_B
    row = lambda a: a.reshape(1, -1)
    for l in range(depth):
        lam_init = 0.8 - 0.6 * math.exp(-0.3 * l)
        x2 = _ffn(x2, row(norm_ffn1[l]), ffn1_in[l].astype(BF16), ffn1_out[l].astype(BF16))
        qg = row(jnp.tile(q_gain[l], n_qk)) * (HEAD_B ** -0.5 * LOG2E)
        kg = row(jnp.tile(k_gain[l], n_qk))
        prw, q, k, vt, gts = _proj(x2, row(norm_mix[l]), w_in[l].astype(BF16), qg, kg, bd_mean)
        r, vv, kk, lw0, k0, a0, lw1, k1, a1, gate, bonus = _prep(
            prw, seq, row(rwkv_mu[l]), row(decay_w0[l]), _block_diag2(decay_w2[l]).astype(BF16),
            row(iclr_a0[l]), _block_diag2(iclr_a2[l]).astype(BF16), gate_g2[l].astype(BF16),
            row(k_k[l]), row(k_a[l]), row(r_k[l]), bd_sum)
        yf, yb = _scan(r, vv, kk, lw0, k0, a0, lw1, k1, a1, batch, seq)
        ob = _attn(q, k, vt, diff_lambda[l], row(subln_g[l]), qg, kg, lam_init, batch, seq)
        x2 = _merge(yf, yb, gate, bonus, ob, gts, x2, row(ln_x_g[l]), row(ln_x_b[l]), bd_mean,
                    w_branch[l, 0].astype(BF16), w_branch[l, 1].astype(BF16), w_out[l].astype(BF16))
        x2 = _ffn(x2, row(norm_ffn2[l]), ffn2_in[l].astype(BF16), ffn2_out[l].astype(BF16))
    return x2.reshape(batch, seq, d)
```

```python
import functools
import math

import jax
import jax.numpy as jnp
from jax import lax
from jax.experimental import pallas as pl
from jax.experimental.pallas import tpu as pltpu

F32 = jnp.float32
BF16 = jnp.bfloat16

D_MODEL = 1024
D_A = 512
HEAD_A = 64
LORA = 64
LORA_GATE = 128
C_RWKV = 3 * D_A + 2 * LORA + 2 * LORA + LORA_GATE
D_B = 512
HEAD_B = 64
N_HEADS_B = D_B // (2 * HEAD_B)
D_FF = 2816
DECAY_SCALE = math.exp(-0.5)
GN_EPS = 64e-5
NORM_EPS = 1e-6
LOG2E = math.log2(math.e)
F32_ZERO_EXP2 = 150.0
ROUNDING_MARGIN = 1.02

LANES = 128
SUBLANES = 8
VMEM_LIMIT = 56 * 1024 * 1024

TOKEN_TILE = 512
FF_TILE = 1408
CHUNK = 64
ATTN_TQ = 256

_NT = (((1,), (1,)), ((), ()))


def _mm(a, b):
    return jnp.dot(a.astype(BF16), b.astype(BF16), preferred_element_type=F32)


def _mm_nt(a, b):
    return lax.dot_general(a.astype(BF16), b.astype(BF16), _NT, preferred_element_type=F32)


def _split_bf16(x, n):
    parts = []
    for _ in range(n - 1):
        hi = x.astype(BF16)
        parts.append(hi)
        x = x - hi.astype(F32)
    parts.append(x.astype(BF16))
    return parts


def _mm_split_lhs(x, w, n=2):
    return sum(jnp.dot(p, w, preferred_element_type=F32) for p in _split_bf16(x, n))


def _rms_norm(x, g):
    return x * lax.rsqrt(jnp.mean(x * x, axis=-1, keepdims=True) + NORM_EPS) * g


def _const_spec(shape):
    return pl.BlockSpec(shape, lambda *_: (0,) * len(shape), pipeline_mode=pl.Buffered(1))


def _params(n_axes):
    return pltpu.CompilerParams(dimension_semantics=("arbitrary",) * n_axes,
                                vmem_limit_bytes=VMEM_LIMIT)


def _ffn_body(x_ref, g_ref, win_ref, wout_ref, o_ref):
    x = x_ref[...]
    h = _rms_norm(x, g_ref[...]).astype(BF16)
    acc = jnp.zeros(x.shape, F32)
    for j in range(D_FF // FF_TILE):
        lo, hi = j * FF_TILE, (j + 1) * FF_TILE
        gate = jnp.dot(h, win_ref[:, lo:hi], preferred_element_type=F32)
        up = jnp.dot(h, win_ref[:, D_FF + lo:D_FF + hi], preferred_element_type=F32)
        act = (gate * jax.nn.sigmoid(gate) * up).astype(BF16)
        acc = acc + jnp.dot(act, wout_ref[lo:hi, :], preferred_element_type=F32)
    o_ref[...] = x + 0.5 * acc


def _ffn(x2, g, w_in, w_out):
    t = x2.shape[0]
    tm = min(TOKEN_TILE, t)
    return pl.pallas_call(
        _ffn_body,
        grid=(t // tm,),
        in_specs=[pl.BlockSpec((tm, D_MODEL), lambda i: (i, 0)),
                  _const_spec((1, D_MODEL)),
                  _const_spec((D_MODEL, 2 * D_FF)),
                  _const_spec((D_FF, D_MODEL))],
        out_specs=pl.BlockSpec((tm, D_MODEL), lambda i: (i, 0)),
        out_shape=jax.ShapeDtypeStruct((t, D_MODEL), F32),
        compiler_params=_params(1),
        name="ffn",
    )(x2, g, w_in, w_out)


_Q0 = C_RWKV
_K0 = C_RWKV + D_B
_V0 = C_RWKV + 2 * D_B
_G0 = C_RWKV + 3 * D_B


def _proj_body(x_ref, g_ref, w_ref, qg_ref, kg_ref, bd_ref, prw_ref, q_ref, k_ref, vt_ref, gt_ref):
    h = _rms_norm(x_ref[...], g_ref[...]).astype(BF16)
    prw_ref[...] = jnp.dot(h, w_ref[:, :C_RWKV], preferred_element_type=F32)
    bd = bd_ref[...]
    q = jnp.dot(h, w_ref[:, _Q0:_K0], preferred_element_type=F32)
    q_ref[...] = (q * lax.rsqrt(_mm_split_lhs(q * q, bd) + NORM_EPS) * qg_ref[...]).astype(BF16)
    k = jnp.dot(h, w_ref[:, _K0:_V0], preferred_element_type=F32)
    k_ref[...] = (k * lax.rsqrt(_mm_split_lhs(k * k, bd) + NORM_EPS) * kg_ref[...]).astype(BF16)
    vt_ref[0] = jnp.dot(h, w_ref[:, _V0:_G0], preferred_element_type=F32).T.astype(BF16)
    gt_ref[...] = jax.nn.sigmoid(jnp.dot(h, w_ref[:, _G0:], preferred_element_type=F32)).astype(BF16)


def _proj(x2, g, w, qg, kg, bd_mean):
    t = x2.shape[0]
    tm = TOKEN_TILE
    c_in = w.shape[1]
    row = lambda width: pl.BlockSpec((tm, width), lambda i: (i, 0))
    return pl.pallas_call(
        _proj_body,
        grid=(t // tm,),
        in_specs=[row(D_MODEL), _const_spec((1, D_MODEL)), _const_spec((D_MODEL, c_in)),
                  _const_spec((1, D_B)), _const_spec((1, D_B)), _const_spec((D_B, D_B))],
        out_specs=[row(C_RWKV), row(D_B), row(D_B),
                   pl.BlockSpec((1, D_B, tm), lambda i: (i, 0, 0)), row(2 * D_MODEL)],
        out_shape=[jax.ShapeDtypeStruct((t, C_RWKV), F32),
                   jax.ShapeDtypeStruct((t, D_B), BF16),
                   jax.ShapeDtypeStruct((t, D_B), BF16),
                   jax.ShapeDtypeStruct((t // tm, D_B, tm), BF16),
                   jax.ShapeDtypeStruct((t, 2 * D_MODEL), BF16)],
        compiler_params=_params(1),
        name="proj",
    )(x2, g, w, qg, kg, bd_mean)


def _prep_body(p_ref, pp_ref, pn_ref, mu_ref, w0_ref, w2_ref, a0_ref, a2_ref, g2_ref, kk_ref, ka_ref,
               rk_ref, bd_ref,
               r_out, v_out, kk_out, lw0_out, k0_out, a0_out, lw1_out, k1_out, a1_out, gate_out,
               bonus_out, *, seq):
    i = pl.program_id(0)
    p = p_ref[...]
    tm = p.shape[0]
    row = lax.broadcasted_iota(jnp.int32, p.shape, 0)
    keep_prev = jnp.where((i * tm) % seq == 0, 0.0, 1.0)
    keep_next = jnp.where(((i + 1) * tm) % seq == 0, 0.0, 1.0)
    prev = jnp.where(row == 0, keep_prev * pp_ref[SUBLANES - 1:SUBLANES, :], pltpu.roll(p, 1, 0))
    nxt = jnp.where(row == tm - 1, keep_next * pn_ref[0:1, :], pltpu.roll(p, tm - 1, 0))
    p = p + mu_ref[...] * (0.5 * (prev + nxt) - p)

    r = p[:, :D_A]
    k = p[:, D_A:2 * D_A]
    v = p[:, 2 * D_A:3 * D_A]
    dw = p[:, 3 * D_A:3 * D_A + 2 * LORA]
    da = p[:, 3 * D_A + 2 * LORA:3 * D_A + 4 * LORA]
    dg = p[:, 3 * D_A + 4 * LORA:]
    bd = bd_ref[...]

    logw = -DECAY_SCALE * jax.nn.sigmoid(w0_ref[...] + _mm(jnp.tanh(dw), w2_ref[...]))
    iclr = jax.nn.sigmoid(a0_ref[...] + _mm(da, a2_ref[...]))
    gate_out[...] = _mm(jax.nn.sigmoid(dg), g2_ref[...])

    kk = k * kk_ref[...]
    norm = jnp.sqrt(_mm_split_lhs(kk * kk, bd))
    kk_out[...] = kk / jnp.maximum(norm, 1e-12)

    ka = ka_ref[...]
    a_f = iclr[:, :D_A]
    a_b = iclr[:, D_A:]
    k_f = k * (1.0 + (a_f - 1.0) * ka)
    k_b = k * (1.0 + (a_b - 1.0) * ka)
    bonus = _mm_split_lhs(r * rk_ref[...] * (k_f + k_b), bd)
    r_out[...] = r
    v_out[...] = v
    lw0_out[...] = logw[:, :D_A]
    k0_out[...] = k_f
    a0_out[...] = a_f
    lw1_out[...] = logw[:, D_A:]
    k1_out[...] = k_b
    a1_out[...] = a_b
    bonus_out[...] = bonus * v


def _prep(prw, seq, mu, w0, w2cat, a0, a2cat, g2, k_k, k_a, r_k, bd_sum):
    t = prw.shape[0]
    tm = min(TOKEN_TILE, seq)
    nb = tm // SUBLANES
    last = t // SUBLANES - 1
    row = lambda width: pl.BlockSpec((tm, width), lambda i: (i, 0))
    out = jax.ShapeDtypeStruct((t, D_A), F32)
    return pl.pallas_call(
        functools.partial(_prep_body, seq=seq),
        grid=(t // tm,),
        in_specs=[row(C_RWKV),
                  pl.BlockSpec((SUBLANES, C_RWKV), lambda i: (jnp.maximum(i * nb - 1, 0), 0)),
                  pl.BlockSpec((SUBLANES, C_RWKV), lambda i: (jnp.minimum((i + 1) * nb, last), 0)),
                  _const_spec((1, C_RWKV)),
                  _const_spec((1, 2 * D_A)), _const_spec((2 * LORA, 2 * D_A)),
                  _const_spec((1, 2 * D_A)), _const_spec((2 * LORA, 2 * D_A)),
                  _const_spec((LORA_GATE, D_A)),
                  _const_spec((1, D_A)), _const_spec((1, D_A)), _const_spec((1, D_A)),
                  _const_spec((D_A, D_A))],
        out_specs=[row(D_A)] * 11,
        out_shape=[out] * 11,
        compiler_params=_params(1),
        name="rwkv_prep",
    )(prw, prw, prw, mu, w0, w2cat, a0, a2cat, g2, k_k, k_a, r_k, bd_sum)


def _scan_body(rf, vf, kkf, lwf, kf, af, rb, vb, kkb, lwb, kb, ab, yf_ref, yb_ref, h_ref):
    @pl.when(pl.program_id(1) == 0)
    def _():
        h_ref[...] = jnp.zeros(h_ref.shape, F32)

    L = rf.shape[0]
    n_pairs = D_A // LANES
    ti = lax.broadcasted_iota(jnp.int32, (L, L), 0)
    tj = lax.broadcasted_iota(jnp.int32, (L, L), 1)
    pi = lax.broadcasted_iota(jnp.int32, (L, LANES), 0)
    lane = lax.broadcasted_iota(jnp.int32, (L, LANES), 1)
    pj = lane & (HEAD_A - 1)
    lane_lo = lane < HEAD_A
    eye = jnp.where(pj == pi, 1.0, 0.0)
    blk = lambda s: (pi // s) == (pj // s)
    row2 = lax.broadcasted_iota(jnp.int32, (2 * L, LANES), 0)
    col2 = lax.broadcasted_iota(jnp.int32, (2 * L, LANES), 1)
    same_head = (row2 // HEAD_A) == (col2 // HEAD_A)
    diag2 = row2 == col2
    zeros_l = jnp.zeros((L, LANES), F32)
    zeros_2l = jnp.zeros((2 * L, LANES), F32)

    def stack(x):
        return jnp.concatenate([jnp.where(lane_lo, x, 0.0), jnp.where(lane_lo, 0.0, x)], axis=0)

    def mmp(x, y):
        return _mm(x, stack(y))

    cat = jnp.concatenate
    chains = []
    for reverse, (r_, v_, kk_, lw_, k_, a_) in ((False, (rf, vf, kkf, lwf, kf, af)),
                                               (True, (rb, vb, kkb, lwb, kb, ab))):
        r, v, kk, lw, k, a = r_[...], v_[...], kk_[...], lw_[...], k_[...], a_[...]
        tri = ((tj >= ti) if reverse else (tj <= ti)).astype(BF16)
        G = sum(jnp.dot(tri, part, preferred_element_type=F32) for part in _split_bf16(lw, 3))
        g_end = G[0:1] if reverse else G[L - 1:L]
        gam_end = jnp.exp(g_end)
        inv_gam = jnp.exp(-G)
        rel_end = jnp.exp(g_end - G)
        b = kk * a
        full = dict(At=-kk * jnp.exp(G - lw), Bt=b * inv_gam, Kt=k * inv_gam, Rt=r * jnp.exp(G),
                    Kh=k * rel_end, Bh=b * rel_end, v=v)
        strict = (pj > pi) if reverse else (pj < pi)
        incl = (pj >= pi) if reverse else (pj <= pi)
        for p in range(n_pairs):
            sl = slice(p * LANES, (p + 1) * LANES)
            c = {name: val[:, sl] for name, val in full.items()}
            c.update(strict=strict, incl=incl, gam_end=gam_end[:, sl])
            chains.append(c)

    for c in chains:
        a4 = _mm_nt(cat([c["At"], c["Rt"]], axis=0), cat([stack(c["Bt"]), stack(c["Kt"])], axis=0))
        c["A_ab"] = jnp.where(c["strict"], a4[:L, :LANES], 0.0)
        c["A_ak"] = jnp.where(c["strict"], a4[:L, LANES:], 0.0)
        c["A_rb"] = jnp.where(c["incl"], a4[L:, :LANES], 0.0)
        c["A_rk"] = jnp.where(c["incl"], a4[L:, LANES:], 0.0)
    for c in chains:
        c["A0"] = jnp.where(blk(8), c["A_ab"], 0.0)
        c["A2"] = mmp(c["A0"], c["A0"])
    for c in chains:
        c["A4"] = mmp(c["A2"], c["A2"])
        c["T"] = eye + c["A0"]
    for c in chains:
        c["T"] = c["T"] + mmp(c["T"], c["A2"])
    for c in chains:
        c["T"] = c["T"] + mmp(c["T"], c["A4"])
    s = 8
    while s < L:
        outer = blk(2 * s) & jnp.logical_not(blk(s))
        for c in chains:
            c["TO"] = mmp(c["T"], jnp.where(outer, c["A_ab"], 0.0))
        for c in chains:
            c["T"] = c["T"] + mmp(c["TO"], c["T"])
        s *= 2
    for c in chains:
        c["AkV"] = mmp(c["A_ak"], c["v"])
    for c in chains:
        wu = _mm(c["T"], cat([stack(c["At"]), stack(c["AkV"])], axis=1))
        c["W"], c["U0"] = wu[:, :LANES], wu[:, LANES:]
    for c in chains:
        kb_t = cat([c["Kh"], c["Bh"]], axis=0).T
        mn = _mm(kb_t, cat([cat([zeros_l, c["v"]], axis=1), cat([c["W"], c["U0"]], axis=1)], axis=0))
        c["M"] = jnp.where(same_head, mn[:, :LANES], 0.0) + jnp.where(diag2, c["gam_end"], 0.0)
        c["N"] = jnp.where(same_head, mn[:, LANES:], 0.0)
    for c in chains:
        qy = _mm(cat([c["A_rb"], c["A_rk"]], axis=1),
                 cat([cat([stack(c["W"]), stack(c["U0"])], axis=1),
                      cat([zeros_2l, stack(c["v"])], axis=1)], axis=0))
        c["Q"] = c["Rt"] + qy[:, :LANES]
        c["Y0"] = qy[:, LANES:]
    ys = []
    for slot, c in enumerate(chains):
        H = h_ref[slot]
        ys.append(_mm(c["Q"], H) + c["Y0"])
        h_ref[slot] = _mm(c["M"], H) + c["N"]
    yf_ref[...] = cat(ys[:n_pairs], axis=1)
    yb_ref[...] = cat(ys[n_pairs:], axis=1)


def _scan(r, v, kk, lw0, k0, a0, lw1, k1, a1, batch, seq):
    t = r.shape[0]
    nc = seq // CHUNK
    fwd = pl.BlockSpec((CHUNK, D_A), lambda b, c: (b * nc + c, 0))
    bwd = pl.BlockSpec((CHUNK, D_A), lambda b, c: (b * nc + nc - 1 - c, 0))
    out = jax.ShapeDtypeStruct((t, D_A), F32)
    return pl.pallas_call(
        _scan_body,
        grid=(batch, nc),
        in_specs=[fwd] * 6 + [bwd] * 6,
        out_specs=[fwd, bwd],
        out_shape=[out, out],
        scratch_shapes=[pltpu.VMEM((2 * (D_A // LANES), LANES, LANES), F32)],
        compiler_params=_params(2),
        name="rwkv_scan",
    )(r, v, kk, lw0, k0, a0, r, v, kk, lw1, k1, a1)


def _attn_body(q_ref, k_ref, vt_ref, lv_ref, sg_ref, qg_ref, kg_ref, o_ref, qs_ref, m_ref, l_ref, acc_ref,
               src_ref, *, lam_init):
    tq = src_ref.shape[1]
    n_kv, _, tk = vt_ref.shape
    seq = n_kv * tk
    h = pl.program_id(1)
    i = pl.program_id(2)
    slope_bits = (127 - 2 * (h + 1)) << 23
    slope = (lax.bitcast_convert_type(jnp.full((1, 1), slope_bits, jnp.int32), jnp.float32).astype(F32)
             * LOG2E)
    bound = (HEAD_B * ROUNDING_MARGIN * jnp.max(jnp.abs(qg_ref[...]), axis=-1, keepdims=True)
             * jnp.max(jnp.abs(kg_ref[...]), axis=-1, keepdims=True))
    reach = jnp.minimum((F32_ZERO_EXP2 + 2.0 * bound) / slope + 1.0, float(seq)).astype(jnp.int32)[0, 0]

    @pl.when(i == 0)
    def _():
        kv_pos = lax.broadcasted_iota(jnp.int32, (tk, tq), 0)
        q_pos = lax.broadcasted_iota(jnp.int32, (tk, tq), 1)
        src_ref[...] = slope * (q_pos - kv_pos).astype(F32)

    n_st = tk // tq
    lane = lax.broadcasted_iota(jnp.int32, (tq, 2 * HEAD_B), 1)
    for st in range(n_st):
        q = q_ref[st * tq:(st + 1) * tq, :]
        zero = jnp.zeros_like(q)
        qs_ref[st, :tq, :] = jnp.where(lane < HEAD_B, q, zero)
        qs_ref[st, tq:, :] = jnp.where(lane < HEAD_B, zero, q)
    m_ref[...] = jnp.full(m_ref.shape, -1e30, F32)
    l_ref[...] = jnp.zeros(l_ref.shape, F32)
    acc_ref[...] = jnp.zeros(acc_ref.shape, F32)
    q0 = i * tk

    def step(j, side):
        k0 = j * tk
        kb = k_ref[pl.ds(pl.multiple_of(k0, tk), tk), :]
        vtb = vt_ref[j]
        src = src_ref[...]
        src2 = jnp.concatenate([src, src], axis=1)
        scores = [lax.dot_general(kb, qs_ref[st], _NT, preferred_element_type=F32)
                  for st in range(n_st)]
        probs = []
        for st in range(n_st):
            dist0 = (q0 + st * tq - k0).astype(F32)
            if side == 0:
                bias = jnp.abs(src + slope * dist0)
                s = scores[st] - jnp.concatenate([bias, bias], axis=1)
                cst = jnp.zeros((1, 1), F32)
            elif side < 0:
                s = scores[st] - src2
                cst = -slope * dist0
            else:
                s = scores[st] + src2
                cst = slope * dist0
            m_old = m_ref[st]
            m_new = jnp.maximum(m_old, jnp.max(s, axis=0, keepdims=True) + cst)
            alpha = jnp.exp2(m_old - m_new)
            p = jnp.exp2(s - (m_new - cst))
            l_ref[st] = alpha * l_ref[st] + jnp.sum(p, axis=0, keepdims=True)
            m_ref[st] = m_new
            probs.append((alpha, p.astype(BF16)))
        for st, (alpha, p) in enumerate(probs):
            acc_ref[st] = alpha * acc_ref[st] + jnp.dot(vtb, p, preferred_element_type=F32)

    j_lo = jnp.maximum(q0 - reach, 0) // tk
    j_hi = jnp.minimum(q0 + tk - 1 + reach, seq - 1) // tk + 1
    lax.fori_loop(j_lo, i, lambda j, c: (step(j, -1), c)[1], 0)
    step(i, 0)
    lax.fori_loop(i + 1, j_hi, lambda j, c: (step(j, 1), c)[1], 0)

    lv = lv_ref[...]
    lam = (jnp.exp(jnp.sum(lv[0:1] * lv[1:2], axis=-1, keepdims=True))
           - jnp.exp(jnp.sum(lv[2:3] * lv[3:4], axis=-1, keepdims=True)) + lam_init)
    for st in range(n_st):
        acc = acc_ref[st]
        l = l_ref[st]
        o_t = acc[:, :tq] / l[:, :tq] - lam * (acc[:, tq:] / l[:, tq:])
        o_t = o_t * lax.rsqrt(jnp.mean(o_t * o_t, axis=0, keepdims=True) + NORM_EPS)
        o_ref[st * tq:(st + 1) * tq, :] = (o_t.T * sg_ref[...] * (1.0 - lam_init)).astype(o_ref.dtype)


def _attn(q, k, vt, lam_vecs, sub_g, qg, kg, lam_init, batch, seq):
    t = q.shape[0]
    tk = vt.shape[2]
    tq = ATTN_TQ
    n_st = tk // tq
    nq = seq // tk
    n_kv = seq // tk
    hw = 2 * HEAD_B
    return pl.pallas_call(
        functools.partial(_attn_body, lam_init=lam_init),
        grid=(batch, N_HEADS_B, nq),
        in_specs=[pl.BlockSpec((tk, hw), lambda b, h, i: (b * nq + i, h)),
                  pl.BlockSpec((seq, hw), lambda b, h, i: (b, h)),
                  pl.BlockSpec((n_kv, hw, tk), lambda b, h, i: (b, h, 0)),
                  _const_spec((4, HEAD_B)), _const_spec((1, hw)),
                  _const_spec((1, D_B)), _const_spec((1, D_B))],
        out_specs=pl.BlockSpec((tk, hw), lambda b, h, i: (b * nq + i, h)),
        out_shape=jax.ShapeDtypeStruct((t, D_B), BF16),
        scratch_shapes=[pltpu.VMEM((n_st, 2 * tq, hw), BF16),
                        pltpu.VMEM((n_st, 1, 2 * tq), F32),
                        pltpu.VMEM((n_st, 1, 2 * tq), F32),
                        pltpu.VMEM((n_st, hw, 2 * tq), F32),
                        pltpu.VMEM((tk, tq), F32)],
        compiler_params=_params(3),
        name="diff_attn",
    )(q, k, vt, lam_vecs, sub_g, qg, kg)


def _merge_body(yf_ref, yb_ref, gate_ref, bonus_ref, ob_ref, gt_ref, x_ref, lng_ref, lnb_ref, bd_ref,
                wb0_ref, wb1_ref, wo_ref, o_ref):
    bd = bd_ref[...]
    y = yf_ref[...] + yb_ref[...]
    yc = y - _mm_split_lhs(y, bd)
    var = _mm_split_lhs(yc * yc, bd)
    yn = yc * lax.rsqrt(var + GN_EPS) * lng_ref[...] + lnb_ref[...]
    oa = (yn + bonus_ref[...]) * gate_ref[...]
    gts = gt_ref[...].astype(F32)
    merged = (gts[:, :D_MODEL] * _mm(oa, wb0_ref[...])
              + gts[:, D_MODEL:] * jnp.dot(ob_ref[...], wb1_ref[...], preferred_element_type=F32))
    o_ref[...] = x_ref[...] + _mm(merged, wo_ref[...])


def _merge(yf, yb, gate, bonus, ob, gts, x2, ln_g, ln_b, bd_mean, wb0, wb1, wo):
    t = x2.shape[0]
    tm = min(TOKEN_TILE, t)
    row = lambda width: pl.BlockSpec((tm, width), lambda i: (i, 0))
    return pl.pallas_call(
        _merge_body,
        grid=(t // tm,),
        in_specs=[row(D_A), row(D_A), row(D_A), row(D_A), row(D_B), row(2 * D_MODEL), row(D_MODEL),
                  _const_spec((1, D_A)), _const_spec((1, D_A)), _const_spec((D_A, D_A)),
                  _const_spec((D_A, D_MODEL)), _const_spec((D_B, D_MODEL)),
                  _const_spec((D_MODEL, D_MODEL))],
        out_specs=row(D_MODEL),
        out_shape=jax.ShapeDtypeStruct((t, D_MODEL), F32),
        compiler_params=_params(1),
        name="merge",
    )(yf, yb, gate, bonus, ob, gts, x2, ln_g, ln_b, bd_mean, wb0, wb1, wo)


def _block_diag2(w):
    z = jnp.zeros_like(w[0])
    return jnp.concatenate([jnp.concatenate([w[0], z], axis=1), jnp.concatenate([z, w[1]], axis=1)], axis=0)


def kernel(x, norm_ffn1, ffn1_in, ffn1_out, norm_mix, w_in, rwkv_mu, decay_w0, decay_w2, iclr_a0, iclr_a2,
           gate_g2, k_k, k_a, r_k, ln_x_g, ln_x_b, q_gain, k_gain, diff_lambda, subln_g, w_branch, w_out,
           norm_ffn2, ffn2_in, ffn2_out):
    batch, seq, d = x.shape
    depth = norm_ffn1.shape[0]
    assert seq % TOKEN_TILE == 0 and d == D_MODEL
    x2 = x.reshape(batch * seq, d)
    head_id = jnp.arange(D_A) // HEAD_A
    same = head_id[:, None] == head_id[None, :]
    bd_sum = same.astype(BF16)
    bd_mean = (same.astype(F32) / HEAD_A).astype(BF16)
    n_qk = D_B // HEAD_B
    row = lambda a: a.reshape(1, -1)
    for l in range(depth):
        lam_init = 0.8 - 0.6 * math.exp(-0.3 * l)
        x2 = _ffn(x2, row(norm_ffn1[l]), ffn1_in[l].astype(BF16), ffn1_out[l].astype(BF16))
        qg = row(jnp.tile(q_gain[l], n_qk)) * (HEAD_B ** -0.5 * LOG2E)
        kg = row(jnp.tile(k_gain[l], n_qk))
        prw, q, k, vt, gts = _proj(x2, row(norm_mix[l]), w_in[l].astype(BF16), qg, kg, bd_mean)
        r, vv, kk, lw0, k0, a0, lw1, k1, a1, gate, bonus = _prep(
            prw, seq, row(rwkv_mu[l]), row(decay_w0[l]), _block_diag2(decay_w2[l]).astype(BF16),
            row(iclr_a0[l]), _block_diag2(iclr_a2[l]).astype(BF16), gate_g2[l].astype(BF16),
            row(k_k[l]), row(k_a[l]), row(r_k[l]), bd_sum)
        yf, yb = _scan(r, vv, kk, lw0, k0, a0, lw1, k1, a1, batch, seq)
        ob = _attn(q, k, vt, diff_lambda[l], row(subln_g[l]), qg, kg, lam_init, batch, seq)
        x2 = _merge(yf, yb, gate, bonus, ob, gts, x2, row(ln_x_g[l]), row(ln_x_b[l]), bd_mean,
                    w_branch[l, 0].astype(BF16), w_branch[l, 1].astype(BF16), w_out[l].astype(BF16))
        x2 = _ffn(x2, row(norm_ffn2[l]), ffn2_in[l].astype(BF16), ffn2_out[l].astype(BF16))
    return x2.reshape(batch, seq, d)
```

```python
import functools
import math

import jax
import jax.numpy as jnp
from jax import lax
from jax.experimental import pallas as pl
from jax.experimental.pallas import tpu as pltpu

F32 = jnp.float32
BF16 = jnp.bfloat16

D_MODEL = 1024
D_A = 512
HEAD_A = 64
LORA = 64
LORA_GATE = 128
C_RWKV = 3 * D_A + 2 * LORA + 2 * LORA + LORA_GATE
D_B = 512
HEAD_B = 64
N_HEADS_B = D_B // (2 * HEAD_B)
D_FF = 2816
DECAY_SCALE = math.exp(-0.5)
GN_EPS = 64e-5
NORM_EPS = 1e-6
LOG2E = math.log2(math.e)
F32_ZERO_EXP2 = 150.0
ROUNDING_MARGIN = 1.02

LANES = 128
SUBLANES = 8
VMEM_LIMIT = 56 * 1024 * 1024

TOKEN_TILE = 512
FF_TILE = 1408
CHUNK = 64
ATTN_TQ = 256

_NT = (((1,), (1,)), ((), ()))


def _mm(a, b):
    return jnp.dot(a.astype(BF16), b.astype(BF16), preferred_element_type=F32)


def _mm_nt(a, b):
    return lax.dot_general(a.astype(BF16), b.astype(BF16), _NT, preferred_element_type=F32)


def _split_bf16(x, n):
    parts = []
    for _ in range(n - 1):
        hi = x.astype(BF16)
        parts.append(hi)
        x = x - hi.astype(F32)
    parts.append(x.astype(BF16))
    return parts


def _mm_split_lhs(x, w, n=2):
    return sum(jnp.dot(p, w, preferred_element_type=F32) for p in _split_bf16(x, n))


def _rms_norm(x, g):
    return x * lax.rsqrt(jnp.mean(x * x, axis=-1, keepdims=True) + NORM_EPS) * g


def _const_spec(shape):
    return pl.BlockSpec(shape, lambda *_: (0,) * len(shape), pipeline_mode=pl.Buffered(1))


def _params(n_axes):
    return pltpu.CompilerParams(dimension_semantics=("arbitrary",) * n_axes,
                                vmem_limit_bytes=VMEM_LIMIT)


def _ffn_body(x_ref, g_ref, win_ref, wout_ref, o_ref):
    x = x_ref[...]
    h = _rms_norm(x, g_ref[...]).astype(BF16)
    acc = jnp.zeros(x.shape, F32)
    for j in range(D_FF // FF_TILE):
        lo, hi = j * FF_TILE, (j + 1) * FF_TILE
        gate = jnp.dot(h, win_ref[:, lo:hi], preferred_element_type=F32)
        up = jnp.dot(h, win_ref[:, D_FF + lo:D_FF + hi], preferred_element_type=F32)
        act = (gate * jax.nn.sigmoid(gate) * up).astype(BF16)
        acc = acc + jnp.dot(act, wout_ref[lo:hi, :], preferred_element_type=F32)
    o_ref[...] = x + 0.5 * acc


def _ffn(x2, g, w_in, w_out):
    t = x2.shape[0]
    tm = min(TOKEN_TILE, t)
    return pl.pallas_call(
        _ffn_body,
        grid=(t // tm,),
        in_specs=[pl.BlockSpec((tm, D_MODEL), lambda i: (i, 0)),
                  _const_spec((1, D_MODEL)),
                  _const_spec((D_MODEL, 2 * D_FF)),
                  _const_spec((D_FF, D_MODEL))],
        out_specs=pl.BlockSpec((tm, D_MODEL), lambda i: (i, 0)),
        out_shape=jax.ShapeDtypeStruct((t, D_MODEL), F32),
        compiler_params=_params(1),
        name="ffn",
    )(x2, g, w_in, w_out)


_Q0 = C_RWKV
_K0 = C_RWKV + D_B
_V0 = C_RWKV + 2 * D_B
_G0 = C_RWKV + 3 * D_B


def _proj_body(x_ref, g_ref, w_ref, qg_ref, kg_ref, bd_ref, prw_ref, q_ref, k_ref, vt_ref, gt_ref):
    h = _rms_norm(x_ref[...], g_ref[...]).astype(BF16)
    prw_ref[...] = jnp.dot(h, w_ref[:, :C_RWKV], preferred_element_type=F32)
    bd = bd_ref[...]
    q = jnp.dot(h, w_ref[:, _Q0:_K0], preferred_element_type=F32)
    q_ref[...] = (q * lax.rsqrt(_mm_split_lhs(q * q, bd) + NORM_EPS) * qg_ref[...]).astype(BF16)
    k = jnp.dot(h, w_ref[:, _K0:_V0], preferred_element_type=F32)
    k_ref[...] = (k * lax.rsqrt(_mm_split_lhs(k * k, bd) + NORM_EPS) * kg_ref[...]).astype(BF16)
    vt_ref[0] = jnp.dot(h, w_ref[:, _V0:_G0], preferred_element_type=F32).T.astype(BF16)
    gt_ref[...] = jax.nn.sigmoid(jnp.dot(h, w_ref[:, _G0:], preferred_element_type=F32)).astype(BF16)


def _proj(x2, g, w, qg, kg, bd_mean):
    t = x2.shape[0]
    tm = TOKEN_TILE
    c_in = w.shape[1]
    row = lambda width: pl.BlockSpec((tm, width), lambda i: (i, 0))
    return pl.pallas_call(
        _proj_body,
        grid=(t // tm,),
        in_specs=[row(D_MODEL), _const_spec((1, D_MODEL)), _const_spec((D_MODEL, c_in)),
                  _const_spec((1, D_B)), _const_spec((1, D_B)), _const_spec((D_B, D_B))],
        out_specs=[row(C_RWKV), row(D_B), row(D_B),
                   pl.BlockSpec((1, D_B, tm), lambda i: (i, 0, 0)), row(2 * D_MODEL)],
        out_shape=[jax.ShapeDtypeStruct((t, C_RWKV), F32),
                   jax.ShapeDtypeStruct((t, D_B), BF16),
                   jax.ShapeDtypeStruct((t, D_B), BF16),
                   jax.ShapeDtypeStruct((t // tm, D_B, tm), BF16),
                   jax.ShapeDtypeStruct((t, 2 * D_MODEL), BF16)],
        compiler_params=_params(1),
        name="proj",
    )(x2, g, w, qg, kg, bd_mean)


def _prep_body(p_ref, pp_ref, pn_ref, mu_ref, w0_ref, w2_ref, a0_ref, a2_ref, g2_ref, kk_ref, ka_ref,
               rk_ref, bd_ref,
               r_out, v_out, kk_out, lw0_out, k0_out, a0_out, lw1_out, k1_out, a1_out, gate_out,
               bonus_out, *, seq):
    i = pl.program_id(0)
    p = p_ref[...]
    tm = p.shape[0]
    row = lax.broadcasted_iota(jnp.int32, p.shape, 0)
    keep_prev = jnp.where((i * tm) % seq == 0, 0.0, 1.0)
    keep_next = jnp.where(((i + 1) * tm) % seq == 0, 0.0, 1.0)
    prev = jnp.where(row == 0, keep_prev * pp_ref[SUBLANES - 1:SUBLANES, :], pltpu.roll(p, 1, 0))
    nxt = jnp.where(row == tm - 1, keep_next * pn_ref[0:1, :], pltpu.roll(p, tm - 1, 0))
    p = p + mu_ref[...] * (0.5 * (prev + nxt) - p)

    r = p[:, :D_A]
    k = p[:, D_A:2 * D_A]
    v = p[:, 2 * D_A:3 * D_A]
    dw = p[:, 3 * D_A:3 * D_A + 2 * LORA]
    da = p[:, 3 * D_A + 2 * LORA:3 * D_A + 4 * LORA]
    dg = p[:, 3 * D_A + 4 * LORA:]
    bd = bd_ref[...]

    logw = -DECAY_SCALE * jax.nn.sigmoid(w0_ref[...] + _mm(jnp.tanh(dw), w2_ref[...]))
    iclr = jax.nn.sigmoid(a0_ref[...] + _mm(da, a2_ref[...]))
    gate_out[...] = _mm(jax.nn.sigmoid(dg), g2_ref[...])

    kk = k * kk_ref[...]
    norm = jnp.sqrt(_mm_split_lhs(kk * kk, bd))
    kk_out[...] = kk / jnp.maximum(norm, 1e-12)

    ka = ka_ref[...]
    a_f = iclr[:, :D_A]
    a_b = iclr[:, D_A:]
    k_f = k * (1.0 + (a_f - 1.0) * ka)
    k_b = k * (1.0 + (a_b - 1.0) * ka)
    bonus = _mm_split_lhs(r * rk_ref[...] * (k_f + k_b), bd)
    r_out[...] = r
    v_out[...] = v
    lw0_out[...] = logw[:, :D_A]
    k0_out[...] = k_f
    a0_out[...] = a_f
    lw1_out[...] = logw[:, D_A:]
    k1_out[...] = k_b
    a1_out[...] = a_b
    bonus_out[...] = bonus * v


def _prep(prw, seq, mu, w0, w2cat, a0, a2cat, g2, k_k, k_a, r_k, bd_sum):
    t = prw.shape[0]
    tm = min(TOKEN_TILE, seq)
    nb = tm // SUBLANES
    last = t // SUBLANES - 1
    row = lambda width: pl.BlockSpec((tm, width), lambda i: (i, 0))
    out = jax.ShapeDtypeStruct((t, D_A), F32)
    return pl.pallas_call(
        functools.partial(_prep_body, seq=seq),
        grid=(t // tm,),
        in_specs=[row(C_RWKV),
                  pl.BlockSpec((SUBLANES, C_RWKV), lambda i: (jnp.maximum(i * nb - 1, 0), 0)),
                  pl.BlockSpec((SUBLANES, C_RWKV), lambda i: (jnp.minimum((i + 1) * nb, last), 0)),
                  _const_spec((1, C_RWKV)),
                  _const_spec((1, 2 * D_A)), _const_spec((2 * LORA, 2 * D_A)),
                  _const_spec((1, 2 * D_A)), _const_spec((2 * LORA, 2 * D_A)),
                  _const_spec((LORA_GATE, D_A)),
                  _const_spec((1, D_A)), _const_spec((1, D_A)), _const_spec((1, D_A)),
                  _const_spec((D_A, D_A))],
        out_specs=[row(D_A)] * 11,
        out_shape=[out] * 11,
        compiler_params=_params(1),
        name="rwkv_prep",
    )(prw, prw, prw, mu, w0, w2cat, a0, a2cat, g2, k_k, k_a, r_k, bd_sum)


def _scan_body(rf, vf, kkf, lwf, kf, af, rb, vb, kkb, lwb, kb, ab, yf_ref, yb_ref, h_ref):
    @pl.when(pl.program_id(1) == 0)
    def _():
        h_ref[...] = jnp.zeros(h_ref.shape, F32)

    L = rf.shape[0]
    n_pairs = D_A // LANES
    ti = lax.broadcasted_iota(jnp.int32, (L, L), 0)
    tj = lax.broadcasted_iota(jnp.int32, (L, L), 1)
    pi = lax.broadcasted_iota(jnp.int32, (L, LANES), 0)
    lane = lax.broadcasted_iota(jnp.int32, (L, LANES), 1)
    pj = lane & (HEAD_A - 1)
    lane_lo = lane < HEAD_A
    eye = jnp.where(pj == pi, 1.0, 0.0)
    blk = lambda s: (pi // s) == (pj // s)
    row2 = lax.broadcasted_iota(jnp.int32, (2 * L, LANES), 0)
    col2 = lax.broadcasted_iota(jnp.int32, (2 * L, LANES), 1)
    same_head = (row2 // HEAD_A) == (col2 // HEAD_A)
    diag2 = row2 == col2
    zeros_l = jnp.zeros((L, LANES), F32)
    zeros_2l = jnp.zeros((2 * L, LANES), F32)

    def stack(x):
        return jnp.concatenate([jnp.where(lane_lo, x, 0.0), jnp.where(lane_lo, 0.0, x)], axis=0)

    def mmp(x, y):
        return _mm(x, stack(y))

    cat = jnp.concatenate
    chains = []
    for reverse, (r_, v_, kk_, lw_, k_, a_) in ((False, (rf, vf, kkf, lwf, kf, af)),
                                               (True, (rb, vb, kkb, lwb, kb, ab))):
        r, v, kk, lw, k, a = r_[...], v_[...], kk_[...], lw_[...], k_[...], a_[...]
        tri = ((tj >= ti) if reverse else (tj <= ti)).astype(BF16)
        G = sum(jnp.dot(tri, part, preferred_element_type=F32) for part in _split_bf16(lw, 3))
        g_end = G[0:1] if reverse else G[L - 1:L]
        gam_end = jnp.exp(g_end)
        inv_gam = jnp.exp(-G)
        rel_end = jnp.exp(g_end - G)
        b = kk * a
        full = dict(At=-kk * jnp.exp(G - lw), Bt=b * inv_gam, Kt=k * inv_gam, Rt=r * jnp.exp(G),
                    Kh=k * rel_end, Bh=b * rel_end, v=v)
        strict = (pj > pi) if reverse else (pj < pi)
        incl = (pj >= pi) if reverse else (pj <= pi)
        for p in range(n_pairs):
            sl = slice(p * LANES, (p + 1) * LANES)
            c = {name: val[:, sl] for name, val in full.items()}
            c.update(strict=strict, incl=incl, gam_end=gam_end[:, sl])
            chains.append(c)

    for c in chains:
        a4 = _mm_nt(cat([c["At"], c["Rt"]], axis=0), cat([stack(c["Bt"]), stack(c["Kt"])], axis=0))
        c["A_ab"] = jnp.where(c["strict"], a4[:L, :LANES], 0.0)
        c["A_ak"] = jnp.where(c["strict"], a4[:L, LANES:], 0.0)
        c["A_rb"] = jnp.where(c["incl"], a4[L:, :LANES], 0.0)
        c["A_rk"] = jnp.where(c["incl"], a4[L:, LANES:], 0.0)
    for c in chains:
        c["A0"] = jnp.where(blk(8), c["A_ab"], 0.0)
        c["A2"] = mmp(c["A0"], c["A0"])
    for c in chains:
        c["A4"] = mmp(c["A2"], c["A2"])
        c["T"] = eye + c["A0"]
    for c in chains:
        c["T"] = c["T"] + mmp(c["T"], c["A2"])
    for c in chains:
        c["T"] = c["T"] + mmp(c["T"], c["A4"])
    s = 8
    while s < L:
        outer = blk(2 * s) & jnp.logical_not(blk(s))
        for c in chains:
            c["TO"] = mmp(c["T"], jnp.where(outer, c["A_ab"], 0.0))
        for c in chains:
            c["T"] = c["T"] + mmp(c["TO"], c["T"])
        s *= 2
    for c in chains:
        c["AkV"] = mmp(c["A_ak"], c["v"])
    for c in chains:
        wu = _mm(c["T"], cat([stack(c["At"]), stack(c["AkV"])], axis=1))
        c["W"], c["U0"] = wu[:, :LANES], wu[:, LANES:]
    for c in chains:
        kb_t = cat([c["Kh"], c["Bh"]], axis=0).T
        mn = _mm(kb_t, cat([cat([zeros_l, c["v"]], axis=1), cat([c["W"], c["U0"]], axis=1)], axis=0))
        c["M"] = jnp.where(same_head, mn[:, :LANES], 0.0) + jnp.where(diag2, c["gam_end"], 0.0)
        c["N"] = jnp.where(same_head, mn[:, LANES:], 0.0)
    for c in chains:
        qy = _mm(cat([c["A_rb"], c["A_rk"]], axis=1),
                 cat([cat([stack(c["W"]), stack(c["U0"])], axis=1),
                      cat([zeros_2l, stack(c["v"])], axis=1)], axis=0))
        c["Q"] = c["Rt"] + qy[:, :LANES]
        c["Y0"] = qy[:, LANES:]
    ys = []
    for slot, c in enumerate(chains):
        H = h_ref[slot]
        ys.append(_mm(c["Q"], H) + c["Y0"])
        h_ref[slot] = _mm(c["M"], H) + c["N"]
    yf_ref[...] = cat(ys[:n_pairs], axis=1)
    yb_ref[...] = cat(ys[n_pairs:], axis=1)


def _scan(r, v, kk, lw0, k0, a0, lw1, k1, a1, batch, seq):
    t = r.shape[0]
    nc = seq // CHUNK
    fwd = pl.BlockSpec((CHUNK, D_A), lambda b, c: (b * nc + c, 0))
    bwd = pl.BlockSpec((CHUNK, D_A), lambda b, c: (b * nc + nc - 1 - c, 0))
    out = jax.ShapeDtypeStruct((t, D_A), F32)
    return pl.pallas_call(
        _scan_body,
        grid=(batch, nc),
        in_specs=[fwd] * 6 + [bwd] * 6,
        out_specs=[fwd, bwd],
        out_shape=[out, out],
        scratch_shapes=[pltpu.VMEM((2 * (D_A // LANES), LANES, LANES), F32)],
        compiler_params=_params(2),
        name="rwkv_scan",
    )(r, v, kk, lw0, k0, a0, r, v, kk, lw1, k1, a1)


def _bias_features(slope, n, width, key_side):
    pos = lax.broadcasted_iota(jnp.int32, (n, width), 0).astype(F32)
    lane = lax.broadcasted_iota(jnp.int32, (n, width), 1)
    hi, mid, lo = (t.astype(F32) for t in _split_bf16(slope * pos, 3))
    terms = jnp.where(lane % 3 == 0, hi, jnp.where(lane % 3 == 1, mid, lo))
    if key_side:
        return jnp.where(lane < 3, -1.0, jnp.where(lane < 6, terms, 0.0))
    return jnp.where(lane < 3, terms, jnp.where(lane < 6, 1.0, 0.0))


def _attn_body(q_ref, k_ref, vt_ref, lv_ref, sg_ref, qg_ref, kg_ref, o_ref, qs_ref, feat_ref, src_ref,
               s_ref, p_ref, al_ref, m_ref, l_ref, acc_ref, *, lam_init):
    tq = src_ref.shape[1]
    n_kv, _, tk = vt_ref.shape
    seq = n_kv * tk
    h = pl.program_id(1)
    i = pl.program_id(2)
    slope_bits = (127 - 2 * (h + 1)) << 23
    slope = (lax.bitcast_convert_type(jnp.full((1, 1), slope_bits, jnp.int32), jnp.float32).astype(F32)
             * LOG2E)
    bound = (HEAD_B * ROUNDING_MARGIN * jnp.max(jnp.abs(qg_ref[...]), axis=-1, keepdims=True)
             * jnp.max(jnp.abs(kg_ref[...]), axis=-1, keepdims=True))
    reach = jnp.minimum((F32_ZERO_EXP2 + 2.0 * bound) / slope + 1.0, float(seq)).astype(jnp.int32)[0, 0]

    n_st = tk // tq
    hw = 2 * HEAD_B

    @pl.when(i == 0)
    def _():
        kv_pos = lax.broadcasted_iota(jnp.int32, (tk, tq), 0)
        q_pos = lax.broadcasted_iota(jnp.int32, (tk, tq), 1)
        src_ref[...] = slope * (q_pos - kv_pos).astype(F32)
        key_feat = _bias_features(slope, tk, hw, key_side=True)
        feat_ref[0] = key_feat.astype(BF16)
        feat_ref[1] = (-key_feat).astype(BF16)
        feat_ref[2] = jnp.zeros((tk, hw), BF16)
        q_feat = _bias_features(slope, tq, hw, key_side=False).astype(BF16)
        for st in range(n_st):
            qs_ref[st, :tq, hw:] = q_feat
            qs_ref[st, tq:, hw:] = q_feat

    lane = lax.broadcasted_iota(jnp.int32, (tq, hw), 1)
    for st in range(n_st):
        q = q_ref[st * tq:(st + 1) * tq, :]
        zero = jnp.zeros_like(q)
        qs_ref[st, :tq, :hw] = jnp.where(lane < HEAD_B, q, zero)
        qs_ref[st, tq:, :hw] = jnp.where(lane < HEAD_B, zero, q)
    q0 = i * tk
    j_lo = jnp.maximum(q0 - reach, 0) // tk
    j_hi = jnp.minimum(q0 + tk - 1 + reach, seq - 1) // tk + 1
    n_left = i - j_lo
    n_off = n_left + (j_hi - i - 1)

    def scores(j, side, st):
        kb = k_ref[pl.ds(pl.multiple_of(j * tk, tk), tk), :]
        return lax.dot_general(jnp.concatenate([kb, feat_ref[side]], axis=1), qs_ref[st], _NT,
                               preferred_element_type=F32)

    def softmax_update(s, cst, st):
        m_old = m_ref[st]
        m_new = jnp.maximum(m_old, jnp.max(s, axis=0, keepdims=True) + cst)
        alpha = jnp.exp2(m_old - m_new)
        p = jnp.exp2(s - (m_new - cst))
        l_ref[st] = alpha * l_ref[st] + jnp.sum(p, axis=0, keepdims=True)
        m_ref[st] = m_new
        return alpha, p.astype(BF16)

    m_ref[...] = jnp.full(m_ref.shape, -1e30, F32)
    l_ref[...] = jnp.zeros(l_ref.shape, F32)
    src = src_ref[...]
    for st in range(n_st):
        bias = jnp.abs(src + slope * float(st * tq))
        s = scores(i, 2, st) - jnp.concatenate([bias, bias], axis=1)
        _, p = softmax_update(s, jnp.zeros((1, 1), F32), st)
        acc_ref[st] = jnp.dot(vt_ref[i], p, preferred_element_type=F32)

    def block(n):
        j = jnp.clip(j_lo + n + (n >= n_left).astype(jnp.int32), 0, n_kv - 1)
        return j, (j > i).astype(jnp.int32), jnp.logical_and(n >= 0, n < n_off)

    def qk_stage(n, slot):
        j, side, _ = block(n)
        for st in range(n_st):
            s_ref[slot, st] = scores(j, side, st)

    def softmax_stage(n, slot):
        j, _, valid = block(n)
        for st in range(n_st):
            dist0 = jnp.abs(q0 + st * tq - j * tk).astype(F32)
            cst = jnp.where(valid, -slope * dist0, -1e30)
            alpha, p = softmax_update(s_ref[slot, st], cst, st)
            al_ref[slot, st] = alpha
            p_ref[slot, st] = p

    def pv_stage(n, slot):
        j, _, _ = block(n)
        for st in range(n_st):
            acc_ref[st] = al_ref[slot, st] * acc_ref[st] + jnp.dot(vt_ref[j], p_ref[slot, st],
                                                                    preferred_element_type=F32)

    p_ref[1] = jnp.zeros(p_ref.shape[1:], BF16)
    al_ref[1] = jnp.ones(al_ref.shape[1:], F32)
    qk_stage(0, 0)

    def pipelined_pair(t, carry):
        n = 2 * t
        qk_stage(n + 1, 1)
        softmax_stage(n, 0)
        pv_stage(n - 1, 1)
        qk_stage(n + 2, 0)
        softmax_stage(n + 1, 1)
        pv_stage(n, 0)
        return carry

    n_pairs = (n_off + 1) // 2
    lax.fori_loop(0, n_pairs, pipelined_pair, 0)
    pv_stage(2 * n_pairs - 1, 1)

    lv = lv_ref[...]
    lam = (jnp.exp(jnp.sum(lv[0:1] * lv[1:2], axis=-1, keepdims=True))
           - jnp.exp(jnp.sum(lv[2:3] * lv[3:4], axis=-1, keepdims=True)) + lam_init)
    for st in range(n_st):
        acc = acc_ref[st]
        l = l_ref[st]
        o_t = acc[:, :tq] / l[:, :tq] - lam * (acc[:, tq:] / l[:, tq:])
        o_t = o_t * lax.rsqrt(jnp.mean(o_t * o_t, axis=0, keepdims=True) + NORM_EPS)
        o_ref[st * tq:(st + 1) * tq, :] = (o_t.T * sg_ref[...] * (1.0 - lam_init)).astype(o_ref.dtype)


def _attn(q, k, vt, lam_vecs, sub_g, qg, kg, lam_init, batch, seq):
    t = q.shape[0]
    tk = vt.shape[2]
    tq = ATTN_TQ
    n_st = tk // tq
    nq = seq // tk
    n_kv = seq // tk
    hw = 2 * HEAD_B
    return pl.pallas_call(
        functools.partial(_attn_body, lam_init=lam_init),
        grid=(batch, N_HEADS_B, nq),
        in_specs=[pl.BlockSpec((tk, hw), lambda b, h, i: (b * nq + i, h)),
                  pl.BlockSpec((seq, hw), lambda b, h, i: (b, h)),
                  pl.BlockSpec((n_kv, hw, tk), lambda b, h, i: (b, h, 0)),
                  _const_spec((4, HEAD_B)), _const_spec((1, hw)),
                  _const_spec((1, D_B)), _const_spec((1, D_B))],
        out_specs=pl.BlockSpec((tk, hw), lambda b, h, i: (b * nq + i, h)),
        out_shape=jax.ShapeDtypeStruct((t, D_B), BF16),
        scratch_shapes=[pltpu.VMEM((n_st, 2 * tq, 2 * hw), BF16),
                        pltpu.VMEM((3, tk, hw), BF16),
                        pltpu.VMEM((tk, tq), F32),
                        pltpu.VMEM((2, n_st, tk, 2 * tq), F32),
                        pltpu.VMEM((2, n_st, tk, 2 * tq), BF16),
                        pltpu.VMEM((2, n_st, 1, 2 * tq), F32),
                        pltpu.VMEM((n_st, 1, 2 * tq), F32),
                        pltpu.VMEM((n_st, 1, 2 * tq), F32),
                        pltpu.VMEM((n_st, hw, 2 * tq), F32)],
        compiler_params=_params(3),
        name="diff_attn",
    )(q, k, vt, lam_vecs, sub_g, qg, kg)


def _merge_body(yf_ref, yb_ref, gate_ref, bonus_ref, ob_ref, gt_ref, x_ref, lng_ref, lnb_ref, bd_ref,
                wb0_ref, wb1_ref, wo_ref, o_ref):
    bd = bd_ref[...]
    y = yf_ref[...] + yb_ref[...]
    yc = y - _mm_split_lhs(y, bd)
    var = _mm_split_lhs(yc * yc, bd)
    yn = yc * lax.rsqrt(var + GN_EPS) * lng_ref[...] + lnb_ref[...]
    oa = (yn + bonus_ref[...]) * gate_ref[...]
    gts = gt_ref[...].astype(F32)
    merged = (gts[:, :D_MODEL] * _mm(oa, wb0_ref[...])
              + gts[:, D_MODEL:] * jnp.dot(ob_ref[...], wb1_ref[...], preferred_element_type=F32))
    o_ref[...] = x_ref[...] + _mm(merged, wo_ref[...])


def _merge(yf, yb, gate, bonus, ob, gts, x2, ln_g, ln_b, bd_mean, wb0, wb1, wo):
    t = x2.shape[0]
    tm = min(TOKEN_TILE, t)
    row = lambda width: pl.BlockSpec((tm, width), lambda i: (i, 0))
    return pl.pallas_call(
        _merge_body,
        grid=(t // tm,),
        in_specs=[row(D_A), row(D_A), row(D_A), row(D_A), row(D_B), row(2 * D_MODEL), row(D_MODEL),
                  _const_spec((1, D_A)), _const_spec((1, D_A)), _const_spec((D_A, D_A)),
                  _const_spec((D_A, D_MODEL)), _const_spec((D_B, D_MODEL)),
                  _const_spec((D_MODEL, D_MODEL))],
        out_specs=row(D_MODEL),
        out_shape=jax.ShapeDtypeStruct((t, D_MODEL), F32),
        compiler_params=_params(1),
        name="merge",
    )(yf, yb, gate, bonus, ob, gts, x2, ln_g, ln_b, bd_mean, wb0, wb1, wo)


def _block_diag2(w):
    z = jnp.zeros_like(w[0])
    return jnp.concatenate([jnp.concatenate([w[0], z], axis=1), jnp.concatenate([z, w[1]], axis=1)], axis=0)


def kernel(x, norm_ffn1, ffn1_in, ffn1_out, norm_mix, w_in, rwkv_mu, decay_w0, decay_w2, iclr_a0, iclr_a2,
           gate_g2, k_k, k_a, r_k, ln_x_g, ln_x_b, q_gain, k_gain, diff_lambda, subln_g, w_branch, w_out,
           norm_ffn2, ffn2_in, ffn2_out):
    batch, seq, d = x.shape
    depth = norm_ffn1.shape[0]
    assert seq % TOKEN_TILE == 0 and d == D_MODEL
    x2 = x.reshape(batch * seq, d)
    head_id = jnp.arange(D_A) // HEAD_A
    same = head_id[:, None] == head_id[None, :]
    bd_sum = same.astype(BF16)
    bd_mean = (same.astype(F32) / HEAD_A).astype(BF16)
    n_qk = D_B // HEAD_B
    row = lambda a: a.reshape(1, -1)
    for l in range(depth):
        lam_init = 0.8 - 0.6 * math.exp(-0.3 * l)
        x2 = _ffn(x2, row(norm_ffn1[l]), ffn1_in[l].astype(BF16), ffn1_out[l].astype(BF16))
        qg = row(jnp.tile(q_gain[l], n_qk)) * (HEAD_B ** -0.5 * LOG2E)
        kg = row(jnp.tile(k_gain[l], n_qk))
        prw, q, k, vt, gts = _proj(x2, row(norm_mix[l]), w_in[l].astype(BF16), qg, kg, bd_mean)
        r, vv, kk, lw0, k0, a0, lw1, k1, a1, gate, bonus = _prep(
            prw, seq, row(rwkv_mu[l]), row(decay_w0[l]), _block_diag2(decay_w2[l]).astype(BF16),
            row(iclr_a0[l]), _block_diag2(iclr_a2[l]).astype(BF16), gate_g2[l].astype(BF16),
            row(k_k[l]), row(k_a[l]), row(r_k[l]), bd_sum)
        yf, yb = _scan(r, vv, kk, lw0, k0, a0, lw1, k1, a1, batch, seq)
        ob = _attn(q, k, vt, diff_lambda[l], row(subln_g[l]), qg, kg, lam_init, batch, seq)
        x2 = _merge(yf, yb, gate, bonus, ob, gts, x2, row(ln_x_g[l]), row(ln_x_b[l]), bd_mean,
                    w_branch[l, 0].astype(BF16), w_branch[l, 1].astype(BF16), w_out[l].astype(BF16))
        x2 = _ffn(x2, row(norm_ffn2[l]), ffn2_in[l].astype(BF16), ffn2_out[l].astype(BF16))
    return x2.reshape(batch, seq, d)
```

```python
import functools
import math

import jax
import jax.numpy as jnp
from jax import lax
from jax.experimental import pallas as pl
from jax.experimental.pallas import tpu as pltpu

F32 = jnp.float32
BF16 = jnp.bfloat16

D_MODEL = 1024
D_A = 512
HEAD_A = 64
LORA = 64
LORA_GATE = 128
C_RWKV = 3 * D_A + 2 * LORA + 2 * LORA + LORA_GATE
D_B = 512
HEAD_B = 64
N_HEADS_B = D_B // (2 * HEAD_B)
D_FF = 2816
DECAY_SCALE = math.exp(-0.5)
GN_EPS = 64e-5
NORM_EPS = 1e-6
LOG2E = math.log2(math.e)
F32_ZERO_EXP2 = 150.0
ROUNDING_MARGIN = 1.02

LANES = 128
SUBLANES = 8
VMEM_LIMIT = 56 * 1024 * 1024

TOKEN_TILE = 512
FF_TILE = 1408
CHUNK = 64
SCAN_CHUNKS_PER_STEP = 2
ATTN_TQ = 256

_NT = (((1,), (1,)), ((), ()))


def _mm(a, b):
    return jnp.dot(a.astype(BF16), b.astype(BF16), preferred_element_type=F32)


def _mm_nt(a, b):
    return lax.dot_general(a.astype(BF16), b.astype(BF16), _NT, preferred_element_type=F32)


def _split_bf16(x, n):
    parts = []
    for _ in range(n - 1):
        hi = x.astype(BF16)
        parts.append(hi)
        x = x - hi.astype(F32)
    parts.append(x.astype(BF16))
    return parts


def _mm_split_lhs(x, w, n=2):
    return sum(jnp.dot(p, w, preferred_element_type=F32) for p in _split_bf16(x, n))


def _rms_norm(x, g):
    return x * lax.rsqrt(jnp.mean(x * x, axis=-1, keepdims=True) + NORM_EPS) * g


def _const_spec(shape):
    return pl.BlockSpec(shape, lambda *_: (0,) * len(shape), pipeline_mode=pl.Buffered(1))


def _layer_spec(shape, layer, *lead):
    index = (layer,) + lead + (0,) * len(shape)
    return pl.BlockSpec((None,) * (1 + len(lead)) + shape, lambda *_: index, pipeline_mode=pl.Buffered(1))


def _params(n_axes):
    return pltpu.CompilerParams(dimension_semantics=("arbitrary",) * n_axes,
                                vmem_limit_bytes=VMEM_LIMIT)


def _ffn_body(x_ref, g_ref, win_ref, wout_ref, o_ref):
    x = x_ref[...]
    h = _rms_norm(x, g_ref[...]).astype(BF16)
    acc = jnp.zeros(x.shape, F32)
    for j in range(D_FF // FF_TILE):
        lo, hi = j * FF_TILE, (j + 1) * FF_TILE
        gate = jnp.dot(h, win_ref[:, lo:hi], preferred_element_type=F32)
        up = jnp.dot(h, win_ref[:, D_FF + lo:D_FF + hi], preferred_element_type=F32)
        act = (gate * jax.nn.sigmoid(gate) * up).astype(BF16)
        acc = acc + jnp.dot(act, wout_ref[lo:hi, :], preferred_element_type=F32)
    o_ref[...] = x + 0.5 * acc


def _ffn(x2, g, w_in, w_out, layer):
    t = x2.shape[0]
    tm = min(TOKEN_TILE, t)
    return pl.pallas_call(
        _ffn_body,
        grid=(t // tm,),
        in_specs=[pl.BlockSpec((tm, D_MODEL), lambda i: (i, 0)),
                  _layer_spec((1, D_MODEL), layer),
                  _layer_spec((D_MODEL, 2 * D_FF), layer),
                  _layer_spec((D_FF, D_MODEL), layer)],
        out_specs=pl.BlockSpec((tm, D_MODEL), lambda i: (i, 0)),
        out_shape=jax.ShapeDtypeStruct((t, D_MODEL), F32),
        compiler_params=_params(1),
        name="ffn",
    )(x2, g, w_in, w_out)


_Q0 = C_RWKV
_K0 = C_RWKV + D_B
_V0 = C_RWKV + 2 * D_B
_G0 = C_RWKV + 3 * D_B


def _proj_body(x_ref, g_ref, w_ref, qg_ref, kg_ref, bd_ref, prw_ref, q_ref, k_ref, vt_ref, gt_ref):
    h = _rms_norm(x_ref[...], g_ref[...]).astype(BF16)
    prw_ref[...] = jnp.dot(h, w_ref[:, :C_RWKV], preferred_element_type=F32)
    bd = bd_ref[...]
    q = jnp.dot(h, w_ref[:, _Q0:_K0], preferred_element_type=F32)
    q_ref[...] = (q * lax.rsqrt(_mm_split_lhs(q * q, bd) + NORM_EPS) * qg_ref[...]).astype(BF16)
    k = jnp.dot(h, w_ref[:, _K0:_V0], preferred_element_type=F32)
    k_ref[...] = (k * lax.rsqrt(_mm_split_lhs(k * k, bd) + NORM_EPS) * kg_ref[...]).astype(BF16)
    vt_ref[0] = jnp.dot(h, w_ref[:, _V0:_G0], preferred_element_type=F32).T.astype(BF16)
    gt_ref[...] = jax.nn.sigmoid(jnp.dot(h, w_ref[:, _G0:], preferred_element_type=F32)).astype(BF16)


def _proj(x2, g, w, qg, kg, bd_mean, layer):
    t = x2.shape[0]
    tm = TOKEN_TILE
    c_in = w.shape[2]
    row = lambda width: pl.BlockSpec((tm, width), lambda i: (i, 0))
    return pl.pallas_call(
        _proj_body,
        grid=(t // tm,),
        in_specs=[row(D_MODEL), _layer_spec((1, D_MODEL), layer), _layer_spec((D_MODEL, c_in), layer),
                  _layer_spec((1, D_B), layer), _layer_spec((1, D_B), layer), _const_spec((D_B, D_B))],
        out_specs=[row(C_RWKV), row(D_B), row(D_B),
                   pl.BlockSpec((1, D_B, tm), lambda i: (i, 0, 0)), row(2 * D_MODEL)],
        out_shape=[jax.ShapeDtypeStruct((t, C_RWKV), F32),
                   jax.ShapeDtypeStruct((t, D_B), BF16),
                   jax.ShapeDtypeStruct((t, D_B), BF16),
                   jax.ShapeDtypeStruct((t // tm, D_B, tm), BF16),
                   jax.ShapeDtypeStruct((t, 2 * D_MODEL), BF16)],
        compiler_params=_params(1),
        name="proj",
    )(x2, g, w, qg, kg, bd_mean)


def _prep_body(p_ref, pp_ref, pn_ref, mu_ref, w0_ref, w2_ref, a0_ref, a2_ref, g2_ref, kk_ref, ka_ref,
               rk_ref, bd_ref,
               r_out, v_out, kk_out, lw0_out, k0_out, a0_out, lw1_out, k1_out, a1_out, gate_out,
               bonus_out, *, seq):
    i = pl.program_id(0)
    p = p_ref[...]
    tm = p.shape[0]
    row = lax.broadcasted_iota(jnp.int32, p.shape, 0)
    keep_prev = jnp.where((i * tm) % seq == 0, 0.0, 1.0)
    keep_next = jnp.where(((i + 1) * tm) % seq == 0, 0.0, 1.0)
    prev = jnp.where(row == 0, keep_prev * pp_ref[SUBLANES - 1:SUBLANES, :], pltpu.roll(p, 1, 0))
    nxt = jnp.where(row == tm - 1, keep_next * pn_ref[0:1, :], pltpu.roll(p, tm - 1, 0))
    p = p + mu_ref[...] * (0.5 * (prev + nxt) - p)

    r = p[:, :D_A]
    k = p[:, D_A:2 * D_A]
    v = p[:, 2 * D_A:3 * D_A]
    dw = p[:, 3 * D_A:3 * D_A + 2 * LORA]
    da = p[:, 3 * D_A + 2 * LORA:3 * D_A + 4 * LORA]
    dg = p[:, 3 * D_A + 4 * LORA:]
    bd = bd_ref[...]

    logw = -DECAY_SCALE * jax.nn.sigmoid(w0_ref[...] + _mm(jnp.tanh(dw), w2_ref[...]))
    iclr = jax.nn.sigmoid(a0_ref[...] + _mm(da, a2_ref[...]))
    gate_out[...] = _mm(jax.nn.sigmoid(dg), g2_ref[...])

    kk = k * kk_ref[...]
    norm = jnp.sqrt(_mm_split_lhs(kk * kk, bd))
    kk_out[...] = (kk / jnp.maximum(norm, 1e-12)).astype(kk_out.dtype)

    ka = ka_ref[...]
    a_f = iclr[:, :D_A]
    a_b = iclr[:, D_A:]
    k_f = k * (1.0 + (a_f - 1.0) * ka)
    k_b = k * (1.0 + (a_b - 1.0) * ka)
    bonus = _mm_split_lhs(r * rk_ref[...] * (k_f + k_b), bd)
    r_out[...] = r.astype(r_out.dtype)
    v_out[...] = v.astype(v_out.dtype)
    lw0_out[...] = logw[:, :D_A]
    k0_out[...] = k_f.astype(k0_out.dtype)
    a0_out[...] = a_f.astype(a0_out.dtype)
    lw1_out[...] = logw[:, D_A:]
    k1_out[...] = k_b.astype(k1_out.dtype)
    a1_out[...] = a_b.astype(a1_out.dtype)
    bonus_out[...] = bonus * v


def _prep(prw, seq, mu, w0, w2cat, a0, a2cat, g2, k_k, k_a, r_k, bd_sum, layer):
    t = prw.shape[0]
    tm = min(TOKEN_TILE, seq)
    nb = tm // SUBLANES
    last = t // SUBLANES - 1
    row = lambda width: pl.BlockSpec((tm, width), lambda i: (i, 0))
    wide = jax.ShapeDtypeStruct((t, D_A), F32)
    half = jax.ShapeDtypeStruct((t, D_A), BF16)
    return pl.pallas_call(
        functools.partial(_prep_body, seq=seq),
        grid=(t // tm,),
        in_specs=[row(C_RWKV),
                  pl.BlockSpec((SUBLANES, C_RWKV), lambda i: (jnp.maximum(i * nb - 1, 0), 0)),
                  pl.BlockSpec((SUBLANES, C_RWKV), lambda i: (jnp.minimum((i + 1) * nb, last), 0)),
                  _layer_spec((1, C_RWKV), layer),
                  _layer_spec((1, 2 * D_A), layer), _layer_spec((2 * LORA, 2 * D_A), layer),
                  _layer_spec((1, 2 * D_A), layer), _layer_spec((2 * LORA, 2 * D_A), layer),
                  _layer_spec((LORA_GATE, D_A), layer),
                  _layer_spec((1, D_A), layer), _layer_spec((1, D_A), layer), _layer_spec((1, D_A), layer),
                  _const_spec((D_A, D_A))],
        out_specs=[row(D_A)] * 11,
        out_shape=[half, half, half, wide, half, half, wide, half, half, wide, wide],
        compiler_params=_params(1),
        name="rwkv_prep",
    )(prw, prw, prw, mu, w0, w2cat, a0, a2cat, g2, k_k, k_a, r_k, bd_sum)


def _scan_body(rf, vf, kkf, lwf, kf, af, rb, vb, kkb, lwb, kb, ab, yf_ref, yb_ref, h_ref):
    @pl.when(pl.program_id(1) == 0)
    def _():
        h_ref[...] = jnp.zeros(h_ref.shape, F32)

    L = CHUNK
    n_sub = rf.shape[0] // L
    n_pairs = D_A // LANES
    ti = lax.broadcasted_iota(jnp.int32, (L, L), 0)
    tj = lax.broadcasted_iota(jnp.int32, (L, L), 1)
    pi = lax.broadcasted_iota(jnp.int32, (L, LANES), 0)
    lane = lax.broadcasted_iota(jnp.int32, (L, LANES), 1)
    pj = lane & (HEAD_A - 1)
    lane_lo = lane < HEAD_A
    eye = jnp.where(pj == pi, 1.0, 0.0)
    blk = lambda s: (pi // s) == (pj // s)
    row2 = lax.broadcasted_iota(jnp.int32, (2 * L, LANES), 0)
    col2 = lax.broadcasted_iota(jnp.int32, (2 * L, LANES), 1)
    same_head = (row2 // HEAD_A) == (col2 // HEAD_A)
    diag2 = row2 == col2
    zeros_l = jnp.zeros((L, LANES), F32)
    zeros_2l = jnp.zeros((2 * L, LANES), F32)

    def stack(x):
        return jnp.concatenate([jnp.where(lane_lo, x, 0.0), jnp.where(lane_lo, 0.0, x)], axis=0)

    def mmp(x, y):
        return _mm(x, stack(y))

    cat = jnp.concatenate
    chains = []
    for reverse, refs in ((False, (rf, vf, kkf, lwf, kf, af)), (True, (rb, vb, kkb, lwb, kb, ab))):
        tri = ((tj >= ti) if reverse else (tj <= ti)).astype(BF16)
        strict = (pj > pi) if reverse else (pj < pi)
        incl = (pj >= pi) if reverse else (pj <= pi)
        steps = []
        for sub in (range(n_sub - 1, -1, -1) if reverse else range(n_sub)):
            rows = slice(sub * L, (sub + 1) * L)
            r, v, kk, lw, k, a = (ref[rows, :].astype(F32) for ref in refs)
            G = sum(jnp.dot(tri, part, preferred_element_type=F32) for part in _split_bf16(lw, 3))
            g_end = G[0:1] if reverse else G[L - 1:L]
            gam_end = jnp.exp(g_end)
            inv_gam = jnp.exp(-G)
            rel_end = jnp.exp(g_end - G)
            b = kk * a
            full = dict(At=-kk * jnp.exp(G - lw), Bt=b * inv_gam, Kt=k * inv_gam, Rt=r * jnp.exp(G),
                        Kh=k * rel_end, Bh=b * rel_end, v=v)
            pairs = []
            for p in range(n_pairs):
                sl = slice(p * LANES, (p + 1) * LANES)
                c = {name: val[:, sl] for name, val in full.items()}
                c.update(strict=strict, incl=incl, gam_end=gam_end[:, sl], rows=rows)
                pairs.append(c)
            steps.append(pairs)
        chains.append(steps)
    flat = [c for steps in chains for pairs in steps for c in pairs]

    for c in flat:
        a4 = _mm_nt(cat([c["At"], c["Rt"]], axis=0), cat([stack(c["Bt"]), stack(c["Kt"])], axis=0))
        c["A_ab"] = jnp.where(c["strict"], a4[:L, :LANES], 0.0)
        c["A_ak"] = jnp.where(c["strict"], a4[:L, LANES:], 0.0)
        c["A_rb"] = jnp.where(c["incl"], a4[L:, :LANES], 0.0)
        c["A_rk"] = jnp.where(c["incl"], a4[L:, LANES:], 0.0)
    for c in flat:
        c["A0"] = jnp.where(blk(8), c["A_ab"], 0.0)
        c["A2"] = mmp(c["A0"], c["A0"])
    for c in flat:
        c["A4"] = mmp(c["A2"], c["A2"])
        c["T"] = eye + c["A0"]
    for c in flat:
        c["T"] = c["T"] + mmp(c["T"], c["A2"])
    for c in flat:
        c["T"] = c["T"] + mmp(c["T"], c["A4"])
    s = 8
    while s < L:
        outer = blk(2 * s) & jnp.logical_not(blk(s))
        for c in flat:
            c["TO"] = mmp(c["T"], jnp.where(outer, c["A_ab"], 0.0))
        for c in flat:
            c["T"] = c["T"] + mmp(c["TO"], c["T"])
        s *= 2
    for c in flat:
        c["AkV"] = mmp(c["A_ak"], c["v"])
    for c in flat:
        wu = _mm(c["T"], cat([stack(c["At"]), stack(c["AkV"])], axis=1))
        c["W"], c["U0"] = wu[:, :LANES], wu[:, LANES:]
    for c in flat:
        kb_t = cat([c["Kh"], c["Bh"]], axis=0).T
        mn = _mm(kb_t, cat([cat([zeros_l, c["v"]], axis=1), cat([c["W"], c["U0"]], axis=1)], axis=0))
        c["M"] = jnp.where(same_head, mn[:, :LANES], 0.0) + jnp.where(diag2, c["gam_end"], 0.0)
        c["N"] = jnp.where(same_head, mn[:, LANES:], 0.0)
    for c in flat:
        qy = _mm(cat([c["A_rb"], c["A_rk"]], axis=1),
                 cat([cat([stack(c["W"]), stack(c["U0"])], axis=1),
                      cat([zeros_2l, stack(c["v"])], axis=1)], axis=0))
        c["Q"] = c["Rt"] + qy[:, :LANES]
        c["Y0"] = qy[:, LANES:]
    for d, (steps, y_ref) in enumerate(zip(chains, (yf_ref, yb_ref))):
        for pairs in steps:
            ys = []
            for p, c in enumerate(pairs):
                H = h_ref[d * n_pairs + p]
                ys.append(_mm(c["Q"], H) + c["Y0"])
                h_ref[d * n_pairs + p] = _mm(c["M"], H) + c["N"]
            y_ref[pairs[0]["rows"], :] = cat(ys, axis=1)


def _scan(r, v, kk, lw0, k0, a0, lw1, k1, a1, batch, seq):
    t = r.shape[0]
    rows = SCAN_CHUNKS_PER_STEP * CHUNK
    nc = seq // rows
    fwd = pl.BlockSpec((rows, D_A), lambda b, c: (b * nc + c, 0))
    bwd = pl.BlockSpec((rows, D_A), lambda b, c: (b * nc + nc - 1 - c, 0))
    out = jax.ShapeDtypeStruct((t, D_A), F32)
    return pl.pallas_call(
        _scan_body,
        grid=(batch, nc),
        in_specs=[fwd] * 6 + [bwd] * 6,
        out_specs=[fwd, bwd],
        out_shape=[out, out],
        scratch_shapes=[pltpu.VMEM((2 * (D_A // LANES), LANES, LANES), F32)],
        compiler_params=_params(2),
        name="rwkv_scan",
    )(r, v, kk, lw0, k0, a0, r, v, kk, lw1, k1, a1)


def _bias_features(slope, n, width, key_side):
    pos = lax.broadcasted_iota(jnp.int32, (n, width), 0).astype(F32)
    lane = lax.broadcasted_iota(jnp.int32, (n, width), 1)
    hi, mid, lo = (t.astype(F32) for t in _split_bf16(slope * pos, 3))
    terms = jnp.where(lane % 3 == 0, hi, jnp.where(lane % 3 == 1, mid, lo))
    if key_side:
        return jnp.where(lane < 3, -1.0, jnp.where(lane < 6, terms, 0.0))
    return jnp.where(lane < 3, terms, jnp.where(lane < 6, 1.0, 0.0))


def _attn_body(q_ref, k_ref, vt_ref, lv_ref, sg_ref, qg_ref, kg_ref, o_ref, qs_ref, feat_ref, src_ref,
               s_ref, p_ref, al_ref, m_ref, l_ref, acc_ref, *, lam_init):
    tq = src_ref.shape[1]
    n_kv, _, tk = vt_ref.shape
    seq = n_kv * tk
    h = pl.program_id(1)
    i = pl.program_id(2)
    slope_bits = (127 - 2 * (h + 1)) << 23
    slope = (lax.bitcast_convert_type(jnp.full((1, 1), slope_bits, jnp.int32), jnp.float32).astype(F32)
             * LOG2E)
    bound = (HEAD_B * ROUNDING_MARGIN * jnp.max(jnp.abs(qg_ref[...]), axis=-1, keepdims=True)
             * jnp.max(jnp.abs(kg_ref[...]), axis=-1, keepdims=True))
    reach = jnp.minimum((F32_ZERO_EXP2 + 2.0 * bound) / slope + 1.0, float(seq)).astype(jnp.int32)[0, 0]

    n_st = tk // tq
    hw = 2 * HEAD_B

    @pl.when(i == 0)
    def _():
        kv_pos = lax.broadcasted_iota(jnp.int32, (tk, tq), 0)
        q_pos = lax.broadcasted_iota(jnp.int32, (tk, tq), 1)
        src_ref[...] = slope * (q_pos - kv_pos).astype(F32)
        key_feat = _bias_features(slope, tk, hw, key_side=True)
        feat_ref[0] = key_feat.astype(BF16)
        feat_ref[1] = (-key_feat).astype(BF16)
        feat_ref[2] = jnp.zeros((tk, hw), BF16)
        q_feat = _bias_features(slope, tq, hw, key_side=False).astype(BF16)
        for st in range(n_st):
            qs_ref[st, :tq, hw:] = q_feat
            qs_ref[st, tq:, hw:] = q_feat

    lane = lax.broadcasted_iota(jnp.int32, (tq, hw), 1)
    for st in range(n_st):
        q = q_ref[st * tq:(st + 1) * tq, :]
        zero = jnp.zeros_like(q)
        qs_ref[st, :tq, :hw] = jnp.where(lane < HEAD_B, q, zero)
        qs_ref[st, tq:, :hw] = jnp.where(lane < HEAD_B, zero, q)
    q0 = i * tk
    j_lo = jnp.maximum(q0 - reach, 0) // tk
    j_hi = jnp.minimum(q0 + tk - 1 + reach, seq - 1) // tk + 1
    n_left = i - j_lo
    n_off = n_left + (j_hi - i - 1)

    def scores(j, side, st):
        kb = k_ref[pl.ds(pl.multiple_of(j * tk, tk), tk), :]
        return lax.dot_general(jnp.concatenate([kb, feat_ref[side]], axis=1), qs_ref[st], _NT,
                               preferred_element_type=F32)

    def softmax_update(s, cst, st):
        m_old = m_ref[st]
        m_new = jnp.maximum(m_old, jnp.max(s, axis=0, keepdims=True) + cst)
        alpha = jnp.exp2(m_old - m_new)
        p = jnp.exp2(s - (m_new - cst))
        l_ref[st] = alpha * l_ref[st] + jnp.sum(p, axis=0, keepdims=True)
        m_ref[st] = m_new
        return alpha, p.astype(BF16)

    m_ref[...] = jnp.full(m_ref.shape, -1e30, F32)
    l_ref[...] = jnp.zeros(l_ref.shape, F32)
    src = src_ref[...]
    for st in range(n_st):
        bias = jnp.abs(src + slope * float(st * tq))
        s = scores(i, 2, st) - jnp.concatenate([bias, bias], axis=1)
        _, p = softmax_update(s, jnp.zeros((1, 1), F32), st)
        acc_ref[st] = jnp.dot(vt_ref[i], p, preferred_element_type=F32)

    def block(n):
        j = jnp.clip(j_lo + n + (n >= n_left).astype(jnp.int32), 0, n_kv - 1)
        return j, (j > i).astype(jnp.int32), jnp.logical_and(n >= 0, n < n_off)

    def qk_stage(n, slot):
        j, side, _ = block(n)
        for st in range(n_st):
            s_ref[slot, st] = scores(j, side, st)

    def softmax_stage(n, slot):
        j, _, valid = block(n)
        for st in range(n_st):
            dist0 = jnp.abs(q0 + st * tq - j * tk).astype(F32)
            cst = jnp.where(valid, -slope * dist0, -1e30)
            alpha, p = softmax_update(s_ref[slot, st], cst, st)
            al_ref[slot, st] = alpha
            p_ref[slot, st] = p

    def pv_stage(n, slot):
        j, _, _ = block(n)
        for st in range(n_st):
            acc_ref[st] = al_ref[slot, st] * acc_ref[st] + jnp.dot(vt_ref[j], p_ref[slot, st],
                                                                    preferred_element_type=F32)

    p_ref[1] = jnp.zeros(p_ref.shape[1:], BF16)
    al_ref[1] = jnp.ones(al_ref.shape[1:], F32)
    qk_stage(0, 0)

    def pipelined_pair(t, carry):
        n = 2 * t
        qk_stage(n + 1, 1)
        softmax_stage(n, 0)
        pv_stage(n - 1, 1)
        qk_stage(n + 2, 0)
        softmax_stage(n + 1, 1)
        pv_stage(n, 0)
        return carry

    n_pairs = (n_off + 1) // 2
    lax.fori_loop(0, n_pairs, pipelined_pair, 0)
    pv_stage(2 * n_pairs - 1, 1)

    lv = lv_ref[...]
    lam = (jnp.exp(jnp.sum(lv[0:1] * lv[1:2], axis=-1, keepdims=True))
           - jnp.exp(jnp.sum(lv[2:3] * lv[3:4], axis=-1, keepdims=True)) + lam_init)
    for st in range(n_st):
        acc = acc_ref[st]
        l = l_ref[st]
        o_t = acc[:, :tq] / l[:, :tq] - lam * (acc[:, tq:] / l[:, tq:])
        o_t = o_t * lax.rsqrt(jnp.mean(o_t * o_t, axis=0, keepdims=True) + NORM_EPS)
        o_ref[st * tq:(st + 1) * tq, :] = (o_t.T * sg_ref[...] * (1.0 - lam_init)).astype(o_ref.dtype)


def _attn(q, k, vt, lam_vecs, sub_g, qg, kg, lam_init, batch, seq, layer):
    t = q.shape[0]
    tk = vt.shape[2]
    tq = ATTN_TQ
    n_st = tk // tq
    nq = seq // tk
    n_kv = seq // tk
    hw = 2 * HEAD_B
    return pl.pallas_call(
        functools.partial(_attn_body, lam_init=lam_init),
        grid=(batch, N_HEADS_B, nq),
        in_specs=[pl.BlockSpec((tk, hw), lambda b, h, i: (b * nq + i, h)),
                  pl.BlockSpec((seq, hw), lambda b, h, i: (b, h)),
                  pl.BlockSpec((n_kv, hw, tk), lambda b, h, i: (b, h, 0)),
                  _layer_spec((4, HEAD_B), layer), _layer_spec((1, hw), layer),
                  _layer_spec((1, D_B), layer), _layer_spec((1, D_B), layer)],
        out_specs=pl.BlockSpec((tk, hw), lambda b, h, i: (b * nq + i, h)),
        out_shape=jax.ShapeDtypeStruct((t, D_B), BF16),
        scratch_shapes=[pltpu.VMEM((n_st, 2 * tq, 2 * hw), BF16),
                        pltpu.VMEM((3, tk, hw), BF16),
                        pltpu.VMEM((tk, tq), F32),
                        pltpu.VMEM((2, n_st, tk, 2 * tq), F32),
                        pltpu.VMEM((2, n_st, tk, 2 * tq), BF16),
                        pltpu.VMEM((2, n_st, 1, 2 * tq), F32),
                        pltpu.VMEM((n_st, 1, 2 * tq), F32),
                        pltpu.VMEM((n_st, 1, 2 * tq), F32),
                        pltpu.VMEM((n_st, hw, 2 * tq), F32)],
        compiler_params=_params(3),
        name="diff_attn",
    )(q, k, vt, lam_vecs, sub_g, qg, kg)


def _merge_body(yf_ref, yb_ref, gate_ref, bonus_ref, ob_ref, gt_ref, x_ref, lng_ref, lnb_ref, bd_ref,
                wb0_ref, wb1_ref, wo_ref, o_ref):
    bd = bd_ref[...]
    y = yf_ref[...] + yb_ref[...]
    yc = y - _mm_split_lhs(y, bd)
    var = _mm_split_lhs(yc * yc, bd)
    yn = yc * lax.rsqrt(var + GN_EPS) * lng_ref[...] + lnb_ref[...]
    oa = (yn + bonus_ref[...]) * gate_ref[...]
    gts = gt_ref[...].astype(F32)
    merged = (gts[:, :D_MODEL] * _mm(oa, wb0_ref[...])
              + gts[:, D_MODEL:] * jnp.dot(ob_ref[...], wb1_ref[...], preferred_element_type=F32))
    o_ref[...] = x_ref[...] + _mm(merged, wo_ref[...])


def _merge(yf, yb, gate, bonus, ob, gts, x2, ln_g, ln_b, bd_mean, wb, wo, layer):
    t = x2.shape[0]
    tm = min(TOKEN_TILE, t)
    row = lambda width: pl.BlockSpec((tm, width), lambda i: (i, 0))
    return pl.pallas_call(
        _merge_body,
        grid=(t // tm,),
        in_specs=[row(D_A), row(D_A), row(D_A), row(D_A), row(D_B), row(2 * D_MODEL), row(D_MODEL),
                  _layer_spec((1, D_A), layer), _layer_spec((1, D_A), layer), _const_spec((D_A, D_A)),
                  _layer_spec((D_A, D_MODEL), layer, 0), _layer_spec((D_B, D_MODEL), layer, 1),
                  _layer_spec((D_MODEL, D_MODEL), layer)],
        out_specs=row(D_MODEL),
        out_shape=jax.ShapeDtypeStruct((t, D_MODEL), F32),
        compiler_params=_params(1),
        name="merge",
    )(yf, yb, gate, bonus, ob, gts, x2, ln_g, ln_b, bd_mean, wb, wb, wo)


def _block_diag2(w):
    z = jnp.zeros_like(w[:, 0])
    return jnp.concatenate([jnp.concatenate([w[:, 0], z], axis=2), jnp.concatenate([z, w[:, 1]], axis=2)],
                           axis=1)


def kernel(x, norm_ffn1, ffn1_in, ffn1_out, norm_mix, w_in, rwkv_mu, decay_w0, decay_w2, iclr_a0, iclr_a2,
           gate_g2, k_k, k_a, r_k, ln_x_g, ln_x_b, q_gain, k_gain, diff_lambda, subln_g, w_branch, w_out,
           norm_ffn2, ffn2_in, ffn2_out):
    batch, seq, d = x.shape
    depth = norm_ffn1.shape[0]
    assert seq % TOKEN_TILE == 0 and d == D_MODEL
    x2 = x.reshape(batch * seq, d)
    head_id = jnp.arange(D_A) // HEAD_A
    same = head_id[:, None] == head_id[None, :]
    bd_sum = same.astype(BF16)
    bd_mean = (same.astype(F32) / HEAD_A).astype(BF16)
    rows = lambda a: a.reshape(depth, 1, -1)
    bf = lambda a: a.astype(BF16)
    n_qk = D_B // HEAD_B
    qg = rows(jnp.tile(q_gain, (1, n_qk))) * (HEAD_B ** -0.5 * LOG2E)
    kg = rows(jnp.tile(k_gain, (1, n_qk)))
    ffn1 = (rows(norm_ffn1), bf(ffn1_in), bf(ffn1_out))
    ffn2 = (rows(norm_ffn2), bf(ffn2_in), bf(ffn2_out))
    proj_w = (rows(norm_mix), bf(w_in))
    prep_w = (rows(rwkv_mu), rows(decay_w0), bf(_block_diag2(decay_w2)), rows(iclr_a0),
              bf(_block_diag2(iclr_a2)), bf(gate_g2), rows(k_k), rows(k_a), rows(r_k))
    merge_w = (rows(ln_x_g), rows(ln_x_b), bd_mean, bf(w_branch), bf(w_out))
    sub_g = rows(subln_g)
    for l in range(depth):
        lam_init = 0.8 - 0.6 * math.exp(-0.3 * l)
        x2 = _ffn(x2, *ffn1, l)
        prw, q, k, vt, gts = _proj(x2, *proj_w, qg, kg, bd_mean, l)
        r, vv, kk, lw0, k0, a0, lw1, k1, a1, gate, bonus = _prep(prw, seq, *prep_w, bd_sum, l)
        yf, yb = _scan(r, vv, kk, lw0, k0, a0, lw1, k1, a1, batch, seq)
        ob = _attn(q, k, vt, diff_lambda, sub_g, qg, kg, lam_init, batch, seq, l)
        x2 = _merge(yf, yb, gate, bonus, ob, gts, x2, *merge_w, l)
        x2 = _ffn(x2, *ffn2, l)
    return x2.reshape(batch, seq, d)
```

```python
import functools
import math

import jax
import jax.numpy as jnp
from jax import lax
from jax.experimental import pallas as pl
from jax.experimental.pallas import tpu as pltpu

F32 = jnp.float32
BF16 = jnp.bfloat16

D_MODEL = 1024
D_A = 512
HEAD_A = 64
LORA = 64
LORA_GATE = 128
C_RWKV = 3 * D_A + 2 * LORA + 2 * LORA + LORA_GATE
D_B = 512
HEAD_B = 64
N_HEADS_B = D_B // (2 * HEAD_B)
D_FF = 2816
DECAY_SCALE = math.exp(-0.5)
GN_EPS = 64e-5
NORM_EPS = 1e-6
LOG2E = math.log2(math.e)
F32_ZERO_EXP2 = 150.0
ROUNDING_MARGIN = 1.02

LANES = 128
SUBLANES = 8
VMEM_LIMIT = 56 * 1024 * 1024

TOKEN_TILE = 512
MXU_WIDTH = 256
FF_SPLITS = (0, 6 * MXU_WIDTH, D_FF)
CHUNK = 64
SCAN_CHUNKS_PER_STEP = 2
ATTN_TQ = 256
V_ROWS = 2 * HEAD_B + 16

_NT = (((1,), (1,)), ((), ()))


def _mm(a, b):
    return jnp.dot(a.astype(BF16), b.astype(BF16), preferred_element_type=F32)


def _mm_nt(a, b):
    return lax.dot_general(a.astype(BF16), b.astype(BF16), _NT, preferred_element_type=F32)


def _split_bf16(x, n):
    parts = []
    for _ in range(n - 1):
        hi = x.astype(BF16)
        parts.append(hi)
        x = x - hi.astype(F32)
    parts.append(x.astype(BF16))
    return parts


def _mm_split_lhs(x, w, n=2):
    return sum(jnp.dot(p, w, preferred_element_type=F32) for p in _split_bf16(x, n))


def _rms_norm(x, g):
    return x * lax.rsqrt(jnp.mean(x * x, axis=-1, keepdims=True) + NORM_EPS) * g


def _const_spec(shape):
    return pl.BlockSpec(shape, lambda *_: (0,) * len(shape), pipeline_mode=pl.Buffered(1))


def _layer_spec(shape, layer, *lead):
    index = (layer,) + lead + (0,) * len(shape)
    return pl.BlockSpec((None,) * (1 + len(lead)) + shape, lambda *_: index, pipeline_mode=pl.Buffered(1))


def _params(n_axes):
    return pltpu.CompilerParams(dimension_semantics=("arbitrary",) * n_axes,
                                vmem_limit_bytes=VMEM_LIMIT)


def _ffn_body(x_ref, g_ref, win_ref, wout_ref, o_ref):
    x = x_ref[...]
    h = _rms_norm(x, g_ref[...]).astype(BF16)
    acc = jnp.zeros(x.shape, F32)
    for lo, hi in zip(FF_SPLITS[:-1], FF_SPLITS[1:]):
        gate = jnp.dot(h, win_ref[:, lo:hi], preferred_element_type=F32)
        up = jnp.dot(h, win_ref[:, D_FF + lo:D_FF + hi], preferred_element_type=F32)
        act = (gate * jax.nn.sigmoid(gate) * up).astype(BF16)
        acc = acc + jnp.dot(act, wout_ref[lo:hi, :], preferred_element_type=F32)
    o_ref[...] = x + 0.5 * acc


def _ffn(x2, g, w_in, w_out, layer):
    t = x2.shape[0]
    tm = min(TOKEN_TILE, t)
    return pl.pallas_call(
        _ffn_body,
        grid=(t // tm,),
        in_specs=[pl.BlockSpec((tm, D_MODEL), lambda i: (i, 0)),
                  _layer_spec((1, D_MODEL), layer),
                  _layer_spec((D_MODEL, 2 * D_FF), layer),
                  _layer_spec((D_FF, D_MODEL), layer)],
        out_specs=pl.BlockSpec((tm, D_MODEL), lambda i: (i, 0)),
        out_shape=jax.ShapeDtypeStruct((t, D_MODEL), F32),
        compiler_params=_params(1),
        name="ffn",
    )(x2, g, w_in, w_out)


_Q0 = C_RWKV
_K0 = C_RWKV + D_B
_V0 = C_RWKV + 2 * D_B
_G0 = C_RWKV + 3 * D_B


def _proj_body(x_ref, g_ref, w_ref, qg_ref, kg_ref, bd_ref, prw_ref, q_ref, k_ref, vt_ref, gt_ref):
    h = _rms_norm(x_ref[...], g_ref[...]).astype(BF16)
    prw_ref[...] = jnp.dot(h, w_ref[:, :C_RWKV], preferred_element_type=F32)
    bd = bd_ref[...]
    q = jnp.dot(h, w_ref[:, _Q0:_K0], preferred_element_type=F32)
    q_ref[...] = (q * lax.rsqrt(_mm_split_lhs(q * q, bd) + NORM_EPS) * qg_ref[...]).astype(BF16)
    k = jnp.dot(h, w_ref[:, _K0:_V0], preferred_element_type=F32)
    k_ref[...] = (k * lax.rsqrt(_mm_split_lhs(k * k, bd) + NORM_EPS) * kg_ref[...]).astype(BF16)
    vt = jnp.dot(h, w_ref[:, _V0:_G0], preferred_element_type=F32).T.astype(BF16)
    hw = 2 * HEAD_B
    ones = jnp.ones((V_ROWS - hw, vt.shape[1]), BF16)
    vt_ref[0] = jnp.concatenate([part for hd in range(N_HEADS_B)
                                 for part in (vt[hd * hw:(hd + 1) * hw], ones)], axis=0)
    gt_ref[...] = jax.nn.sigmoid(jnp.dot(h, w_ref[:, _G0:], preferred_element_type=F32)).astype(BF16)


def _proj(x2, g, w, qg, kg, bd_mean, layer):
    t = x2.shape[0]
    tm = TOKEN_TILE
    c_in = w.shape[2]
    row = lambda width: pl.BlockSpec((tm, width), lambda i: (i, 0))
    return pl.pallas_call(
        _proj_body,
        grid=(t // tm,),
        in_specs=[row(D_MODEL), _layer_spec((1, D_MODEL), layer), _layer_spec((D_MODEL, c_in), layer),
                  _layer_spec((1, D_B), layer), _layer_spec((1, D_B), layer), _const_spec((D_B, D_B))],
        out_specs=[row(C_RWKV), row(D_B), row(D_B),
                   pl.BlockSpec((1, N_HEADS_B * V_ROWS, tm), lambda i: (i, 0, 0)), row(2 * D_MODEL)],
        out_shape=[jax.ShapeDtypeStruct((t, C_RWKV), F32),
                   jax.ShapeDtypeStruct((t, D_B), BF16),
                   jax.ShapeDtypeStruct((t, D_B), BF16),
                   jax.ShapeDtypeStruct((t // tm, N_HEADS_B * V_ROWS, tm), BF16),
                   jax.ShapeDtypeStruct((t, 2 * D_MODEL), BF16)],
        compiler_params=_params(1),
        name="proj",
    )(x2, g, w, qg, kg, bd_mean)


def _prep_body(p_ref, pp_ref, pn_ref, mu_ref, w0_ref, w2_ref, a0_ref, a2_ref, g2_ref, kk_ref, ka_ref,
               rk_ref, bd_ref,
               r_out, v_out, kk_out, lw0_out, k0_out, a0_out, lw1_out, k1_out, a1_out, gate_out,
               bonus_out, *, seq):
    i = pl.program_id(0)
    p = p_ref[...]
    tm = p.shape[0]
    row = lax.broadcasted_iota(jnp.int32, p.shape, 0)
    keep_prev = jnp.where((i * tm) % seq == 0, 0.0, 1.0)
    keep_next = jnp.where(((i + 1) * tm) % seq == 0, 0.0, 1.0)
    prev = jnp.where(row == 0, keep_prev * pp_ref[SUBLANES - 1:SUBLANES, :], pltpu.roll(p, 1, 0))
    nxt = jnp.where(row == tm - 1, keep_next * pn_ref[0:1, :], pltpu.roll(p, tm - 1, 0))
    p = p + mu_ref[...] * (0.5 * (prev + nxt) - p)

    r = p[:, :D_A]
    k = p[:, D_A:2 * D_A]
    v = p[:, 2 * D_A:3 * D_A]
    dw = p[:, 3 * D_A:3 * D_A + 2 * LORA]
    da = p[:, 3 * D_A + 2 * LORA:3 * D_A + 4 * LORA]
    dg = p[:, 3 * D_A + 4 * LORA:]
    bd = bd_ref[...]

    logw = -DECAY_SCALE * jax.nn.sigmoid(w0_ref[...] + _mm(jnp.tanh(dw), w2_ref[...]))
    iclr = jax.nn.sigmoid(a0_ref[...] + _mm(da, a2_ref[...]))
    gate_out[...] = _mm(jax.nn.sigmoid(dg), g2_ref[...])

    kk = k * kk_ref[...]
    norm = jnp.sqrt(_mm_split_lhs(kk * kk, bd))
    kk_out[...] = (kk / jnp.maximum(norm, 1e-12)).astype(kk_out.dtype)

    ka = ka_ref[...]
    a_f = iclr[:, :D_A]
    a_b = iclr[:, D_A:]
    k_f = k * (1.0 + (a_f - 1.0) * ka)
    k_b = k * (1.0 + (a_b - 1.0) * ka)
    bonus = _mm_split_lhs(r * rk_ref[...] * (k_f + k_b), bd)
    r_out[...] = r.astype(r_out.dtype)
    v_out[...] = v.astype(v_out.dtype)
    lw0_out[...] = logw[:, :D_A]
    k0_out[...] = k_f.astype(k0_out.dtype)
    a0_out[...] = a_f.astype(a0_out.dtype)
    lw1_out[...] = logw[:, D_A:]
    k1_out[...] = k_b.astype(k1_out.dtype)
    a1_out[...] = a_b.astype(a1_out.dtype)
    bonus_out[...] = bonus * v


def _prep(prw, seq, mu, w0, w2cat, a0, a2cat, g2, k_k, k_a, r_k, bd_sum, layer):
    t = prw.shape[0]
    tm = min(TOKEN_TILE, seq)
    nb = tm // SUBLANES
    last = t // SUBLANES - 1
    row = lambda width: pl.BlockSpec((tm, width), lambda i: (i, 0))
    wide = jax.ShapeDtypeStruct((t, D_A), F32)
    half = jax.ShapeDtypeStruct((t, D_A), BF16)
    return pl.pallas_call(
        functools.partial(_prep_body, seq=seq),
        grid=(t // tm,),
        in_specs=[row(C_RWKV),
                  pl.BlockSpec((SUBLANES, C_RWKV), lambda i: (jnp.maximum(i * nb - 1, 0), 0)),
                  pl.BlockSpec((SUBLANES, C_RWKV), lambda i: (jnp.minimum((i + 1) * nb, last), 0)),
                  _layer_spec((1, C_RWKV), layer),
                  _layer_spec((1, 2 * D_A), layer), _layer_spec((2 * LORA, 2 * D_A), layer),
                  _layer_spec((1, 2 * D_A), layer), _layer_spec((2 * LORA, 2 * D_A), layer),
                  _layer_spec((LORA_GATE, D_A), layer),
                  _layer_spec((1, D_A), layer), _layer_spec((1, D_A), layer), _layer_spec((1, D_A), layer),
                  _const_spec((D_A, D_A))],
        out_specs=[row(D_A)] * 11,
        out_shape=[half, half, half, wide, half, half, wide, half, half, wide, wide],
        compiler_params=_params(1),
        name="rwkv_prep",
    )(prw, prw, prw, mu, w0, w2cat, a0, a2cat, g2, k_k, k_a, r_k, bd_sum)


def _scan_body(rf, vf, kkf, lwf, kf, af, rb, vb, kkb, lwb, kb, ab, yf_ref, yb_ref, h_ref):
    @pl.when(pl.program_id(1) == 0)
    def _():
        h_ref[...] = jnp.zeros(h_ref.shape, F32)

    L = CHUNK
    n_sub = rf.shape[0] // L
    n_pairs = D_A // LANES
    ti = lax.broadcasted_iota(jnp.int32, (L, L), 0)
    tj = lax.broadcasted_iota(jnp.int32, (L, L), 1)
    pi = lax.broadcasted_iota(jnp.int32, (L, LANES), 0)
    lane = lax.broadcasted_iota(jnp.int32, (L, LANES), 1)
    pj = lane & (HEAD_A - 1)
    lane_lo = lane < HEAD_A
    eye = jnp.where(pj == pi, 1.0, 0.0)
    blk = lambda s: (pi // s) == (pj // s)
    row2 = lax.broadcasted_iota(jnp.int32, (2 * L, LANES), 0)
    col2 = lax.broadcasted_iota(jnp.int32, (2 * L, LANES), 1)
    same_head = (row2 // HEAD_A) == (col2 // HEAD_A)
    diag2 = row2 == col2
    zeros_l = jnp.zeros((L, LANES), F32)
    zeros_2l = jnp.zeros((2 * L, LANES), F32)

    def stack(x):
        return jnp.concatenate([jnp.where(lane_lo, x, 0.0), jnp.where(lane_lo, 0.0, x)], axis=0)

    def mmp(x, y):
        return _mm(x, stack(y))

    cat = jnp.concatenate
    chains = []
    for reverse, refs in ((False, (rf, vf, kkf, lwf, kf, af)), (True, (rb, vb, kkb, lwb, kb, ab))):
        tri = ((tj >= ti) if reverse else (tj <= ti)).astype(BF16)
        strict = (pj > pi) if reverse else (pj < pi)
        incl = (pj >= pi) if reverse else (pj <= pi)
        steps = []
        for sub in (range(n_sub - 1, -1, -1) if reverse else range(n_sub)):
            rows = slice(sub * L, (sub + 1) * L)
            r, v, kk, lw, k, a = (ref[rows, :].astype(F32) for ref in refs)
            G = sum(jnp.dot(tri, part, preferred_element_type=F32) for part in _split_bf16(lw, 3))
            g_end = G[0:1] if reverse else G[L - 1:L]
            gam_end = jnp.exp(g_end)
            inv_gam = jnp.exp(-G)
            rel_end = jnp.exp(g_end - G)
            b = kk * a
            full = dict(At=-kk * jnp.exp(G - lw), Bt=b * inv_gam, Kt=k * inv_gam, Rt=r * jnp.exp(G),
                        Kh=k * rel_end, Bh=b * rel_end, v=v)
            pairs = []
            for p in range(n_pairs):
                sl = slice(p * LANES, (p + 1) * LANES)
                c = {name: val[:, sl] for name, val in full.items()}
                c.update(strict=strict, incl=incl, gam_end=gam_end[:, sl], rows=rows)
                pairs.append(c)
            steps.append(pairs)
        chains.append(steps)
    flat = [c for steps in chains for pairs in steps for c in pairs]

    for c in flat:
        a4 = _mm_nt(cat([c["At"], c["Rt"]], axis=0), cat([stack(c["Bt"]), stack(c["Kt"])], axis=0))
        c["A_ab"] = jnp.where(c["strict"], a4[:L, :LANES], 0.0)
        c["A_ak"] = jnp.where(c["strict"], a4[:L, LANES:], 0.0)
        c["A_rb"] = jnp.where(c["incl"], a4[L:, :LANES], 0.0)
        c["A_rk"] = jnp.where(c["incl"], a4[L:, LANES:], 0.0)
    for c in flat:
        c["A0"] = jnp.where(blk(8), c["A_ab"], 0.0)
        c["A2"] = mmp(c["A0"], c["A0"])
    for c in flat:
        c["A4"] = mmp(c["A2"], c["A2"])
        c["T"] = eye + c["A0"]
    for c in flat:
        c["T"] = c["T"] + mmp(c["T"], c["A2"])
    for c in flat:
        c["T"] = c["T"] + mmp(c["T"], c["A4"])
    s = 8
    while s < L:
        outer = blk(2 * s) & jnp.logical_not(blk(s))
        for c in flat:
            c["TO"] = mmp(c["T"], jnp.where(outer, c["A_ab"], 0.0))
        for c in flat:
            c["T"] = c["T"] + mmp(c["TO"], c["T"])
        s *= 2
    for c in flat:
        c["AkV"] = mmp(c["A_ak"], c["v"])
    for c in flat:
        wu = _mm(c["T"], cat([stack(c["At"]), stack(c["AkV"])], axis=1))
        c["W"], c["U0"] = wu[:, :LANES], wu[:, LANES:]
    for c in flat:
        kb_t = cat([c["Kh"], c["Bh"]], axis=0).T
        mn = _mm(kb_t, cat([cat([zeros_l, c["v"]], axis=1), cat([c["W"], c["U0"]], axis=1)], axis=0))
        c["M"] = jnp.where(same_head, mn[:, :LANES], 0.0) + jnp.where(diag2, c["gam_end"], 0.0)
        c["N"] = jnp.where(same_head, mn[:, LANES:], 0.0)
    for c in flat:
        qy = _mm(cat([c["A_rb"], c["A_rk"]], axis=1),
                 cat([cat([stack(c["W"]), stack(c["U0"])], axis=1),
                      cat([zeros_2l, stack(c["v"])], axis=1)], axis=0))
        c["Q"] = c["Rt"] + qy[:, :LANES]
        c["Y0"] = qy[:, LANES:]
    for d, (steps, y_ref) in enumerate(zip(chains, (yf_ref, yb_ref))):
        for pairs in steps:
            ys = []
            for p, c in enumerate(pairs):
                H = h_ref[d * n_pairs + p]
                ys.append(_mm(c["Q"], H) + c["Y0"])
                h_ref[d * n_pairs + p] = _mm(c["M"], H) + c["N"]
            y_ref[pairs[0]["rows"], :] = cat(ys, axis=1)


def _scan(r, v, kk, lw0, k0, a0, lw1, k1, a1, batch, seq):
    t = r.shape[0]
    rows = SCAN_CHUNKS_PER_STEP * CHUNK
    nc = seq // rows
    fwd = pl.BlockSpec((rows, D_A), lambda b, c: (b * nc + c, 0))
    bwd = pl.BlockSpec((rows, D_A), lambda b, c: (b * nc + nc - 1 - c, 0))
    out = jax.ShapeDtypeStruct((t, D_A), F32)
    return pl.pallas_call(
        _scan_body,
        grid=(batch, nc),
        in_specs=[fwd] * 6 + [bwd] * 6,
        out_specs=[fwd, bwd],
        out_shape=[out, out],
        scratch_shapes=[pltpu.VMEM((2 * (D_A // LANES), LANES, LANES), F32)],
        compiler_params=_params(2),
        name="rwkv_scan",
    )(r, v, kk, lw0, k0, a0, r, v, kk, lw1, k1, a1)


def _bias_features(slope, n, width, key_side):
    pos = lax.broadcasted_iota(jnp.int32, (n, width), 0).astype(F32)
    lane = lax.broadcasted_iota(jnp.int32, (n, width), 1)
    hi, mid, lo = (t.astype(F32) for t in _split_bf16(slope * pos, 3))
    terms = jnp.where(lane % 3 == 0, hi, jnp.where(lane % 3 == 1, mid, lo))
    if key_side:
        return jnp.where(lane < 3, -1.0, jnp.where(lane < 6, terms, 0.0))
    return jnp.where(lane < 3, terms, jnp.where(lane < 6, 1.0, 0.0))


def _attn_body(q_ref, k_ref, vt_ref, lv_ref, sg_ref, qg_ref, kg_ref, o_ref, qs_ref, feat_ref, src_ref,
               s_ref, mx_ref, p_ref, al_ref, m_ref, acc_ref, *, lam_init):
    tq = src_ref.shape[1]
    n_kv, _, tk = vt_ref.shape
    seq = n_kv * tk
    h = pl.program_id(1)
    i = pl.program_id(2)
    slope_bits = (127 - 2 * (h + 1)) << 23
    slope = (lax.bitcast_convert_type(jnp.full((1, 1), slope_bits, jnp.int32), jnp.float32).astype(F32)
             * LOG2E)
    bound = (HEAD_B * ROUNDING_MARGIN * jnp.max(jnp.abs(qg_ref[...]), axis=-1, keepdims=True)
             * jnp.max(jnp.abs(kg_ref[...]), axis=-1, keepdims=True))
    reach = jnp.minimum((F32_ZERO_EXP2 + 2.0 * bound) / slope + 1.0, float(seq)).astype(jnp.int32)[0, 0]

    n_st = tk // tq
    hw = 2 * HEAD_B

    @pl.when(i == 0)
    def _():
        kv_pos = lax.broadcasted_iota(jnp.int32, (tk, tq), 0)
        q_pos = lax.broadcasted_iota(jnp.int32, (tk, tq), 1)
        src_ref[...] = slope * (q_pos - kv_pos).astype(F32)
        key_feat = _bias_features(slope, tk, hw, key_side=True)
        feat_ref[0] = key_feat.astype(BF16)
        feat_ref[1] = (-key_feat).astype(BF16)
        feat_ref[2] = jnp.zeros((tk, hw), BF16)
        q_feat = _bias_features(slope, tq, hw, key_side=False).astype(BF16)
        for st in range(n_st):
            qs_ref[st, :tq, hw:] = q_feat
            qs_ref[st, tq:, hw:] = q_feat

    lane = lax.broadcasted_iota(jnp.int32, (tq, hw), 1)
    for st in range(n_st):
        q = q_ref[st * tq:(st + 1) * tq, :]
        zero = jnp.zeros_like(q)
        qs_ref[st, :tq, :hw] = jnp.where(lane < HEAD_B, q, zero)
        qs_ref[st, tq:, :hw] = jnp.where(lane < HEAD_B, zero, q)
    q0 = i * tk
    j_lo = jnp.maximum(q0 - reach, 0) // tk
    j_hi = jnp.minimum(q0 + tk - 1 + reach, seq - 1) // tk + 1
    n_left = i - j_lo
    n_off = n_left + (j_hi - i - 1)

    def scores(j, side, st):
        kb = k_ref[pl.ds(pl.multiple_of(j * tk, tk), tk), :]
        return lax.dot_general(jnp.concatenate([kb, feat_ref[side]], axis=1), qs_ref[st], _NT,
                               preferred_element_type=F32)

    def softmax_update(s, s_max, cst, st):
        m_old = m_ref[st]
        m_new = jnp.maximum(m_old, s_max + cst)
        alpha = jnp.exp2(m_old - m_new)
        p = jnp.exp2(s - (m_new - cst))
        m_ref[st] = m_new
        return alpha, p.astype(BF16)

    def block(n):
        j = jnp.clip(j_lo + n + (n >= n_left).astype(jnp.int32), 0, n_kv - 1)
        j = jnp.where(n < 0, i, j)
        return j, (j > i).astype(jnp.int32), jnp.logical_and(n >= 0, n < n_off)

    def qk_stage(n, slot):
        j, side, _ = block(n)
        for st in range(n_st):
            s = scores(j, side, st)
            s_ref[slot, st] = s
            mx_ref[slot, st] = jnp.max(s, axis=0, keepdims=True)

    def softmax_stage(n, slot):
        j, _, valid = block(n)
        for st in range(n_st):
            dist0 = jnp.abs(q0 + st * tq - j * tk).astype(F32)
            cst = jnp.where(valid, -slope * dist0, -1e30)
            alpha, p = softmax_update(s_ref[slot, st], mx_ref[slot, st], cst, st)
            al_ref[slot, st] = alpha
            p_ref[slot, st] = p

    def pv_stage(n, slot):
        j, _, _ = block(n)
        for st in range(n_st):
            acc_ref[st] = al_ref[slot, st] * acc_ref[st] + jnp.dot(vt_ref[j], p_ref[slot, st],
                                                                    preferred_element_type=F32)

    m_ref[...] = jnp.full(m_ref.shape, -1e30, F32)
    acc_ref[...] = jnp.zeros(acc_ref.shape, F32)
    diag_scores = [scores(i, 2, st) for st in range(n_st)]
    qk_stage(0, 0)
    src = src_ref[...]
    for st in range(n_st):
        bias = jnp.abs(src + slope * float(st * tq))
        s = diag_scores[st] - jnp.concatenate([bias, bias], axis=1)
        alpha, p = softmax_update(s, jnp.max(s, axis=0, keepdims=True), jnp.zeros((1, 1), F32), st)
        al_ref[1, st] = alpha
        p_ref[1, st] = p

    def pipelined_pair(t, carry):
        n = 2 * t
        qk_stage(n + 1, 1)
        softmax_stage(n, 0)
        pv_stage(n - 1, 1)
        qk_stage(n + 2, 0)
        softmax_stage(n + 1, 1)
        pv_stage(n, 0)
        return carry

    n_pairs = (n_off + 1) // 2
    lax.fori_loop(0, n_pairs, pipelined_pair, 0)
    pv_stage(2 * n_pairs - 1, 1)

    lv = lv_ref[...]
    lam = (jnp.exp(jnp.sum(lv[0:1] * lv[1:2], axis=-1, keepdims=True))
           - jnp.exp(jnp.sum(lv[2:3] * lv[3:4], axis=-1, keepdims=True)) + lam_init)
    for st in range(n_st):
        acc = acc_ref[st, :hw, :]
        l = acc_ref[st, hw:hw + 1, :]
        o_t = acc[:, :tq] / l[:, :tq] - lam * (acc[:, tq:] / l[:, tq:])
        o_t = o_t * lax.rsqrt(jnp.mean(o_t * o_t, axis=0, keepdims=True) + NORM_EPS)
        o_ref[st * tq:(st + 1) * tq, :] = (o_t.T * sg_ref[...] * (1.0 - lam_init)).astype(o_ref.dtype)


def _attn(q, k, vt, lam_vecs, sub_g, qg, kg, lam_init, batch, seq, layer):
    t = q.shape[0]
    tk = vt.shape[2]
    tq = ATTN_TQ
    n_st = tk // tq
    nq = seq // tk
    n_kv = seq // tk
    hw = 2 * HEAD_B
    return pl.pallas_call(
        functools.partial(_attn_body, lam_init=lam_init),
        grid=(batch, N_HEADS_B, nq),
        in_specs=[pl.BlockSpec((tk, hw), lambda b, h, i: (b * nq + i, h)),
                  pl.BlockSpec((seq, hw), lambda b, h, i: (b, h)),
                  pl.BlockSpec((n_kv, V_ROWS, tk), lambda b, h, i: (b, h, 0)),
                  _layer_spec((4, HEAD_B), layer), _layer_spec((1, hw), layer),
                  _layer_spec((1, D_B), layer), _layer_spec((1, D_B), layer)],
        out_specs=pl.BlockSpec((tk, hw), lambda b, h, i: (b * nq + i, h)),
        out_shape=jax.ShapeDtypeStruct((t, D_B), BF16),
        scratch_shapes=[pltpu.VMEM((n_st, 2 * tq, 2 * hw), BF16),
                        pltpu.VMEM((3, tk, hw), BF16),
                        pltpu.VMEM((tk, tq), F32),
                        pltpu.VMEM((2, n_st, tk, 2 * tq), F32),
                        pltpu.VMEM((2, n_st, 1, 2 * tq), F32),
                        pltpu.VMEM((2, n_st, tk, 2 * tq), BF16),
                        pltpu.VMEM((2, n_st, 1, 2 * tq), F32),
                        pltpu.VMEM((n_st, 1, 2 * tq), F32),
                        pltpu.VMEM((n_st, V_ROWS, 2 * tq), F32)],
        compiler_params=_params(3),
        name="diff_attn",
    )(q, k, vt, lam_vecs, sub_g, qg, kg)


def _merge_body(yf_ref, yb_ref, gate_ref, bonus_ref, ob_ref, gt_ref, x_ref, lng_ref, lnb_ref, bd_ref,
                wb0_ref, wb1_ref, wo_ref, o_ref):
    bd = bd_ref[...]
    y = yf_ref[...] + yb_ref[...]
    yc = y - _mm_split_lhs(y, bd)
    var = _mm_split_lhs(yc * yc, bd)
    yn = yc * lax.rsqrt(var + GN_EPS) * lng_ref[...] + lnb_ref[...]
    oa = (yn + bonus_ref[...]) * gate_ref[...]
    gts = gt_ref[...].astype(F32)
    merged = (gts[:, :D_MODEL] * _mm(oa, wb0_ref[...])
              + gts[:, D_MODEL:] * jnp.dot(ob_ref[...], wb1_ref[...], preferred_element_type=F32))
    o_ref[...] = x_ref[...] + _mm(merged, wo_ref[...])


def _merge(yf, yb, gate, bonus, ob, gts, x2, ln_g, ln_b, bd_mean, wb, wo, layer):
    t = x2.shape[0]
    tm = min(TOKEN_TILE, t)
    row = lambda width: pl.BlockSpec((tm, width), lambda i: (i, 0))
    return pl.pallas_call(
        _merge_body,
        grid=(t // tm,),
        in_specs=[row(D_A), row(D_A), row(D_A), row(D_A), row(D_B), row(2 * D_MODEL), row(D_MODEL),
                  _layer_spec((1, D_A), layer), _layer_spec((1, D_A), layer), _const_spec((D_A, D_A)),
                  _layer_spec((D_A, D_MODEL), layer, 0), _layer_spec((D_B, D_MODEL), layer, 1),
                  _layer_spec((D_MODEL, D_MODEL), layer)],
        out_specs=row(D_MODEL),
        out_shape=jax.ShapeDtypeStruct((t, D_MODEL), F32),
        compiler_params=_params(1),
        name="merge",
    )(yf, yb, gate, bonus, ob, gts, x2, ln_g, ln_b, bd_mean, wb, wb, wo)


def _block_diag2(w):
    z = jnp.zeros_like(w[:, 0])
    return jnp.concatenate([jnp.concatenate([w[:, 0], z], axis=2), jnp.concatenate([z, w[:, 1]], axis=2)],
                           axis=1)


def kernel(x, norm_ffn1, ffn1_in, ffn1_out, norm_mix, w_in, rwkv_mu, decay_w0, decay_w2, iclr_a0, iclr_a2,
           gate_g2, k_k, k_a, r_k, ln_x_g, ln_x_b, q_gain, k_gain, diff_lambda, subln_g, w_branch, w_out,
           norm_ffn2, ffn2_in, ffn2_out):
    batch, seq, d = x.shape
    depth = norm_ffn1.shape[0]
    assert seq % TOKEN_TILE == 0 and d == D_MODEL
    x2 = x.reshape(batch * seq, d)
    head_id = jnp.arange(D_A) // HEAD_A
    same = head_id[:, None] == head_id[None, :]
    bd_sum = same.astype(BF16)
    bd_mean = (same.astype(F32) / HEAD_A).astype(BF16)
    rows = lambda a: a.reshape(depth, 1, -1)
    bf = lambda a: a.astype(BF16)
    n_qk = D_B // HEAD_B
    qg = rows(jnp.tile(q_gain, (1, n_qk))) * (HEAD_B ** -0.5 * LOG2E)
    kg = rows(jnp.tile(k_gain, (1, n_qk)))
    ffn1 = (rows(norm_ffn1), bf(ffn1_in), bf(ffn1_out))
    ffn2 = (rows(norm_ffn2), bf(ffn2_in), bf(ffn2_out))
    proj_w = (rows(norm_mix), bf(w_in))
    prep_w = (rows(rwkv_mu), rows(decay_w0), bf(_block_diag2(decay_w2)), rows(iclr_a0),
              bf(_block_diag2(iclr_a2)), bf(gate_g2), rows(k_k), rows(k_a), rows(r_k))
    merge_w = (rows(ln_x_g), rows(ln_x_b), bd_mean, bf(w_branch), bf(w_out))
    sub_g = rows(subln_g)
    for l in range(depth):
        lam_init = 0.8 - 0.6 * math.exp(-0.3 * l)
        x2 = _ffn(x2, *ffn1, l)
        prw, q, k, vt, gts = _proj(x2, *proj_w, qg, kg, bd_mean, l)
        r, vv, kk, lw0, k0, a0, lw1, k1, a1, gate, bonus = _prep(prw, seq, *prep_w, bd_sum, l)
        yf, yb = _scan(r, vv, kk, lw0, k0, a0, lw1, k1, a1, batch, seq)
        ob = _attn(q, k, vt, diff_lambda, sub_g, qg, kg, lam_init, batch, seq, l)
        x2 = _merge(yf, yb, gate, bonus, ob, gts, x2, *merge_w, l)
        x2 = _ffn(x2, *ffn2, l)
    return x2.reshape(batch, seq, d)
```

```python
import functools
import math

import jax
import jax.numpy as jnp
from jax import lax
from jax.experimental import pallas as pl
from jax.experimental.pallas import tpu as pltpu

F32 = jnp.float32
BF16 = jnp.bfloat16

D_MODEL = 1024
D_A = 512
HEAD_A = 64
LORA = 64
LORA_GATE = 128
C_RWKV = 3 * D_A + 2 * LORA + 2 * LORA + LORA_GATE
D_B = 512
HEAD_B = 64
N_HEADS_B = D_B // (2 * HEAD_B)
D_FF = 2816
DECAY_SCALE = math.exp(-0.5)
GN_EPS = 64e-5
NORM_EPS = 1e-6
LOG2E = math.log2(math.e)
F32_ZERO_EXP2 = 150.0
ROUNDING_MARGIN = 1.02

LANES = 128
SUBLANES = 8
VMEM_LIMIT = 56 * 1024 * 1024

TOKEN_TILE = 512
MXU_WIDTH = 256
FF_SPLITS = (0, 6 * MXU_WIDTH, D_FF)
CHUNK = 64
SCAN_CHUNKS_PER_STEP = 2
ATTN_TQ = 256
V_ROWS = 2 * HEAD_B + 16

_NT = (((1,), (1,)), ((), ()))


def _mm(a, b):
    return jnp.dot(a.astype(BF16), b.astype(BF16), preferred_element_type=F32)


def _mm_nt(a, b):
    return lax.dot_general(a.astype(BF16), b.astype(BF16), _NT, preferred_element_type=F32)


def _split_bf16(x, n):
    parts = []
    for _ in range(n - 1):
        hi = x.astype(BF16)
        parts.append(hi)
        x = x - hi.astype(F32)
    parts.append(x.astype(BF16))
    return parts


def _mm_split_lhs(x, w, n=2):
    return sum(jnp.dot(p, w, preferred_element_type=F32) for p in _split_bf16(x, n))


def _rms_norm(x, g):
    return x * lax.rsqrt(jnp.mean(x * x, axis=-1, keepdims=True) + NORM_EPS) * g


def _const_spec(shape):
    return pl.BlockSpec(shape, lambda *_: (0,) * len(shape), pipeline_mode=pl.Buffered(1))


def _layer_spec(shape, layer, *lead):
    index = (layer,) + lead + (0,) * len(shape)
    return pl.BlockSpec((None,) * (1 + len(lead)) + shape, lambda *_: index, pipeline_mode=pl.Buffered(1))


def _params(n_axes):
    return pltpu.CompilerParams(dimension_semantics=("arbitrary",) * n_axes,
                                vmem_limit_bytes=VMEM_LIMIT)


def _ffn_body(x_ref, g_ref, win_ref, wout_ref, o_ref):
    x = x_ref[...]
    h = _rms_norm(x, g_ref[...]).astype(BF16)
    acc = jnp.zeros(x.shape, F32)
    for lo, hi in zip(FF_SPLITS[:-1], FF_SPLITS[1:]):
        gate = jnp.dot(h, win_ref[:, lo:hi], preferred_element_type=F32)
        up = jnp.dot(h, win_ref[:, D_FF + lo:D_FF + hi], preferred_element_type=F32)
        act = (gate * jax.nn.sigmoid(gate) * up).astype(BF16)
        acc = acc + jnp.dot(act, wout_ref[lo:hi, :], preferred_element_type=F32)
    o_ref[...] = x + 0.5 * acc


def _ffn(x2, g, w_in, w_out, layer):
    t = x2.shape[0]
    tm = min(TOKEN_TILE, t)
    return pl.pallas_call(
        _ffn_body,
        grid=(t // tm,),
        in_specs=[pl.BlockSpec((tm, D_MODEL), lambda i: (i, 0)),
                  _layer_spec((1, D_MODEL), layer),
                  _layer_spec((D_MODEL, 2 * D_FF), layer),
                  _layer_spec((D_FF, D_MODEL), layer)],
        out_specs=pl.BlockSpec((tm, D_MODEL), lambda i: (i, 0)),
        out_shape=jax.ShapeDtypeStruct((t, D_MODEL), F32),
        compiler_params=_params(1),
        name="ffn",
    )(x2, g, w_in, w_out)


_Q0 = C_RWKV
_K0 = C_RWKV + D_B
_V0 = C_RWKV + 2 * D_B
_G0 = C_RWKV + 3 * D_B


def _proj_body(x_ref, g_ref, w_ref, qg_ref, kg_ref, bd_ref, prw_ref, q_ref, k_ref, vt_ref, gt_ref):
    h = _rms_norm(x_ref[...], g_ref[...]).astype(BF16)
    prw_ref[...] = jnp.dot(h, w_ref[:, :C_RWKV], preferred_element_type=F32)
    bd = bd_ref[...]
    q = jnp.dot(h, w_ref[:, _Q0:_K0], preferred_element_type=F32)
    q_ref[...] = (q * lax.rsqrt(_mm_split_lhs(q * q, bd) + NORM_EPS) * qg_ref[...]).astype(BF16)
    k = jnp.dot(h, w_ref[:, _K0:_V0], preferred_element_type=F32)
    k_ref[...] = (k * lax.rsqrt(_mm_split_lhs(k * k, bd) + NORM_EPS) * kg_ref[...]).astype(BF16)
    vt = jnp.dot(h, w_ref[:, _V0:_G0], preferred_element_type=F32).T.astype(BF16)
    hw = 2 * HEAD_B
    ones = jnp.ones((V_ROWS - hw, vt.shape[1]), BF16)
    vt_ref[0] = jnp.concatenate([part for hd in range(N_HEADS_B)
                                 for part in (vt[hd * hw:(hd + 1) * hw], ones)], axis=0)
    gt_ref[...] = jax.nn.sigmoid(jnp.dot(h, w_ref[:, _G0:], preferred_element_type=F32)).astype(BF16)


def _proj(x2, g, w, qg, kg, bd_mean, layer):
    t = x2.shape[0]
    tm = TOKEN_TILE
    c_in = w.shape[2]
    row = lambda width: pl.BlockSpec((tm, width), lambda i: (i, 0))
    return pl.pallas_call(
        _proj_body,
        grid=(t // tm,),
        in_specs=[row(D_MODEL), _layer_spec((1, D_MODEL), layer), _layer_spec((D_MODEL, c_in), layer),
                  _layer_spec((1, D_B), layer), _layer_spec((1, D_B), layer), _const_spec((D_B, D_B))],
        out_specs=[row(C_RWKV), row(D_B), row(D_B),
                   pl.BlockSpec((1, N_HEADS_B * V_ROWS, tm), lambda i: (i, 0, 0)), row(2 * D_MODEL)],
        out_shape=[jax.ShapeDtypeStruct((t, C_RWKV), F32),
                   jax.ShapeDtypeStruct((t, D_B), BF16),
                   jax.ShapeDtypeStruct((t, D_B), BF16),
                   jax.ShapeDtypeStruct((t // tm, N_HEADS_B * V_ROWS, tm), BF16),
                   jax.ShapeDtypeStruct((t, 2 * D_MODEL), BF16)],
        compiler_params=_params(1),
        name="proj",
    )(x2, g, w, qg, kg, bd_mean)


def _prep_body(p_ref, pp_ref, pn_ref, mu_ref, w0_ref, w2_ref, a0_ref, a2_ref, g2_ref, kk_ref, ka_ref,
               rk_ref, bd_ref,
               r_out, v_out, kk_out, lw0_out, k0_out, a0_out, lw1_out, k1_out, a1_out, gate_out,
               bonus_out, *, seq):
    i = pl.program_id(0)
    p = p_ref[...]
    tm = p.shape[0]
    row = lax.broadcasted_iota(jnp.int32, p.shape, 0)
    keep_prev = jnp.where((i * tm) % seq == 0, 0.0, 1.0)
    keep_next = jnp.where(((i + 1) * tm) % seq == 0, 0.0, 1.0)
    prev = jnp.where(row == 0, keep_prev * pp_ref[SUBLANES - 1:SUBLANES, :], pltpu.roll(p, 1, 0))
    nxt = jnp.where(row == tm - 1, keep_next * pn_ref[0:1, :], pltpu.roll(p, tm - 1, 0))
    p = p + mu_ref[...] * (0.5 * (prev + nxt) - p)

    r = p[:, :D_A]
    k = p[:, D_A:2 * D_A]
    v = p[:, 2 * D_A:3 * D_A]
    dw = p[:, 3 * D_A:3 * D_A + 2 * LORA]
    da = p[:, 3 * D_A + 2 * LORA:3 * D_A + 4 * LORA]
    dg = p[:, 3 * D_A + 4 * LORA:]
    bd = bd_ref[...]

    logw = -DECAY_SCALE * jax.nn.sigmoid(w0_ref[...] + _mm(jnp.tanh(dw), w2_ref[...]))
    iclr = jax.nn.sigmoid(a0_ref[...] + _mm(da, a2_ref[...]))
    gate_out[...] = _mm(jax.nn.sigmoid(dg), g2_ref[...])

    kk = k * kk_ref[...]
    norm = jnp.sqrt(_mm_split_lhs(kk * kk, bd))
    kk_out[...] = (kk / jnp.maximum(norm, 1e-12)).astype(kk_out.dtype)

    ka = ka_ref[...]
    a_f = iclr[:, :D_A]
    a_b = iclr[:, D_A:]
    k_f = k * (1.0 + (a_f - 1.0) * ka)
    k_b = k * (1.0 + (a_b - 1.0) * ka)
    bonus = _mm_split_lhs(r * rk_ref[...] * (k_f + k_b), bd)
    r_out[...] = r.astype(r_out.dtype)
    v_out[...] = v.astype(v_out.dtype)
    lw0_out[...] = logw[:, :D_A]
    k0_out[...] = k_f.astype(k0_out.dtype)
    a0_out[...] = a_f.astype(a0_out.dtype)
    lw1_out[...] = logw[:, D_A:]
    k1_out[...] = k_b.astype(k1_out.dtype)
    a1_out[...] = a_b.astype(a1_out.dtype)
    bonus_out[...] = bonus * v


def _prep(prw, seq, mu, w0, w2cat, a0, a2cat, g2, k_k, k_a, r_k, bd_sum, layer):
    t = prw.shape[0]
    tm = min(TOKEN_TILE, seq)
    nb = tm // SUBLANES
    last = t // SUBLANES - 1
    row = lambda width: pl.BlockSpec((tm, width), lambda i: (i, 0))
    wide = jax.ShapeDtypeStruct((t, D_A), F32)
    half = jax.ShapeDtypeStruct((t, D_A), BF16)
    return pl.pallas_call(
        functools.partial(_prep_body, seq=seq),
        grid=(t // tm,),
        in_specs=[row(C_RWKV),
                  pl.BlockSpec((SUBLANES, C_RWKV), lambda i: (jnp.maximum(i * nb - 1, 0), 0)),
                  pl.BlockSpec((SUBLANES, C_RWKV), lambda i: (jnp.minimum((i + 1) * nb, last), 0)),
                  _layer_spec((1, C_RWKV), layer),
                  _layer_spec((1, 2 * D_A), layer), _layer_spec((2 * LORA, 2 * D_A), layer),
                  _layer_spec((1, 2 * D_A), layer), _layer_spec((2 * LORA, 2 * D_A), layer),
                  _layer_spec((LORA_GATE, D_A), layer),
                  _layer_spec((1, D_A), layer), _layer_spec((1, D_A), layer), _layer_spec((1, D_A), layer),
                  _const_spec((D_A, D_A))],
        out_specs=[row(D_A)] * 11,
        out_shape=[half, half, half, wide, half, half, wide, half, half, wide, wide],
        compiler_params=_params(1),
        name="rwkv_prep",
    )(prw, prw, prw, mu, w0, w2cat, a0, a2cat, g2, k_k, k_a, r_k, bd_sum)


def _scan_body(rf, vf, kkf, lwf, kf, af, rb, vb, kkb, lwb, kb, ab, yf_ref, yb_ref, h_ref):
    @pl.when(pl.program_id(1) == 0)
    def _():
        h_ref[...] = jnp.zeros(h_ref.shape, F32)

    L = CHUNK
    n_sub = rf.shape[0] // L
    n_pairs = D_A // LANES
    ti = lax.broadcasted_iota(jnp.int32, (L, L), 0)
    tj = lax.broadcasted_iota(jnp.int32, (L, L), 1)
    pi = lax.broadcasted_iota(jnp.int32, (L, LANES), 0)
    lane = lax.broadcasted_iota(jnp.int32, (L, LANES), 1)
    pj = lane & (HEAD_A - 1)
    lane_lo = lane < HEAD_A
    eye = jnp.where(pj == pi, 1.0, 0.0)
    blk = lambda s: (pi // s) == (pj // s)
    row2 = lax.broadcasted_iota(jnp.int32, (2 * L, LANES), 0)
    col2 = lax.broadcasted_iota(jnp.int32, (2 * L, LANES), 1)
    same_head = (row2 // HEAD_A) == (col2 // HEAD_A)
    diag2 = row2 == col2
    zeros_l = jnp.zeros((L, LANES), F32)
    zeros_2l = jnp.zeros((2 * L, LANES), F32)

    def stack(x):
        return jnp.concatenate([jnp.where(lane_lo, x, 0.0), jnp.where(lane_lo, 0.0, x)], axis=0)

    def mmp(x, y):
        return _mm(x, stack(y))

    cat = jnp.concatenate
    chains = []
    for reverse, refs in ((False, (rf, vf, kkf, lwf, kf, af)), (True, (rb, vb, kkb, lwb, kb, ab))):
        tri = ((tj >= ti) if reverse else (tj <= ti)).astype(BF16)
        strict = (pj > pi) if reverse else (pj < pi)
        incl = (pj >= pi) if reverse else (pj <= pi)
        steps = []
        for sub in (range(n_sub - 1, -1, -1) if reverse else range(n_sub)):
            rows = slice(sub * L, (sub + 1) * L)
            r, v, kk, lw, k, a = (ref[rows, :].astype(F32) for ref in refs)
            G = sum(jnp.dot(tri, part, preferred_element_type=F32) for part in _split_bf16(lw, 3))
            g_end = G[0:1] if reverse else G[L - 1:L]
            gam_end = jnp.exp(g_end)
            inv_gam = jnp.exp(-G)
            rel_end = jnp.exp(g_end - G)
            b = kk * a
            full = dict(At=-kk * jnp.exp(G - lw), Bt=b * inv_gam, Kt=k * inv_gam, Rt=r * jnp.exp(G),
                        Kh=k * rel_end, Bh=b * rel_end, v=v)
            pairs = []
            for p in range(n_pairs):
                sl = slice(p * LANES, (p + 1) * LANES)
                c = {name: val[:, sl] for name, val in full.items()}
                c.update(strict=strict, incl=incl, gam_end=gam_end[:, sl], rows=rows)
                pairs.append(c)
            steps.append(pairs)
        chains.append(steps)
    flat = [c for steps in chains for pairs in steps for c in pairs]

    for c in flat:
        a4 = _mm_nt(cat([c["At"], c["Rt"]], axis=0), cat([stack(c["Bt"]), stack(c["Kt"])], axis=0))
        c["A_ab"] = jnp.where(c["strict"], a4[:L, :LANES], 0.0)
        c["A_ak"] = jnp.where(c["strict"], a4[:L, LANES:], 0.0)
        c["A_rb"] = jnp.where(c["incl"], a4[L:, :LANES], 0.0)
        c["A_rk"] = jnp.where(c["incl"], a4[L:, LANES:], 0.0)
    for c in flat:
        c["A0"] = jnp.where(blk(8), c["A_ab"], 0.0)
        c["A2"] = mmp(c["A0"], c["A0"])
    for c in flat:
        c["A4"] = mmp(c["A2"], c["A2"])
        c["T"] = eye + c["A0"]
    for c in flat:
        c["T"] = c["T"] + mmp(c["T"], c["A2"])
    for c in flat:
        c["T"] = c["T"] + mmp(c["T"], c["A4"])
    s = 8
    while s < L:
        outer = blk(2 * s) & jnp.logical_not(blk(s))
        for c in flat:
            c["TO"] = mmp(c["T"], jnp.where(outer, c["A_ab"], 0.0))
        for c in flat:
            c["T"] = c["T"] + mmp(c["TO"], c["T"])
        s *= 2
    for c in flat:
        c["AkV"] = mmp(c["A_ak"], c["v"])
    for c in flat:
        wu = _mm(c["T"], cat([stack(c["At"]), stack(c["AkV"])], axis=1))
        c["W"], c["U0"] = wu[:, :LANES], wu[:, LANES:]
    for c in flat:
        kb_t = cat([c["Kh"], c["Bh"]], axis=0).T
        mn = _mm(kb_t, cat([cat([zeros_l, c["v"]], axis=1), cat([c["W"], c["U0"]], axis=1)], axis=0))
        c["M"] = jnp.where(same_head, mn[:, :LANES], 0.0) + jnp.where(diag2, c["gam_end"], 0.0)
        c["N"] = jnp.where(same_head, mn[:, LANES:], 0.0)
    for c in flat:
        qy = _mm(cat([c["A_rb"], c["A_rk"]], axis=1),
                 cat([cat([stack(c["W"]), stack(c["U0"])], axis=1),
                      cat([zeros_2l, stack(c["v"])], axis=1)], axis=0))
        c["Q"] = c["Rt"] + qy[:, :LANES]
        c["Y0"] = qy[:, LANES:]
    for d, (steps, y_ref) in enumerate(zip(chains, (yf_ref, yb_ref))):
        for pairs in steps:
            ys = []
            for p, c in enumerate(pairs):
                H = h_ref[d * n_pairs + p]
                ys.append(_mm(c["Q"], H) + c["Y0"])
                h_ref[d * n_pairs + p] = _mm(c["M"], H) + c["N"]
            y_ref[pairs[0]["rows"], :] = cat(ys, axis=1)


def _scan(r, v, kk, lw0, k0, a0, lw1, k1, a1, batch, seq):
    t = r.shape[0]
    rows = SCAN_CHUNKS_PER_STEP * CHUNK
    nc = seq // rows
    fwd = pl.BlockSpec((rows, D_A), lambda b, c: (b * nc + c, 0))
    bwd = pl.BlockSpec((rows, D_A), lambda b, c: (b * nc + nc - 1 - c, 0))
    out = jax.ShapeDtypeStruct((t, D_A), F32)
    return pl.pallas_call(
        _scan_body,
        grid=(batch, nc),
        in_specs=[fwd] * 6 + [bwd] * 6,
        out_specs=[fwd, bwd],
        out_shape=[out, out],
        scratch_shapes=[pltpu.VMEM((2 * (D_A // LANES), LANES, LANES), F32)],
        compiler_params=_params(2),
        name="rwkv_scan",
    )(r, v, kk, lw0, k0, a0, r, v, kk, lw1, k1, a1)


def _bias_features(slope, n, width, key_side):
    pos = lax.broadcasted_iota(jnp.int32, (n, width), 0).astype(F32)
    lane = lax.broadcasted_iota(jnp.int32, (n, width), 1)
    hi, mid, lo = (t.astype(F32) for t in _split_bf16(slope * pos, 3))
    terms = jnp.where(lane % 3 == 0, hi, jnp.where(lane % 3 == 1, mid, lo))
    if key_side:
        return jnp.where(lane < 3, -1.0, jnp.where(lane < 6, terms, 0.0))
    return jnp.where(lane < 3, terms, jnp.where(lane < 6, 1.0, 0.0))


def _attn_body(q_ref, k_ref, vt_ref, lv_ref, sg_ref, qg_ref, kg_ref, o_ref, qs_ref, feat_ref, src_ref,
               s_ref, mx_ref, p_ref, al_ref, m_ref, acc_ref, *, lam_init):
    tq = src_ref.shape[1]
    n_kv, _, tk = vt_ref.shape
    seq = n_kv * tk
    h = pl.program_id(1)
    i = pl.program_id(2)
    slope_bits = (127 - 2 * (h + 1)) << 23
    slope = (lax.bitcast_convert_type(jnp.full((1, 1), slope_bits, jnp.int32), jnp.float32).astype(F32)
             * LOG2E)
    bound = (HEAD_B * ROUNDING_MARGIN * jnp.max(jnp.abs(qg_ref[...]), axis=-1, keepdims=True)
             * jnp.max(jnp.abs(kg_ref[...]), axis=-1, keepdims=True))
    reach = jnp.minimum((F32_ZERO_EXP2 + 2.0 * bound) / slope + 1.0, float(seq)).astype(jnp.int32)[0, 0]

    n_st = tk // tq
    hw = 2 * HEAD_B

    @pl.when(i == 0)
    def _():
        kv_pos = lax.broadcasted_iota(jnp.int32, (tk, tq), 0)
        q_pos = lax.broadcasted_iota(jnp.int32, (tk, tq), 1)
        src_ref[...] = slope * (q_pos - kv_pos).astype(F32)
        key_feat = _bias_features(slope, tk, hw, key_side=True)
        feat_ref[0] = key_feat.astype(BF16)
        feat_ref[1] = (-key_feat).astype(BF16)
        feat_ref[2] = jnp.zeros((tk, hw), BF16)
        q_feat = _bias_features(slope, tq, hw, key_side=False).T.astype(BF16)
        for st in range(n_st):
            qs_ref[st, hw:, :tq] = q_feat
            qs_ref[st, hw:, tq:] = q_feat

    chan = lax.broadcasted_iota(jnp.int32, (hw, tq), 0)
    for st in range(n_st):
        q_t = q_ref[st * tq:(st + 1) * tq, :].astype(F32).T.astype(BF16)
        zero = jnp.zeros_like(q_t)
        qs_ref[st, :hw, :tq] = jnp.where(chan < HEAD_B, q_t, zero)
        qs_ref[st, :hw, tq:] = jnp.where(chan < HEAD_B, zero, q_t)
    q0 = i * tk
    j_lo = jnp.maximum(q0 - reach, 0) // tk
    j_hi = jnp.minimum(q0 + tk - 1 + reach, seq - 1) // tk + 1
    n_left = i - j_lo
    n_off = n_left + (j_hi - i - 1)

    def scores(j, side, st):
        kb = k_ref[pl.ds(pl.multiple_of(j * tk, tk), tk), :]
        return jnp.dot(jnp.concatenate([kb, feat_ref[side]], axis=1), qs_ref[st],
                       preferred_element_type=F32)

    def softmax_update(s, s_max, cst, st):
        m_old = m_ref[st]
        m_new = jnp.maximum(m_old, s_max + cst)
        alpha = jnp.exp2(m_old - m_new)
        p = jnp.exp2(s - (m_new - cst))
        m_ref[st] = m_new
        return alpha, p.astype(BF16)

    def block(n):
        j = jnp.clip(j_lo + n + (n >= n_left).astype(jnp.int32), 0, n_kv - 1)
        j = jnp.where(n < 0, i, j)
        return j, (j > i).astype(jnp.int32), jnp.logical_and(n >= 0, n < n_off)

    def qk_stage(n, slot):
        j, side, _ = block(n)
        for st in range(n_st):
            s = scores(j, side, st)
            s_ref[slot, st] = s
            mx_ref[slot, st] = jnp.max(s, axis=0, keepdims=True)

    def softmax_stage(n, slot):
        j, _, valid = block(n)
        for st in range(n_st):
            dist0 = jnp.abs(q0 + st * tq - j * tk).astype(F32)
            cst = jnp.where(valid, -slope * dist0, -1e30)
            alpha, p = softmax_update(s_ref[slot, st], mx_ref[slot, st], cst, st)
            al_ref[slot, st] = alpha
            p_ref[slot, st] = p

    def pv_stage(n, slot):
        j, _, _ = block(n)
        for st in range(n_st):
            acc_ref[st] = al_ref[slot, st] * acc_ref[st] + jnp.dot(vt_ref[j], p_ref[slot, st],
                                                                    preferred_element_type=F32)

    m_ref[...] = jnp.full(m_ref.shape, -1e30, F32)
    acc_ref[...] = jnp.zeros(acc_ref.shape, F32)
    diag_scores = [scores(i, 2, st) for st in range(n_st)]
    qk_stage(0, 0)
    src = src_ref[...]
    for st in range(n_st):
        bias = jnp.abs(src + slope * float(st * tq))
        s = diag_scores[st] - jnp.concatenate([bias, bias], axis=1)
        alpha, p = softmax_update(s, jnp.max(s, axis=0, keepdims=True), jnp.zeros((1, 1), F32), st)
        al_ref[1, st] = alpha
        p_ref[1, st] = p

    def pipelined_pair(t, carry):
        n = 2 * t
        qk_stage(n + 1, 1)
        softmax_stage(n, 0)
        pv_stage(n - 1, 1)
        qk_stage(n + 2, 0)
        softmax_stage(n + 1, 1)
        pv_stage(n, 0)
        return carry

    n_pairs = (n_off + 1) // 2
    lax.fori_loop(0, n_pairs, pipelined_pair, 0)
    pv_stage(2 * n_pairs - 1, 1)

    lv = lv_ref[...]
    lam = (jnp.exp(jnp.sum(lv[0:1] * lv[1:2], axis=-1, keepdims=True))
           - jnp.exp(jnp.sum(lv[2:3] * lv[3:4], axis=-1, keepdims=True)) + lam_init)
    for st in range(n_st):
        acc = acc_ref[st, :hw, :]
        l = acc_ref[st, hw:hw + 1, :]
        o_t = acc[:, :tq] / l[:, :tq] - lam * (acc[:, tq:] / l[:, tq:])
        o_t = o_t * lax.rsqrt(jnp.mean(o_t * o_t, axis=0, keepdims=True) + NORM_EPS)
        o_ref[st * tq:(st + 1) * tq, :] = (o_t.T * sg_ref[...] * (1.0 - lam_init)).astype(o_ref.dtype)


def _attn(q, k, vt, lam_vecs, sub_g, qg, kg, lam_init, batch, seq, layer):
    t = q.shape[0]
    tk = vt.shape[2]
    tq = ATTN_TQ
    n_st = tk // tq
    nq = seq // tk
    n_kv = seq // tk
    hw = 2 * HEAD_B
    return pl.pallas_call(
        functools.partial(_attn_body, lam_init=lam_init),
        grid=(batch, N_HEADS_B, nq),
        in_specs=[pl.BlockSpec((tk, hw), lambda b, h, i: (b * nq + i, h)),
                  pl.BlockSpec((seq, hw), lambda b, h, i: (b, h)),
                  pl.BlockSpec((n_kv, V_ROWS, tk), lambda b, h, i: (b, h, 0)),
                  _layer_spec((4, HEAD_B), layer), _layer_spec((1, hw), layer),
                  _layer_spec((1, D_B), layer), _layer_spec((1, D_B), layer)],
        out_specs=pl.BlockSpec((tk, hw), lambda b, h, i: (b * nq + i, h)),
        out_shape=jax.ShapeDtypeStruct((t, D_B), BF16),
        scratch_shapes=[pltpu.VMEM((n_st, 2 * hw, 2 * tq), BF16),
                        pltpu.VMEM((3, tk, hw), BF16),
                        pltpu.VMEM((tk, tq), F32),
                        pltpu.VMEM((2, n_st, tk, 2 * tq), F32),
                        pltpu.VMEM((2, n_st, 1, 2 * tq), F32),
                        pltpu.VMEM((2, n_st, tk, 2 * tq), BF16),
                        pltpu.VMEM((2, n_st, 1, 2 * tq), F32),
                        pltpu.VMEM((n_st, 1, 2 * tq), F32),
                        pltpu.VMEM((n_st, V_ROWS, 2 * tq), F32)],
        compiler_params=_params(3),
        name="diff_attn",
    )(q, k, vt, lam_vecs, sub_g, qg, kg)


def _merge_body(yf_ref, yb_ref, gate_ref, bonus_ref, ob_ref, gt_ref, x_ref, lng_ref, lnb_ref, bd_ref,
                wb0_ref, wb1_ref, wo_ref, o_ref):
    bd = bd_ref[...]
    y = yf_ref[...] + yb_ref[...]
    yc = y - _mm_split_lhs(y, bd)
    var = _mm_split_lhs(yc * yc, bd)
    yn = yc * lax.rsqrt(var + GN_EPS) * lng_ref[...] + lnb_ref[...]
    oa = (yn + bonus_ref[...]) * gate_ref[...]
    gts = gt_ref[...].astype(F32)
    merged = (gts[:, :D_MODEL] * _mm(oa, wb0_ref[...])
              + gts[:, D_MODEL:] * jnp.dot(ob_ref[...], wb1_ref[...], preferred_element_type=F32))
    o_ref[...] = x_ref[...] + _mm(merged, wo_ref[...])


def _merge(yf, yb, gate, bonus, ob, gts, x2, ln_g, ln_b, bd_mean, wb, wo, layer):
    t = x2.shape[0]
    tm = min(TOKEN_TILE, t)
    row = lambda width: pl.BlockSpec((tm, width), lambda i: (i, 0))
    return pl.pallas_call(
        _merge_body,
        grid=(t // tm,),
        in_specs=[row(D_A), row(D_A), row(D_A), row(D_A), row(D_B), row(2 * D_MODEL), row(D_MODEL),
                  _layer_spec((1, D_A), layer), _layer_spec((1, D_A), layer), _const_spec((D_A, D_A)),
                  _layer_spec((D_A, D_MODEL), layer, 0), _layer_spec((D_B, D_MODEL), layer, 1),
                  _layer_spec((D_MODEL, D_MODEL), layer)],
        out_specs=row(D_MODEL),
        out_shape=jax.ShapeDtypeStruct((t, D_MODEL), F32),
        compiler_params=_params(1),
        name="merge",
    )(yf, yb, gate, bonus, ob, gts, x2, ln_g, ln_b, bd_mean, wb, wb, wo)


def _block_diag2(w):
    z = jnp.zeros_like(w[:, 0])
    return jnp.concatenate([jnp.concatenate([w[:, 0], z], axis=2), jnp.concatenate([z, w[:, 1]], axis=2)],
                           axis=1)


def kernel(x, norm_ffn1, ffn1_in, ffn1_out, norm_mix, w_in, rwkv_mu, decay_w0, decay_w2, iclr_a0, iclr_a2,
           gate_g2, k_k, k_a, r_k, ln_x_g, ln_x_b, q_gain, k_gain, diff_lambda, subln_g, w_branch, w_out,
           norm_ffn2, ffn2_in, ffn2_out):
    batch, seq, d = x.shape
    depth = norm_ffn1.shape[0]
    assert seq % TOKEN_TILE == 0 and d == D_MODEL
    x2 = x.reshape(batch * seq, d)
    head_id = jnp.arange(D_A) // HEAD_A
    same = head_id[:, None] == head_id[None, :]
    bd_sum = same.astype(BF16)
    bd_mean = (same.astype(F32) / HEAD_A).astype(BF16)
    rows = lambda a: a.reshape(depth, 1, -1)
    bf = lambda a: a.astype(BF16)
    n_qk = D_B // HEAD_B
    qg = rows(jnp.tile(q_gain, (1, n_qk))) * (HEAD_B ** -0.5 * LOG2E)
    kg = rows(jnp.tile(k_gain, (1, n_qk)))
    ffn1 = (rows(norm_ffn1), bf(ffn1_in), bf(ffn1_out))
    ffn2 = (rows(norm_ffn2), bf(ffn2_in), bf(ffn2_out))
    proj_w = (rows(norm_mix), bf(w_in))
    prep_w = (rows(rwkv_mu), rows(decay_w0), bf(_block_diag2(decay_w2)), rows(iclr_a0),
              bf(_block_diag2(iclr_a2)), bf(gate_g2), rows(k_k), rows(k_a), rows(r_k))
    merge_w = (rows(ln_x_g), rows(ln_x_b), bd_mean, bf(w_branch), bf(w_out))
    sub_g = rows(subln_g)
    for l in range(depth):
        lam_init = 0.8 - 0.6 * math.exp(-0.3 * l)
        x2 = _ffn(x2, *ffn1, l)
        prw, q, k, vt, gts = _proj(x2, *proj_w, qg, kg, bd_mean, l)
        r, vv, kk, lw0, k0, a0, lw1, k1, a1, gate, bonus = _prep(prw, seq, *prep_w, bd_sum, l)
        yf, yb = _scan(r, vv, kk, lw0, k0, a0, lw1, k1, a1, batch, seq)
        ob = _attn(q, k, vt, diff_lambda, sub_g, qg, kg, lam_init, batch, seq, l)
        x2 = _merge(yf, yb, gate, bonus, ob, gts, x2, *merge_w, l)
        x2 = _ffn(x2, *ffn2, l)
    return x2.reshape(batch, seq, d)
```

```python
import functools
import math

import jax
import jax.numpy as jnp
from jax import lax
from jax.experimental import pallas as pl
from jax.experimental.pallas import tpu as pltpu

F32 = jnp.float32
BF16 = jnp.bfloat16

D_MODEL = 1024
D_A = 512
HEAD_A = 64
LORA = 64
LORA_GATE = 128
C_RWKV = 3 * D_A + 2 * LORA + 2 * LORA + LORA_GATE
D_B = 512
HEAD_B = 64
N_HEADS_B = D_B // (2 * HEAD_B)
D_FF = 2816
DECAY_SCALE = math.exp(-0.5)
GN_EPS = 64e-5
NORM_EPS = 1e-6
LOG2E = math.log2(math.e)
F32_ZERO_EXP2 = 150.0
ROUNDING_MARGIN = 1.02

LANES = 128
SUBLANES = 8
VMEM_LIMIT = 56 * 1024 * 1024

TOKEN_TILE = 512
MXU_WIDTH = 256
FFN_TOKEN_TILE = 1024
FF_SPLITS = (0, 4 * MXU_WIDTH, 8 * MXU_WIDTH, D_FF)
CHUNK = 64
SCAN_CHUNKS_PER_STEP = 2
ATTN_TQ = 256
V_ROWS = 2 * HEAD_B + 16

_NT = (((1,), (1,)), ((), ()))


def _mm(a, b):
    return jnp.dot(a.astype(BF16), b.astype(BF16), preferred_element_type=F32)


def _mm_nt(a, b):
    return lax.dot_general(a.astype(BF16), b.astype(BF16), _NT, preferred_element_type=F32)


def _split_bf16(x, n):
    parts = []
    for _ in range(n - 1):
        hi = x.astype(BF16)
        parts.append(hi)
        x = x - hi.astype(F32)
    parts.append(x.astype(BF16))
    return parts


def _mm_split_lhs(x, w, n=2):
    return sum(jnp.dot(p, w, preferred_element_type=F32) for p in _split_bf16(x, n))


def _rms_norm(x, g):
    return x * lax.rsqrt(jnp.mean(x * x, axis=-1, keepdims=True) + NORM_EPS) * g


def _const_spec(shape):
    return pl.BlockSpec(shape, lambda *_: (0,) * len(shape), pipeline_mode=pl.Buffered(1))


def _layer_spec(shape, layer, *lead):
    index = (layer,) + lead + (0,) * len(shape)
    return pl.BlockSpec((None,) * (1 + len(lead)) + shape, lambda *_: index, pipeline_mode=pl.Buffered(1))


def _params(n_axes):
    return pltpu.CompilerParams(dimension_semantics=("arbitrary",) * n_axes,
                                vmem_limit_bytes=VMEM_LIMIT)


def _ffn_body(x_ref, g_ref, win_ref, wout_ref, o_ref):
    x = x_ref[...]
    h = _rms_norm(x, g_ref[...]).astype(BF16)
    acc = jnp.zeros(x.shape, F32)
    for lo, hi in zip(FF_SPLITS[:-1], FF_SPLITS[1:]):
        gate = jnp.dot(h, win_ref[:, lo:hi], preferred_element_type=F32)
        up = jnp.dot(h, win_ref[:, D_FF + lo:D_FF + hi], preferred_element_type=F32)
        act = (gate * jax.nn.sigmoid(gate) * up).astype(BF16)
        acc = acc + jnp.dot(act, wout_ref[lo:hi, :], preferred_element_type=F32)
    o_ref[...] = x + 0.5 * acc


def _ffn(x2, g, w_in, w_out, layer):
    t = x2.shape[0]
    tm = min(FFN_TOKEN_TILE, t)
    return pl.pallas_call(
        _ffn_body,
        grid=(t // tm,),
        in_specs=[pl.BlockSpec((tm, D_MODEL), lambda i: (i, 0)),
                  _layer_spec((1, D_MODEL), layer),
                  _layer_spec((D_MODEL, 2 * D_FF), layer),
                  _layer_spec((D_FF, D_MODEL), layer)],
        out_specs=pl.BlockSpec((tm, D_MODEL), lambda i: (i, 0)),
        out_shape=jax.ShapeDtypeStruct((t, D_MODEL), F32),
        compiler_params=_params(1),
        name="ffn",
    )(x2, g, w_in, w_out)


_Q0 = C_RWKV
_K0 = C_RWKV + D_B
_V0 = C_RWKV + 2 * D_B
_G0 = C_RWKV + 3 * D_B


def _proj_body(x_ref, g_ref, w_ref, qg_ref, kg_ref, bd_ref, prw_ref, q_ref, k_ref, vt_ref, gt_ref):
    h = _rms_norm(x_ref[...], g_ref[...]).astype(BF16)
    prw_ref[...] = jnp.dot(h, w_ref[:, :C_RWKV], preferred_element_type=F32)
    bd = bd_ref[...]
    q = jnp.dot(h, w_ref[:, _Q0:_K0], preferred_element_type=F32)
    q_ref[...] = (q * lax.rsqrt(_mm_split_lhs(q * q, bd) + NORM_EPS) * qg_ref[...]).astype(BF16)
    k = jnp.dot(h, w_ref[:, _K0:_V0], preferred_element_type=F32)
    k_ref[...] = (k * lax.rsqrt(_mm_split_lhs(k * k, bd) + NORM_EPS) * kg_ref[...]).astype(BF16)
    vt = jnp.dot(h, w_ref[:, _V0:_G0], preferred_element_type=F32).T.astype(BF16)
    hw = 2 * HEAD_B
    ones = jnp.ones((V_ROWS - hw, vt.shape[1]), BF16)
    vt_ref[0] = jnp.concatenate([part for hd in range(N_HEADS_B)
                                 for part in (vt[hd * hw:(hd + 1) * hw], ones)], axis=0)
    gt_ref[...] = jax.nn.sigmoid(jnp.dot(h, w_ref[:, _G0:], preferred_element_type=F32)).astype(BF16)


def _proj(x2, g, w, qg, kg, bd_mean, layer):
    t = x2.shape[0]
    tm = TOKEN_TILE
    c_in = w.shape[2]
    row = lambda width: pl.BlockSpec((tm, width), lambda i: (i, 0))
    return pl.pallas_call(
        _proj_body,
        grid=(t // tm,),
        in_specs=[row(D_MODEL), _layer_spec((1, D_MODEL), layer), _layer_spec((D_MODEL, c_in), layer),
                  _layer_spec((1, D_B), layer), _layer_spec((1, D_B), layer), _const_spec((D_B, D_B))],
        out_specs=[row(C_RWKV), row(D_B), row(D_B),
                   pl.BlockSpec((1, N_HEADS_B * V_ROWS, tm), lambda i: (i, 0, 0)), row(2 * D_MODEL)],
        out_shape=[jax.ShapeDtypeStruct((t, C_RWKV), F32),
                   jax.ShapeDtypeStruct((t, D_B), BF16),
                   jax.ShapeDtypeStruct((t, D_B), BF16),
                   jax.ShapeDtypeStruct((t // tm, N_HEADS_B * V_ROWS, tm), BF16),
                   jax.ShapeDtypeStruct((t, 2 * D_MODEL), BF16)],
        compiler_params=_params(1),
        name="proj",
    )(x2, g, w, qg, kg, bd_mean)


def _prep_body(p_ref, pp_ref, pn_ref, mu_ref, w0_ref, w2_ref, a0_ref, a2_ref, g2_ref, kk_ref, ka_ref,
               rk_ref, bd_ref,
               r_out, v_out, kk_out, lw0_out, k0_out, a0_out, lw1_out, k1_out, a1_out, gate_out,
               bonus_out, *, seq):
    i = pl.program_id(0)
    p = p_ref[...]
    tm = p.shape[0]
    row = lax.broadcasted_iota(jnp.int32, p.shape, 0)
    keep_prev = jnp.where((i * tm) % seq == 0, 0.0, 1.0)
    keep_next = jnp.where(((i + 1) * tm) % seq == 0, 0.0, 1.0)
    prev = jnp.where(row == 0, keep_prev * pp_ref[SUBLANES - 1:SUBLANES, :], pltpu.roll(p, 1, 0))
    nxt = jnp.where(row == tm - 1, keep_next * pn_ref[0:1, :], pltpu.roll(p, tm - 1, 0))
    p = p + mu_ref[...] * (0.5 * (prev + nxt) - p)

    r = p[:, :D_A]
    k = p[:, D_A:2 * D_A]
    v = p[:, 2 * D_A:3 * D_A]
    dw = p[:, 3 * D_A:3 * D_A + 2 * LORA]
    da = p[:, 3 * D_A + 2 * LORA:3 * D_A + 4 * LORA]
    dg = p[:, 3 * D_A + 4 * LORA:]
    bd = bd_ref[...]

    logw = -DECAY_SCALE * jax.nn.sigmoid(w0_ref[...] + _mm(jnp.tanh(dw), w2_ref[...]))
    iclr = jax.nn.sigmoid(a0_ref[...] + _mm(da, a2_ref[...]))
    gate_out[...] = _mm(jax.nn.sigmoid(dg), g2_ref[...])

    kk = k * kk_ref[...]
    norm = jnp.sqrt(_mm_split_lhs(kk * kk, bd))
    kk_out[...] = (kk / jnp.maximum(norm, 1e-12)).astype(kk_out.dtype)

    ka = ka_ref[...]
    a_f = iclr[:, :D_A]
    a_b = iclr[:, D_A:]
    k_f = k * (1.0 + (a_f - 1.0) * ka)
    k_b = k * (1.0 + (a_b - 1.0) * ka)
    bonus = _mm_split_lhs(r * rk_ref[...] * (k_f + k_b), bd)
    r_out[...] = r.astype(r_out.dtype)
    v_out[...] = v.astype(v_out.dtype)
    lw0_out[...] = logw[:, :D_A]
    k0_out[...] = k_f.astype(k0_out.dtype)
    a0_out[...] = a_f.astype(a0_out.dtype)
    lw1_out[...] = logw[:, D_A:]
    k1_out[...] = k_b.astype(k1_out.dtype)
    a1_out[...] = a_b.astype(a1_out.dtype)
    bonus_out[...] = bonus * v


def _prep(prw, seq, mu, w0, w2cat, a0, a2cat, g2, k_k, k_a, r_k, bd_sum, layer):
    t = prw.shape[0]
    tm = min(TOKEN_TILE, seq)
    nb = tm // SUBLANES
    last = t // SUBLANES - 1
    row = lambda width: pl.BlockSpec((tm, width), lambda i: (i, 0))
    wide = jax.ShapeDtypeStruct((t, D_A), F32)
    half = jax.ShapeDtypeStruct((t, D_A), BF16)
    return pl.pallas_call(
        functools.partial(_prep_body, seq=seq),
        grid=(t // tm,),
        in_specs=[row(C_RWKV),
                  pl.BlockSpec((SUBLANES, C_RWKV), lambda i: (jnp.maximum(i * nb - 1, 0), 0)),
                  pl.BlockSpec((SUBLANES, C_RWKV), lambda i: (jnp.minimum((i + 1) * nb, last), 0)),
                  _layer_spec((1, C_RWKV), layer),
                  _layer_spec((1, 2 * D_A), layer), _layer_spec((2 * LORA, 2 * D_A), layer),
                  _layer_spec((1, 2 * D_A), layer), _layer_spec((2 * LORA, 2 * D_A), layer),
                  _layer_spec((LORA_GATE, D_A), layer),
                  _layer_spec((1, D_A), layer), _layer_spec((1, D_A), layer), _layer_spec((1, D_A), layer),
                  _const_spec((D_A, D_A))],
        out_specs=[row(D_A)] * 11,
        out_shape=[half, half, half, wide, half, half, wide, half, half, wide, wide],
        compiler_params=_params(1),
        name="rwkv_prep",
    )(prw, prw, prw, mu, w0, w2cat, a0, a2cat, g2, k_k, k_a, r_k, bd_sum)


def _scan_body(rf, vf, kkf, lwf, kf, af, rb, vb, kkb, lwb, kb, ab, yf_ref, yb_ref, h_ref):
    @pl.when(pl.program_id(1) == 0)
    def _():
        h_ref[...] = jnp.zeros(h_ref.shape, F32)

    L = CHUNK
    n_sub = rf.shape[0] // L
    n_pairs = D_A // LANES
    ti = lax.broadcasted_iota(jnp.int32, (L, L), 0)
    tj = lax.broadcasted_iota(jnp.int32, (L, L), 1)
    pi = lax.broadcasted_iota(jnp.int32, (L, LANES), 0)
    lane = lax.broadcasted_iota(jnp.int32, (L, LANES), 1)
    pj = lane & (HEAD_A - 1)
    lane_lo = lane < HEAD_A
    eye = jnp.where(pj == pi, 1.0, 0.0)
    blk = lambda s: (pi // s) == (pj // s)
    row2 = lax.broadcasted_iota(jnp.int32, (2 * L, LANES), 0)
    col2 = lax.broadcasted_iota(jnp.int32, (2 * L, LANES), 1)
    same_head = (row2 // HEAD_A) == (col2 // HEAD_A)
    diag2 = row2 == col2
    zeros_l = jnp.zeros((L, LANES), F32)
    zeros_2l = jnp.zeros((2 * L, LANES), F32)

    def stack(x):
        return jnp.concatenate([jnp.where(lane_lo, x, 0.0), jnp.where(lane_lo, 0.0, x)], axis=0)

    def mmp(x, y):
        return _mm(x, stack(y))

    cat = jnp.concatenate
    chains = []
    for reverse, refs in ((False, (rf, vf, kkf, lwf, kf, af)), (True, (rb, vb, kkb, lwb, kb, ab))):
        tri = ((tj >= ti) if reverse else (tj <= ti)).astype(BF16)
        strict = (pj > pi) if reverse else (pj < pi)
        incl = (pj >= pi) if reverse else (pj <= pi)
        steps = []
        for sub in (range(n_sub - 1, -1, -1) if reverse else range(n_sub)):
            rows = slice(sub * L, (sub + 1) * L)
            r, v, kk, lw, k, a = (ref[rows, :].astype(F32) for ref in refs)
            G = sum(jnp.dot(tri, part, preferred_element_type=F32) for part in _split_bf16(lw, 3))
            g_end = G[0:1] if reverse else G[L - 1:L]
            gam_end = jnp.exp(g_end)
            inv_gam = jnp.exp(-G)
            rel_end = jnp.exp(g_end - G)
            b = kk * a
            full = dict(At=-kk * jnp.exp(G - lw), Bt=b * inv_gam, Kt=k * inv_gam, Rt=r * jnp.exp(G),
                        Kh=k * rel_end, Bh=b * rel_end, v=v)
            pairs = []
            for p in range(n_pairs):
                sl = slice(p * LANES, (p + 1) * LANES)
                c = {name: val[:, sl] for name, val in full.items()}
                c.update(strict=strict, incl=incl, gam_end=gam_end[:, sl], rows=rows)
                pairs.append(c)
            steps.append(pairs)
        chains.append(steps)
    flat = [c for steps in chains for pairs in steps for c in pairs]

    for c in flat:
        a4 = _mm_nt(cat([c["At"], c["Rt"]], axis=0), cat([stack(c["Bt"]), stack(c["Kt"])], axis=0))
        c["A_ab"] = jnp.where(c["strict"], a4[:L, :LANES], 0.0)
        c["A_ak"] = jnp.where(c["strict"], a4[:L, LANES:], 0.0)
        c["A_rb"] = jnp.where(c["incl"], a4[L:, :LANES], 0.0)
        c["A_rk"] = jnp.where(c["incl"], a4[L:, LANES:], 0.0)
    for c in flat:
        c["A0"] = jnp.where(blk(8), c["A_ab"], 0.0)
        c["A2"] = mmp(c["A0"], c["A0"])
    for c in flat:
        c["A4"] = mmp(c["A2"], c["A2"])
        c["T"] = eye + c["A0"]
    for c in flat:
        c["T"] = c["T"] + mmp(c["T"], c["A2"])
    for c in flat:
        c["T"] = c["T"] + mmp(c["T"], c["A4"])
    s = 8
    while s < L:
        outer = blk(2 * s) & jnp.logical_not(blk(s))
        for c in flat:
            c["TO"] = mmp(c["T"], jnp.where(outer, c["A_ab"], 0.0))
        for c in flat:
            c["T"] = c["T"] + mmp(c["TO"], c["T"])
        s *= 2
    for c in flat:
        c["AkV"] = mmp(c["A_ak"], c["v"])
    for c in flat:
        wu = _mm(c["T"], cat([stack(c["At"]), stack(c["AkV"])], axis=1))
        c["W"], c["U0"] = wu[:, :LANES], wu[:, LANES:]
    for c in flat:
        kb_t = cat([c["Kh"], c["Bh"]], axis=0).T
        mn = _mm(kb_t, cat([cat([zeros_l, c["v"]], axis=1), cat([c["W"], c["U0"]], axis=1)], axis=0))
        c["M"] = jnp.where(same_head, mn[:, :LANES], 0.0) + jnp.where(diag2, c["gam_end"], 0.0)
        c["N"] = jnp.where(same_head, mn[:, LANES:], 0.0)
    for c in flat:
        qy = _mm(cat([c["A_rb"], c["A_rk"]], axis=1),
                 cat([cat([stack(c["W"]), stack(c["U0"])], axis=1),
                      cat([zeros_2l, stack(c["v"])], axis=1)], axis=0))
        c["Q"] = c["Rt"] + qy[:, :LANES]
        c["Y0"] = qy[:, LANES:]
    for d, (steps, y_ref) in enumerate(zip(chains, (yf_ref, yb_ref))):
        for pairs in steps:
            ys = []
            for p, c in enumerate(pairs):
                H = h_ref[d * n_pairs + p]
                ys.append(_mm(c["Q"], H) + c["Y0"])
                h_ref[d * n_pairs + p] = _mm(c["M"], H) + c["N"]
            y_ref[pairs[0]["rows"], :] = cat(ys, axis=1)


def _scan(r, v, kk, lw0, k0, a0, lw1, k1, a1, batch, seq):
    t = r.shape[0]
    rows = SCAN_CHUNKS_PER_STEP * CHUNK
    nc = seq // rows
    fwd = pl.BlockSpec((rows, D_A), lambda b, c: (b * nc + c, 0))
    bwd = pl.BlockSpec((rows, D_A), lambda b, c: (b * nc + nc - 1 - c, 0))
    out = jax.ShapeDtypeStruct((t, D_A), F32)
    return pl.pallas_call(
        _scan_body,
        grid=(batch, nc),
        in_specs=[fwd] * 6 + [bwd] * 6,
        out_specs=[fwd, bwd],
        out_shape=[out, out],
        scratch_shapes=[pltpu.VMEM((2 * (D_A // LANES), LANES, LANES), F32)],
        compiler_params=_params(2),
        name="rwkv_scan",
    )(r, v, kk, lw0, k0, a0, r, v, kk, lw1, k1, a1)


def _bias_features(slope, n, width, key_side):
    pos = lax.broadcasted_iota(jnp.int32, (n, width), 0).astype(F32)
    lane = lax.broadcasted_iota(jnp.int32, (n, width), 1)
    hi, mid, lo = (t.astype(F32) for t in _split_bf16(slope * pos, 3))
    terms = jnp.where(lane % 3 == 0, hi, jnp.where(lane % 3 == 1, mid, lo))
    if key_side:
        return jnp.where(lane < 3, -1.0, jnp.where(lane < 6, terms, 0.0))
    return jnp.where(lane < 3, terms, jnp.where(lane < 6, 1.0, 0.0))


def _attn_body(q_ref, k_ref, vt_ref, lv_ref, sg_ref, qg_ref, kg_ref, o_ref, qs_ref, feat_ref, src_ref,
               s_ref, mx_ref, p_ref, al_ref, m_ref, acc_ref, *, lam_init):
    tq = src_ref.shape[1]
    n_kv, _, tk = vt_ref.shape
    seq = n_kv * tk
    h = pl.program_id(1)
    i = pl.program_id(2)
    slope_bits = (127 - 2 * (h + 1)) << 23
    slope = (lax.bitcast_convert_type(jnp.full((1, 1), slope_bits, jnp.int32), jnp.float32).astype(F32)
             * LOG2E)
    bound = (HEAD_B * ROUNDING_MARGIN * jnp.max(jnp.abs(qg_ref[...]), axis=-1, keepdims=True)
             * jnp.max(jnp.abs(kg_ref[...]), axis=-1, keepdims=True))
    reach = jnp.minimum((F32_ZERO_EXP2 + 2.0 * bound) / slope + 1.0, float(seq)).astype(jnp.int32)[0, 0]

    n_st = tk // tq
    hw = 2 * HEAD_B

    @pl.when(i == 0)
    def _():
        kv_pos = lax.broadcasted_iota(jnp.int32, (tk, tq), 0)
        q_pos = lax.broadcasted_iota(jnp.int32, (tk, tq), 1)
        src_ref[...] = slope * (q_pos - kv_pos).astype(F32)
        key_feat = _bias_features(slope, tk, hw, key_side=True)
        feat_ref[0] = key_feat.astype(BF16)
        feat_ref[1] = (-key_feat).astype(BF16)
        feat_ref[2] = jnp.zeros((tk, hw), BF16)
        q_feat = _bias_features(slope, tq, hw, key_side=False).T.astype(BF16)
        for st in range(n_st):
            qs_ref[st, hw:, :tq] = q_feat
            qs_ref[st, hw:, tq:] = q_feat

    chan = lax.broadcasted_iota(jnp.int32, (hw, tq), 0)
    for st in range(n_st):
        q_t = q_ref[st * tq:(st + 1) * tq, :].astype(F32).T.astype(BF16)
        zero = jnp.zeros_like(q_t)
        qs_ref[st, :hw, :tq] = jnp.where(chan < HEAD_B, q_t, zero)
        qs_ref[st, :hw, tq:] = jnp.where(chan < HEAD_B, zero, q_t)
    q0 = i * tk
    j_lo = jnp.maximum(q0 - reach, 0) // tk
    j_hi = jnp.minimum(q0 + tk - 1 + reach, seq - 1) // tk + 1
    n_left = i - j_lo
    n_off = n_left + (j_hi - i - 1)

    def scores(j, side, st):
        kb = k_ref[pl.ds(pl.multiple_of(j * tk, tk), tk), :]
        return jnp.dot(jnp.concatenate([kb, feat_ref[side]], axis=1), qs_ref[st],
                       preferred_element_type=F32)

    def softmax_update(s, s_max, cst, st):
        m_old = m_ref[st]
        m_new = jnp.maximum(m_old, s_max + cst)
        alpha = jnp.exp2(m_old - m_new)
        p = jnp.exp2(s - (m_new - cst))
        m_ref[st] = m_new
        return alpha, p.astype(BF16)

    def block(n):
        j = jnp.clip(j_lo + n + (n >= n_left).astype(jnp.int32), 0, n_kv - 1)
        j = jnp.where(n < 0, i, j)
        return j, (j > i).astype(jnp.int32), jnp.logical_and(n >= 0, n < n_off)

    def qk_stage(n, slot):
        j, side, _ = block(n)
        for st in range(n_st):
            s = scores(j, side, st)
            s_ref[slot, st] = s
            mx_ref[slot, st] = jnp.max(s, axis=0, keepdims=True)

    def softmax_stage(n, slot):
        j, _, valid = block(n)
        for st in range(n_st):
            dist0 = jnp.abs(q0 + st * tq - j * tk).astype(F32)
            cst = jnp.where(valid, -slope * dist0, -1e30)
            alpha, p = softmax_update(s_ref[slot, st], mx_ref[slot, st], cst, st)
            al_ref[slot, st] = alpha
            p_ref[slot, st] = p

    def pv_stage(n, slot):
        j, _, _ = block(n)
        for st in range(n_st):
            acc_ref[st] = al_ref[slot, st] * acc_ref[st] + jnp.dot(vt_ref[j], p_ref[slot, st],
                                                                    preferred_element_type=F32)

    m_ref[...] = jnp.full(m_ref.shape, -1e30, F32)
    acc_ref[...] = jnp.zeros(acc_ref.shape, F32)
    diag_scores = [scores(i, 2, st) for st in range(n_st)]
    qk_stage(0, 0)
    src = src_ref[...]
    for st in range(n_st):
        bias = jnp.abs(src + slope * float(st * tq))
        s = diag_scores[st] - jnp.concatenate([bias, bias], axis=1)
        alpha, p = softmax_update(s, jnp.max(s, axis=0, keepdims=True), jnp.zeros((1, 1), F32), st)
        al_ref[1, st] = alpha
        p_ref[1, st] = p

    def pipelined_pair(t, carry):
        n = 2 * t
        qk_stage(n + 1, 1)
        softmax_stage(n, 0)
        pv_stage(n - 1, 1)
        qk_stage(n + 2, 0)
        softmax_stage(n + 1, 1)
        pv_stage(n, 0)
        return carry

    n_pairs = (n_off + 1) // 2
    lax.fori_loop(0, n_pairs, pipelined_pair, 0)
    pv_stage(2 * n_pairs - 1, 1)

    lv = lv_ref[...]
    lam = (jnp.exp(jnp.sum(lv[0:1] * lv[1:2], axis=-1, keepdims=True))
           - jnp.exp(jnp.sum(lv[2:3] * lv[3:4], axis=-1, keepdims=True)) + lam_init)
    for st in range(n_st):
        acc = acc_ref[st, :hw, :]
        l = acc_ref[st, hw:hw + 1, :]
        o_t = acc[:, :tq] / l[:, :tq] - lam * (acc[:, tq:] / l[:, tq:])
        o_t = o_t * lax.rsqrt(jnp.mean(o_t * o_t, axis=0, keepdims=True) + NORM_EPS)
        o_ref[st * tq:(st + 1) * tq, :] = (o_t.T * sg_ref[...] * (1.0 - lam_init)).astype(o_ref.dtype)


def _attn(q, k, vt, lam_vecs, sub_g, qg, kg, lam_init, batch, seq, layer):
    t = q.shape[0]
    tk = vt.shape[2]
    tq = ATTN_TQ
    n_st = tk // tq
    nq = seq // tk
    n_kv = seq // tk
    hw = 2 * HEAD_B
    return pl.pallas_call(
        functools.partial(_attn_body, lam_init=lam_init),
        grid=(batch, N_HEADS_B, nq),
        in_specs=[pl.BlockSpec((tk, hw), lambda b, h, i: (b * nq + i, h)),
                  pl.BlockSpec((seq, hw), lambda b, h, i: (b, h)),
                  pl.BlockSpec((n_kv, V_ROWS, tk), lambda b, h, i: (b, h, 0)),
                  _layer_spec((4, HEAD_B), layer), _layer_spec((1, hw), layer),
                  _layer_spec((1, D_B), layer), _layer_spec((1, D_B), layer)],
        out_specs=pl.BlockSpec((tk, hw), lambda b, h, i: (b * nq + i, h)),
        out_shape=jax.ShapeDtypeStruct((t, D_B), BF16),
        scratch_shapes=[pltpu.VMEM((n_st, 2 * hw, 2 * tq), BF16),
                        pltpu.VMEM((3, tk, hw), BF16),
                        pltpu.VMEM((tk, tq), F32),
                        pltpu.VMEM((2, n_st, tk, 2 * tq), F32),
                        pltpu.VMEM((2, n_st, 1, 2 * tq), F32),
                        pltpu.VMEM((2, n_st, tk, 2 * tq), BF16),
                        pltpu.VMEM((2, n_st, 1, 2 * tq), F32),
                        pltpu.VMEM((n_st, 1, 2 * tq), F32),
                        pltpu.VMEM((n_st, V_ROWS, 2 * tq), F32)],
        compiler_params=_params(3),
        name="diff_attn",
    )(q, k, vt, lam_vecs, sub_g, qg, kg)


def _merge_body(yf_ref, yb_ref, gate_ref, bonus_ref, ob_ref, gt_ref, x_ref, lng_ref, lnb_ref, bd_ref,
                wb0_ref, wb1_ref, wo_ref, o_ref):
    bd = bd_ref[...]
    y = yf_ref[...] + yb_ref[...]
    yc = y - _mm_split_lhs(y, bd)
    var = _mm_split_lhs(yc * yc, bd)
    yn = yc * lax.rsqrt(var + GN_EPS) * lng_ref[...] + lnb_ref[...]
    oa = (yn + bonus_ref[...]) * gate_ref[...]
    gts = gt_ref[...].astype(F32)
    merged = (gts[:, :D_MODEL] * _mm(oa, wb0_ref[...])
              + gts[:, D_MODEL:] * jnp.dot(ob_ref[...], wb1_ref[...], preferred_element_type=F32))
    o_ref[...] = x_ref[...] + _mm(merged, wo_ref[...])


def _merge(yf, yb, gate, bonus, ob, gts, x2, ln_g, ln_b, bd_mean, wb, wo, layer):
    t = x2.shape[0]
    tm = min(TOKEN_TILE, t)
    row = lambda width: pl.BlockSpec((tm, width), lambda i: (i, 0))
    return pl.pallas_call(
        _merge_body,
        grid=(t // tm,),
        in_specs=[row(D_A), row(D_A), row(D_A), row(D_A), row(D_B), row(2 * D_MODEL), row(D_MODEL),
                  _layer_spec((1, D_A), layer), _layer_spec((1, D_A), layer), _const_spec((D_A, D_A)),
                  _layer_spec((D_A, D_MODEL), layer, 0), _layer_spec((D_B, D_MODEL), layer, 1),
                  _layer_spec((D_MODEL, D_MODEL), layer)],
        out_specs=row(D_MODEL),
        out_shape=jax.ShapeDtypeStruct((t, D_MODEL), F32),
        compiler_params=_params(1),
        name="merge",
    )(yf, yb, gate, bonus, ob, gts, x2, ln_g, ln_b, bd_mean, wb, wb, wo)


def _block_diag2(w):
    z = jnp.zeros_like(w[:, 0])
    return jnp.concatenate([jnp.concatenate([w[:, 0], z], axis=2), jnp.concatenate([z, w[:, 1]], axis=2)],
                           axis=1)


def kernel(x, norm_ffn1, ffn1_in, ffn1_out, norm_mix, w_in, rwkv_mu, decay_w0, decay_w2, iclr_a0, iclr_a2,
           gate_g2, k_k, k_a, r_k, ln_x_g, ln_x_b, q_gain, k_gain, diff_lambda, subln_g, w_branch, w_out,
           norm_ffn2, ffn2_in, ffn2_out):
    batch, seq, d = x.shape
    depth = norm_ffn1.shape[0]
    assert seq % TOKEN_TILE == 0 and d == D_MODEL
    x2 = x.reshape(batch * seq, d)
    head_id = jnp.arange(D_A) // HEAD_A
    same = head_id[:, None] == head_id[None, :]
    bd_sum = same.astype(BF16)
    bd_mean = (same.astype(F32) / HEAD_A).astype(BF16)
    rows = lambda a: a.reshape(depth, 1, -1)
    bf = lambda a: a.astype(BF16)
    n_qk = D_B // HEAD_B
    qg = rows(jnp.tile(q_gain, (1, n_qk))) * (HEAD_B ** -0.5 * LOG2E)
    kg = rows(jnp.tile(k_gain, (1, n_qk)))
    ffn1 = (rows(norm_ffn1), bf(ffn1_in), bf(ffn1_out))
    ffn2 = (rows(norm_ffn2), bf(ffn2_in), bf(ffn2_out))
    proj_w = (rows(norm_mix), bf(w_in))
    prep_w = (rows(rwkv_mu), rows(decay_w0), bf(_block_diag2(decay_w2)), rows(iclr_a0),
              bf(_block_diag2(iclr_a2)), bf(gate_g2), rows(k_k), rows(k_a), rows(r_k))
    merge_w = (rows(ln_x_g), rows(ln_x_b), bd_mean, bf(w_branch), bf(w_out))
    sub_g = rows(subln_g)
    for l in range(depth):
        lam_init = 0.8 - 0.6 * math.exp(-0.3 * l)
        x2 = _ffn(x2, *ffn1, l)
        prw, q, k, vt, gts = _proj(x2, *proj_w, qg, kg, bd_mean, l)
        r, vv, kk, lw0, k0, a0, lw1, k1, a1, gate, bonus = _prep(prw, seq, *prep_w, bd_sum, l)
        yf, yb = _scan(r, vv, kk, lw0, k0, a0, lw1, k1, a1, batch, seq)
        ob = _attn(q, k, vt, diff_lambda, sub_g, qg, kg, lam_init, batch, seq, l)
        x2 = _merge(yf, yb, gate, bonus, ob, gts, x2, *merge_w, l)
        x2 = _ffn(x2, *ffn2, l)
    return x2.reshape(batch, seq, d)
```

```python
import functools
import math

import jax
import jax.numpy as jnp
from jax import lax
from jax.experimental import pallas as pl
from jax.experimental.pallas import tpu as pltpu

F32 = jnp.float32
BF16 = jnp.bfloat16

D_MODEL = 1024
D_A = 512
HEAD_A = 64
LORA = 64
LORA_GATE = 128
C_RWKV = 3 * D_A + 2 * LORA + 2 * LORA + LORA_GATE
D_B = 512
HEAD_B = 64
N_HEADS_B = D_B // (2 * HEAD_B)
D_FF = 2816
DECAY_SCALE = math.exp(-0.5)
GN_EPS = 64e-5
NORM_EPS = 1e-6
LOG2E = math.log2(math.e)
F32_ZERO_EXP2 = 150.0
ROUNDING_MARGIN = 1.02

LANES = 128
SUBLANES = 8
VMEM_LIMIT = 56 * 1024 * 1024

TOKEN_TILE = 512
MXU_WIDTH = 256
FF_SPLITS = (0, 6 * MXU_WIDTH, D_FF)
CHUNK = 64
SCAN_CHUNKS_PER_STEP = 2
ATTN_TQ = 256
V_ROWS = 2 * HEAD_B + 16

_NT = (((1,), (1,)), ((), ()))


def _mm(a, b):
    return jnp.dot(a.astype(BF16), b.astype(BF16), preferred_element_type=F32)


def _mm_nt(a, b):
    return lax.dot_general(a.astype(BF16), b.astype(BF16), _NT, preferred_element_type=F32)


def _split_bf16(x, n):
    parts = []
    for _ in range(n - 1):
        hi = x.astype(BF16)
        parts.append(hi)
        x = x - hi.astype(F32)
    parts.append(x.astype(BF16))
    return parts


def _mm_split_lhs(x, w, n=2):
    return sum(jnp.dot(p, w, preferred_element_type=F32) for p in _split_bf16(x, n))


def _rms_norm(x, g):
    return x * lax.rsqrt(jnp.mean(x * x, axis=-1, keepdims=True) + NORM_EPS) * g


def _const_spec(shape):
    return pl.BlockSpec(shape, lambda *_: (0,) * len(shape), pipeline_mode=pl.Buffered(1))


def _layer_spec(shape, layer, *lead):
    index = (layer,) + lead + (0,) * len(shape)
    return pl.BlockSpec((None,) * (1 + len(lead)) + shape, lambda *_: index, pipeline_mode=pl.Buffered(1))


def _params(n_axes):
    return pltpu.CompilerParams(dimension_semantics=("arbitrary",) * n_axes,
                                vmem_limit_bytes=VMEM_LIMIT)


def _ffn_body(x_ref, g_ref, win_ref, wout_ref, o_ref):
    x = x_ref[...]
    h = _rms_norm(x, g_ref[...]).astype(BF16)
    acc = jnp.zeros(x.shape, F32)
    for lo, hi in zip(FF_SPLITS[:-1], FF_SPLITS[1:]):
        gate = jnp.dot(h, win_ref[:, lo:hi], preferred_element_type=F32)
        up = jnp.dot(h, win_ref[:, D_FF + lo:D_FF + hi], preferred_element_type=F32)
        act = (gate * jax.nn.sigmoid(gate) * up).astype(BF16)
        acc = acc + jnp.dot(act, wout_ref[lo:hi, :], preferred_element_type=F32)
    o_ref[...] = x + 0.5 * acc


def _ffn(x2, g, w_in, w_out, layer):
    t = x2.shape[0]
    tm = min(TOKEN_TILE, t)
    return pl.pallas_call(
        _ffn_body,
        grid=(t // tm,),
        in_specs=[pl.BlockSpec((tm, D_MODEL), lambda i: (i, 0)),
                  _layer_spec((1, D_MODEL), layer),
                  _layer_spec((D_MODEL, 2 * D_FF), layer),
                  _layer_spec((D_FF, D_MODEL), layer)],
        out_specs=pl.BlockSpec((tm, D_MODEL), lambda i: (i, 0)),
        out_shape=jax.ShapeDtypeStruct((t, D_MODEL), F32),
        compiler_params=_params(1),
        name="ffn",
    )(x2, g, w_in, w_out)


_Q0 = C_RWKV
_K0 = C_RWKV + D_B
_V0 = C_RWKV + 2 * D_B
_G0 = C_RWKV + 3 * D_B


def _proj_body(x_ref, xp_ref, xn_ref, g_ref, w_ref, qg_ref, kg_ref, bdm_ref, mu_ref, w0_ref, w2_ref, a0_ref,
               a2_ref, g2_ref, kk_ref, ka_ref, rk_ref, bds_ref,
               q_ref, k_ref, vt_ref, gt_ref, r_out, v_out, kk_out, lw0_out, k0_out, a0_out, lw1_out, k1_out,
               a1_out, gate_out, bonus_out, *, seq):
    i = pl.program_id(0)
    g = g_ref[...]
    h = _rms_norm(x_ref[...], g).astype(BF16)
    tm = h.shape[0]

    bd = bdm_ref[...]
    q = jnp.dot(h, w_ref[:, _Q0:_K0], preferred_element_type=F32)
    q_ref[...] = (q * lax.rsqrt(_mm_split_lhs(q * q, bd) + NORM_EPS) * qg_ref[...]).astype(BF16)
    k = jnp.dot(h, w_ref[:, _K0:_V0], preferred_element_type=F32)
    k_ref[...] = (k * lax.rsqrt(_mm_split_lhs(k * k, bd) + NORM_EPS) * kg_ref[...]).astype(BF16)
    vt = jnp.dot(h, w_ref[:, _V0:_G0], preferred_element_type=F32).T.astype(BF16)
    hw = 2 * HEAD_B
    ones = jnp.ones((V_ROWS - hw, vt.shape[1]), BF16)
    vt_ref[0] = jnp.concatenate([part for hd in range(N_HEADS_B)
                                 for part in (vt[hd * hw:(hd + 1) * hw], ones)], axis=0)
    gt_ref[...] = jax.nn.sigmoid(jnp.dot(h, w_ref[:, _G0:], preferred_element_type=F32)).astype(BF16)

    w_rwkv = w_ref[:, :C_RWKV]
    p = jnp.dot(h, w_rwkv, preferred_element_type=F32)
    edge = jnp.concatenate([xp_ref[...], xn_ref[...]], axis=0)
    p_edge = jnp.dot(_rms_norm(edge, g).astype(BF16), w_rwkv, preferred_element_type=F32)
    keep_prev = jnp.where((i * tm) % seq == 0, 0.0, 1.0)
    keep_next = jnp.where(((i + 1) * tm) % seq == 0, 0.0, 1.0)
    row = lax.broadcasted_iota(jnp.int32, p.shape, 0)
    prev = jnp.where(row == 0, keep_prev * p_edge[SUBLANES - 1:SUBLANES, :], pltpu.roll(p, 1, 0))
    nxt = jnp.where(row == tm - 1, keep_next * p_edge[SUBLANES:SUBLANES + 1, :], pltpu.roll(p, tm - 1, 0))
    p = p + mu_ref[...] * (0.5 * (prev + nxt) - p)

    r = p[:, :D_A]
    k = p[:, D_A:2 * D_A]
    v = p[:, 2 * D_A:3 * D_A]
    dw = p[:, 3 * D_A:3 * D_A + 2 * LORA]
    da = p[:, 3 * D_A + 2 * LORA:3 * D_A + 4 * LORA]
    dg = p[:, 3 * D_A + 4 * LORA:]
    bd = bds_ref[...]

    logw = -DECAY_SCALE * jax.nn.sigmoid(w0_ref[...] + _mm(jnp.tanh(dw), w2_ref[...]))
    iclr = jax.nn.sigmoid(a0_ref[...] + _mm(da, a2_ref[...]))
    gate_out[...] = _mm(jax.nn.sigmoid(dg), g2_ref[...])

    kk = k * kk_ref[...]
    norm = jnp.sqrt(_mm_split_lhs(kk * kk, bd))
    kk_out[...] = (kk / jnp.maximum(norm, 1e-12)).astype(kk_out.dtype)

    ka = ka_ref[...]
    a_f = iclr[:, :D_A]
    a_b = iclr[:, D_A:]
    k_f = k * (1.0 + (a_f - 1.0) * ka)
    k_b = k * (1.0 + (a_b - 1.0) * ka)
    bonus = _mm_split_lhs(r * rk_ref[...] * (k_f + k_b), bd)
    r_out[...] = r.astype(r_out.dtype)
    v_out[...] = v.astype(v_out.dtype)
    lw0_out[...] = logw[:, :D_A]
    k0_out[...] = k_f.astype(k0_out.dtype)
    a0_out[...] = a_f.astype(a0_out.dtype)
    lw1_out[...] = logw[:, D_A:]
    k1_out[...] = k_b.astype(k1_out.dtype)
    a1_out[...] = a_b.astype(a1_out.dtype)
    bonus_out[...] = bonus * v


def _proj(x2, seq, g, w, qg, kg, bd_mean, mu, w0, w2cat, a0, a2cat, g2, k_k, k_a, r_k, bd_sum, layer):
    t = x2.shape[0]
    tm = TOKEN_TILE
    c_in = w.shape[2]
    nb = tm // SUBLANES
    last = t // SUBLANES - 1
    row = lambda width: pl.BlockSpec((tm, width), lambda i: (i, 0))
    wide = jax.ShapeDtypeStruct((t, D_A), F32)
    half = jax.ShapeDtypeStruct((t, D_A), BF16)
    return pl.pallas_call(
        functools.partial(_proj_body, seq=seq),
        grid=(t // tm,),
        in_specs=[row(D_MODEL),
                  pl.BlockSpec((SUBLANES, D_MODEL), lambda i: (jnp.maximum(i * nb - 1, 0), 0)),
                  pl.BlockSpec((SUBLANES, D_MODEL), lambda i: (jnp.minimum((i + 1) * nb, last), 0)),
                  _layer_spec((1, D_MODEL), layer), _layer_spec((D_MODEL, c_in), layer),
                  _layer_spec((1, D_B), layer), _layer_spec((1, D_B), layer), _const_spec((D_B, D_B)),
                  _layer_spec((1, C_RWKV), layer),
                  _layer_spec((1, 2 * D_A), layer), _layer_spec((2 * LORA, 2 * D_A), layer),
                  _layer_spec((1, 2 * D_A), layer), _layer_spec((2 * LORA, 2 * D_A), layer),
                  _layer_spec((LORA_GATE, D_A), layer),
                  _layer_spec((1, D_A), layer), _layer_spec((1, D_A), layer), _layer_spec((1, D_A), layer),
                  _const_spec((D_A, D_A))],
        out_specs=[row(D_B), row(D_B),
                   pl.BlockSpec((1, N_HEADS_B * V_ROWS, tm), lambda i: (i, 0, 0)), row(2 * D_MODEL)]
                  + [row(D_A)] * 11,
        out_shape=[jax.ShapeDtypeStruct((t, D_B), BF16),
                   jax.ShapeDtypeStruct((t, D_B), BF16),
                   jax.ShapeDtypeStruct((t // tm, N_HEADS_B * V_ROWS, tm), BF16),
                   jax.ShapeDtypeStruct((t, 2 * D_MODEL), BF16),
                   half, half, half, wide, half, half, wide, half, half, wide, wide],
        compiler_params=_params(1),
        name="proj",
    )(x2, x2, x2, g, w, qg, kg, bd_mean, mu, w0, w2cat, a0, a2cat, g2, k_k, k_a, r_k, bd_sum)


def _scan_body(rf, vf, kkf, lwf, kf, af, rb, vb, kkb, lwb, kb, ab, yf_ref, yb_ref, h_ref):
    @pl.when(pl.program_id(1) == 0)
    def _():
        h_ref[...] = jnp.zeros(h_ref.shape, F32)

    L = CHUNK
    n_sub = rf.shape[0] // L
    n_pairs = D_A // LANES
    ti = lax.broadcasted_iota(jnp.int32, (L, L), 0)
    tj = lax.broadcasted_iota(jnp.int32, (L, L), 1)
    pi = lax.broadcasted_iota(jnp.int32, (L, LANES), 0)
    lane = lax.broadcasted_iota(jnp.int32, (L, LANES), 1)
    pj = lane & (HEAD_A - 1)
    lane_lo = lane < HEAD_A
    eye = jnp.where(pj == pi, 1.0, 0.0)
    blk = lambda s: (pi // s) == (pj // s)
    row2 = lax.broadcasted_iota(jnp.int32, (2 * L, LANES), 0)
    col2 = lax.broadcasted_iota(jnp.int32, (2 * L, LANES), 1)
    same_head = (row2 // HEAD_A) == (col2 // HEAD_A)
    diag2 = row2 == col2
    zeros_l = jnp.zeros((L, LANES), F32)
    zeros_2l = jnp.zeros((2 * L, LANES), F32)

    def stack(x):
        return jnp.concatenate([jnp.where(lane_lo, x, 0.0), jnp.where(lane_lo, 0.0, x)], axis=0)

    def mmp(x, y):
        return _mm(x, stack(y))

    cat = jnp.concatenate
    chains = []
    for reverse, refs in ((False, (rf, vf, kkf, lwf, kf, af)), (True, (rb, vb, kkb, lwb, kb, ab))):
        tri = ((tj >= ti) if reverse else (tj <= ti)).astype(BF16)
        strict = (pj > pi) if reverse else (pj < pi)
        incl = (pj >= pi) if reverse else (pj <= pi)
        steps = []
        for sub in (range(n_sub - 1, -1, -1) if reverse else range(n_sub)):
            rows = slice(sub * L, (sub + 1) * L)
            r, v, kk, lw, k, a = (ref[rows, :].astype(F32) for ref in refs)
            G = sum(jnp.dot(tri, part, preferred_element_type=F32) for part in _split_bf16(lw, 3))
            g_end = G[0:1] if reverse else G[L - 1:L]
            gam_end = jnp.exp(g_end)
            inv_gam = jnp.exp(-G)
            rel_end = jnp.exp(g_end - G)
            b = kk * a
            full = dict(At=-kk * jnp.exp(G - lw), Bt=b * inv_gam, Kt=k * inv_gam, Rt=r * jnp.exp(G),
                        Kh=k * rel_end, Bh=b * rel_end, v=v)
            pairs = []
            for p in range(n_pairs):
                sl = slice(p * LANES, (p + 1) * LANES)
                c = {name: val[:, sl] for name, val in full.items()}
                c.update(strict=strict, incl=incl, gam_end=gam_end[:, sl], rows=rows)
                pairs.append(c)
            steps.append(pairs)
        chains.append(steps)
    flat = [c for steps in chains for pairs in steps for c in pairs]

    for c in flat:
        a4 = _mm_nt(cat([c["At"], c["Rt"]], axis=0), cat([stack(c["Bt"]), stack(c["Kt"])], axis=0))
        c["A_ab"] = jnp.where(c["strict"], a4[:L, :LANES], 0.0)
        c["A_ak"] = jnp.where(c["strict"], a4[:L, LANES:], 0.0)
        c["A_rb"] = jnp.where(c["incl"], a4[L:, :LANES], 0.0)
        c["A_rk"] = jnp.where(c["incl"], a4[L:, LANES:], 0.0)
    for c in flat:
        c["A0"] = jnp.where(blk(8), c["A_ab"], 0.0)
        c["A2"] = mmp(c["A0"], c["A0"])
    for c in flat:
        c["A4"] = mmp(c["A2"], c["A2"])
        c["T"] = eye + c["A0"]
    for c in flat:
        c["T"] = c["T"] + mmp(c["T"], c["A2"])
    for c in flat:
        c["T"] = c["T"] + mmp(c["T"], c["A4"])
    s = 8
    while s < L:
        outer = blk(2 * s) & jnp.logical_not(blk(s))
        for c in flat:
            c["TO"] = mmp(c["T"], jnp.where(outer, c["A_ab"], 0.0))
        for c in flat:
            c["T"] = c["T"] + mmp(c["TO"], c["T"])
        s *= 2
    for c in flat:
        c["AkV"] = mmp(c["A_ak"], c["v"])
    for c in flat:
        wu = _mm(c["T"], cat([stack(c["At"]), stack(c["AkV"])], axis=1))
        c["W"], c["U0"] = wu[:, :LANES], wu[:, LANES:]
    for c in flat:
        kb_t = cat([c["Kh"], c["Bh"]], axis=0).T
        mn = _mm(kb_t, cat([cat([zeros_l, c["v"]], axis=1), cat([c["W"], c["U0"]], axis=1)], axis=0))
        c["M"] = jnp.where(same_head, mn[:, :LANES], 0.0) + jnp.where(diag2, c["gam_end"], 0.0)
        c["N"] = jnp.where(same_head, mn[:, LANES:], 0.0)
    for c in flat:
        qy = _mm(cat([c["A_rb"], c["A_rk"]], axis=1),
                 cat([cat([stack(c["W"]), stack(c["U0"])], axis=1),
                      cat([zeros_2l, stack(c["v"])], axis=1)], axis=0))
        c["Q"] = c["Rt"] + qy[:, :LANES]
        c["Y0"] = qy[:, LANES:]
    for d, (steps, y_ref) in enumerate(zip(chains, (yf_ref, yb_ref))):
        for pairs in steps:
            ys = []
            for p, c in enumerate(pairs):
                H = h_ref[d * n_pairs + p]
                ys.append(_mm(c["Q"], H) + c["Y0"])
                h_ref[d * n_pairs + p] = _mm(c["M"], H) + c["N"]
            y_ref[pairs[0]["rows"], :] = cat(ys, axis=1)


def _scan(r, v, kk, lw0, k0, a0, lw1, k1, a1, batch, seq):
    t = r.shape[0]
    rows = SCAN_CHUNKS_PER_STEP * CHUNK
    nc = seq // rows
    fwd = pl.BlockSpec((rows, D_A), lambda b, c: (b * nc + c, 0))
    bwd = pl.BlockSpec((rows, D_A), lambda b, c: (b * nc + nc - 1 - c, 0))
    out = jax.ShapeDtypeStruct((t, D_A), F32)
    return pl.pallas_call(
        _scan_body,
        grid=(batch, nc),
        in_specs=[fwd] * 6 + [bwd] * 6,
        out_specs=[fwd, bwd],
        out_shape=[out, out],
        scratch_shapes=[pltpu.VMEM((2 * (D_A // LANES), LANES, LANES), F32)],
        compiler_params=_params(2),
        name="rwkv_scan",
    )(r, v, kk, lw0, k0, a0, r, v, kk, lw1, k1, a1)


def _bias_features(slope, n, width, key_side):
    pos = lax.broadcasted_iota(jnp.int32, (n, width), 0).astype(F32)
    lane = lax.broadcasted_iota(jnp.int32, (n, width), 1)
    hi, mid, lo = (t.astype(F32) for t in _split_bf16(slope * pos, 3))
    terms = jnp.where(lane % 3 == 0, hi, jnp.where(lane % 3 == 1, mid, lo))
    if key_side:
        return jnp.where(lane < 3, -1.0, jnp.where(lane < 6, terms, 0.0))
    return jnp.where(lane < 3, terms, jnp.where(lane < 6, 1.0, 0.0))


def _attn_body(q_ref, k_ref, vt_ref, lv_ref, sg_ref, qg_ref, kg_ref, o_ref, qs_ref, feat_ref, src_ref,
               s_ref, mx_ref, p_ref, al_ref, m_ref, acc_ref, *, lam_init):
    tq = src_ref.shape[1]
    n_kv, _, tk = vt_ref.shape
    seq = n_kv * tk
    h = pl.program_id(1)
    i = pl.program_id(2)
    slope_bits = (127 - 2 * (h + 1)) << 23
    slope = (lax.bitcast_convert_type(jnp.full((1, 1), slope_bits, jnp.int32), jnp.float32).astype(F32)
             * LOG2E)
    bound = (HEAD_B * ROUNDING_MARGIN * jnp.max(jnp.abs(qg_ref[...]), axis=-1, keepdims=True)
             * jnp.max(jnp.abs(kg_ref[...]), axis=-1, keepdims=True))
    reach = jnp.minimum((F32_ZERO_EXP2 + 2.0 * bound) / slope + 1.0, float(seq)).astype(jnp.int32)[0, 0]

    n_st = tk // tq
    hw = 2 * HEAD_B

    @pl.when(i == 0)
    def _():
        kv_pos = lax.broadcasted_iota(jnp.int32, (tk, tq), 0)
        q_pos = lax.broadcasted_iota(jnp.int32, (tk, tq), 1)
        src_ref[...] = slope * (q_pos - kv_pos).astype(F32)
        key_feat = _bias_features(slope, tk, hw, key_side=True)
        feat_ref[0] = key_feat.astype(BF16)
        feat_ref[1] = (-key_feat).astype(BF16)
        feat_ref[2] = jnp.zeros((tk, hw), BF16)
        q_feat = _bias_features(slope, tq, hw, key_side=False).T.astype(BF16)
        for st in range(n_st):
            qs_ref[st, hw:, :tq] = q_feat
            qs_ref[st, hw:, tq:] = q_feat

    chan = lax.broadcasted_iota(jnp.int32, (hw, tq), 0)
    for st in range(n_st):
        q_t = q_ref[st * tq:(st + 1) * tq, :].astype(F32).T.astype(BF16)
        zero = jnp.zeros_like(q_t)
        qs_ref[st, :hw, :tq] = jnp.where(chan < HEAD_B, q_t, zero)
        qs_ref[st, :hw, tq:] = jnp.where(chan < HEAD_B, zero, q_t)
    q0 = i * tk
    j_lo = jnp.maximum(q0 - reach, 0) // tk
    j_hi = jnp.minimum(q0 + tk - 1 + reach, seq - 1) // tk + 1
    n_left = i - j_lo
    n_off = n_left + (j_hi - i - 1)

    def scores(j, side, st):
        kb = k_ref[pl.ds(pl.multiple_of(j * tk, tk), tk), :]
        return jnp.dot(jnp.concatenate([kb, feat_ref[side]], axis=1), qs_ref[st],
                       preferred_element_type=F32)

    def softmax_update(s, s_max, cst, st):
        m_old = m_ref[st]
        m_new = jnp.maximum(m_old, s_max + cst)
        alpha = jnp.exp2(m_old - m_new)
        p = jnp.exp2(s - (m_new - cst))
        m_ref[st] = m_new
        return alpha, p.astype(BF16)

    def block(n):
        j = jnp.clip(j_lo + n + (n >= n_left).astype(jnp.int32), 0, n_kv - 1)
        j = jnp.where(n < 0, i, j)
        return j, (j > i).astype(jnp.int32), jnp.logical_and(n >= 0, n < n_off)

    def qk_stage(n, slot):
        j, side, _ = block(n)
        for st in range(n_st):
            s = scores(j, side, st)
            s_ref[slot, st] = s
            mx_ref[slot, st] = jnp.max(s, axis=0, keepdims=True)

    def softmax_stage(n, slot):
        j, _, valid = block(n)
        for st in range(n_st):
            dist0 = jnp.abs(q0 + st * tq - j * tk).astype(F32)
            cst = jnp.where(valid, -slope * dist0, -1e30)
            alpha, p = softmax_update(s_ref[slot, st], mx_ref[slot, st], cst, st)
            al_ref[slot, st] = alpha
            p_ref[slot, st] = p

    def pv_stage(n, slot):
        j, _, _ = block(n)
        for st in range(n_st):
            acc_ref[st] = al_ref[slot, st] * acc_ref[st] + jnp.dot(vt_ref[j], p_ref[slot, st],
                                                                    preferred_element_type=F32)

    m_ref[...] = jnp.full(m_ref.shape, -1e30, F32)
    acc_ref[...] = jnp.zeros(acc_ref.shape, F32)
    diag_scores = [scores(i, 2, st) for st in range(n_st)]
    qk_stage(0, 0)
    src = src_ref[...]
    for st in range(n_st):
        bias = jnp.abs(src + slope * float(st * tq))
        s = diag_scores[st] - jnp.concatenate([bias, bias], axis=1)
        alpha, p = softmax_update(s, jnp.max(s, axis=0, keepdims=True), jnp.zeros((1, 1), F32), st)
        al_ref[1, st] = alpha
        p_ref[1, st] = p

    def pipelined_pair(t, carry):
        n = 2 * t
        qk_stage(n + 1, 1)
        softmax_stage(n, 0)
        pv_stage(n - 1, 1)
        qk_stage(n + 2, 0)
        softmax_stage(n + 1, 1)
        pv_stage(n, 0)
        return carry

    n_pairs = (n_off + 1) // 2
    lax.fori_loop(0, n_pairs, pipelined_pair, 0)
    pv_stage(2 * n_pairs - 1, 1)

    lv = lv_ref[...]
    lam = (jnp.exp(jnp.sum(lv[0:1] * lv[1:2], axis=-1, keepdims=True))
           - jnp.exp(jnp.sum(lv[2:3] * lv[3:4], axis=-1, keepdims=True)) + lam_init)
    for st in range(n_st):
        acc = acc_ref[st, :hw, :]
        l = acc_ref[st, hw:hw + 1, :]
        o_t = acc[:, :tq] / l[:, :tq] - lam * (acc[:, tq:] / l[:, tq:])
        o_t = o_t * lax.rsqrt(jnp.mean(o_t * o_t, axis=0, keepdims=True) + NORM_EPS)
        o_ref[st * tq:(st + 1) * tq, :] = (o_t.T * sg_ref[...] * (1.0 - lam_init)).astype(o_ref.dtype)


def _attn(q, k, vt, lam_vecs, sub_g, qg, kg, lam_init, batch, seq, layer):
    t = q.shape[0]
    tk = vt.shape[2]
    tq = ATTN_TQ
    n_st = tk // tq
    nq = seq // tk
    n_kv = seq // tk
    hw = 2 * HEAD_B
    return pl.pallas_call(
        functools.partial(_attn_body, lam_init=lam_init),
        grid=(batch, N_HEADS_B, nq),
        in_specs=[pl.BlockSpec((tk, hw), lambda b, h, i: (b * nq + i, h)),
                  pl.BlockSpec((seq, hw), lambda b, h, i: (b, h)),
                  pl.BlockSpec((n_kv, V_ROWS, tk), lambda b, h, i: (b, h, 0)),
                  _layer_spec((4, HEAD_B), layer), _layer_spec((1, hw), layer),
                  _layer_spec((1, D_B), layer), _layer_spec((1, D_B), layer)],
        out_specs=pl.BlockSpec((tk, hw), lambda b, h, i: (b * nq + i, h)),
        out_shape=jax.ShapeDtypeStruct((t, D_B), BF16),
        scratch_shapes=[pltpu.VMEM((n_st, 2 * hw, 2 * tq), BF16),
                        pltpu.VMEM((3, tk, hw), BF16),
                        pltpu.VMEM((tk, tq), F32),
                        pltpu.VMEM((2, n_st, tk, 2 * tq), F32),
                        pltpu.VMEM((2, n_st, 1, 2 * tq), F32),
                        pltpu.VMEM((2, n_st, tk, 2 * tq), BF16),
                        pltpu.VMEM((2, n_st, 1, 2 * tq), F32),
                        pltpu.VMEM((n_st, 1, 2 * tq), F32),
                        pltpu.VMEM((n_st, V_ROWS, 2 * tq), F32)],
        compiler_params=_params(3),
        name="diff_attn",
    )(q, k, vt, lam_vecs, sub_g, qg, kg)


def _merge_body(yf_ref, yb_ref, gate_ref, bonus_ref, ob_ref, gt_ref, x_ref, lng_ref, lnb_ref, bd_ref,
                wb0_ref, wb1_ref, wo_ref, o_ref):
    bd = bd_ref[...]
    y = yf_ref[...] + yb_ref[...]
    yc = y - _mm_split_lhs(y, bd)
    var = _mm_split_lhs(yc * yc, bd)
    yn = yc * lax.rsqrt(var + GN_EPS) * lng_ref[...] + lnb_ref[...]
    oa = (yn + bonus_ref[...]) * gate_ref[...]
    gts = gt_ref[...].astype(F32)
    merged = (gts[:, :D_MODEL] * _mm(oa, wb0_ref[...])
              + gts[:, D_MODEL:] * jnp.dot(ob_ref[...], wb1_ref[...], preferred_element_type=F32))
    o_ref[...] = x_ref[...] + _mm(merged, wo_ref[...])


def _merge(yf, yb, gate, bonus, ob, gts, x2, ln_g, ln_b, bd_mean, wb, wo, layer):
    t = x2.shape[0]
    tm = min(TOKEN_TILE, t)
    row = lambda width: pl.BlockSpec((tm, width), lambda i: (i, 0))
    return pl.pallas_call(
        _merge_body,
        grid=(t // tm,),
        in_specs=[row(D_A), row(D_A), row(D_A), row(D_A), row(D_B), row(2 * D_MODEL), row(D_MODEL),
                  _layer_spec((1, D_A), layer), _layer_spec((1, D_A), layer), _const_spec((D_A, D_A)),
                  _layer_spec((D_A, D_MODEL), layer, 0), _layer_spec((D_B, D_MODEL), layer, 1),
                  _layer_spec((D_MODEL, D_MODEL), layer)],
        out_specs=row(D_MODEL),
        out_shape=jax.ShapeDtypeStruct((t, D_MODEL), F32),
        compiler_params=_params(1),
        name="merge",
    )(yf, yb, gate, bonus, ob, gts, x2, ln_g, ln_b, bd_mean, wb, wb, wo)


def _block_diag2(w):
    z = jnp.zeros_like(w[:, 0])
    return jnp.concatenate([jnp.concatenate([w[:, 0], z], axis=2), jnp.concatenate([z, w[:, 1]], axis=2)],
                           axis=1)


def kernel(x, norm_ffn1, ffn1_in, ffn1_out, norm_mix, w_in, rwkv_mu, decay_w0, decay_w2, iclr_a0, iclr_a2,
           gate_g2, k_k, k_a, r_k, ln_x_g, ln_x_b, q_gain, k_gain, diff_lambda, subln_g, w_branch, w_out,
           norm_ffn2, ffn2_in, ffn2_out):
    batch, seq, d = x.shape
    depth = norm_ffn1.shape[0]
    assert seq % TOKEN_TILE == 0 and d == D_MODEL
    x2 = x.reshape(batch * seq, d)
    head_id = jnp.arange(D_A) // HEAD_A
    same = head_id[:, None] == head_id[None, :]
    bd_sum = same.astype(BF16)
    bd_mean = (same.astype(F32) / HEAD_A).astype(BF16)
    rows = lambda a: a.reshape(depth, 1, -1)
    bf = lambda a: a.astype(BF16)
    n_qk = D_B // HEAD_B
    qg = rows(jnp.tile(q_gain, (1, n_qk))) * (HEAD_B ** -0.5 * LOG2E)
    kg = rows(jnp.tile(k_gain, (1, n_qk)))
    ffn1 = (rows(norm_ffn1), bf(ffn1_in), bf(ffn1_out))
    ffn2 = (rows(norm_ffn2), bf(ffn2_in), bf(ffn2_out))
    proj_w = (rows(norm_mix), bf(w_in))
    prep_w = (rows(rwkv_mu), rows(decay_w0), bf(_block_diag2(decay_w2)), rows(iclr_a0),
              bf(_block_diag2(iclr_a2)), bf(gate_g2), rows(k_k), rows(k_a), rows(r_k))
    merge_w = (rows(ln_x_g), rows(ln_x_b), bd_mean, bf(w_branch), bf(w_out))
    sub_g = rows(subln_g)
    for l in range(depth):
        lam_init = 0.8 - 0.6 * math.exp(-0.3 * l)
        x2 = _ffn(x2, *ffn1, l)
        q, k, vt, gts, r, vv, kk, lw0, k0, a0, lw1, k1, a1, gate, bonus = _proj(
            x2, seq, *proj_w, qg, kg, bd_mean, *prep_w, bd_sum, l)
        yf, yb = _scan(r, vv, kk, lw0, k0, a0, lw1, k1, a1, batch, seq)
        ob = _attn(q, k, vt, diff_lambda, sub_g, qg, kg, lam_init, batch, seq, l)
        x2 = _merge(yf, yb, gate, bonus, ob, gts, x2, *merge_w, l)
        x2 = _ffn(x2, *ffn2, l)
    return x2.reshape(batch, seq, d)
```

```python
import functools
import math

import jax
import jax.numpy as jnp
from jax import lax
from jax.experimental import pallas as pl
from jax.experimental.pallas import tpu as pltpu

F32 = jnp.float32
BF16 = jnp.bfloat16

D_MODEL = 1024
D_A = 512
HEAD_A = 64
LORA = 64
LORA_GATE = 128
C_RWKV = 3 * D_A + 2 * LORA + 2 * LORA + LORA_GATE
D_B = 512
HEAD_B = 64
N_HEADS_B = D_B // (2 * HEAD_B)
D_FF = 2816
DECAY_SCALE = math.exp(-0.5)
GN_EPS = 64e-5
NORM_EPS = 1e-6
LOG2E = math.log2(math.e)
F32_ZERO_EXP2 = 150.0
ROUNDING_MARGIN = 1.02

LANES = 128
SUBLANES = 8
VMEM_LIMIT = 56 * 1024 * 1024

TOKEN_TILE = 512
MXU_WIDTH = 256
FF_SPLITS = (0, 6 * MXU_WIDTH, D_FF)
CHUNK = 64
SCAN_CHUNKS_PER_STEP = 2
ATTN_TQ = 256
V_ROWS = 2 * HEAD_B + 16

_NT = (((1,), (1,)), ((), ()))


def _mm(a, b):
    return jnp.dot(a.astype(BF16), b.astype(BF16), preferred_element_type=F32)


def _mm_nt(a, b):
    return lax.dot_general(a.astype(BF16), b.astype(BF16), _NT, preferred_element_type=F32)


def _split_bf16(x, n):
    parts = []
    for _ in range(n - 1):
        hi = x.astype(BF16)
        parts.append(hi)
        x = x - hi.astype(F32)
    parts.append(x.astype(BF16))
    return parts


def _mm_split_lhs(x, w, n=2):
    return sum(jnp.dot(p, w, preferred_element_type=F32) for p in _split_bf16(x, n))


def _rms_norm(x, g):
    return x * lax.rsqrt(jnp.mean(x * x, axis=-1, keepdims=True) + NORM_EPS) * g


def _const_spec(shape):
    return pl.BlockSpec(shape, lambda *_: (0,) * len(shape), pipeline_mode=pl.Buffered(1))


def _layer_spec(shape, layer, *lead):
    index = (layer,) + lead + (0,) * len(shape)
    return pl.BlockSpec((None,) * (1 + len(lead)) + shape, lambda *_: index, pipeline_mode=pl.Buffered(1))


def _params(n_axes):
    return pltpu.CompilerParams(dimension_semantics=("arbitrary",) * n_axes,
                                vmem_limit_bytes=VMEM_LIMIT)


def _ffn_body(x_ref, g_ref, win_ref, wout_ref, o_ref):
    x = x_ref[...]
    h = _rms_norm(x, g_ref[...]).astype(BF16)
    acc = jnp.zeros(x.shape, F32)
    for lo, hi in zip(FF_SPLITS[:-1], FF_SPLITS[1:]):
        gate = jnp.dot(h, win_ref[:, lo:hi], preferred_element_type=F32)
        up = jnp.dot(h, win_ref[:, D_FF + lo:D_FF + hi], preferred_element_type=F32)
        act = (gate * jax.nn.sigmoid(gate) * up).astype(BF16)
        acc = acc + jnp.dot(act, wout_ref[lo:hi, :], preferred_element_type=F32)
    o_ref[...] = x + 0.5 * acc


def _ffn(x2, g, w_in, w_out, layer):
    t = x2.shape[0]
    tm = min(TOKEN_TILE, t)
    return pl.pallas_call(
        _ffn_body,
        grid=(t // tm,),
        in_specs=[pl.BlockSpec((tm, D_MODEL), lambda i: (i, 0)),
                  _layer_spec((1, D_MODEL), layer),
                  _layer_spec((D_MODEL, 2 * D_FF), layer),
                  _layer_spec((D_FF, D_MODEL), layer)],
        out_specs=pl.BlockSpec((tm, D_MODEL), lambda i: (i, 0)),
        out_shape=jax.ShapeDtypeStruct((t, D_MODEL), F32),
        compiler_params=_params(1),
        name="ffn",
    )(x2, g, w_in, w_out)


_Q0 = C_RWKV
_K0 = C_RWKV + D_B
_V0 = C_RWKV + 2 * D_B
_G0 = C_RWKV + 3 * D_B


def _proj_body(x_ref, xp_ref, xn_ref, g_ref, w_ref, qg_ref, kg_ref, bdm_ref, mu_ref, w0_ref, w2_ref, a0_ref,
               a2_ref, g2_ref, kk_ref, ka_ref, rk_ref, bds_ref,
               q_ref, k_ref, vt_ref, gt_ref, r_out, v_out, kk_out, lw0_out, k0_out, a0_out, lw1_out, k1_out,
               a1_out, gate_out, bonus_out, *, seq):
    i = pl.program_id(0)
    g = g_ref[...]
    h = _rms_norm(x_ref[...], g).astype(BF16)
    tm = h.shape[0]

    bd = bdm_ref[...]
    q = jnp.dot(h, w_ref[:, _Q0:_K0], preferred_element_type=F32)
    q_ref[...] = (q * lax.rsqrt(_mm(q * q, bd) + NORM_EPS) * qg_ref[...]).astype(BF16)
    k = jnp.dot(h, w_ref[:, _K0:_V0], preferred_element_type=F32)
    k_ref[...] = (k * lax.rsqrt(_mm(k * k, bd) + NORM_EPS) * kg_ref[...]).astype(BF16)
    vt = jnp.dot(h, w_ref[:, _V0:_G0], preferred_element_type=F32).T.astype(BF16)
    hw = 2 * HEAD_B
    ones = jnp.ones((V_ROWS - hw, vt.shape[1]), BF16)
    vt_ref[0] = jnp.concatenate([part for hd in range(N_HEADS_B)
                                 for part in (vt[hd * hw:(hd + 1) * hw], ones)], axis=0)
    gt_ref[...] = jax.nn.sigmoid(jnp.dot(h, w_ref[:, _G0:], preferred_element_type=F32)).astype(BF16)

    w_rwkv = w_ref[:, :C_RWKV]
    p = jnp.dot(h, w_rwkv, preferred_element_type=F32)
    edge = jnp.concatenate([xp_ref[...], xn_ref[...]], axis=0)
    p_edge = jnp.dot(_rms_norm(edge, g).astype(BF16), w_rwkv, preferred_element_type=F32)
    keep_prev = jnp.where((i * tm) % seq == 0, 0.0, 1.0)
    keep_next = jnp.where(((i + 1) * tm) % seq == 0, 0.0, 1.0)
    row = lax.broadcasted_iota(jnp.int32, p.shape, 0)
    prev = jnp.where(row == 0, keep_prev * p_edge[SUBLANES - 1:SUBLANES, :], pltpu.roll(p, 1, 0))
    nxt = jnp.where(row == tm - 1, keep_next * p_edge[SUBLANES:SUBLANES + 1, :], pltpu.roll(p, tm - 1, 0))
    p = p + mu_ref[...] * (0.5 * (prev + nxt) - p)

    r = p[:, :D_A]
    k = p[:, D_A:2 * D_A]
    v = p[:, 2 * D_A:3 * D_A]
    dw = p[:, 3 * D_A:3 * D_A + 2 * LORA]
    da = p[:, 3 * D_A + 2 * LORA:3 * D_A + 4 * LORA]
    dg = p[:, 3 * D_A + 4 * LORA:]
    bd = bds_ref[...]

    logw = -DECAY_SCALE * jax.nn.sigmoid(w0_ref[...] + _mm(jnp.tanh(dw), w2_ref[...]))
    iclr = jax.nn.sigmoid(a0_ref[...] + _mm(da, a2_ref[...]))
    gate_out[...] = _mm(jax.nn.sigmoid(dg), g2_ref[...])

    kk = k * kk_ref[...]
    norm = jnp.sqrt(_mm(kk * kk, bd))
    kk_out[...] = (kk / jnp.maximum(norm, 1e-12)).astype(kk_out.dtype)

    ka = ka_ref[...]
    a_f = iclr[:, :D_A]
    a_b = iclr[:, D_A:]
    k_f = k * (1.0 + (a_f - 1.0) * ka)
    k_b = k * (1.0 + (a_b - 1.0) * ka)
    bonus = _mm_split_lhs(r * rk_ref[...] * (k_f + k_b), bd)
    r_out[...] = r.astype(r_out.dtype)
    v_out[...] = v.astype(v_out.dtype)
    lw0_out[...] = logw[:, :D_A]
    k0_out[...] = k_f.astype(k0_out.dtype)
    a0_out[...] = a_f.astype(a0_out.dtype)
    lw1_out[...] = logw[:, D_A:]
    k1_out[...] = k_b.astype(k1_out.dtype)
    a1_out[...] = a_b.astype(a1_out.dtype)
    bonus_out[...] = bonus * v


def _proj(x2, seq, g, w, qg, kg, bd_mean, mu, w0, w2cat, a0, a2cat, g2, k_k, k_a, r_k, bd_sum, layer):
    t = x2.shape[0]
    tm = TOKEN_TILE
    c_in = w.shape[2]
    nb = tm // SUBLANES
    last = t // SUBLANES - 1
    row = lambda width: pl.BlockSpec((tm, width), lambda i: (i, 0))
    wide = jax.ShapeDtypeStruct((t, D_A), F32)
    half = jax.ShapeDtypeStruct((t, D_A), BF16)
    return pl.pallas_call(
        functools.partial(_proj_body, seq=seq),
        grid=(t // tm,),
        in_specs=[row(D_MODEL),
                  pl.BlockSpec((SUBLANES, D_MODEL), lambda i: (jnp.maximum(i * nb - 1, 0), 0)),
                  pl.BlockSpec((SUBLANES, D_MODEL), lambda i: (jnp.minimum((i + 1) * nb, last), 0)),
                  _layer_spec((1, D_MODEL), layer), _layer_spec((D_MODEL, c_in), layer),
                  _layer_spec((1, D_B), layer), _layer_spec((1, D_B), layer), _const_spec((D_B, D_B)),
                  _layer_spec((1, C_RWKV), layer),
                  _layer_spec((1, 2 * D_A), layer), _layer_spec((2 * LORA, 2 * D_A), layer),
                  _layer_spec((1, 2 * D_A), layer), _layer_spec((2 * LORA, 2 * D_A), layer),
                  _layer_spec((LORA_GATE, D_A), layer),
                  _layer_spec((1, D_A), layer), _layer_spec((1, D_A), layer), _layer_spec((1, D_A), layer),
                  _const_spec((D_A, D_A))],
        out_specs=[row(D_B), row(D_B),
                   pl.BlockSpec((1, N_HEADS_B * V_ROWS, tm), lambda i: (i, 0, 0)), row(2 * D_MODEL)]
                  + [row(D_A)] * 11,
        out_shape=[jax.ShapeDtypeStruct((t, D_B), BF16),
                   jax.ShapeDtypeStruct((t, D_B), BF16),
                   jax.ShapeDtypeStruct((t // tm, N_HEADS_B * V_ROWS, tm), BF16),
                   jax.ShapeDtypeStruct((t, 2 * D_MODEL), BF16),
                   half, half, half, wide, half, half, wide, half, half, wide, wide],
        compiler_params=_params(1),
        name="proj",
    )(x2, x2, x2, g, w, qg, kg, bd_mean, mu, w0, w2cat, a0, a2cat, g2, k_k, k_a, r_k, bd_sum)


def _scan_body(rf, vf, kkf, lwf, kf, af, rb, vb, kkb, lwb, kb, ab, yf_ref, yb_ref, h_ref):
    @pl.when(pl.program_id(1) == 0)
    def _():
        h_ref[...] = jnp.zeros(h_ref.shape, F32)

    L = CHUNK
    n_sub = rf.shape[0] // L
    n_pairs = D_A // LANES
    ti = lax.broadcasted_iota(jnp.int32, (L, L), 0)
    tj = lax.broadcasted_iota(jnp.int32, (L, L), 1)
    pi = lax.broadcasted_iota(jnp.int32, (L, LANES), 0)
    lane = lax.broadcasted_iota(jnp.int32, (L, LANES), 1)
    pj = lane & (HEAD_A - 1)
    lane_lo = lane < HEAD_A
    eye = jnp.where(pj == pi, 1.0, 0.0)
    blk = lambda s: (pi // s) == (pj // s)
    row2 = lax.broadcasted_iota(jnp.int32, (2 * L, LANES), 0)
    col2 = lax.broadcasted_iota(jnp.int32, (2 * L, LANES), 1)
    same_head = (row2 // HEAD_A) == (col2 // HEAD_A)
    diag2 = row2 == col2
    zeros_l = jnp.zeros((L, LANES), F32)
    zeros_2l = jnp.zeros((2 * L, LANES), F32)

    def stack(x):
        return jnp.concatenate([jnp.where(lane_lo, x, 0.0), jnp.where(lane_lo, 0.0, x)], axis=0)

    def mmp(x, y):
        return _mm(x, stack(y))

    cat = jnp.concatenate
    chains = []
    for reverse, refs in ((False, (rf, vf, kkf, lwf, kf, af)), (True, (rb, vb, kkb, lwb, kb, ab))):
        tri = ((tj >= ti) if reverse else (tj <= ti)).astype(BF16)
        strict = (pj > pi) if reverse else (pj < pi)
        incl = (pj >= pi) if reverse else (pj <= pi)
        steps = []
        for sub in (range(n_sub - 1, -1, -1) if reverse else range(n_sub)):
            rows = slice(sub * L, (sub + 1) * L)
            r, v, kk, lw, k, a = (ref[rows, :].astype(F32) for ref in refs)
            G = sum(jnp.dot(tri, part, preferred_element_type=F32) for part in _split_bf16(lw, 3))
            g_end = G[0:1] if reverse else G[L - 1:L]
            gam_end = jnp.exp(g_end)
            inv_gam = jnp.exp(-G)
            rel_end = jnp.exp(g_end - G)
            b = kk * a
            full = dict(At=-kk * jnp.exp(G - lw), Bt=b * inv_gam, Kt=k * inv_gam, Rt=r * jnp.exp(G),
                        Kh=k * rel_end, Bh=b * rel_end, v=v)
            pairs = []
            for p in range(n_pairs):
                sl = slice(p * LANES, (p + 1) * LANES)
                c = {name: val[:, sl] for name, val in full.items()}
                c.update(strict=strict, incl=incl, gam_end=gam_end[:, sl], rows=rows)
                pairs.append(c)
            steps.append(pairs)
        chains.append(steps)
    flat = [c for steps in chains for pairs in steps for c in pairs]

    for c in flat:
        a4 = _mm_nt(cat([c["At"], c["Rt"]], axis=0), cat([stack(c["Bt"]), stack(c["Kt"])], axis=0))
        c["A_ab"] = jnp.where(c["strict"], a4[:L, :LANES], 0.0)
        c["A_ak"] = jnp.where(c["strict"], a4[:L, LANES:], 0.0)
        c["A_rb"] = jnp.where(c["incl"], a4[L:, :LANES], 0.0)
        c["A_rk"] = jnp.where(c["incl"], a4[L:, LANES:], 0.0)
    for c in flat:
        c["A0"] = jnp.where(blk(8), c["A_ab"], 0.0)
        c["A2"] = mmp(c["A0"], c["A0"])
    for c in flat:
        c["A4"] = mmp(c["A2"], c["A2"])
        c["T"] = eye + c["A0"]
    for c in flat:
        c["T"] = c["T"] + mmp(c["T"], c["A2"])
    for c in flat:
        c["T"] = c["T"] + mmp(c["T"], c["A4"])
    s = 8
    while s < L:
        outer = blk(2 * s) & jnp.logical_not(blk(s))
        for c in flat:
            c["TO"] = mmp(c["T"], jnp.where(outer, c["A_ab"], 0.0))
        for c in flat:
            c["T"] = c["T"] + mmp(c["TO"], c["T"])
        s *= 2
    for c in flat:
        c["AkV"] = mmp(c["A_ak"], c["v"])
    for c in flat:
        wu = _mm(c["T"], cat([stack(c["At"]), stack(c["AkV"])], axis=1))
        c["W"], c["U0"] = wu[:, :LANES], wu[:, LANES:]
    for c in flat:
        kb_t = cat([c["Kh"], c["Bh"]], axis=0).T
        mn = _mm(kb_t, cat([cat([zeros_l, c["v"]], axis=1), cat([c["W"], c["U0"]], axis=1)], axis=0))
        c["M"] = jnp.where(same_head, mn[:, :LANES], 0.0) + jnp.where(diag2, c["gam_end"], 0.0)
        c["N"] = jnp.where(same_head, mn[:, LANES:], 0.0)
    for c in flat:
        qy = _mm(cat([c["A_rb"], c["A_rk"]], axis=1),
                 cat([cat([stack(c["W"]), stack(c["U0"])], axis=1),
                      cat([zeros_2l, stack(c["v"])], axis=1)], axis=0))
        c["Q"] = c["Rt"] + qy[:, :LANES]
        c["Y0"] = qy[:, LANES:]
    for d, (steps, y_ref) in enumerate(zip(chains, (yf_ref, yb_ref))):
        for pairs in steps:
            ys = []
            for p, c in enumerate(pairs):
                H = h_ref[d * n_pairs + p]
                ys.append(_mm(c["Q"], H) + c["Y0"])
                h_ref[d * n_pairs + p] = _mm(c["M"], H) + c["N"]
            y_ref[pairs[0]["rows"], :] = cat(ys, axis=1)


def _scan(r, v, kk, lw0, k0, a0, lw1, k1, a1, batch, seq):
    t = r.shape[0]
    rows = SCAN_CHUNKS_PER_STEP * CHUNK
    nc = seq // rows
    fwd = pl.BlockSpec((rows, D_A), lambda b, c: (b * nc + c, 0))
    bwd = pl.BlockSpec((rows, D_A), lambda b, c: (b * nc + nc - 1 - c, 0))
    out = jax.ShapeDtypeStruct((t, D_A), F32)
    return pl.pallas_call(
        _scan_body,
        grid=(batch, nc),
        in_specs=[fwd] * 6 + [bwd] * 6,
        out_specs=[fwd, bwd],
        out_shape=[out, out],
        scratch_shapes=[pltpu.VMEM((2 * (D_A // LANES), LANES, LANES), F32)],
        compiler_params=_params(2),
        name="rwkv_scan",
    )(r, v, kk, lw0, k0, a0, r, v, kk, lw1, k1, a1)


def _bias_features(slope, n, width, key_side):
    pos = lax.broadcasted_iota(jnp.int32, (n, width), 0).astype(F32)
    lane = lax.broadcasted_iota(jnp.int32, (n, width), 1)
    hi, mid, lo = (t.astype(F32) for t in _split_bf16(slope * pos, 3))
    terms = jnp.where(lane % 3 == 0, hi, jnp.where(lane % 3 == 1, mid, lo))
    if key_side:
        return jnp.where(lane < 3, -1.0, jnp.where(lane < 6, terms, 0.0))
    return jnp.where(lane < 3, terms, jnp.where(lane < 6, 1.0, 0.0))


def _attn_body(q_ref, k_ref, vt_ref, lv_ref, sg_ref, qg_ref, kg_ref, o_ref, qs_ref, feat_ref, src_ref,
               s_ref, mx_ref, p_ref, al_ref, m_ref, acc_ref, *, lam_init):
    tq = src_ref.shape[1]
    n_kv, _, tk = vt_ref.shape
    seq = n_kv * tk
    h = pl.program_id(1)
    i = pl.program_id(2)
    slope_bits = (127 - 2 * (h + 1)) << 23
    slope = (lax.bitcast_convert_type(jnp.full((1, 1), slope_bits, jnp.int32), jnp.float32).astype(F32)
             * LOG2E)
    bound = (HEAD_B * ROUNDING_MARGIN * jnp.max(jnp.abs(qg_ref[...]), axis=-1, keepdims=True)
             * jnp.max(jnp.abs(kg_ref[...]), axis=-1, keepdims=True))
    reach = jnp.minimum((F32_ZERO_EXP2 + 2.0 * bound) / slope + 1.0, float(seq)).astype(jnp.int32)[0, 0]

    n_st = tk // tq
    hw = 2 * HEAD_B

    @pl.when(i == 0)
    def _():
        kv_pos = lax.broadcasted_iota(jnp.int32, (tk, tq), 0)
        q_pos = lax.broadcasted_iota(jnp.int32, (tk, tq), 1)
        src_ref[...] = slope * (q_pos - kv_pos).astype(F32)
        key_feat = _bias_features(slope, tk, hw, key_side=True)
        feat_ref[0] = key_feat.astype(BF16)
        feat_ref[1] = (-key_feat).astype(BF16)
        feat_ref[2] = jnp.zeros((tk, hw), BF16)
        q_feat = _bias_features(slope, tq, hw, key_side=False).T.astype(BF16)
        for st in range(n_st):
            qs_ref[st, hw:, :tq] = q_feat
            qs_ref[st, hw:, tq:] = q_feat

    chan = lax.broadcasted_iota(jnp.int32, (hw, tq), 0)
    for st in range(n_st):
        q_t = q_ref[st * tq:(st + 1) * tq, :].astype(F32).T.astype(BF16)
        zero = jnp.zeros_like(q_t)
        qs_ref[st, :hw, :tq] = jnp.where(chan < HEAD_B, q_t, zero)
        qs_ref[st, :hw, tq:] = jnp.where(chan < HEAD_B, zero, q_t)
    q0 = i * tk
    j_lo = jnp.maximum(q0 - reach, 0) // tk
    j_hi = jnp.minimum(q0 + tk - 1 + reach, seq - 1) // tk + 1
    n_left = i - j_lo
    n_off = n_left + (j_hi - i - 1)

    def scores(j, side, st):
        kb = k_ref[pl.ds(pl.multiple_of(j * tk, tk), tk), :]
        return jnp.dot(jnp.concatenate([kb, feat_ref[side]], axis=1), qs_ref[st],
                       preferred_element_type=F32)

    def softmax_update(s, s_max, cst, st):
        m_old = m_ref[st]
        m_new = jnp.maximum(m_old, s_max + cst)
        alpha = jnp.exp2(m_old - m_new)
        p = jnp.exp2(s - (m_new - cst))
        m_ref[st] = m_new
        return alpha, p.astype(BF16)

    def block(n):
        j = jnp.clip(j_lo + n + (n >= n_left).astype(jnp.int32), 0, n_kv - 1)
        j = jnp.where(n < 0, i, j)
        return j, (j > i).astype(jnp.int32), jnp.logical_and(n >= 0, n < n_off)

    def qk_stage(n, slot):
        j, side, _ = block(n)
        for st in range(n_st):
            s = scores(j, side, st)
            s_ref[slot, st] = s
            mx_ref[slot, st] = jnp.max(s, axis=0, keepdims=True)

    def softmax_stage(n, slot):
        j, _, valid = block(n)
        for st in range(n_st):
            dist0 = jnp.abs(q0 + st * tq - j * tk).astype(F32)
            cst = jnp.where(valid, -slope * dist0, -1e30)
            alpha, p = softmax_update(s_ref[slot, st], mx_ref[slot, st], cst, st)
            al_ref[slot, st] = alpha
            p_ref[slot, st] = p

    def pv_stage(n, slot):
        j, _, _ = block(n)
        for st in range(n_st):
            acc_ref[st] = al_ref[slot, st] * acc_ref[st] + jnp.dot(vt_ref[j], p_ref[slot, st],
                                                                    preferred_element_type=F32)

    m_ref[...] = jnp.full(m_ref.shape, -1e30, F32)
    acc_ref[...] = jnp.zeros(acc_ref.shape, F32)
    diag_scores = [scores(i, 2, st) for st in range(n_st)]
    qk_stage(0, 0)
    src = src_ref[...]
    for st in range(n_st):
        bias = jnp.abs(src + slope * float(st * tq))
        s = diag_scores[st] - jnp.concatenate([bias, bias], axis=1)
        alpha, p = softmax_update(s, jnp.max(s, axis=0, keepdims=True), jnp.zeros((1, 1), F32), st)
        al_ref[1, st] = alpha
        p_ref[1, st] = p

    def pipelined_pair(t, carry):
        n = 2 * t
        qk_stage(n + 1, 1)
        softmax_stage(n, 0)
        pv_stage(n - 1, 1)
        qk_stage(n + 2, 0)
        softmax_stage(n + 1, 1)
        pv_stage(n, 0)
        return carry

    n_pairs = (n_off + 1) // 2
    lax.fori_loop(0, n_pairs, pipelined_pair, 0)
    pv_stage(2 * n_pairs - 1, 1)

    lv = lv_ref[...]
    lam = (jnp.exp(jnp.sum(lv[0:1] * lv[1:2], axis=-1, keepdims=True))
           - jnp.exp(jnp.sum(lv[2:3] * lv[3:4], axis=-1, keepdims=True)) + lam_init)
    for st in range(n_st):
        acc = acc_ref[st, :hw, :]
        l = acc_ref[st, hw:hw + 1, :]
        o_t = acc[:, :tq] / l[:, :tq] - lam * (acc[:, tq:] / l[:, tq:])
        o_t = o_t * lax.rsqrt(jnp.mean(o_t * o_t, axis=0, keepdims=True) + NORM_EPS)
        o_ref[st * tq:(st + 1) * tq, :] = (o_t.T * sg_ref[...] * (1.0 - lam_init)).astype(o_ref.dtype)


def _attn(q, k, vt, lam_vecs, sub_g, qg, kg, lam_init, batch, seq, layer):
    t = q.shape[0]
    tk = vt.shape[2]
    tq = ATTN_TQ
    n_st = tk // tq
    nq = seq // tk
    n_kv = seq // tk
    hw = 2 * HEAD_B
    return pl.pallas_call(
        functools.partial(_attn_body, lam_init=lam_init),
        grid=(batch, N_HEADS_B, nq),
        in_specs=[pl.BlockSpec((tk, hw), lambda b, h, i: (b * nq + i, h)),
                  pl.BlockSpec((seq, hw), lambda b, h, i: (b, h)),
                  pl.BlockSpec((n_kv, V_ROWS, tk), lambda b, h, i: (b, h, 0)),
                  _layer_spec((4, HEAD_B), layer), _layer_spec((1, hw), layer),
                  _layer_spec((1, D_B), layer), _layer_spec((1, D_B), layer)],
        out_specs=pl.BlockSpec((tk, hw), lambda b, h, i: (b * nq + i, h)),
        out_shape=jax.ShapeDtypeStruct((t, D_B), BF16),
        scratch_shapes=[pltpu.VMEM((n_st, 2 * hw, 2 * tq), BF16),
                        pltpu.VMEM((3, tk, hw), BF16),
                        pltpu.VMEM((tk, tq), F32),
                        pltpu.VMEM((2, n_st, tk, 2 * tq), F32),
                        pltpu.VMEM((2, n_st, 1, 2 * tq), F32),
                        pltpu.VMEM((2, n_st, tk, 2 * tq), BF16),
                        pltpu.VMEM((2, n_st, 1, 2 * tq), F32),
                        pltpu.VMEM((n_st, 1, 2 * tq), F32),
                        pltpu.VMEM((n_st, V_ROWS, 2 * tq), F32)],
        compiler_params=_params(3),
        name="diff_attn",
    )(q, k, vt, lam_vecs, sub_g, qg, kg)


def _merge_body(yf_ref, yb_ref, gate_ref, bonus_ref, ob_ref, gt_ref, x_ref, lng_ref, lnb_ref, bd_ref,
                wb0_ref, wb1_ref, wo_ref, o_ref):
    bd = bd_ref[...]
    y = yf_ref[...] + yb_ref[...]
    yc = y - _mm_split_lhs(y, bd)
    var = _mm(yc * yc, bd)
    yn = yc * lax.rsqrt(var + GN_EPS) * lng_ref[...] + lnb_ref[...]
    oa = (yn + bonus_ref[...]) * gate_ref[...]
    gts = gt_ref[...].astype(F32)
    merged = (gts[:, :D_MODEL] * _mm(oa, wb0_ref[...])
              + gts[:, D_MODEL:] * jnp.dot(ob_ref[...], wb1_ref[...], preferred_element_type=F32))
    o_ref[...] = x_ref[...] + _mm(merged, wo_ref[...])


def _merge(yf, yb, gate, bonus, ob, gts, x2, ln_g, ln_b, bd_mean, wb, wo, layer):
    t = x2.shape[0]
    tm = min(TOKEN_TILE, t)
    row = lambda width: pl.BlockSpec((tm, width), lambda i: (i, 0))
    return pl.pallas_call(
        _merge_body,
        grid=(t // tm,),
        in_specs=[row(D_A), row(D_A), row(D_A), row(D_A), row(D_B), row(2 * D_MODEL), row(D_MODEL),
                  _layer_spec((1, D_A), layer), _layer_spec((1, D_A), layer), _const_spec((D_A, D_A)),
                  _layer_spec((D_A, D_MODEL), layer, 0), _layer_spec((D_B, D_MODEL), layer, 1),
                  _layer_spec((D_MODEL, D_MODEL), layer)],
        out_specs=row(D_MODEL),
        out_shape=jax.ShapeDtypeStruct((t, D_MODEL), F32),
        compiler_params=_params(1),
        name="merge",
    )(yf, yb, gate, bonus, ob, gts, x2, ln_g, ln_b, bd_mean, wb, wb, wo)


def _block_diag2(w):
    z = jnp.zeros_like(w[:, 0])
    return jnp.concatenate([jnp.concatenate([w[:, 0], z], axis=2), jnp.concatenate([z, w[:, 1]], axis=2)],
                           axis=1)


def kernel(x, norm_ffn1, ffn1_in, ffn1_out, norm_mix, w_in, rwkv_mu, decay_w0, decay_w2, iclr_a0, iclr_a2,
           gate_g2, k_k, k_a, r_k, ln_x_g, ln_x_b, q_gain, k_gain, diff_lambda, subln_g, w_branch, w_out,
           norm_ffn2, ffn2_in, ffn2_out):
    batch, seq, d = x.shape
    depth = norm_ffn1.shape[0]
    assert seq % TOKEN_TILE == 0 and d == D_MODEL
    x2 = x.reshape(batch * seq, d)
    head_id = jnp.arange(D_A) // HEAD_A
    same = head_id[:, None] == head_id[None, :]
    bd_sum = same.astype(BF16)
    bd_mean = (same.astype(F32) / HEAD_A).astype(BF16)
    rows = lambda a: a.reshape(depth, 1, -1)
    bf = lambda a: a.astype(BF16)
    n_qk = D_B // HEAD_B
    qg = rows(jnp.tile(q_gain, (1, n_qk))) * (HEAD_B ** -0.5 * LOG2E)
    kg = rows(jnp.tile(k_gain, (1, n_qk)))
    ffn1 = (rows(norm_ffn1), bf(ffn1_in), bf(ffn1_out))
    ffn2 = (rows(norm_ffn2), bf(ffn2_in), bf(ffn2_out))
    proj_w = (rows(norm_mix), bf(w_in))
    prep_w = (rows(rwkv_mu), rows(decay_w0), bf(_block_diag2(decay_w2)), rows(iclr_a0),
              bf(_block_diag2(iclr_a2)), bf(gate_g2), rows(k_k), rows(k_a), rows(r_k))
    merge_w = (rows(ln_x_g), rows(ln_x_b), bd_mean, bf(w_branch), bf(w_out))
    sub_g = rows(subln_g)
    for l in range(depth):
        lam_init = 0.8 - 0.6 * math.exp(-0.3 * l)
        x2 = _ffn(x2, *ffn1, l)
        q, k, vt, gts, r, vv, kk, lw0, k0, a0, lw1, k1, a1, gate, bonus = _proj(
            x2, seq, *proj_w, qg, kg, bd_mean, *prep_w, bd_sum, l)
        yf, yb = _scan(r, vv, kk, lw0, k0, a0, lw1, k1, a1, batch, seq)
        ob = _attn(q, k, vt, diff_lambda, sub_g, qg, kg, lam_init, batch, seq, l)
        x2 = _merge(yf, yb, gate, bonus, ob, gts, x2, *merge_w, l)
        x2 = _ffn(x2, *ffn2, l)
    return x2.reshape(batch, seq, d)
```

```python
import functools
import math

import jax
import jax.numpy as jnp
from jax import lax
from jax.experimental import pallas as pl
from jax.experimental.pallas import tpu as pltpu

F32 = jnp.float32
BF16 = jnp.bfloat16

D_MODEL = 1024
D_A = 512
HEAD_A = 64
LORA = 64
LORA_GATE = 128
C_RWKV = 3 * D_A + 2 * LORA + 2 * LORA + LORA_GATE
D_B = 512
HEAD_B = 64
N_HEADS_B = D_B // (2 * HEAD_B)
D_FF = 2816
DECAY_SCALE = math.exp(-0.5)
GN_EPS = 64e-5
NORM_EPS = 1e-6
LOG2E = math.log2(math.e)
F32_ZERO_EXP2 = 150.0
ROUNDING_MARGIN = 1.02

LANES = 128
SUBLANES = 8
VMEM_LIMIT = 56 * 1024 * 1024

TOKEN_TILE = 512
MXU_WIDTH = 256
FF_SPLITS = (0, 6 * MXU_WIDTH, D_FF)
CHUNK = 64
SCAN_CHUNKS_PER_STEP = 2
ATTN_TQ = 256
V_ROWS = 2 * HEAD_B + 16

_NT = (((1,), (1,)), ((), ()))


def _mm(a, b):
    return jnp.dot(a.astype(BF16), b.astype(BF16), preferred_element_type=F32)


def _mm_nt(a, b):
    return lax.dot_general(a.astype(BF16), b.astype(BF16), _NT, preferred_element_type=F32)


def _split_bf16(x, n):
    parts = []
    for _ in range(n - 1):
        hi = x.astype(BF16)
        parts.append(hi)
        x = x - hi.astype(F32)
    parts.append(x.astype(BF16))
    return parts


def _mm_split_lhs(x, w, n=2):
    return sum(jnp.dot(p, w, preferred_element_type=F32) for p in _split_bf16(x, n))


def _rms_norm(x, g):
    return x * lax.rsqrt(jnp.mean(x * x, axis=-1, keepdims=True) + NORM_EPS) * g


def _const_spec(shape):
    return pl.BlockSpec(shape, lambda *_: (0,) * len(shape), pipeline_mode=pl.Buffered(1))


def _layer_spec(shape, layer, *lead):
    index = (layer,) + lead + (0,) * len(shape)
    return pl.BlockSpec((None,) * (1 + len(lead)) + shape, lambda *_: index, pipeline_mode=pl.Buffered(1))


def _params(n_axes):
    return pltpu.CompilerParams(dimension_semantics=("arbitrary",) * n_axes,
                                vmem_limit_bytes=VMEM_LIMIT)


def _ffn_body(x_ref, g_ref, win_ref, wout_ref, o_ref):
    x = x_ref[...]
    h = _rms_norm(x, g_ref[...]).astype(BF16)
    acc = jnp.zeros(x.shape, F32)
    for lo, hi in zip(FF_SPLITS[:-1], FF_SPLITS[1:]):
        gate = jnp.dot(h, win_ref[:, lo:hi], preferred_element_type=F32)
        up = jnp.dot(h, win_ref[:, D_FF + lo:D_FF + hi], preferred_element_type=F32)
        act = (gate * jax.nn.sigmoid(gate) * up).astype(BF16)
        acc = acc + jnp.dot(act, wout_ref[lo:hi, :], preferred_element_type=F32)
    o_ref[...] = x + 0.5 * acc


def _ffn(x2, g, w_in, w_out, layer):
    t = x2.shape[0]
    tm = min(TOKEN_TILE, t)
    return pl.pallas_call(
        _ffn_body,
        grid=(t // tm,),
        in_specs=[pl.BlockSpec((tm, D_MODEL), lambda i: (i, 0)),
                  _layer_spec((1, D_MODEL), layer),
                  _layer_spec((D_MODEL, 2 * D_FF), layer),
                  _layer_spec((D_FF, D_MODEL), layer)],
        out_specs=pl.BlockSpec((tm, D_MODEL), lambda i: (i, 0)),
        out_shape=jax.ShapeDtypeStruct((t, D_MODEL), F32),
        compiler_params=_params(1),
        name="ffn",
    )(x2, g, w_in, w_out)


_Q0 = C_RWKV
_K0 = C_RWKV + D_B
_V0 = C_RWKV + 2 * D_B
_G0 = C_RWKV + 3 * D_B


def _proj_body(x_ref, xp_ref, xn_ref, g_ref, w_ref, qg_ref, kg_ref, bdm_ref, mu_ref, w0_ref, w2_ref, a0_ref,
               a2_ref, g2_ref, kk_ref, ka_ref, rk_ref, bds_ref,
               q_ref, k_ref, vt_ref, gt_ref, r_out, v_out, kk_out, lw0_out, k0_out, a0_out, lw1_out, k1_out,
               a1_out, gate_out, bonus_out, *, seq):
    i = pl.program_id(0)
    g = g_ref[...]
    h = _rms_norm(x_ref[...], g).astype(BF16)
    tm = h.shape[0]

    bd = bdm_ref[...]
    q = jnp.dot(h, w_ref[:, _Q0:_K0], preferred_element_type=F32)
    q_ref[...] = (q * lax.rsqrt(_mm(q * q, bd) + NORM_EPS) * qg_ref[...]).astype(BF16)
    k = jnp.dot(h, w_ref[:, _K0:_V0], preferred_element_type=F32)
    k_ref[...] = (k * lax.rsqrt(_mm(k * k, bd) + NORM_EPS) * kg_ref[...]).astype(BF16)
    vt = jnp.dot(h, w_ref[:, _V0:_G0], preferred_element_type=F32).T.astype(BF16)
    hw = 2 * HEAD_B
    ones = jnp.ones((V_ROWS - hw, vt.shape[1]), BF16)
    vt_ref[0] = jnp.concatenate([part for hd in range(N_HEADS_B)
                                 for part in (vt[hd * hw:(hd + 1) * hw], ones)], axis=0)
    gt_ref[...] = jax.nn.sigmoid(jnp.dot(h, w_ref[:, _G0:], preferred_element_type=F32)).astype(BF16)

    w_rwkv = w_ref[:, :C_RWKV]
    p = jnp.dot(h, w_rwkv, preferred_element_type=F32)
    edge = jnp.concatenate([xp_ref[...], xn_ref[...]], axis=0)
    p_edge = jnp.dot(_rms_norm(edge, g).astype(BF16), w_rwkv, preferred_element_type=F32)
    keep_prev = jnp.where((i * tm) % seq == 0, 0.0, 1.0)
    keep_next = jnp.where(((i + 1) * tm) % seq == 0, 0.0, 1.0)
    row = lax.broadcasted_iota(jnp.int32, p.shape, 0)
    prev = jnp.where(row == 0, keep_prev * p_edge[SUBLANES - 1:SUBLANES, :], pltpu.roll(p, 1, 0))
    nxt = jnp.where(row == tm - 1, keep_next * p_edge[SUBLANES:SUBLANES + 1, :], pltpu.roll(p, tm - 1, 0))
    p = p + mu_ref[...] * (0.5 * (prev + nxt) - p)

    r = p[:, :D_A]
    k = p[:, D_A:2 * D_A]
    v = p[:, 2 * D_A:3 * D_A]
    dw = p[:, 3 * D_A:3 * D_A + 2 * LORA]
    da = p[:, 3 * D_A + 2 * LORA:3 * D_A + 4 * LORA]
    dg = p[:, 3 * D_A + 4 * LORA:]
    bd = bds_ref[...]

    logw = -DECAY_SCALE * jax.nn.sigmoid(w0_ref[...] + _mm(jnp.tanh(dw), w2_ref[...]))
    iclr = jax.nn.sigmoid(a0_ref[...] + _mm(da, a2_ref[...]))
    gate_out[...] = _mm(jax.nn.sigmoid(dg), g2_ref[...])

    kk = k * kk_ref[...]
    norm = jnp.sqrt(_mm(kk * kk, bd))
    kk_out[...] = (kk / jnp.maximum(norm, 1e-12)).astype(kk_out.dtype)

    ka = ka_ref[...]
    a_f = iclr[:, :D_A]
    a_b = iclr[:, D_A:]
    k_f = k * (1.0 + (a_f - 1.0) * ka)
    k_b = k * (1.0 + (a_b - 1.0) * ka)
    bonus = _mm_split_lhs(r * rk_ref[...] * (k_f + k_b), bd)
    r_out[...] = r.astype(r_out.dtype)
    v_out[...] = v.astype(v_out.dtype)
    lw0_out[...] = logw[:, :D_A]
    k0_out[...] = k_f.astype(k0_out.dtype)
    a0_out[...] = a_f.astype(a0_out.dtype)
    lw1_out[...] = logw[:, D_A:]
    k1_out[...] = k_b.astype(k1_out.dtype)
    a1_out[...] = a_b.astype(a1_out.dtype)
    bonus_out[...] = bonus * v


def _proj(x2, seq, g, w, qg, kg, bd_mean, mu, w0, w2cat, a0, a2cat, g2, k_k, k_a, r_k, bd_sum, layer):
    t = x2.shape[0]
    tm = TOKEN_TILE
    c_in = w.shape[2]
    nb = tm // SUBLANES
    last = t // SUBLANES - 1
    row = lambda width: pl.BlockSpec((tm, width), lambda i: (i, 0))
    wide = jax.ShapeDtypeStruct((t, D_A), F32)
    half = jax.ShapeDtypeStruct((t, D_A), BF16)
    return pl.pallas_call(
        functools.partial(_proj_body, seq=seq),
        grid=(t // tm,),
        in_specs=[row(D_MODEL),
                  pl.BlockSpec((SUBLANES, D_MODEL), lambda i: (jnp.maximum(i * nb - 1, 0), 0)),
                  pl.BlockSpec((SUBLANES, D_MODEL), lambda i: (jnp.minimum((i + 1) * nb, last), 0)),
                  _layer_spec((1, D_MODEL), layer), _layer_spec((D_MODEL, c_in), layer),
                  _layer_spec((1, D_B), layer), _layer_spec((1, D_B), layer), _const_spec((D_B, D_B)),
                  _layer_spec((1, C_RWKV), layer),
                  _layer_spec((1, 2 * D_A), layer), _layer_spec((2 * LORA, 2 * D_A), layer),
                  _layer_spec((1, 2 * D_A), layer), _layer_spec((2 * LORA, 2 * D_A), layer),
                  _layer_spec((LORA_GATE, D_A), layer),
                  _layer_spec((1, D_A), layer), _layer_spec((1, D_A), layer), _layer_spec((1, D_A), layer),
                  _const_spec((D_A, D_A))],
        out_specs=[row(D_B), row(D_B),
                   pl.BlockSpec((1, N_HEADS_B * V_ROWS, tm), lambda i: (i, 0, 0)), row(2 * D_MODEL)]
                  + [row(D_A)] * 11,
        out_shape=[jax.ShapeDtypeStruct((t, D_B), BF16),
                   jax.ShapeDtypeStruct((t, D_B), BF16),
                   jax.ShapeDtypeStruct((t // tm, N_HEADS_B * V_ROWS, tm), BF16),
                   jax.ShapeDtypeStruct((t, 2 * D_MODEL), BF16),
                   half, half, half, wide, half, half, wide, half, half, wide, wide],
        compiler_params=_params(1),
        name="proj",
    )(x2, x2, x2, g, w, qg, kg, bd_mean, mu, w0, w2cat, a0, a2cat, g2, k_k, k_a, r_k, bd_sum)


def _scan_body(rf, vf, kkf, lwf, kf, af, rb, vb, kkb, lwb, kb, ab, yf_ref, yb_ref, h_ref):
    @pl.when(pl.program_id(1) == 0)
    def _():
        h_ref[...] = jnp.zeros(h_ref.shape, F32)

    L = CHUNK
    n_sub = rf.shape[0] // L
    n_pairs = D_A // LANES
    ti = lax.broadcasted_iota(jnp.int32, (L, L), 0)
    tj = lax.broadcasted_iota(jnp.int32, (L, L), 1)
    pi = lax.broadcasted_iota(jnp.int32, (L, LANES), 0)
    lane = lax.broadcasted_iota(jnp.int32, (L, LANES), 1)
    pj = lane & (HEAD_A - 1)
    lane_lo = lane < HEAD_A
    eye = jnp.where(pj == pi, 1.0, 0.0)
    blk = lambda s: (pi // s) == (pj // s)
    row2 = lax.broadcasted_iota(jnp.int32, (2 * L, LANES), 0)
    col2 = lax.broadcasted_iota(jnp.int32, (2 * L, LANES), 1)
    same_head = (row2 // HEAD_A) == (col2 // HEAD_A)
    diag2 = row2 == col2
    zeros_l = jnp.zeros((L, LANES), F32)
    zeros_2l = jnp.zeros((2 * L, LANES), F32)

    def stack(x):
        return jnp.concatenate([jnp.where(lane_lo, x, 0.0), jnp.where(lane_lo, 0.0, x)], axis=0)

    def mmp(x, y):
        return _mm(x, stack(y))

    cat = jnp.concatenate
    chains = []
    for reverse, refs in ((False, (rf, vf, kkf, lwf, kf, af)), (True, (rb, vb, kkb, lwb, kb, ab))):
        tri = ((tj >= ti) if reverse else (tj <= ti)).astype(BF16)
        strict = (pj > pi) if reverse else (pj < pi)
        incl = (pj >= pi) if reverse else (pj <= pi)
        steps = []
        for sub in (range(n_sub - 1, -1, -1) if reverse else range(n_sub)):
            rows = slice(sub * L, (sub + 1) * L)
            r, v, kk, lw, k, a = (ref[rows, :].astype(F32) for ref in refs)
            G = sum(jnp.dot(tri, part, preferred_element_type=F32) for part in _split_bf16(lw, 3))
            g_end = G[0:1] if reverse else G[L - 1:L]
            gam_end = jnp.exp(g_end)
            inv_gam = jnp.exp(-G)
            rel_end = jnp.exp(g_end - G)
            b = kk * a
            full = dict(At=-kk * jnp.exp(G - lw), Bt=b * inv_gam, Kt=k * inv_gam, Rt=r * jnp.exp(G),
                        Kh=k * rel_end, Bh=b * rel_end, v=v)
            pairs = []
            for p in range(n_pairs):
                sl = slice(p * LANES, (p + 1) * LANES)
                c = {name: val[:, sl] for name, val in full.items()}
                c.update(strict=strict, incl=incl, gam_end=gam_end[:, sl], rows=rows)
                pairs.append(c)
            steps.append(pairs)
        chains.append(steps)
    flat = [c for steps in chains for pairs in steps for c in pairs]

    for c in flat:
        a4 = _mm_nt(cat([c["At"], c["Rt"]], axis=0), cat([stack(c["Bt"]), stack(c["Kt"])], axis=0))
        c["A_ab"] = jnp.where(c["strict"], a4[:L, :LANES], 0.0)
        c["A_ak"] = jnp.where(c["strict"], a4[:L, LANES:], 0.0)
        c["A_rb"] = jnp.where(c["incl"], a4[L:, :LANES], 0.0)
        c["A_rk"] = jnp.where(c["incl"], a4[L:, LANES:], 0.0)
    for c in flat:
        c["A0"] = jnp.where(blk(8), c["A_ab"], 0.0)
        c["A2"] = mmp(c["A0"], c["A0"])
    for c in flat:
        t1 = eye + c["A0"]
        both = mmp(cat([c["A2"], t1], axis=0), c["A2"])
        c["A4"] = both[:L]
        c["T"] = t1 + both[L:]
    for c in flat:
        c["T"] = c["T"] + mmp(c["T"], c["A4"])
    s = 8
    while s < L:
        outer = blk(2 * s) & jnp.logical_not(blk(s))
        for c in flat:
            c["TO"] = mmp(c["T"], jnp.where(outer, c["A_ab"], 0.0))
        for c in flat:
            c["T"] = c["T"] + mmp(c["TO"], c["T"])
        s *= 2
    for c in flat:
        c["AkV"] = mmp(c["A_ak"], c["v"])
    for c in flat:
        wu = _mm(c["T"], cat([stack(c["At"]), stack(c["AkV"])], axis=1))
        c["W"], c["U0"] = wu[:, :LANES], wu[:, LANES:]
    for c in flat:
        kb_t = cat([c["Kh"], c["Bh"]], axis=0).T
        mn = _mm(kb_t, cat([cat([zeros_l, c["v"]], axis=1), cat([c["W"], c["U0"]], axis=1)], axis=0))
        c["M"] = jnp.where(same_head, mn[:, :LANES], 0.0) + jnp.where(diag2, c["gam_end"], 0.0)
        c["N"] = jnp.where(same_head, mn[:, LANES:], 0.0)
    for c in flat:
        qy = _mm(cat([c["A_rb"], c["A_rk"]], axis=1),
                 cat([cat([stack(c["W"]), stack(c["U0"])], axis=1),
                      cat([zeros_2l, stack(c["v"])], axis=1)], axis=0))
        c["Q"] = c["Rt"] + qy[:, :LANES]
        c["Y0"] = qy[:, LANES:]
    for d, (steps, y_ref) in enumerate(zip(chains, (yf_ref, yb_ref))):
        for pairs in steps:
            ys = []
            for p, c in enumerate(pairs):
                qm_h = _mm(cat([c["Q"], c["M"]], axis=0), h_ref[d * n_pairs + p])
                ys.append(qm_h[:L] + c["Y0"])
                h_ref[d * n_pairs + p] = qm_h[L:] + c["N"]
            y_ref[pairs[0]["rows"], :] = cat(ys, axis=1)


def _scan(r, v, kk, lw0, k0, a0, lw1, k1, a1, batch, seq):
    t = r.shape[0]
    rows = SCAN_CHUNKS_PER_STEP * CHUNK
    nc = seq // rows
    fwd = pl.BlockSpec((rows, D_A), lambda b, c: (b * nc + c, 0))
    bwd = pl.BlockSpec((rows, D_A), lambda b, c: (b * nc + nc - 1 - c, 0))
    out = jax.ShapeDtypeStruct((t, D_A), F32)
    return pl.pallas_call(
        _scan_body,
        grid=(batch, nc),
        in_specs=[fwd] * 6 + [bwd] * 6,
        out_specs=[fwd, bwd],
        out_shape=[out, out],
        scratch_shapes=[pltpu.VMEM((2 * (D_A // LANES), LANES, LANES), F32)],
        compiler_params=_params(2),
        name="rwkv_scan",
    )(r, v, kk, lw0, k0, a0, r, v, kk, lw1, k1, a1)


def _bias_features(slope, n, width, key_side):
    pos = lax.broadcasted_iota(jnp.int32, (n, width), 0).astype(F32)
    lane = lax.broadcasted_iota(jnp.int32, (n, width), 1)
    hi, mid, lo = (t.astype(F32) for t in _split_bf16(slope * pos, 3))
    terms = jnp.where(lane % 3 == 0, hi, jnp.where(lane % 3 == 1, mid, lo))
    if key_side:
        return jnp.where(lane < 3, -1.0, jnp.where(lane < 6, terms, 0.0))
    return jnp.where(lane < 3, terms, jnp.where(lane < 6, 1.0, 0.0))


def _attn_body(q_ref, k_ref, vt_ref, lv_ref, sg_ref, qg_ref, kg_ref, o_ref, qs_ref, feat_ref, src_ref,
               s_ref, mx_ref, p_ref, al_ref, m_ref, acc_ref, *, lam_init):
    tq = src_ref.shape[1]
    n_kv, _, tk = vt_ref.shape
    seq = n_kv * tk
    h = pl.program_id(1)
    i = pl.program_id(2)
    slope_bits = (127 - 2 * (h + 1)) << 23
    slope = (lax.bitcast_convert_type(jnp.full((1, 1), slope_bits, jnp.int32), jnp.float32).astype(F32)
             * LOG2E)
    bound = (HEAD_B * ROUNDING_MARGIN * jnp.max(jnp.abs(qg_ref[...]), axis=-1, keepdims=True)
             * jnp.max(jnp.abs(kg_ref[...]), axis=-1, keepdims=True))
    reach = jnp.minimum((F32_ZERO_EXP2 + 2.0 * bound) / slope + 1.0, float(seq)).astype(jnp.int32)[0, 0]

    n_st = tk // tq
    hw = 2 * HEAD_B

    @pl.when(i == 0)
    def _():
        kv_pos = lax.broadcasted_iota(jnp.int32, (tk, tq), 0)
        q_pos = lax.broadcasted_iota(jnp.int32, (tk, tq), 1)
        src_ref[...] = slope * (q_pos - kv_pos).astype(F32)
        key_feat = _bias_features(slope, tk, hw, key_side=True)
        feat_ref[0] = key_feat.astype(BF16)
        feat_ref[1] = (-key_feat).astype(BF16)
        feat_ref[2] = jnp.zeros((tk, hw), BF16)
        q_feat = _bias_features(slope, tq, hw, key_side=False).T.astype(BF16)
        for st in range(n_st):
            qs_ref[st, hw:, :tq] = q_feat
            qs_ref[st, hw:, tq:] = q_feat

    chan = lax.broadcasted_iota(jnp.int32, (hw, tq), 0)
    for st in range(n_st):
        q_t = q_ref[st * tq:(st + 1) * tq, :].astype(F32).T.astype(BF16)
        zero = jnp.zeros_like(q_t)
        qs_ref[st, :hw, :tq] = jnp.where(chan < HEAD_B, q_t, zero)
        qs_ref[st, :hw, tq:] = jnp.where(chan < HEAD_B, zero, q_t)
    q0 = i * tk
    j_lo = jnp.maximum(q0 - reach, 0) // tk
    j_hi = jnp.minimum(q0 + tk - 1 + reach, seq - 1) // tk + 1
    n_left = i - j_lo
    n_off = n_left + (j_hi - i - 1)

    def scores(j, side, st):
        kb = k_ref[pl.ds(pl.multiple_of(j * tk, tk), tk), :]
        return jnp.dot(jnp.concatenate([kb, feat_ref[side]], axis=1), qs_ref[st],
                       preferred_element_type=F32)

    def softmax_update(s, s_max, cst, st):
        m_old = m_ref[st]
        m_new = jnp.maximum(m_old, s_max + cst)
        alpha = jnp.exp2(m_old - m_new)
        p = jnp.exp2(s - (m_new - cst))
        m_ref[st] = m_new
        return alpha, p.astype(BF16)

    def block(n):
        j = jnp.clip(j_lo + n + (n >= n_left).astype(jnp.int32), 0, n_kv - 1)
        j = jnp.where(n < 0, i, j)
        return j, (j > i).astype(jnp.int32), jnp.logical_and(n >= 0, n < n_off)

    def qk_stage(n, slot):
        j, side, _ = block(n)
        for st in range(n_st):
            s = scores(j, side, st)
            s_ref[slot, st] = s
            mx_ref[slot, st] = jnp.max(s, axis=0, keepdims=True)

    def softmax_stage(n, slot):
        j, _, valid = block(n)
        for st in range(n_st):
            dist0 = jnp.abs(q0 + st * tq - j * tk).astype(F32)
            cst = jnp.where(valid, -slope * dist0, -1e30)
            alpha, p = softmax_update(s_ref[slot, st], mx_ref[slot, st], cst, st)
            al_ref[slot, st] = alpha
            p_ref[slot, st] = p

    def pv_stage(n, slot):
        j, _, _ = block(n)
        for st in range(n_st):
            acc_ref[st] = al_ref[slot, st] * acc_ref[st] + jnp.dot(vt_ref[j], p_ref[slot, st],
                                                                    preferred_element_type=F32)

    m_ref[...] = jnp.full(m_ref.shape, -1e30, F32)
    acc_ref[...] = jnp.zeros(acc_ref.shape, F32)
    diag_scores = [scores(i, 2, st) for st in range(n_st)]
    qk_stage(0, 0)
    src = src_ref[...]
    for st in range(n_st):
        bias = jnp.abs(src + slope * float(st * tq))
        s = diag_scores[st] - jnp.concatenate([bias, bias], axis=1)
        alpha, p = softmax_update(s, jnp.max(s, axis=0, keepdims=True), jnp.zeros((1, 1), F32), st)
        al_ref[1, st] = alpha
        p_ref[1, st] = p

    def pipelined_pair(t, carry):
        n = 2 * t
        qk_stage(n + 1, 1)
        softmax_stage(n, 0)
        pv_stage(n - 1, 1)
        qk_stage(n + 2, 0)
        softmax_stage(n + 1, 1)
        pv_stage(n, 0)
        return carry

    n_pairs = (n_off + 1) // 2
    lax.fori_loop(0, n_pairs, pipelined_pair, 0)
    pv_stage(2 * n_pairs - 1, 1)

    lv = lv_ref[...]
    lam = (jnp.exp(jnp.sum(lv[0:1] * lv[1:2], axis=-1, keepdims=True))
           - jnp.exp(jnp.sum(lv[2:3] * lv[3:4], axis=-1, keepdims=True)) + lam_init)
    for st in range(n_st):
        acc = acc_ref[st, :hw, :]
        l = acc_ref[st, hw:hw + 1, :]
        o_t = acc[:, :tq] / l[:, :tq] - lam * (acc[:, tq:] / l[:, tq:])
        o_t = o_t * lax.rsqrt(jnp.mean(o_t * o_t, axis=0, keepdims=True) + NORM_EPS)
        o_ref[st * tq:(st + 1) * tq, :] = (o_t.T * sg_ref[...] * (1.0 - lam_init)).astype(o_ref.dtype)


def _attn(q, k, vt, lam_vecs, sub_g, qg, kg, lam_init, batch, seq, layer):
    t = q.shape[0]
    tk = vt.shape[2]
    tq = ATTN_TQ
    n_st = tk // tq
    nq = seq // tk
    n_kv = seq // tk
    hw = 2 * HEAD_B
    return pl.pallas_call(
        functools.partial(_attn_body, lam_init=lam_init),
        grid=(batch, N_HEADS_B, nq),
        in_specs=[pl.BlockSpec((tk, hw), lambda b, h, i: (b * nq + i, h)),
                  pl.BlockSpec((seq, hw), lambda b, h, i: (b, h)),
                  pl.BlockSpec((n_kv, V_ROWS, tk), lambda b, h, i: (b, h, 0)),
                  _layer_spec((4, HEAD_B), layer), _layer_spec((1, hw), layer),
                  _layer_spec((1, D_B), layer), _layer_spec((1, D_B), layer)],
        out_specs=pl.BlockSpec((tk, hw), lambda b, h, i: (b * nq + i, h)),
        out_shape=jax.ShapeDtypeStruct((t, D_B), BF16),
        scratch_shapes=[pltpu.VMEM((n_st, 2 * hw, 2 * tq), BF16),
                        pltpu.VMEM((3, tk, hw), BF16),
                        pltpu.VMEM((tk, tq), F32),
                        pltpu.VMEM((2, n_st, tk, 2 * tq), F32),
                        pltpu.VMEM((2, n_st, 1, 2 * tq), F32),
                        pltpu.VMEM((2, n_st, tk, 2 * tq), BF16),
                        pltpu.VMEM((2, n_st, 1, 2 * tq), F32),
                        pltpu.VMEM((n_st, 1, 2 * tq), F32),
                        pltpu.VMEM((n_st, V_ROWS, 2 * tq), F32)],
        compiler_params=_params(3),
        name="diff_attn",
    )(q, k, vt, lam_vecs, sub_g, qg, kg)


def _merge_body(yf_ref, yb_ref, gate_ref, bonus_ref, ob_ref, gt_ref, x_ref, lng_ref, lnb_ref, bd_ref,
                wb0_ref, wb1_ref, wo_ref, o_ref):
    bd = bd_ref[...]
    y = yf_ref[...] + yb_ref[...]
    yc = y - _mm_split_lhs(y, bd)
    var = _mm(yc * yc, bd)
    yn = yc * lax.rsqrt(var + GN_EPS) * lng_ref[...] + lnb_ref[...]
    oa = (yn + bonus_ref[...]) * gate_ref[...]
    gts = gt_ref[...].astype(F32)
    merged = (gts[:, :D_MODEL] * _mm(oa, wb0_ref[...])
              + gts[:, D_MODEL:] * jnp.dot(ob_ref[...], wb1_ref[...], preferred_element_type=F32))
    o_ref[...] = x_ref[...] + _mm(merged, wo_ref[...])


def _merge(yf, yb, gate, bonus, ob, gts, x2, ln_g, ln_b, bd_mean, wb, wo, layer):
    t = x2.shape[0]
    tm = min(TOKEN_TILE, t)
    row = lambda width: pl.BlockSpec((tm, width), lambda i: (i, 0))
    return pl.pallas_call(
        _merge_body,
        grid=(t // tm,),
        in_specs=[row(D_A), row(D_A), row(D_A), row(D_A), row(D_B), row(2 * D_MODEL), row(D_MODEL),
                  _layer_spec((1, D_A), layer), _layer_spec((1, D_A), layer), _const_spec((D_A, D_A)),
                  _layer_spec((D_A, D_MODEL), layer, 0), _layer_spec((D_B, D_MODEL), layer, 1),
                  _layer_spec((D_MODEL, D_MODEL), layer)],
        out_specs=row(D_MODEL),
        out_shape=jax.ShapeDtypeStruct((t, D_MODEL), F32),
        compiler_params=_params(1),
        name="merge",
    )(yf, yb, gate, bonus, ob, gts, x2, ln_g, ln_b, bd_mean, wb, wb, wo)


def _block_diag2(w):
    z = jnp.zeros_like(w[:, 0])
    return jnp.concatenate([jnp.concatenate([w[:, 0], z], axis=2), jnp.concatenate([z, w[:, 1]], axis=2)],
                           axis=1)


def kernel(x, norm_ffn1, ffn1_in, ffn1_out, norm_mix, w_in, rwkv_mu, decay_w0, decay_w2, iclr_a0, iclr_a2,
           gate_g2, k_k, k_a, r_k, ln_x_g, ln_x_b, q_gain, k_gain, diff_lambda, subln_g, w_branch, w_out,
           norm_ffn2, ffn2_in, ffn2_out):
    batch, seq, d = x.shape
    depth = norm_ffn1.shape[0]
    assert seq % TOKEN_TILE == 0 and d == D_MODEL
    x2 = x.reshape(batch * seq, d)
    head_id = jnp.arange(D_A) // HEAD_A
    same = head_id[:, None] == head_id[None, :]
    bd_sum = same.astype(BF16)
    bd_mean = (same.astype(F32) / HEAD_A).astype(BF16)
    rows = lambda a: a.reshape(depth, 1, -1)
    bf = lambda a: a.astype(BF16)
    n_qk = D_B // HEAD_B
    qg = rows(jnp.tile(q_gain, (1, n_qk))) * (HEAD_B ** -0.5 * LOG2E)
    kg = rows(jnp.tile(k_gain, (1, n_qk)))
    ffn1 = (rows(norm_ffn1), bf(ffn1_in), bf(ffn1_out))
    ffn2 = (rows(norm_ffn2), bf(ffn2_in), bf(ffn2_out))
    proj_w = (rows(norm_mix), bf(w_in))
    prep_w = (rows(rwkv_mu), rows(decay_w0), bf(_block_diag2(decay_w2)), rows(iclr_a0),
              bf(_block_diag2(iclr_a2)), bf(gate_g2), rows(k_k), rows(k_a), rows(r_k))
    merge_w = (rows(ln_x_g), rows(ln_x_b), bd_mean, bf(w_branch), bf(w_out))
    sub_g = rows(subln_g)
    for l in range(depth):
        lam_init = 0.8 - 0.6 * math.exp(-0.3 * l)
        x2 = _ffn(x2, *ffn1, l)
        q, k, vt, gts, r, vv, kk, lw0, k0, a0, lw1, k1, a1, gate, bonus = _proj(
            x2, seq, *proj_w, qg, kg, bd_mean, *prep_w, bd_sum, l)
        yf, yb = _scan(r, vv, kk, lw0, k0, a0, lw1, k1, a1, batch, seq)
        ob = _attn(q, k, vt, diff_lambda, sub_g, qg, kg, lam_init, batch, seq, l)
        x2 = _merge(yf, yb, gate, bonus, ob, gts, x2, *merge_w, l)
        x2 = _ffn(x2, *ffn2, l)
    return x2.reshape(batch, seq, d)
```

```python
import functools
import math

import jax
import jax.numpy as jnp
from jax import lax
from jax.experimental import pallas as pl
from jax.experimental.pallas import tpu as pltpu

F32 = jnp.float32
BF16 = jnp.bfloat16

D_MODEL = 1024
D_A = 512
HEAD_A = 64
LORA = 64
LORA_GATE = 128
C_RWKV = 3 * D_A + 2 * LORA + 2 * LORA + LORA_GATE
D_B = 512
HEAD_B = 64
N_HEADS_B = D_B // (2 * HEAD_B)
D_FF = 2816
DECAY_SCALE = math.exp(-0.5)
GN_EPS = 64e-5
NORM_EPS = 1e-6
LOG2E = math.log2(math.e)
F32_ZERO_EXP2 = 150.0
ROUNDING_MARGIN = 1.02

LANES = 128
SUBLANES = 8
VMEM_LIMIT = 56 * 1024 * 1024

TOKEN_TILE = 512
MXU_WIDTH = 256
FF_SPLITS = (0, 6 * MXU_WIDTH, D_FF)
CHUNK = 64
SCAN_CHUNKS_PER_STEP = 2
ATTN_TQ = 256
V_ROWS = 2 * HEAD_B + 16

_NT = (((1,), (1,)), ((), ()))


def _mm(a, b):
    return jnp.dot(a.astype(BF16), b.astype(BF16), preferred_element_type=F32)


def _mm_nt(a, b):
    return lax.dot_general(a.astype(BF16), b.astype(BF16), _NT, preferred_element_type=F32)


def _split_bf16(x, n):
    parts = []
    for _ in range(n - 1):
        hi = x.astype(BF16)
        parts.append(hi)
        x = x - hi.astype(F32)
    parts.append(x.astype(BF16))
    return parts


def _mm_split_lhs(x, w, n=2):
    return sum(jnp.dot(p, w, preferred_element_type=F32) for p in _split_bf16(x, n))


def _rms_norm(x, g):
    return x * lax.rsqrt(jnp.mean(x * x, axis=-1, keepdims=True) + NORM_EPS) * g


def _const_spec(shape):
    return pl.BlockSpec(shape, lambda *_: (0,) * len(shape), pipeline_mode=pl.Buffered(1))


def _layer_spec(shape, layer, *lead):
    index = (layer,) + lead + (0,) * len(shape)
    return pl.BlockSpec((None,) * (1 + len(lead)) + shape, lambda *_: index, pipeline_mode=pl.Buffered(1))


def _params(n_axes):
    return pltpu.CompilerParams(dimension_semantics=("arbitrary",) * n_axes,
                                vmem_limit_bytes=VMEM_LIMIT)


def _ffn_body(x_ref, g_ref, win_ref, wout_ref, o_ref):
    x = x_ref[...]
    h = _rms_norm(x, g_ref[...]).astype(BF16)
    acc = jnp.zeros(x.shape, F32)
    for lo, hi in zip(FF_SPLITS[:-1], FF_SPLITS[1:]):
        gate = jnp.dot(h, win_ref[:, lo:hi], preferred_element_type=F32)
        up = jnp.dot(h, win_ref[:, D_FF + lo:D_FF + hi], preferred_element_type=F32)
        act = (gate * jax.nn.sigmoid(gate) * up).astype(BF16)
        acc = acc + jnp.dot(act, wout_ref[lo:hi, :], preferred_element_type=F32)
    o_ref[...] = x + 0.5 * acc


def _ffn(x2, g, w_in, w_out, layer):
    t = x2.shape[0]
    tm = min(TOKEN_TILE, t)
    return pl.pallas_call(
        _ffn_body,
        grid=(t // tm,),
        in_specs=[pl.BlockSpec((tm, D_MODEL), lambda i: (i, 0)),
                  _layer_spec((1, D_MODEL), layer),
                  _layer_spec((D_MODEL, 2 * D_FF), layer),
                  _layer_spec((D_FF, D_MODEL), layer)],
        out_specs=pl.BlockSpec((tm, D_MODEL), lambda i: (i, 0)),
        out_shape=jax.ShapeDtypeStruct((t, D_MODEL), F32),
        compiler_params=_params(1),
        name="ffn",
    )(x2, g, w_in, w_out)


_Q0 = C_RWKV
_K0 = C_RWKV + D_B
_V0 = C_RWKV + 2 * D_B
_G0 = C_RWKV + 3 * D_B


def _proj_body(x_ref, xp_ref, xn_ref, g_ref, w_ref, qg_ref, kg_ref, bdm_ref, mu_ref, w0_ref, w2_ref, a0_ref,
               a2_ref, g2_ref, kk_ref, ka_ref, rk_ref, bds_ref,
               q_ref, k_ref, vt_ref, gt_ref, r_out, v_out, kk_out, lw0_out, k0_out, a0_out, lw1_out, k1_out,
               a1_out, gate_out, bonus_out, *, seq):
    i = pl.program_id(0)
    g = g_ref[...]
    h = _rms_norm(x_ref[...], g).astype(BF16)
    tm = h.shape[0]

    bd = bdm_ref[...]
    q = jnp.dot(h, w_ref[:, _Q0:_K0], preferred_element_type=F32)
    q_ref[...] = (q * lax.rsqrt(_mm(q * q, bd) + NORM_EPS) * qg_ref[...]).astype(BF16)
    k = jnp.dot(h, w_ref[:, _K0:_V0], preferred_element_type=F32)
    k_ref[...] = (k * lax.rsqrt(_mm(k * k, bd) + NORM_EPS) * kg_ref[...]).astype(BF16)
    vt = jnp.dot(h, w_ref[:, _V0:_G0], preferred_element_type=F32).T.astype(BF16)
    hw = 2 * HEAD_B
    ones = jnp.ones((V_ROWS - hw, vt.shape[1]), BF16)
    vt_ref[0] = jnp.concatenate([part for hd in range(N_HEADS_B)
                                 for part in (vt[hd * hw:(hd + 1) * hw], ones)], axis=0)
    gt_ref[...] = jax.nn.sigmoid(jnp.dot(h, w_ref[:, _G0:], preferred_element_type=F32)).astype(BF16)

    w_rwkv = w_ref[:, :C_RWKV]
    p = jnp.dot(h, w_rwkv, preferred_element_type=F32)
    edge = jnp.concatenate([xp_ref[...], xn_ref[...]], axis=0)
    p_edge = jnp.dot(_rms_norm(edge, g).astype(BF16), w_rwkv, preferred_element_type=F32)
    keep_prev = jnp.where((i * tm) % seq == 0, 0.0, 1.0)
    keep_next = jnp.where(((i + 1) * tm) % seq == 0, 0.0, 1.0)
    row = lax.broadcasted_iota(jnp.int32, p.shape, 0)
    prev = jnp.where(row == 0, keep_prev * p_edge[SUBLANES - 1:SUBLANES, :], pltpu.roll(p, 1, 0))
    nxt = jnp.where(row == tm - 1, keep_next * p_edge[SUBLANES:SUBLANES + 1, :], pltpu.roll(p, tm - 1, 0))
    p = p + mu_ref[...] * (0.5 * (prev + nxt) - p)

    r = p[:, :D_A]
    k = p[:, D_A:2 * D_A]
    v = p[:, 2 * D_A:3 * D_A]
    dw = p[:, 3 * D_A:3 * D_A + 2 * LORA]
    da = p[:, 3 * D_A + 2 * LORA:3 * D_A + 4 * LORA]
    dg = p[:, 3 * D_A + 4 * LORA:]
    bd = bds_ref[...]

    logw = -DECAY_SCALE * jax.nn.sigmoid(w0_ref[...] + _mm(jnp.tanh(dw), w2_ref[...]))
    iclr = jax.nn.sigmoid(a0_ref[...] + _mm(da, a2_ref[...]))
    gate_out[...] = _mm(jax.nn.sigmoid(dg), g2_ref[...])

    kk = k * kk_ref[...]
    norm = jnp.sqrt(_mm(kk * kk, bd))
    kk_out[...] = (kk / jnp.maximum(norm, 1e-12)).astype(kk_out.dtype)

    ka = ka_ref[...]
    a_f = iclr[:, :D_A]
    a_b = iclr[:, D_A:]
    k_f = k * (1.0 + (a_f - 1.0) * ka)
    k_b = k * (1.0 + (a_b - 1.0) * ka)
    bonus = _mm_split_lhs(r * rk_ref[...] * (k_f + k_b), bd)
    r_out[...] = r.astype(r_out.dtype)
    v_out[...] = v.astype(v_out.dtype)
    lw0_out[...] = logw[:, :D_A]
    k0_out[...] = k_f.astype(k0_out.dtype)
    a0_out[...] = a_f.astype(a0_out.dtype)
    lw1_out[...] = logw[:, D_A:]
    k1_out[...] = k_b.astype(k1_out.dtype)
    a1_out[...] = a_b.astype(a1_out.dtype)
    bonus_out[...] = bonus * v


def _proj(x2, seq, g, w, qg, kg, bd_mean, mu, w0, w2cat, a0, a2cat, g2, k_k, k_a, r_k, bd_sum, layer):
    t = x2.shape[0]
    tm = TOKEN_TILE
    c_in = w.shape[2]
    nb = tm // SUBLANES
    last = t // SUBLANES - 1
    row = lambda width: pl.BlockSpec((tm, width), lambda i: (i, 0))
    wide = jax.ShapeDtypeStruct((t, D_A), F32)
    half = jax.ShapeDtypeStruct((t, D_A), BF16)
    return pl.pallas_call(
        functools.partial(_proj_body, seq=seq),
        grid=(t // tm,),
        in_specs=[row(D_MODEL),
                  pl.BlockSpec((SUBLANES, D_MODEL), lambda i: (jnp.maximum(i * nb - 1, 0), 0)),
                  pl.BlockSpec((SUBLANES, D_MODEL), lambda i: (jnp.minimum((i + 1) * nb, last), 0)),
                  _layer_spec((1, D_MODEL), layer), _layer_spec((D_MODEL, c_in), layer),
                  _layer_spec((1, D_B), layer), _layer_spec((1, D_B), layer), _const_spec((D_B, D_B)),
                  _layer_spec((1, C_RWKV), layer),
                  _layer_spec((1, 2 * D_A), layer), _layer_spec((2 * LORA, 2 * D_A), layer),
                  _layer_spec((1, 2 * D_A), layer), _layer_spec((2 * LORA, 2 * D_A), layer),
                  _layer_spec((LORA_GATE, D_A), layer),
                  _layer_spec((1, D_A), layer), _layer_spec((1, D_A), layer), _layer_spec((1, D_A), layer),
                  _const_spec((D_A, D_A))],
        out_specs=[row(D_B), row(D_B),
                   pl.BlockSpec((1, N_HEADS_B * V_ROWS, tm), lambda i: (i, 0, 0)), row(2 * D_MODEL)]
                  + [row(D_A)] * 11,
        out_shape=[jax.ShapeDtypeStruct((t, D_B), BF16),
                   jax.ShapeDtypeStruct((t, D_B), BF16),
                   jax.ShapeDtypeStruct((t // tm, N_HEADS_B * V_ROWS, tm), BF16),
                   jax.ShapeDtypeStruct((t, 2 * D_MODEL), BF16),
                   half, half, half, wide, half, half, wide, half, half, wide, wide],
        compiler_params=_params(1),
        name="proj",
    )(x2, x2, x2, g, w, qg, kg, bd_mean, mu, w0, w2cat, a0, a2cat, g2, k_k, k_a, r_k, bd_sum)


def _scan_body(rf, vf, kkf, lwf, kf, af, rb, vb, kkb, lwb, kb, ab, yf_ref, yb_ref, h_ref):
    @pl.when(pl.program_id(1) == 0)
    def _():
        h_ref[...] = jnp.zeros(h_ref.shape, F32)

    L = CHUNK
    n_sub = rf.shape[0] // L
    n_pairs = D_A // LANES
    ti = lax.broadcasted_iota(jnp.int32, (L, L), 0)
    tj = lax.broadcasted_iota(jnp.int32, (L, L), 1)
    pi = lax.broadcasted_iota(jnp.int32, (L, LANES), 0)
    lane = lax.broadcasted_iota(jnp.int32, (L, LANES), 1)
    pj = lane & (HEAD_A - 1)
    lane_lo = lane < HEAD_A
    eye = jnp.where(pj == pi, 1.0, 0.0)
    blk = lambda s: (pi // s) == (pj // s)
    row2 = lax.broadcasted_iota(jnp.int32, (2 * L, LANES), 0)
    col2 = lax.broadcasted_iota(jnp.int32, (2 * L, LANES), 1)
    same_head = (row2 // HEAD_A) == (col2 // HEAD_A)
    diag2 = row2 == col2
    zeros_l = jnp.zeros((L, LANES), F32)
    zeros_2l = jnp.zeros((2 * L, LANES), F32)

    def stack(x):
        return jnp.concatenate([jnp.where(lane_lo, x, 0.0), jnp.where(lane_lo, 0.0, x)], axis=0)

    def mmp(x, y):
        return _mm(x, stack(y))

    cat = jnp.concatenate
    chains = []
    for reverse, refs in ((False, (rf, vf, kkf, lwf, kf, af)), (True, (rb, vb, kkb, lwb, kb, ab))):
        tri = ((tj >= ti) if reverse else (tj <= ti)).astype(BF16)
        strict = (pj > pi) if reverse else (pj < pi)
        incl = (pj >= pi) if reverse else (pj <= pi)
        steps = []
        for sub in (range(n_sub - 1, -1, -1) if reverse else range(n_sub)):
            rows = slice(sub * L, (sub + 1) * L)
            r, v, kk, lw, k, a = (ref[rows, :].astype(F32) for ref in refs)
            G = sum(jnp.dot(tri, part, preferred_element_type=F32) for part in _split_bf16(lw, 3))
            g_end = G[0:1] if reverse else G[L - 1:L]
            gam_end = jnp.exp(g_end)
            inv_gam = jnp.exp(-G)
            rel_end = jnp.exp(g_end - G)
            b = kk * a
            full = dict(At=-kk * jnp.exp(G - lw), Bt=b * inv_gam, Kt=k * inv_gam, Rt=r * jnp.exp(G),
                        Kh=k * rel_end, Bh=b * rel_end, v=v)
            pairs = []
            for p in range(n_pairs):
                sl = slice(p * LANES, (p + 1) * LANES)
                c = {name: val[:, sl] for name, val in full.items()}
                c.update(strict=strict, incl=incl, gam_end=gam_end[:, sl], rows=rows)
                pairs.append(c)
            steps.append(pairs)
        chains.append(steps)
    flat = [c for steps in chains for pairs in steps for c in pairs]

    for c in flat:
        a4 = _mm_nt(cat([c["At"], c["Rt"]], axis=0), cat([stack(c["Bt"]), stack(c["Kt"])], axis=0))
        c["A_ab"] = jnp.where(c["strict"], a4[:L, :LANES], 0.0)
        c["A_ak"] = jnp.where(c["strict"], a4[:L, LANES:], 0.0)
        c["A_rb"] = jnp.where(c["incl"], a4[L:, :LANES], 0.0)
        c["A_rk"] = jnp.where(c["incl"], a4[L:, LANES:], 0.0)
    for c in flat:
        c["A0"] = jnp.where(blk(8), c["A_ab"], 0.0)
        c["A2"] = mmp(c["A0"], c["A0"])
    for c in flat:
        t1 = eye + c["A0"]
        both = mmp(cat([c["A2"], t1], axis=0), c["A2"])
        c["A4"] = both[:L]
        c["T"] = t1 + both[L:]
    for c in flat:
        c["T"] = c["T"] + mmp(c["T"], c["A4"])
    s = 8
    while s < L:
        outer = blk(2 * s) & jnp.logical_not(blk(s))
        for c in flat:
            c["TO"] = mmp(c["T"], jnp.where(outer, c["A_ab"], 0.0))
        for c in flat:
            c["T"] = c["T"] + mmp(c["TO"], c["T"])
        s *= 2
    for c in flat:
        c["AkV"] = mmp(c["A_ak"], c["v"])
    for c in flat:
        wu = _mm(c["T"], cat([stack(c["At"]), stack(c["AkV"])], axis=1))
        c["W"], c["U0"] = wu[:, :LANES], wu[:, LANES:]
    for c in flat:
        kb_t = cat([c["Kh"], c["Bh"]], axis=0).T
        mn = _mm(kb_t, cat([cat([zeros_l, c["v"]], axis=1), cat([c["W"], c["U0"]], axis=1)], axis=0))
        c["M"] = jnp.where(same_head, mn[:, :LANES], 0.0) + jnp.where(diag2, c["gam_end"], 0.0)
        c["N"] = jnp.where(same_head, mn[:, LANES:], 0.0)
    for c in flat:
        qy = _mm(cat([c["A_rb"], c["A_rk"]], axis=1),
                 cat([cat([stack(c["W"]), stack(c["U0"])], axis=1),
                      cat([zeros_2l, stack(c["v"])], axis=1)], axis=0))
        c["Q"] = c["Rt"] + qy[:, :LANES]
        c["Y0"] = qy[:, LANES:]
    for d, (steps, y_ref) in enumerate(zip(chains, (yf_ref, yb_ref))):
        for pairs in steps:
            ys = []
            for p, c in enumerate(pairs):
                qm_h = _mm(cat([c["Q"], c["M"]], axis=0), h_ref[d * n_pairs + p])
                ys.append(qm_h[:L] + c["Y0"])
                h_ref[d * n_pairs + p] = qm_h[L:] + c["N"]
            y_ref[pairs[0]["rows"], :] = cat(ys, axis=1)


def _scan(r, v, kk, lw0, k0, a0, lw1, k1, a1, batch, seq):
    t = r.shape[0]
    rows = SCAN_CHUNKS_PER_STEP * CHUNK
    nc = seq // rows
    fwd = pl.BlockSpec((rows, D_A), lambda b, c: (b * nc + c, 0))
    bwd = pl.BlockSpec((rows, D_A), lambda b, c: (b * nc + nc - 1 - c, 0))
    out = jax.ShapeDtypeStruct((t, D_A), F32)
    return pl.pallas_call(
        _scan_body,
        grid=(batch, nc),
        in_specs=[fwd] * 6 + [bwd] * 6,
        out_specs=[fwd, bwd],
        out_shape=[out, out],
        scratch_shapes=[pltpu.VMEM((2 * (D_A // LANES), LANES, LANES), F32)],
        compiler_params=_params(2),
        name="rwkv_scan",
    )(r, v, kk, lw0, k0, a0, r, v, kk, lw1, k1, a1)


def _bias_features(slope, n, width, key_side):
    pos = lax.broadcasted_iota(jnp.int32, (n, width), 0).astype(F32)
    lane = lax.broadcasted_iota(jnp.int32, (n, width), 1)
    hi, mid, lo = (t.astype(F32) for t in _split_bf16(slope * pos, 3))
    terms = jnp.where(lane % 3 == 0, hi, jnp.where(lane % 3 == 1, mid, lo))
    if key_side:
        return jnp.where(lane < 3, -1.0, jnp.where(lane < 6, terms, 0.0))
    return jnp.where(lane < 3, terms, jnp.where(lane < 6, 1.0, 0.0))


def _attn_body(q_ref, k_ref, vt_ref, lv_ref, sg_ref, qg_ref, kg_ref, o_ref, qs_ref, feat_ref, src_ref,
               s_ref, mx_ref, p_ref, al_ref, m_ref, acc_ref, *, lam_init):
    tq = src_ref.shape[1]
    n_kv, _, tk = vt_ref.shape
    seq = n_kv * tk
    h = pl.program_id(1)
    i = pl.program_id(2)
    slope_bits = (127 - 2 * (h + 1)) << 23
    slope = (lax.bitcast_convert_type(jnp.full((1, 1), slope_bits, jnp.int32), jnp.float32).astype(F32)
             * LOG2E)
    bound =(HEAD_B * ROUNDING_MARGIN * jnp.max(jnp.abs(qg_ref[...]), axis=-1, keepdims=True)
             * jnp.max(jnp.abs(kg_ref[...]), axis=-1, keepdims=True))
    reach = jnp.minimum((F32_ZERO_EXP2 + 2.0 * bound) / slope + 1.0, float(seq)).astype(jnp.int32)[0, 0]

    n_st = tk // tq
    hw = 2 * HEAD_B

    @pl.when(i == 0)
    def _():
        kv_pos = lax.broadcasted_iota(jnp.int32, (tk, tq), 0)
        q_pos = lax.broadcasted_iota(jnp.int32, (tk, tq), 1)
        src_ref[...] = slope * (q_pos - kv_pos).astype(F32)
        key_feat = _bias_features(slope, tk, hw, key_side=True)
        feat_ref[0] = key_feat.astype(BF16)
        feat_ref[1] = (-key_feat).astype(BF16)
        feat_ref[2] = jnp.zeros((tk, hw), BF16)
        q_feat = _bias_features(slope, tq, hw, key_side=False).T.astype(BF16)
        for st in range(n_st):
            qs_ref[st, hw:, :tq] = q_feat
            qs_ref[st, hw:, tq:] = q_feat

    chan =lax.broadcasted_iota(jnp.int32, (hw, tq), 0)
    for st in range(n_st):
        q_t = q_ref[st * tq:(st + 1) * tq, :].astype(F32).T.astype(BF16)
        zero = jnp.zeros_like(q_t)
        qs_ref[st, :hw, :tq] = jnp.where(chan < HEAD_B, q_t, zero)
        qs_ref[st, :hw, tq:] = jnp.where(chan < HEAD_B, zero, q_t)
    q0 = i * tk
    j_lo = jnp.maximum(q0 - reach, 0) // tk
    j_hi = jnp.minimum(q0 + tk - 1 + reach, seq - 1) // tk + 1
    n_left = i - j_lo
    n_off = n_left + (j_hi - i - 1)

    def scores(j, side, st):
        kb = k_ref[pl.ds(pl.multiple_of(j * tk, tk), tk), :]
        return jnp.dot(jnp.concatenate([kb, feat_ref[side]], axis=1), qs_ref[st],
                       preferred_element_type=F32)

    def softmax_update(s, s_max, cst, st):
        m_old = m_ref[st]
        m_new = jnp.maximum(m_old, s_max + cst)
        alpha = jnp.exp2(m_old - m_new)
        p = jnp.exp2(s - (m_new - cst))
        m_ref[st] = m_new
        return alpha, p.astype(BF16)

    def block(n):
        j = jnp.clip(j_lo + n + (n >= n_left).astype(jnp.int32), 0, n_kv - 1)
        j = jnp.where(n < 0, i, j)
        return j, (j > i).astype(jnp.int32), jnp.logical_and(n >= 0, n < n_off)

    def qk_stage(n, slot):
        j, side, _ = block(n)
        for st in range(n_st):
            s = scores(j, side, st)
            s_ref[slot, st] = s
            mx_ref[slot, st] = jnp.max(s, axis=0, keepdims=True)

    def softmax_stage(n, slot):
        j, _, valid = block(n)
        for st in range(n_st):
            dist0 = jnp.abs(q0 + st * tq - j * tk).astype(F32)
            cst = jnp.where(valid, -slope * dist0, -1e30)
            alpha, p = softmax_update(s_ref[slot, st], mx_ref[slot, st], cst, st)
            al_ref[slot, st] = alpha
            p_ref[slot, st] = p

    def pv_stage(n, slot):
        j, _, _ = block(n)
        for st in range(n_st):
            acc_ref[st] = al_ref[slot, st] * acc_ref[st] + jnp.dot(vt_ref[j], p_ref[slot, st],
                                                                    preferred_element_type=F32)

    m_ref[...] = jnp.full(m_ref.shape, -1e30, F32)
    acc_ref[...] = jnp.zeros(acc_ref.shape, F32)
    diag_scores = [scores(i, 2, st) for st in range(n_st)]
    qk_stage(0, 0)
    src = src_ref[...]
    for st in range(n_st):
        bias = jnp.abs(src + slope * float(st * tq))
        s = diag_scores[st] - jnp.concatenate([bias, bias], axis=1)
        alpha, p = softmax_update(s, jnp.max(s, axis=0, keepdims=True), jnp.zeros((1, 1), F32), st)
        al_ref[1, st] = alpha
        p_ref[1, st] = p

    def pipelined_pair(t, carry):
        n = 2 * t
        qk_stage(n + 1, 1)
        softmax_stage(n, 0)
        pv_stage(n - 1, 1)
        qk_stage(n + 2, 0)
        softmax_stage(n + 1, 1)
        pv_stage(n, 0)
        return carry

    n_pairs = (n_off + 1) // 2
    lax.fori_loop(0, n_pairs, pipelined_pair, 0)
    pv_stage(2 * n_pairs - 1, 1)

    lv = lv_ref[...]
    lam = (jnp.exp(jnp.sum(lv[0:1] * lv[1:2], axis=-1, keepdims=True))
           - jnp.exp(jnp.sum(lv[2:3] * lv[3:4], axis=-1, keepdims=True)) + lam_init)
    for st in range(n_st):
        acc = acc_ref[st, :hw, :]
        l = acc_ref[st, hw:hw + 1, :]
        o_t = acc[:, :tq] / l[:, :tq] - lam * (acc[:, tq:] / l[:, tq:])
        o_t = o_t * lax.rsqrt(jnp.mean(o_t * o_t, axis=0, keepdims=True) + NORM_EPS)
        o_ref[st * tq:(st + 1) * tq, :] = (o_t.T * sg_ref[...] * (1.0 - lam_init)).astype(o_ref.dtype)


def _attn(q, k, vt, lam_vecs, sub_g, qg, kg, lam_init, batch, seq, layer):
    t = q.shape[0]
    tk = vt.shape[2]
    tq = ATTN_TQ
    n_st = tk // tq
    nq = seq // tk
    n_kv = seq // tk
    hw = 2 * HEAD_B
    return pl.pallas_call(
        functools.partial(_attn_body, lam_init=lam_init),
        grid=(batch, N_HEADS_B, nq),
        in_specs=[pl.BlockSpec((tk, hw), lambda b, h, i: (b * nq + i, h)),
                  pl.BlockSpec((seq, hw), lambda b, h, i: (b, h)),
                  pl.BlockSpec((n_kv, V_ROWS, tk), lambda b, h, i: (b, h, 0)),
                  _layer_spec((4, HEAD_B), layer), _layer_spec((1, hw), layer),
                  _layer_spec((1, D_B), layer), _layer_spec((1, D_B), layer)],
        out_specs=pl.BlockSpec((tk, hw), lambda b, h, i: (b * nq + i, h)),
        out_shape=jax.ShapeDtypeStruct((t, D_B), BF16),
        scratch_shapes=[pltpu.VMEM((n_st, 2 * hw, 2 * tq), BF16),
                        pltpu.VMEM((3, tk, hw), BF16),
                        pltpu.VMEM((tk, tq), F32),
                        pltpu.VMEM((2, n_st, tk, 2 * tq), F32),
                        pltpu.VMEM((2, n_st, 1, 2 * tq), F32),
                        pltpu.VMEM((2, n_st, tk, 2 * tq), BF16),
                        pltpu.VMEM((2, n_st, 1, 2 * tq), F32),
                        pltpu.VMEM((n_st, 1, 2 * tq), F32),
                        pltpu.VMEM((n_st, V_ROWS, 2 * tq), F32)],
        compiler_params=_params(3),
        name="diff_attn",
    )(q, k, vt, lam_vecs, sub_g, qg, kg)


def _merge_body(yf_ref, yb_ref, gate_ref, bonus_ref, ob_ref, gt_ref, x_ref, lng_ref, lnb_ref, bd_ref,
                wb0_ref, wb1_ref, wo_ref, o_ref):
    bd = bd_ref[...]
    y = yf_ref[...] + yb_ref[...]
    yc = y - _mm_split_lhs(y, bd)
    var = _mm(yc * yc, bd)
    yn = yc * lax.rsqrt(var + GN_EPS) * lng_ref[...] + lnb_ref[...]
    oa = (yn + bonus_ref[...]) * gate_ref[...]
    gts = gt_ref[...].astype(F32)
    merged = (gts[:, :D_MODEL] * _mm(oa, wb0_ref[...])
              + gts[:, D_MODEL:] * jnp.dot(ob_ref[...], wb1_ref[...], preferred_element_type=F32))
    o_ref[...] = x_ref[...] + _mm(merged, wo_ref[...])


def _merge(yf, yb, gate, bonus, ob, gts, x2, ln_g, ln_b, bd_mean, wb, wo, layer):
    t = x2.shape[0]
    tm = min(TOKEN_TILE, t)
    row = lambda width: pl.BlockSpec((tm, width), lambda i: (i, 0))
    return pl.pallas_call(
        _merge_body,
        grid=(t // tm,),
        in_specs=[row(D_A), row(D_A), row(D_A), row(D_A), row(D_B), row(2 * D_MODEL), row(D_MODEL),
                  _layer_spec((1, D_A), layer), _layer_spec((1, D_A), layer), _const_spec((D_A, D_A)),
                  _layer_spec((D_A, D_MODEL), layer, 0), _layer_spec((D_B, D_MODEL), layer, 1),
                  _layer_spec((D_MODEL, D_MODEL), layer)],
        out_specs=row(D_MODEL),
        out_shape=jax.ShapeDtypeStruct((t, D_MODEL), F32),
        compiler_params=_params(1),
        name="merge",
    )(yf, yb, gate, bonus, ob, gts, x2, ln_g, ln_b, bd_mean, wb, wb, wo)


def _block_diag2(w):
    z = jnp.zeros_like(w[:, 0])
    return jnp.concatenate([jnp.concatenate([w[:, 0], z], axis=2), jnp.concatenate([z, w[:, 1]], axis=2)],
                           axis=1)


def kernel(x, norm_ffn1, ffn1_in, ffn1_out, norm_mix, w_in, rwkv_mu, decay_w0, decay_w2, iclr_a0, iclr_a2,
           gate_g2, k_k, k_a, r_k, ln_x_g, ln_x_b, q_gain, k_gain, diff_lambda, subln_g, w_branch, w_out,
           norm_ffn2, ffn2_in, ffn2_out):
    batch, seq, d = x.shape
    depth = norm_ffn1.shape[0]
    assert seq % TOKEN_TILE == 0 and d == D_MODEL
    x2 = x.reshape(batch * seq, d)
    head_id = jnp.arange(D_A) // HEAD_A
    same = head_id[:, None] == head_id[None, :]
    bd_sum = same.astype(BF16)
    bd_mean = (same.astype(F32) / HEAD_A).astype(BF16)
    rows = lambda a: a.reshape(depth, 1, -1)
    bf = lambda a: a.astype(BF16)
    n_qk = D_B // HEAD_B
    qg = rows(jnp.tile(q_gain, (1, n_qk))) * (HEAD_B ** -0.5 * LOG2E)
    kg = rows(jnp.tile(k_gain, (1, n_qk)))
    ffn1 = (rows(norm_ffn1), bf(ffn1_in), bf(ffn1_out))
    ffn2 = (rows(norm_ffn2), bf(ffn2_in), bf(ffn2_out))
    proj_w = (rows(norm_mix), bf(w_in))
    prep_w = (rows(rwkv_mu), rows(decay_w0), bf(_block_diag2(decay_w2)), rows(iclr_a0),
              bf(_block_diag2(iclr_a2)), bf(gate_g2), rows(k_k), rows(k_a), rows(r_k))
    merge_w = (rows(ln_x_g), rows(ln_x_b), bd_mean, bf(w_branch), bf(w_out))
    sub_g = rows(subln_g)
    for l in range(depth):
        lam_init = 0.8 - 0.6 * math.exp(-0.3 * l)
        x2 = _ffn(x2, *ffn1, l)
        q, k, vt, gts, r, vv, kk, lw0, k0, a0, lw1, k1, a1, gate, bonus = _proj(
            x2, seq, *proj_w, qg, kg, bd_mean, *prep_w, bd_sum, l)
        yf, yb = _scan(r, vv, kk, lw0, k0, a0, lw1, k1, a1, batch, seq)
        ob = _attn(q, k, vt, diff_lambda, sub_g, qg, kg, lam_init, batch, seq, l)
        x2 = _merge(yf, yb, gate, bonus, ob, gts, x2, *merge_w, l)
        x2 = _ffn(x2, *ffn2, l)
    return x2.reshape(batch, seq, d)
```

```python
import functools
import math

import jax
import jax.numpy as jnp
from jax import lax
from jax.experimental import pallas as pl
from jax.experimental.pallas import tpu as pltpu

F32 = jnp.float32
BF16 = jnp.bfloat16

D_MODEL = 1024
D_A = 512
HEAD_A = 64
LORA = 64
LORA_GATE = 128
C_RWKV = 3 * D_A + 2 * LORA + 2 * LORA + LORA_GATE
D_B = 512
HEAD_B = 64
N_HEADS_B = D_B // (2 * HEAD_B)
D_FF = 2816
DECAY_SCALE = math.exp(-0.5)
GN_EPS = 64e-5
NORM_EPS = 1e-6
LOG2E = math.log2(math.e)
F32_ZERO_EXP2 = 150.0
ROUNDING_MARGIN = 1.02

LANES = 128
SUBLANES = 8
VMEM_LIMIT = 56 * 1024 * 1024

TOKEN_TILE = 512
MXU_WIDTH = 256
FF_SPLITS = (0, 6 * MXU_WIDTH, D_FF)
CHUNK = 64
SCAN_CHUNKS_PER_STEP = 4
ATTN_TQ = 256
V_ROWS = 2 * HEAD_B + 16

_NT = (((1,), (1,)), ((), ()))


def _mm(a, b):
    return jnp.dot(a.astype(BF16), b.astype(BF16), preferred_element_type=F32)


def _mm_nt(a, b):
    return lax.dot_general(a.astype(BF16), b.astype(BF16), _NT, preferred_element_type=F32)


def _split_bf16(x, n):
    parts = []
    for _ in range(n - 1):
        hi = x.astype(BF16)
        parts.append(hi)
        x = x - hi.astype(F32)
    parts.append(x.astype(BF16))
    return parts


def _mm_split_lhs(x, w, n=2):
    return sum(jnp.dot(p, w, preferred_element_type=F32) for p in _split_bf16(x, n))


def _rms_norm(x, g):
    return x * lax.rsqrt(jnp.mean(x * x, axis=-1, keepdims=True) + NORM_EPS) * g


def _const_spec(shape):
    return pl.BlockSpec(shape, lambda *_: (0,) * len(shape), pipeline_mode=pl.Buffered(1))


def _layer_spec(shape, layer, *lead):
    index = (layer,) + lead + (0,) * len(shape)
    return pl.BlockSpec((None,) * (1 + len(lead)) + shape, lambda *_: index, pipeline_mode=pl.Buffered(1))


def _params(n_axes):
    return pltpu.CompilerParams(dimension_semantics=("arbitrary",) * n_axes,
                                vmem_limit_bytes=VMEM_LIMIT)


def _ffn_body(x_ref, g_ref, win_ref, wout_ref, o_ref):
    x = x_ref[...]
    h = _rms_norm(x, g_ref[...]).astype(BF16)
    acc = jnp.zeros(x.shape, F32)
    for lo, hi in zip(FF_SPLITS[:-1], FF_SPLITS[1:]):
        gate = jnp.dot(h, win_ref[:, lo:hi], preferred_element_type=F32)
        up = jnp.dot(h, win_ref[:, D_FF + lo:D_FF + hi], preferred_element_type=F32)
        act = (gate * jax.nn.sigmoid(gate) * up).astype(BF16)
        acc = acc + jnp.dot(act, wout_ref[lo:hi, :], preferred_element_type=F32)
    o_ref[...] = x + 0.5 * acc


def _ffn(x2, g, w_in, w_out, layer):
    t = x2.shape[0]
    tm = min(TOKEN_TILE, t)
    return pl.pallas_call(
        _ffn_body,
        grid=(t // tm,),
        in_specs=[pl.BlockSpec((tm, D_MODEL), lambda i: (i, 0)),
                  _layer_spec((1, D_MODEL), layer),
                  _layer_spec((D_MODEL, 2 * D_FF), layer),
                  _layer_spec((D_FF, D_MODEL), layer)],
        out_specs=pl.BlockSpec((tm, D_MODEL), lambda i: (i, 0)),
        out_shape=jax.ShapeDtypeStruct((t, D_MODEL), F32),
        compiler_params=_params(1),
        name="ffn",
    )(x2, g, w_in, w_out)


_Q0 = C_RWKV
_K0 = C_RWKV + D_B
_V0 = C_RWKV + 2 * D_B
_G0 = C_RWKV + 3 * D_B


def _proj_body(x_ref, xp_ref, xn_ref, g_ref, w_ref, qg_ref, kg_ref, bdm_ref, mu_ref, w0_ref, w2_ref, a0_ref,
               a2_ref, g2_ref, kk_ref, ka_ref, rk_ref, bds_ref,
               q_ref, k_ref, vt_ref, gt_ref, r_out, v_out, kk_out, lw0_out, k0_out, a0_out, lw1_out, k1_out,
               a1_out, gate_out, bonus_out, *, seq):
    i = pl.program_id(0)
    g = g_ref[...]
    h = _rms_norm(x_ref[...], g).astype(BF16)
    tm = h.shape[0]

    bd = bdm_ref[...]
    q = jnp.dot(h, w_ref[:, _Q0:_K0], preferred_element_type=F32)
    q_ref[...] = (q * lax.rsqrt(_mm(q * q, bd) + NORM_EPS) * qg_ref[...]).astype(BF16)
    k = jnp.dot(h, w_ref[:, _K0:_V0], preferred_element_type=F32)
    k_ref[...] = (k * lax.rsqrt(_mm(k * k, bd) + NORM_EPS) * kg_ref[...]).astype(BF16)
    vt = jnp.dot(h, w_ref[:, _V0:_G0], preferred_element_type=F32).T.astype(BF16)
    hw = 2 * HEAD_B
    ones = jnp.ones((V_ROWS - hw, vt.shape[1]), BF16)
    vt_ref[0] = jnp.concatenate([part for hd in range(N_HEADS_B)
                                 for part in (vt[hd * hw:(hd + 1) * hw], ones)], axis=0)
    gt_ref[...] = jax.nn.sigmoid(jnp.dot(h, w_ref[:, _G0:], preferred_element_type=F32)).astype(BF16)

    w_rwkv = w_ref[:, :C_RWKV]
    p = jnp.dot(h, w_rwkv, preferred_element_type=F32)
    edge = jnp.concatenate([xp_ref[...], xn_ref[...]], axis=0)
    p_edge = jnp.dot(_rms_norm(edge, g).astype(BF16), w_rwkv, preferred_element_type=F32)
    keep_prev = jnp.where((i * tm) % seq == 0, 0.0, 1.0)
    keep_next = jnp.where(((i + 1) * tm) % seq == 0, 0.0, 1.0)
    row = lax.broadcasted_iota(jnp.int32, p.shape, 0)
    prev = jnp.where(row == 0, keep_prev * p_edge[SUBLANES - 1:SUBLANES, :], pltpu.roll(p, 1, 0))
    nxt = jnp.where(row == tm - 1, keep_next * p_edge[SUBLANES:SUBLANES + 1, :], pltpu.roll(p, tm - 1, 0))
    p = p + mu_ref[...] * (0.5 * (prev + nxt) - p)

    r = p[:, :D_A]
    k = p[:, D_A:2 * D_A]
    v = p[:, 2 * D_A:3 * D_A]
    dw = p[:, 3 * D_A:3 * D_A + 2 * LORA]
    da = p[:, 3 * D_A + 2 * LORA:3 * D_A + 4 * LORA]
    dg = p[:, 3 * D_A + 4 * LORA:]
    bd = bds_ref[...]

    logw = -DECAY_SCALE * jax.nn.sigmoid(w0_ref[...] + _mm(jnp.tanh(dw), w2_ref[...]))
    iclr = jax.nn.sigmoid(a0_ref[...] + _mm(da, a2_ref[...]))
    gate_out[...] = _mm(jax.nn.sigmoid(dg), g2_ref[...])

    kk = k * kk_ref[...]
    norm = jnp.sqrt(_mm(kk * kk, bd))
    kk_out[...] = (kk / jnp.maximum(norm, 1e-12)).astype(kk_out.dtype)

    ka = ka_ref[...]
    a_f = iclr[:, :D_A]
    a_b = iclr[:, D_A:]
    k_f = k * (1.0 + (a_f - 1.0) * ka)
    k_b = k * (1.0 + (a_b - 1.0) * ka)
    bonus = _mm_split_lhs(r * rk_ref[...] * (k_f + k_b), bd)
    r_out[...] = r.astype(r_out.dtype)
    v_out[...] = v.astype(v_out.dtype)
    lw0_out[...] = logw[:, :D_A]
    k0_out[...] = k_f.astype(k0_out.dtype)
    a0_out[...] = a_f.astype(a0_out.dtype)
    lw1_out[...] = logw[:, D_A:]
    k1_out[...] = k_b.astype(k1_out.dtype)
    a1_out[...] = a_b.astype(a1_out.dtype)
    bonus_out[...] = bonus * v


def _proj(x2, seq, g, w, qg, kg, bd_mean, mu, w0, w2cat, a0, a2cat, g2, k_k, k_a, r_k, bd_sum, layer):
    t = x2.shape[0]
    tm = TOKEN_TILE
    c_in = w.shape[2]
    nb = tm // SUBLANES
    last = t // SUBLANES - 1
    row = lambda width: pl.BlockSpec((tm, width), lambda i: (i, 0))
    wide = jax.ShapeDtypeStruct((t, D_A), F32)
    half = jax.ShapeDtypeStruct((t, D_A), BF16)
    return pl.pallas_call(
        functools.partial(_proj_body, seq=seq),
        grid=(t // tm,),
        in_specs=[row(D_MODEL),
                  pl.BlockSpec((SUBLANES, D_MODEL), lambda i: (jnp.maximum(i * nb - 1, 0), 0)),
                  pl.BlockSpec((SUBLANES, D_MODEL), lambda i: (jnp.minimum((i + 1) * nb, last), 0)),
                  _layer_spec((1, D_MODEL), layer), _layer_spec((D_MODEL, c_in), layer),
                  _layer_spec((1, D_B), layer), _layer_spec((1, D_B), layer), _const_spec((D_B, D_B)),
                  _layer_spec((1, C_RWKV), layer),
                  _layer_spec((1, 2 * D_A), layer), _layer_spec((2 * LORA, 2 * D_A), layer),
                  _layer_spec((1, 2 * D_A), layer), _layer_spec((2 * LORA, 2 * D_A), layer),
                  _layer_spec((LORA_GATE, D_A), layer),
                  _layer_spec((1, D_A), layer), _layer_spec((1, D_A), layer), _layer_spec((1, D_A), layer),
                  _const_spec((D_A, D_A))],
        out_specs=[row(D_B), row(D_B),
                   pl.BlockSpec((1, N_HEADS_B * V_ROWS, tm), lambda i: (i, 0, 0)), row(2 * D_MODEL)]
                  + [row(D_A)] * 11,
        out_shape=[jax.ShapeDtypeStruct((t, D_B), BF16),
                   jax.ShapeDtypeStruct((t, D_B), BF16),
                   jax.ShapeDtypeStruct((t // tm, N_HEADS_B * V_ROWS, tm), BF16),
                   jax.ShapeDtypeStruct((t, 2 * D_MODEL), BF16),
                   half, half, half, wide, half, half, wide, half, half, wide, wide],
        compiler_params=_params(1),
        name="proj",
    )(x2, x2, x2, g, w, qg, kg, bd_mean, mu, w0, w2cat, a0, a2cat, g2, k_k, k_a, r_k, bd_sum)


def _scan_body(rf, vf, kkf, lwf, kf, af, rb, vb, kkb, lwb, kb, ab, yf_ref, yb_ref, h_ref):
    @pl.when(pl.program_id(1) == 0)
    def _():
        h_ref[...] = jnp.zeros(h_ref.shape, F32)

    L = CHUNK
    n_sub = rf.shape[0] // L
    n_pairs = D_A // LANES
    ti = lax.broadcasted_iota(jnp.int32, (L, L), 0)
    tj = lax.broadcasted_iota(jnp.int32, (L, L), 1)
    pi = lax.broadcasted_iota(jnp.int32, (L, LANES), 0)
    lane = lax.broadcasted_iota(jnp.int32, (L, LANES), 1)
    pj = lane & (HEAD_A - 1)
    lane_lo = lane < HEAD_A
    eye = jnp.where(pj == pi, 1.0, 0.0)
    blk = lambda s: (pi // s) == (pj // s)
    row2 = lax.broadcasted_iota(jnp.int32, (2 * L, LANES), 0)
    col2 = lax.broadcasted_iota(jnp.int32, (2 * L, LANES), 1)
    same_head = (row2 // HEAD_A) == (col2 // HEAD_A)
    diag2 = row2 == col2
    zeros_l = jnp.zeros((L, LANES), F32)
    zeros_2l = jnp.zeros((2 * L, LANES), F32)

    def stack(x):
        return jnp.concatenate([jnp.where(lane_lo, x, 0.0), jnp.where(lane_lo, 0.0, x)], axis=0)

    def mmp(x, y):
        return _mm(x, stack(y))

    cat = jnp.concatenate
    chains = []
    for reverse, refs in ((False, (rf, vf, kkf, lwf, kf, af)), (True, (rb, vb, kkb, lwb, kb, ab))):
        tri = ((tj >= ti) if reverse else (tj <= ti)).astype(BF16)
        strict = (pj > pi) if reverse else (pj < pi)
        incl = (pj >= pi) if reverse else (pj <= pi)
        steps = []
        for sub in (range(n_sub - 1, -1, -1) if reverse else range(n_sub)):
            rows = slice(sub * L, (sub + 1) * L)
            r, v, kk, lw, k, a = (ref[rows, :].astype(F32) for ref in refs)
            G = sum(jnp.dot(tri, part, preferred_element_type=F32) for part in _split_bf16(lw, 3))
            g_end = G[0:1] if reverse else G[L - 1:L]
            gam_end = jnp.exp(g_end)
            inv_gam = jnp.exp(-G)
            rel_end = jnp.exp(g_end - G)
            b = kk * a
            full = dict(At=-kk * jnp.exp(G - lw), Bt=b * inv_gam, Kt=k * inv_gam, Rt=r * jnp.exp(G),
                        Kh=k * rel_end, Bh=b * rel_end, v=v)
            pairs = []
            for p in range(n_pairs):
                sl = slice(p * LANES, (p + 1) * LANES)
                c = {name: val[:, sl] for name, val in full.items()}
                c.update(strict=strict, incl=incl, gam_end=gam_end[:, sl], rows=rows)
                pairs.append(c)
            steps.append(pairs)
        chains.append(steps)
    flat = [c for steps in chains for pairs in steps for c in pairs]

    for c in flat:
        a4 = _mm_nt(cat([c["At"], c["Rt"]], axis=0), cat([stack(c["Bt"]), stack(c["Kt"])], axis=0))
        c["A_ab"] = jnp.where(c["strict"], a4[:L, :LANES], 0.0)
        c["A_ak"] = jnp.where(c["strict"], a4[:L, LANES:], 0.0)
        c["A_rb"] = jnp.where(c["incl"], a4[L:, :LANES], 0.0)
        c["A_rk"] = jnp.where(c["incl"], a4[L:, LANES:], 0.0)
    for c in flat:
        c["A0"] = jnp.where(blk(8), c["A_ab"], 0.0)
        c["A2"] = mmp(c["A0"], c["A0"])
    for c in flat:
        t1 = eye + c["A0"]
        both = mmp(cat([c["A2"], t1], axis=0), c["A2"])
        c["A4"] = both[:L]
        c["T"] = t1 + both[L:]
    for c in flat:
        c["T"] = c["T"] + mmp(c["T"], c["A4"])
    s = 8
    while s < L:
        outer = blk(2 * s) & jnp.logical_not(blk(s))
        for c in flat:
            c["TO"] = mmp(c["T"], jnp.where(outer, c["A_ab"], 0.0))
        for c in flat:
            c["T"] = c["T"] + mmp(c["TO"], c["T"])
        s *= 2
    for c in flat:
        c["AkV"] = mmp(c["A_ak"], c["v"])
    for c in flat:
        wu = _mm(c["T"], cat([stack(c["At"]), stack(c["AkV"])], axis=1))
        c["W"], c["U0"] = wu[:, :LANES], wu[:, LANES:]
    for c in flat:
        kb_t = cat([c["Kh"], c["Bh"]], axis=0).T
        mn = _mm(kb_t, cat([cat([zeros_l, c["v"]], axis=1), cat([c["W"], c["U0"]], axis=1)], axis=0))
        c["M"] = jnp.where(same_head, mn[:, :LANES], 0.0) + jnp.where(diag2, c["gam_end"], 0.0)
        c["N"] = jnp.where(same_head, mn[:, LANES:], 0.0)
    for c in flat:
        qy = _mm(cat([c["A_rb"], c["A_rk"]], axis=1),
                 cat([cat([stack(c["W"]), stack(c["U0"])], axis=1),
                      cat([zeros_2l, stack(c["v"])], axis=1)], axis=0))
        c["Q"] = c["Rt"] + qy[:, :LANES]
        c["Y0"] = qy[:, LANES:]
    for d, (steps, y_ref) in enumerate(zip(chains, (yf_ref, yb_ref))):
        for pairs in steps:
            ys = []
            for p, c in enumerate(pairs):
                qm_h = _mm(cat([c["Q"], c["M"]], axis=0), h_ref[d * n_pairs + p])
                ys.append(qm_h[:L] + c["Y0"])
                h_ref[d * n_pairs + p] = qm_h[L:] + c["N"]
            y_ref[pairs[0]["rows"], :] = cat(ys, axis=1)


def _scan(r, v, kk, lw0, k0, a0, lw1, k1, a1, batch, seq):
    t = r.shape[0]
    rows = SCAN_CHUNKS_PER_STEP * CHUNK
    nc = seq // rows
    fwd = pl.BlockSpec((rows, D_A), lambda b, c: (b * nc + c, 0))
    bwd = pl.BlockSpec((rows, D_A), lambda b, c: (b * nc + nc - 1 - c, 0))
    out = jax.ShapeDtypeStruct((t, D_A), F32)
    return pl.pallas_call(
        _scan_body,
        grid=(batch, nc),
        in_specs=[fwd] * 6 + [bwd] * 6,
        out_specs=[fwd, bwd],
        out_shape=[out, out],
        scratch_shapes=[pltpu.VMEM((2 * (D_A // LANES), LANES, LANES), F32)],
        compiler_params=_params(2),
        name="rwkv_scan",
    )(r, v, kk, lw0, k0, a0, r, v, kk, lw1, k1, a1)


def _bias_features(slope, n, width, key_side):
    pos = lax.broadcasted_iota(jnp.int32, (n, width), 0).astype(F32)
    lane = lax.broadcasted_iota(jnp.int32, (n, width), 1)
    hi, mid, lo = (t.astype(F32) for t in _split_bf16(slope * pos, 3))
    terms = jnp.where(lane % 3 == 0, hi, jnp.where(lane % 3 == 1, mid, lo))
    if key_side:
        return jnp.where(lane < 3, -1.0, jnp.where(lane < 6, terms, 0.0))
    return jnp.where(lane < 3, terms, jnp.where(lane < 6, 1.0, 0.0))


def _attn_body(q_ref, k_ref, vt_ref, lv_ref, sg_ref, qg_ref, kg_ref, o_ref, qs_ref, feat_ref, src_ref,
               s_ref, mx_ref, p_ref, al_ref, m_ref, acc_ref, *, lam_init):
    tq = src_ref.shape[1]
    n_kv, _, tk = vt_ref.shape
    seq = n_kv * tk
    h = pl.program_id(1)
    i = pl.program_id(2)
    slope_bits = (127 - 2 * (h + 1)) << 23
    slope = (lax.bitcast_convert_type(jnp.full((1, 1), slope_bits, jnp.int32), jnp.float32).astype(F32)
             * LOG2E)
    bound =(HEAD_B * ROUNDING_MARGIN * jnp.max(jnp.abs(qg_ref[...]), axis=-1, keepdims=True)
             * jnp.max(jnp.abs(kg_ref[...]), axis=-1, keepdims=True))
    reach = jnp.minimum((F32_ZERO_EXP2 + 2.0 * bound) / slope + 1.0, float(seq)).astype(jnp.int32)[0, 0]

    n_st = tk // tq
    hw = 2 * HEAD_B

    @pl.when(i == 0)
    def _():
        kv_pos = lax.broadcasted_iota(jnp.int32, (tk, tq), 0)
        q_pos = lax.broadcasted_iota(jnp.int32, (tk, tq), 1)
        src_ref[...] = slope * (q_pos - kv_pos).astype(F32)
        key_feat = _bias_features(slope, tk, hw, key_side=True)
        feat_ref[0] = key_feat.astype(BF16)
        feat_ref[1] = (-key_feat).astype(BF16)
        feat_ref[2] = jnp.zeros((tk, hw), BF16)
        q_feat = _bias_features(slope, tq, hw, key_side=False).T.astype(BF16)
        for st in range(n_st):
            qs_ref[st, hw:, :tq] = q_feat
            qs_ref[st, hw:, tq:] = q_feat

    chan =lax.broadcasted_iota(jnp.int32, (hw, tq), 0)
    for st in range(n_st):
        q_t = q_ref[st * tq:(st + 1) * tq, :].astype(F32).T.astype(BF16)
        zero = jnp.zeros_like(q_t)
        qs_ref[st, :hw, :tq] = jnp.where(chan < HEAD_B, q_t, zero)
        qs_ref[st, :hw, tq:] = jnp.where(chan < HEAD_B, zero, q_t)
    q0 = i * tk
    j_lo = jnp.maximum(q0 - reach, 0) // tk
    j_hi = jnp.minimum(q0 + tk - 1 + reach, seq - 1) // tk + 1
    n_left = i - j_lo
    n_off = n_left + (j_hi - i - 1)

    def scores(j, side, st):
        kb = k_ref[pl.ds(pl.multiple_of(j * tk, tk), tk), :]
        return jnp.dot(jnp.concatenate([kb, feat_ref[side]], axis=1), qs_ref[st],
                       preferred_element_type=F32)

    def softmax_update(s, s_max, cst, st):
        m_old = m_ref[st]
        m_new = jnp.maximum(m_old, s_max + cst)
        alpha = jnp.exp2(m_old - m_new)
        p = jnp.exp2(s - (m_new - cst))
        m_ref[st] = m_new
        return alpha, p.astype(BF16)

    def block(n):
        j = jnp.clip(j_lo + n + (n >= n_left).astype(jnp.int32), 0, n_kv - 1)
        j = jnp.where(n < 0, i, j)
        return j, (j > i).astype(jnp.int32), jnp.logical_and(n >= 0, n < n_off)

    def qk_stage(n, slot):
        j, side, _ = block(n)
        for st in range(n_st):
            s = scores(j, side, st)
            s_ref[slot, st] = s
            mx_ref[slot, st] = jnp.max(s, axis=0, keepdims=True)

    def softmax_stage(n, slot):
        j, _, valid = block(n)
        for st in range(n_st):
            dist0 = jnp.abs(q0 + st * tq - j * tk).astype(F32)
            cst = jnp.where(valid, -slope * dist0, -1e30)
            alpha, p = softmax_update(s_ref[slot, st], mx_ref[slot, st], cst, st)
            al_ref[slot, st] = alpha
            p_ref[slot, st] = p

    def pv_stage(n, slot):
        j, _, _ = block(n)
        for st in range(n_st):
            acc_ref[st] = al_ref[slot, st] * acc_ref[st] + jnp.dot(vt_ref[j], p_ref[slot, st],
                                                                    preferred_element_type=F32)

    m_ref[...] = jnp.full(m_ref.shape, -1e30, F32)
    acc_ref[...] = jnp.zeros(acc_ref.shape, F32)
    diag_scores = [scores(i, 2, st) for st in range(n_st)]
    qk_stage(0, 0)
    src = src_ref[...]
    for st in range(n_st):
        bias = jnp.abs(src + slope * float(st * tq))
        s = diag_scores[st] - jnp.concatenate([bias, bias], axis=1)
        alpha, p = softmax_update(s, jnp.max(s, axis=0, keepdims=True), jnp.zeros((1, 1), F32), st)
        al_ref[1, st] = alpha
        p_ref[1, st] = p

    def pipelined_pair(t, carry):
        n = 2 * t
        qk_stage(n + 1, 1)
        softmax_stage(n, 0)
        pv_stage(n - 1, 1)
        qk_stage(n + 2, 0)
        softmax_stage(n + 1, 1)
        pv_stage(n, 0)
        return carry

    n_pairs = (n_off + 1) // 2
    lax.fori_loop(0, n_pairs, pipelined_pair, 0)
    pv_stage(2 * n_pairs - 1, 1)

    lv = lv_ref[...]
    lam = (jnp.exp(jnp.sum(lv[0:1] * lv[1:2], axis=-1, keepdims=True))
           - jnp.exp(jnp.sum(lv[2:3] * lv[3:4], axis=-1, keepdims=True)) + lam_init)
    for st in range(n_st):
        acc = acc_ref[st, :hw, :]
        l = acc_ref[st, hw:hw + 1, :]
        o_t = acc[:, :tq] / l[:, :tq] - lam * (acc[:, tq:] / l[:, tq:])
        o_t = o_t * lax.rsqrt(jnp.mean(o_t * o_t, axis=0, keepdims=True) + NORM_EPS)
        o_ref[st * tq:(st + 1) * tq, :] = (o_t.T * sg_ref[...] * (1.0 - lam_init)).astype(o_ref.dtype)


def _attn(q, k, vt, lam_vecs, sub_g, qg, kg, lam_init, batch, seq, layer):
    t = q.shape[0]
    tk = vt.shape[2]
    tq = ATTN_TQ
    n_st = tk // tq
    nq = seq // tk
    n_kv = seq // tk
    hw = 2 * HEAD_B
    return pl.pallas_call(
        functools.partial(_attn_body, lam_init=lam_init),
        grid=(batch, N_HEADS_B, nq),
        in_specs=[pl.BlockSpec((tk, hw), lambda b, h, i: (b * nq + i, h)),
                  pl.BlockSpec((seq, hw), lambda b, h, i: (b, h)),
                  pl.BlockSpec((n_kv, V_ROWS, tk), lambda b, h, i: (b, h, 0)),
                  _layer_spec((4, HEAD_B), layer), _layer_spec((1, hw), layer),
                  _layer_spec((1, D_B), layer), _layer_spec((1, D_B), layer)],
        out_specs=pl.BlockSpec((tk, hw), lambda b, h, i: (b * nq + i, h)),
        out_shape=jax.ShapeDtypeStruct((t, D_B), BF16),
        scratch_shapes=[pltpu.VMEM((n_st, 2 * hw, 2 * tq), BF16),
                        pltpu.VMEM((3, tk, hw), BF16),
                        pltpu.VMEM((tk, tq), F32),
                        pltpu.VMEM((2, n_st, tk, 2 * tq), F32),
                        pltpu.VMEM((2, n_st, 1, 2 * tq), F32),
                        pltpu.VMEM((2, n_st, tk, 2 * tq), BF16),
                        pltpu.VMEM((2, n_st, 1, 2 * tq), F32),
                        pltpu.VMEM((n_st, 1, 2 * tq), F32),
                        pltpu.VMEM((n_st, V_ROWS, 2 * tq), F32)],
        compiler_params=_params(3),
        name="diff_attn",
    )(q, k, vt, lam_vecs, sub_g, qg, kg)


def _merge_body(yf_ref, yb_ref, gate_ref, bonus_ref, ob_ref, gt_ref, x_ref, lng_ref, lnb_ref, bd_ref,
                wb0_ref, wb1_ref, wo_ref, o_ref):
    bd = bd_ref[...]
    y = yf_ref[...] + yb_ref[...]
    yc = y - _mm_split_lhs(y, bd)
    var = _mm(yc * yc, bd)
    yn = yc * lax.rsqrt(var + GN_EPS) * lng_ref[...] + lnb_ref[...]
    oa = (yn + bonus_ref[...]) * gate_ref[...]
    gts = gt_ref[...].astype(F32)
    merged = (gts[:, :D_MODEL] * _mm(oa, wb0_ref[...])
              + gts[:, D_MODEL:] * jnp.dot(ob_ref[...], wb1_ref[...], preferred_element_type=F32))
    o_ref[...] = x_ref[...] + _mm(merged, wo_ref[...])


def _merge(yf, yb, gate, bonus, ob, gts, x2, ln_g, ln_b, bd_mean, wb, wo, layer):
    t = x2.shape[0]
    tm = min(TOKEN_TILE, t)
    row = lambda width: pl.BlockSpec((tm, width), lambda i: (i, 0))
    return pl.pallas_call(
        _merge_body,
        grid=(t // tm,),
        in_specs=[row(D_A), row(D_A), row(D_A), row(D_A), row(D_B), row(2 * D_MODEL), row(D_MODEL),
                  _layer_spec((1, D_A), layer), _layer_spec((1, D_A), layer), _const_spec((D_A, D_A)),
                  _layer_spec((D_A, D_MODEL), layer, 0), _layer_spec((D_B, D_MODEL), layer, 1),
                  _layer_spec((D_MODEL, D_MODEL), layer)],
        out_specs=row(D_MODEL),
        out_shape=jax.ShapeDtypeStruct((t, D_MODEL), F32),
        compiler_params=_params(1),
        name="merge",
    )(yf, yb, gate, bonus, ob, gts, x2, ln_g, ln_b, bd_mean, wb, wb, wo)


def _block_diag2(w):
    z = jnp.zeros_like(w[:, 0])
    return jnp.concatenate([jnp.concatenate([w[:, 0], z], axis=2), jnp.concatenate([z, w[:, 1]], axis=2)],
                           axis=1)


def kernel(x, norm_ffn1, ffn1_in, ffn1_out, norm_mix, w_in, rwkv_mu, decay_w0, decay_w2, iclr_a0, iclr_a2,
           gate_g2, k_k, k_a, r_k, ln_x_g, ln_x_b, q_gain, k_gain, diff_lambda, subln_g, w_branch, w_out,
           norm_ffn2, ffn2_in, ffn2_out):
    batch, seq, d = x.shape
    depth = norm_ffn1.shape[0]
    assert seq % TOKEN_TILE == 0 and d == D_MODEL
    x2 = x.reshape(batch * seq, d)
    head_id = jnp.arange(D_A) // HEAD_A
    same = head_id[:, None] == head_id[None, :]
    bd_sum = same.astype(BF16)
    bd_mean = (same.astype(F32) / HEAD_A).astype(BF16)
    rows = lambda a: a.reshape(depth, 1, -1)
    bf = lambda a: a.astype(BF16)
    n_qk = D_B // HEAD_B
    qg = rows(jnp.tile(q_gain, (1, n_qk))) * (HEAD_B ** -0.5 * LOG2E)
    kg = rows(jnp.tile(k_gain, (1, n_qk)))
    ffn1 = (rows(norm_ffn1), bf(ffn1_in), bf(ffn1_out))
    ffn2 = (rows(norm_ffn2), bf(ffn2_in), bf(ffn2_out))
    proj_w = (rows(norm_mix), bf(w_in))
    prep_w = (rows(rwkv_mu), rows(decay_w0), bf(_block_diag2(decay_w2)), rows(iclr_a0),
              bf(_block_diag2(iclr_a2)), bf(gate_g2), rows(k_k), rows(k_a), rows(r_k))
    merge_w = (rows(ln_x_g), rows(ln_x_b), bd_mean, bf(w_branch), bf(w_out))
    sub_g = rows(subln_g)
    for l in range(depth):
        lam_init = 0.8 - 0.6 * math.exp(-0.3 * l)
        x2 = _ffn(x2, *ffn1, l)
        q, k, vt, gts, r, vv, kk, lw0, k0, a0, lw1, k1, a1, gate, bonus = _proj(
            x2, seq, *proj_w, qg, kg, bd_mean, *prep_w, bd_sum, l)
        yf, yb = _scan(r, vv, kk, lw0, k0, a0, lw1, k1, a1, batch, seq)
        ob = _attn(q, k, vt, diff_lambda, sub_g, qg, kg, lam_init, batch, seq, l)
        x2 = _merge(yf, yb, gate, bonus, ob, gts, x2, *merge_w, l)
        x2 = _ffn(x2, *ffn2, l)
    return x2.reshape(batch, seq, d)
```

```python
import functools
import math

import jax
import jax.numpy as jnp
from jax import lax
from jax.experimental import pallas as pl
from jax.experimental.pallas import tpu as pltpu

F32 = jnp.float32
BF16 = jnp.bfloat16

D_MODEL = 1024
D_A = 512
HEAD_A = 64
LORA = 64
LORA_GATE = 128
C_RWKV = 3 * D_A + 2 * LORA + 2 * LORA + LORA_GATE
D_B = 512
HEAD_B = 64
N_HEADS_B = D_B // (2 * HEAD_B)
D_FF = 2816
DECAY_SCALE = math.exp(-0.5)
GN_EPS = 64e-5
NORM_EPS = 1e-6
LOG2E = math.log2(math.e)
F32_ZERO_EXP2 = 150.0
ROUNDING_MARGIN = 1.02

LANES = 128
SUBLANES = 8
VMEM_LIMIT = 56 * 1024 * 1024

TOKEN_TILE = 512
MXU_WIDTH = 256
FF_SPLITS = (0, 6 * MXU_WIDTH, D_FF)
CHUNK = 64
SCAN_CHUNKS_PER_STEP = 2
ATTN_TQ = 256
V_ROWS = 2 * HEAD_B + 16

_NT = (((1,), (1,)), ((), ()))


def _mm(a, b):
    return jnp.dot(a.astype(BF16), b.astype(BF16), preferred_element_type=F32)


def _mm_nt(a, b):
    return lax.dot_general(a.astype(BF16), b.astype(BF16), _NT, preferred_element_type=F32)


def _split_bf16(x, n):
    parts = []
    for _ in range(n - 1):
        hi = x.astype(BF16)
        parts.append(hi)
        x = x - hi.astype(F32)
    parts.append(x.astype(BF16))
    return parts


def _mm_split_lhs(x, w, n=2):
    return sum(jnp.dot(p, w, preferred_element_type=F32) for p in _split_bf16(x, n))


def _rms_norm(x, g):
    return x * lax.rsqrt(jnp.mean(x * x, axis=-1, keepdims=True) + NORM_EPS) * g


def _const_spec(shape):
    return pl.BlockSpec(shape, lambda *_: (0,) * len(shape), pipeline_mode=pl.Buffered(1))


def _layer_spec(shape, layer, *lead):
    index = (layer,) + lead + (0,) * len(shape)
    return pl.BlockSpec((None,) * (1 + len(lead)) + shape, lambda *_: index, pipeline_mode=pl.Buffered(1))


def _params(n_axes):
    return pltpu.CompilerParams(dimension_semantics=("arbitrary",) * n_axes,
                                vmem_limit_bytes=VMEM_LIMIT)


def _ffn_body(x_ref, g_ref, win_ref, wout_ref, o_ref):
    x = x_ref[...]
    h = _rms_norm(x, g_ref[...]).astype(BF16)
    acc = jnp.zeros(x.shape, F32)
    for lo, hi in zip(FF_SPLITS[:-1], FF_SPLITS[1:]):
        gate = jnp.dot(h, win_ref[:, lo:hi], preferred_element_type=F32)
        up = jnp.dot(h, win_ref[:, D_FF + lo:D_FF + hi], preferred_element_type=F32)
        act = (gate * jax.nn.sigmoid(gate) * up).astype(BF16)
        acc = acc + jnp.dot(act, wout_ref[lo:hi, :], preferred_element_type=F32)
    o_ref[...] = x + 0.5 * acc


def _ffn(x2, g, w_in, w_out, layer):
    t = x2.shape[0]
    tm = min(TOKEN_TILE, t)
    return pl.pallas_call(
        _ffn_body,
        grid=(t // tm,),
        in_specs=[pl.BlockSpec((tm, D_MODEL), lambda i: (i, 0)),
                  _layer_spec((1, D_MODEL), layer),
                  _layer_spec((D_MODEL, 2 * D_FF), layer),
                  _layer_spec((D_FF, D_MODEL), layer)],
        out_specs=pl.BlockSpec((tm, D_MODEL), lambda i: (i, 0)),
        out_shape=jax.ShapeDtypeStruct((t, D_MODEL), F32),
        compiler_params=_params(1),
        name="ffn",
    )(x2, g, w_in, w_out)


_Q0 = C_RWKV
_K0 = C_RWKV + D_B
_V0 = C_RWKV + 2 * D_B
_G0 = C_RWKV + 3 * D_B


def _proj_body(x_ref, xp_ref, xn_ref, g_ref, w_ref, qg_ref, kg_ref, bdm_ref, mu_ref, w0_ref, w2_ref, a0_ref,
               a2_ref, g2_ref, kk_ref, ka_ref, rk_ref, bds_ref,
               q_ref, k_ref, vt_ref, gt_ref, r_out, v_out, kk_out, lw0_out, k0_out, a0_out, lw1_out, k1_out,
               a1_out, gate_out, bonus_out, *, seq):
    i = pl.program_id(0)
    g = g_ref[...]
    h = _rms_norm(x_ref[...], g).astype(BF16)
    tm = h.shape[0]

    bd = bdm_ref[...]
    q = jnp.dot(h, w_ref[:, _Q0:_K0], preferred_element_type=F32)
    q_ref[...] = (q * lax.rsqrt(_mm(q * q, bd) + NORM_EPS) * qg_ref[...]).astype(BF16)
    k = jnp.dot(h, w_ref[:, _K0:_V0], preferred_element_type=F32)
    k_ref[...] = (k * lax.rsqrt(_mm(k * k, bd) + NORM_EPS) * kg_ref[...]).astype(BF16)
    vt = jnp.dot(h, w_ref[:, _V0:_G0], preferred_element_type=F32).T.astype(BF16)
    hw = 2 * HEAD_B
    ones = jnp.ones((V_ROWS - hw, vt.shape[1]), BF16)
    vt_ref[0] = jnp.concatenate([part for hd in range(N_HEADS_B)
                                 for part in (vt[hd * hw:(hd + 1) * hw], ones)], axis=0)
    gt_ref[...] = jax.nn.sigmoid(jnp.dot(h, w_ref[:, _G0:], preferred_element_type=F32)).astype(BF16)

    w_rwkv = w_ref[:, :C_RWKV]
    p = jnp.dot(h, w_rwkv, preferred_element_type=F32)
    edge = jnp.concatenate([xp_ref[...], xn_ref[...]], axis=0)
    p_edge = jnp.dot(_rms_norm(edge, g).astype(BF16), w_rwkv, preferred_element_type=F32)
    keep_prev = jnp.where((i * tm) % seq == 0, 0.0, 1.0)
    keep_next = jnp.where(((i + 1) * tm) % seq == 0, 0.0, 1.0)
    row = lax.broadcasted_iota(jnp.int32, p.shape, 0)
    prev = jnp.where(row == 0, keep_prev * p_edge[SUBLANES - 1:SUBLANES, :], pltpu.roll(p, 1, 0))
    nxt = jnp.where(row == tm - 1, keep_next * p_edge[SUBLANES:SUBLANES + 1, :], pltpu.roll(p, tm - 1, 0))
    p = p + mu_ref[...] * (0.5 * (prev + nxt) - p)

    r = p[:, :D_A]
    k = p[:, D_A:2 * D_A]
    v = p[:, 2 * D_A:3 * D_A]
    dw = p[:, 3 * D_A:3 * D_A + 2 * LORA]
    da = p[:, 3 * D_A + 2 * LORA:3 * D_A + 4 * LORA]
    dg = p[:, 3 * D_A + 4 * LORA:]
    bd = bds_ref[...]

    logw = -DECAY_SCALE * jax.nn.sigmoid(w0_ref[...] + _mm(jnp.tanh(dw), w2_ref[...]))
    iclr = jax.nn.sigmoid(a0_ref[...] + _mm(da, a2_ref[...]))
    gate_out[...] = _mm(jax.nn.sigmoid(dg), g2_ref[...])

    kk = k * kk_ref[...]
    norm = jnp.sqrt(_mm(kk * kk, bd))
    kk_out[...] = (kk / jnp.maximum(norm, 1e-12)).astype(kk_out.dtype)

    ka = ka_ref[...]
    a_f = iclr[:, :D_A]
    a_b = iclr[:, D_A:]
    k_f = k * (1.0 + (a_f - 1.0) * ka)
    k_b = k * (1.0 + (a_b - 1.0) * ka)
    bonus = _mm_split_lhs(r * rk_ref[...] * (k_f + k_b), bd)
    r_out[...] = r.astype(r_out.dtype)
    v_out[...] = v.astype(v_out.dtype)
    lw0_out[...] = logw[:, :D_A]
    k0_out[...] = k_f.astype(k0_out.dtype)
    a0_out[...] = a_f.astype(a0_out.dtype)
    lw1_out[...] = logw[:, D_A:]
    k1_out[...] = k_b.astype(k1_out.dtype)
    a1_out[...] = a_b.astype(a1_out.dtype)
    bonus_out[...] = bonus * v


def _proj(x2, seq, g, w, qg, kg, bd_mean, mu, w0, w2cat, a0, a2cat, g2, k_k, k_a, r_k, bd_sum, layer):
    t = x2.shape[0]
    tm = TOKEN_TILE
    c_in = w.shape[2]
    nb = tm // SUBLANES
    last = t // SUBLANES - 1
    row = lambda width: pl.BlockSpec((tm, width), lambda i: (i, 0))
    wide = jax.ShapeDtypeStruct((t, D_A), F32)
    half = jax.ShapeDtypeStruct((t, D_A), BF16)
    return pl.pallas_call(
        functools.partial(_proj_body, seq=seq),
        grid=(t // tm,),
        in_specs=[row(D_MODEL),
                  pl.BlockSpec((SUBLANES, D_MODEL), lambda i: (jnp.maximum(i * nb - 1, 0), 0)),
                  pl.BlockSpec((SUBLANES, D_MODEL), lambda i: (jnp.minimum((i + 1) * nb, last), 0)),
                  _layer_spec((1, D_MODEL), layer), _layer_spec((D_MODEL, c_in), layer),
                  _layer_spec((1, D_B), layer), _layer_spec((1, D_B), layer), _const_spec((D_B, D_B)),
                  _layer_spec((1, C_RWKV), layer),
                  _layer_spec((1, 2 * D_A), layer), _layer_spec((2 * LORA, 2 * D_A), layer),
                  _layer_spec((1, 2 * D_A), layer), _layer_spec((2 * LORA, 2 * D_A), layer),
                  _layer_spec((LORA_GATE, D_A), layer),
                  _layer_spec((1, D_A), layer), _layer_spec((1, D_A), layer), _layer_spec((1, D_A), layer),
                  _const_spec((D_A, D_A))],
        out_specs=[row(D_B), row(D_B),
                   pl.BlockSpec((1, N_HEADS_B * V_ROWS, tm), lambda i: (i, 0, 0)), row(2 * D_MODEL)]
                  + [row(D_A)] * 11,
        out_shape=[jax.ShapeDtypeStruct((t, D_B), BF16),
                   jax.ShapeDtypeStruct((t, D_B), BF16),
                   jax.ShapeDtypeStruct((t // tm, N_HEADS_B * V_ROWS, tm), BF16),
                   jax.ShapeDtypeStruct((t, 2 * D_MODEL), BF16),
                   half, half, half, wide, half, half, wide, half, half, wide, wide],
        compiler_params=_params(1),
        name="proj",
    )(x2, x2, x2, g, w, qg, kg, bd_mean, mu, w0, w2cat, a0, a2cat, g2, k_k, k_a, r_k, bd_sum)


def _scan_body(rf, vf, kkf, lwf, kf, af, rb, vb, kkb, lwb, kb, ab, yf_ref, yb_ref, h_ref):
    @pl.when(pl.program_id(1) == 0)
    def _():
        h_ref[...] = jnp.zeros(h_ref.shape, F32)

    L = CHUNK
    n_sub = rf.shape[0] // L
    n_pairs = D_A // LANES
    ti = lax.broadcasted_iota(jnp.int32, (L, L), 0)
    tj = lax.broadcasted_iota(jnp.int32, (L, L), 1)
    pi = lax.broadcasted_iota(jnp.int32, (L, LANES), 0)
    lane = lax.broadcasted_iota(jnp.int32, (L, LANES), 1)
    pj = lane & (HEAD_A - 1)
    lane_lo = lane < HEAD_A
    eye = jnp.where(pj == pi, 1.0, 0.0)
    blk = lambda s: (pi // s) == (pj // s)
    row2 = lax.broadcasted_iota(jnp.int32, (2 * L, LANES), 0)
    col2 = lax.broadcasted_iota(jnp.int32, (2 * L, LANES), 1)
    same_head = (row2 // HEAD_A) == (col2 // HEAD_A)
    diag2 = row2 == col2
    zeros_l = jnp.zeros((L, LANES), F32)
    zeros_2l = jnp.zeros((2 * L, LANES), F32)

    def stack(x):
        return jnp.concatenate([jnp.where(lane_lo, x, 0.0), jnp.where(lane_lo, 0.0, x)], axis=0)

    def mmp(x, y):
        return _mm(x, stack(y))

    cat = jnp.concatenate
    chains = []
    for reverse, refs in ((False, (rf, vf, kkf, lwf, kf, af)), (True, (rb, vb, kkb, lwb, kb, ab))):
        tri = ((tj >= ti) if reverse else (tj <= ti)).astype(BF16)
        strict = (pj > pi) if reverse else (pj < pi)
        incl = (pj >= pi) if reverse else (pj <= pi)
        steps = []
        for sub in (range(n_sub - 1, -1, -1) if reverse else range(n_sub)):
            rows = slice(sub * L, (sub + 1) * L)
            r, v, kk, lw, k, a = (ref[rows, :].astype(F32) for ref in refs)
            G = sum(jnp.dot(tri, part, preferred_element_type=F32) for part in _split_bf16(lw, 3))
            g_end = G[0:1] if reverse else G[L - 1:L]
            gam_end = jnp.exp(g_end)
            inv_gam = jnp.exp(-G)
            rel_end = jnp.exp(g_end - G)
            b = kk * a
            full = dict(At=-kk * jnp.exp(G - lw), Bt=b * inv_gam, Kt=k * inv_gam, Rt=r * jnp.exp(G),
                        Kh=k * rel_end, Bh=b * rel_end, v=v)
            pairs = []
            for p in range(n_pairs):
                sl = slice(p * LANES, (p + 1) * LANES)
                c = {name: val[:, sl] for name, val in full.items()}
                c.update(strict=strict, incl=incl, gam_end=gam_end[:, sl], rows=rows)
                pairs.append(c)
            steps.append(pairs)
        chains.append(steps)
    flat = [c for steps in chains for pairs in steps for c in pairs]

    for c in flat:
        a4 = _mm_nt(cat([c["At"], c["Rt"]], axis=0), cat([stack(c["Bt"]), stack(c["Kt"])], axis=0))
        c["A_ab"] = jnp.where(c["strict"], a4[:L, :LANES], 0.0)
        c["A_ak"] = jnp.where(c["strict"], a4[:L, LANES:], 0.0)
        c["A_rb"] = jnp.where(c["incl"], a4[L:, :LANES], 0.0)
        c["A_rk"] = jnp.where(c["incl"], a4[L:, LANES:], 0.0)
    for c in flat:
        c["A0"] = jnp.where(blk(8), c["A_ab"], 0.0)
        c["A2"] = mmp(c["A0"], c["A0"])
    for c in flat:
        t1 = eye + c["A0"]
        both = mmp(cat([c["A2"], t1], axis=0), c["A2"])
        c["A4"] = both[:L]
        c["T"] = t1 + both[L:]
    for c in flat:
        c["T"] = c["T"] + mmp(c["T"], c["A4"])
    s = 8
    while s < L:
        outer = blk(2 * s) & jnp.logical_not(blk(s))
        for c in flat:
            c["TO"] = mmp(c["T"], jnp.where(outer, c["A_ab"], 0.0))
        for c in flat:
            c["T"] = c["T"] + mmp(c["TO"], c["T"])
        s *= 2
    for c in flat:
        c["AkV"] = mmp(c["A_ak"], c["v"])
    for c in flat:
        wu = _mm(c["T"], cat([stack(c["At"]), stack(c["AkV"])], axis=1))
        c["W"], c["U0"] = wu[:, :LANES], wu[:, LANES:]
    for c in flat:
        kb_t = cat([c["Kh"], c["Bh"]], axis=0).T
        mn = _mm(kb_t, cat([cat([zeros_l, c["v"]], axis=1), cat([c["W"], c["U0"]], axis=1)], axis=0))
        c["M"] = jnp.where(same_head, mn[:, :LANES], 0.0) + jnp.where(diag2, c["gam_end"], 0.0)
        c["N"] = jnp.where(same_head, mn[:, LANES:], 0.0)
    for c in flat:
        qy = _mm(cat([c["A_rb"], c["A_rk"]], axis=1),
                 cat([cat([stack(c["W"]), stack(c["U0"])], axis=1),
                      cat([zeros_2l, stack(c["v"])], axis=1)], axis=0))
        c["Q"] = c["Rt"] + qy[:, :LANES]
        c["Y0"] = qy[:, LANES:]
    for d, (steps, y_ref) in enumerate(zip(chains, (yf_ref, yb_ref))):
        for pairs in steps:
            ys = []
            for p, c in enumerate(pairs):
                qm_h = _mm(cat([c["Q"], c["M"]], axis=0), h_ref[d * n_pairs + p])
                ys.append(qm_h[:L] + c["Y0"])
                h_ref[d * n_pairs + p] = qm_h[L:] + c["N"]
            y_ref[pairs[0]["rows"], :] = cat(ys, axis=1)


def _scan(r, v, kk, lw0, k0, a0, lw1, k1, a1, batch, seq):
    t = r.shape[0]
    rows = SCAN_CHUNKS_PER_STEP * CHUNK
    nc = seq // rows
    fwd = pl.BlockSpec((rows, D_A), lambda b, c: (b * nc + c, 0))
    bwd = pl.BlockSpec((rows, D_A), lambda b, c: (b * nc + nc - 1 - c, 0))
    out = jax.ShapeDtypeStruct((t, D_A), F32)
    return pl.pallas_call(
        _scan_body,
        grid=(batch, nc),
        in_specs=[fwd] * 6 + [bwd] * 6,
        out_specs=[fwd, bwd],
        out_shape=[out, out],
        scratch_shapes=[pltpu.VMEM((2 * (D_A // LANES), LANES, LANES), F32)],
        compiler_params=_params(2),
        name="rwkv_scan",
    )(r, v, kk, lw0, k0, a0, r, v, kk, lw1, k1, a1)


def _bias_features(slope, n, width, key_side):
    pos = lax.broadcasted_iota(jnp.int32, (n, width), 0).astype(F32)
    lane = lax.broadcasted_iota(jnp.int32, (n, width), 1)
    hi, mid, lo = (t.astype(F32) for t in _split_bf16(slope * pos, 3))
    terms = jnp.where(lane % 3 == 0, hi, jnp.where(lane % 3 == 1, mid, lo))
    if key_side:
        return jnp.where(lane < 3, -1.0, jnp.where(lane < 6, terms, 0.0))
    return jnp.where(lane < 3, terms, jnp.where(lane < 6, 1.0, 0.0))


def _attn_body(q_ref, k_ref, vt_ref, lv_ref, sg_ref, qg_ref, kg_ref, o_ref, qs_ref, feat_ref, src_ref,
               s_ref, mx_ref, p_ref, al_ref, m_ref, acc_ref, *, lam_init):
    tq = src_ref.shape[1]
    n_kv, _, tk = vt_ref.shape
    seq = n_kv * tk
    h = pl.program_id(1)
    i = pl.program_id(2)
    slope_bits = (127 - 2 * (h + 1)) << 23
    slope = (lax.bitcast_convert_type(jnp.full((1, 1), slope_bits, jnp.int32), jnp.float32).astype(F32)
             * LOG2E)
    bound =(HEAD_B * ROUNDING_MARGIN * jnp.max(jnp.abs(qg_ref[...]), axis=-1, keepdims=True)
             * jnp.max(jnp.abs(kg_ref[...]), axis=-1, keepdims=True))
    reach = jnp.minimum((F32_ZERO_EXP2 + 2.0 * bound) / slope + 1.0, float(seq)).astype(jnp.int32)[0, 0]

    n_st = tk // tq
    hw = 2 * HEAD_B

    @pl.when(i == 0)
    def _():
        kv_pos = lax.broadcasted_iota(jnp.int32, (tk, tq), 0)
        q_pos = lax.broadcasted_iota(jnp.int32, (tk, tq), 1)
        src_ref[...] = slope * (q_pos - kv_pos).astype(F32)
        key_feat = _bias_features(slope, tk, hw, key_side=True)
        feat_ref[0] = key_feat.astype(BF16)
        feat_ref[1] = (-key_feat).astype(BF16)
        feat_ref[2] = jnp.zeros((tk, hw), BF16)
        q_feat = _bias_features(slope, tq, hw, key_side=False).T.astype(BF16)
        for st in range(n_st):
            qs_ref[st, hw:, :tq] = q_feat
            qs_ref[st, hw:, tq:] = q_feat

    chan =lax.broadcasted_iota(jnp.int32, (hw, tq), 0)
    for st in range(n_st):
        q_t = q_ref[st * tq:(st + 1) * tq, :].astype(F32).T.astype(BF16)
        zero = jnp.zeros_like(q_t)
        qs_ref[st, :hw, :tq] = jnp.where(chan < HEAD_B, q_t, zero)
        qs_ref[st, :hw, tq:] = jnp.where(chan < HEAD_B, zero, q_t)
    q0 = i * tk
    j_lo = jnp.maximum(q0 - reach, 0) // tk
    j_hi = jnp.minimum(q0 + tk - 1 + reach, seq - 1) // tk + 1
    n_left = i - j_lo
    n_off = n_left + (j_hi - i - 1)

    def scores(j, side, st):
        kb = k_ref[pl.ds(pl.multiple_of(j * tk, tk), tk), :]
        return jnp.dot(jnp.concatenate([kb, feat_ref[side]], axis=1), qs_ref[st],
                       preferred_element_type=F32)

    def softmax_update(s, s_max, cst, st):
        m_old = m_ref[st]
        m_new = jnp.maximum(m_old, s_max + cst)
        alpha = jnp.exp2(m_old - m_new)
        p = jnp.exp2(s - (m_new - cst))
        m_ref[st] = m_new
        return alpha, p.astype(BF16)

    def block(n):
        j = jnp.clip(j_lo + n + (n >= n_left).astype(jnp.int32), 0, n_kv - 1)
        j = jnp.where(n < 0, i, j)
        return j, (j > i).astype(jnp.int32), jnp.logical_and(n >= 0, n < n_off)

    def qk_stage(n, slot):
        j, side, _ = block(n)
        for st in range(n_st):
            s = scores(j, side, st)
            s_ref[slot, st] = s
            mx_ref[slot, st] = jnp.max(s, axis=0, keepdims=True)

    def softmax_stage(n, slot):
        j, _, valid = block(n)
        for st in range(n_st):
            dist0 = jnp.abs(q0 + st * tq - j * tk).astype(F32)
            cst = jnp.where(valid, -slope * dist0, -1e30)
            alpha, p = softmax_update(s_ref[slot, st], mx_ref[slot, st], cst, st)
            al_ref[slot, st] = alpha
            p_ref[slot, st] = p

    def pv_stage(n, slot):
        j, _, _ = block(n)
        for st in range(n_st):
            acc_ref[st] = al_ref[slot, st] * acc_ref[st] + jnp.dot(vt_ref[j], p_ref[slot, st],
                                                                    preferred_element_type=F32)

    m_ref[...] = jnp.full(m_ref.shape, -1e30, F32)
    acc_ref[...] = jnp.zeros(acc_ref.shape, F32)
    diag_scores = [scores(i, 2, st) for st in range(n_st)]
    qk_stage(0, 0)
    src = src_ref[...]
    for st in range(n_st):
        bias = jnp.abs(src + slope * float(st * tq))
        s = diag_scores[st] - jnp.concatenate([bias, bias], axis=1)
        alpha, p = softmax_update(s, jnp.max(s, axis=0, keepdims=True), jnp.zeros((1, 1), F32), st)
        al_ref[1, st] = alpha
        p_ref[1, st] = p

    def pipelined_pair(t, carry):
        n = 2 * t
        qk_stage(n + 1, 1)
        softmax_stage(n, 0)
        pv_stage(n - 1, 1)
        qk_stage(n + 2, 0)
        softmax_stage(n + 1, 1)
        pv_stage(n, 0)
        return carry

    n_pairs = n_off // 2
    lax.fori_loop(0, n_pairs, pipelined_pair, 0)
    last = 2 * n_pairs

    @pl.when(n_off % 2 == 1)
    def _():
        softmax_stage(last, 0)
        pv_stage(last - 1, 1)
        pv_stage(last, 0)

    @pl.when(n_off % 2 == 0)
    def _():
        pv_stage(last - 1, 1)

    lv = lv_ref[...]
    lam = (jnp.exp(jnp.sum(lv[0:1] * lv[1:2], axis=-1, keepdims=True))
           - jnp.exp(jnp.sum(lv[2:3] * lv[3:4], axis=-1, keepdims=True)) + lam_init)
    for st in range(n_st):
        acc = acc_ref[st, :hw, :]
        l = acc_ref[st, hw:hw + 1, :]
        o_t = acc[:, :tq] / l[:, :tq] - lam * (acc[:, tq:] / l[:, tq:])
        o_t = o_t * lax.rsqrt(jnp.mean(o_t * o_t, axis=0, keepdims=True) + NORM_EPS)
        o_ref[st * tq:(st + 1) * tq, :] = (o_t.T * sg_ref[...] * (1.0 - lam_init)).astype(o_ref.dtype)


def _attn(q, k, vt, lam_vecs, sub_g, qg, kg, lam_init, batch, seq, layer):
    t = q.shape[0]
    tk = vt.shape[2]
    tq = ATTN_TQ
    n_st = tk // tq
    nq = seq // tk
    n_kv = seq // tk
    hw = 2 * HEAD_B
    return pl.pallas_call(
        functools.partial(_attn_body, lam_init=lam_init),
        grid=(batch, N_HEADS_B, nq),
        in_specs=[pl.BlockSpec((tk, hw), lambda b, h, i: (b * nq + i, h)),
                  pl.BlockSpec((seq, hw), lambda b, h, i: (b, h)),
                  pl.BlockSpec((n_kv, V_ROWS, tk), lambda b, h, i: (b, h, 0)),
                  _layer_spec((4, HEAD_B), layer), _layer_spec((1, hw), layer),
                  _layer_spec((1, D_B), layer), _layer_spec((1, D_B), layer)],
        out_specs=pl.BlockSpec((tk, hw), lambda b, h, i: (b * nq + i, h)),
        out_shape=jax.ShapeDtypeStruct((t, D_B), BF16),
        scratch_shapes=[pltpu.VMEM((n_st, 2 * hw, 2 * tq), BF16),
                        pltpu.VMEM((3, tk, hw), BF16),
                        pltpu.VMEM((tk, tq), F32),
                        pltpu.VMEM((2, n_st, tk, 2 * tq), F32),
                        pltpu.VMEM((2, n_st, 1, 2 * tq), F32),
                        pltpu.VMEM((2, n_st, tk, 2 * tq), BF16),
                        pltpu.VMEM((2, n_st, 1, 2 * tq), F32),
                        pltpu.VMEM((n_st, 1, 2 * tq), F32),
                        pltpu.VMEM((n_st, V_ROWS, 2 * tq), F32)],
        compiler_params=_params(3),
        name="diff_attn",
    )(q, k, vt, lam_vecs, sub_g, qg, kg)


def _merge_body(yf_ref, yb_ref, gate_ref, bonus_ref, ob_ref, gt_ref, x_ref, lng_ref, lnb_ref, bd_ref,
                wb0_ref, wb1_ref, wo_ref, o_ref):
    bd = bd_ref[...]
    y = yf_ref[...] + yb_ref[...]
    yc = y - _mm_split_lhs(y, bd)
    var = _mm(yc * yc, bd)
    yn = yc * lax.rsqrt(var + GN_EPS) * lng_ref[...] + lnb_ref[...]
    oa = (yn + bonus_ref[...]) * gate_ref[...]
    gts = gt_ref[...].astype(F32)
    merged = (gts[:, :D_MODEL] * _mm(oa, wb0_ref[...])
              + gts[:, D_MODEL:] * jnp.dot(ob_ref[...], wb1_ref[...], preferred_element_type=F32))
    o_ref[...] = x_ref[...] + _mm(merged, wo_ref[...])


def _merge(yf, yb, gate, bonus, ob, gts, x2, ln_g, ln_b, bd_mean, wb, wo, layer):
    t = x2.shape[0]
    tm = min(TOKEN_TILE, t)
    row = lambda width: pl.BlockSpec((tm, width), lambda i: (i, 0))
    return pl.pallas_call(
        _merge_body,
        grid=(t // tm,),
        in_specs=[row(D_A), row(D_A), row(D_A), row(D_A), row(D_B), row(2 * D_MODEL), row(D_MODEL),
                  _layer_spec((1, D_A), layer), _layer_spec((1, D_A), layer), _const_spec((D_A, D_A)),
                  _layer_spec((D_A, D_MODEL), layer, 0), _layer_spec((D_B, D_MODEL), layer, 1),
                  _layer_spec((D_MODEL, D_MODEL), layer)],
        out_specs=row(D_MODEL),
        out_shape=jax.ShapeDtypeStruct((t, D_MODEL), F32),
        compiler_params=_params(1),
        name="merge",
    )(yf, yb, gate, bonus, ob, gts, x2, ln_g, ln_b, bd_mean, wb, wb, wo)


def _block_diag2(w):
    z = jnp.zeros_like(w[:, 0])
    return jnp.concatenate([jnp.concatenate([w[:, 0], z], axis=2), jnp.concatenate([z, w[:, 1]], axis=2)],
                           axis=1)


def kernel(x, norm_ffn1, ffn1_in, ffn1_out, norm_mix, w_in, rwkv_mu, decay_w0, decay_w2, iclr_a0, iclr_a2,
           gate_g2, k_k, k_a, r_k, ln_x_g, ln_x_b, q_gain, k_gain, diff_lambda, subln_g, w_branch, w_out,
           norm_ffn2, ffn2_in, ffn2_out):
    batch, seq, d = x.shape
    depth = norm_ffn1.shape[0]
    assert seq % TOKEN_TILE == 0 and d == D_MODEL
    x2 = x.reshape(batch * seq, d)
    head_id = jnp.arange(D_A) // HEAD_A
    same = head_id[:, None] == head_id[None, :]
    bd_sum = same.astype(BF16)
    bd_mean = (same.astype(F32) / HEAD_A).astype(BF16)
    rows = lambda a: a.reshape(depth, 1, -1)
    bf = lambda a: a.astype(BF16)
    n_qk = D_B // HEAD_B
    qg = rows(jnp.tile(q_gain, (1, n_qk))) * (HEAD_B ** -0.5 * LOG2E)
    kg = rows(jnp.tile(k_gain, (1, n_qk)))
    ffn1 = (rows(norm_ffn1), bf(ffn1_in), bf(ffn1_out))
    ffn2 = (rows(norm_ffn2), bf(ffn2_in), bf(ffn2_out))
    proj_w = (rows(norm_mix), bf(w_in))
    prep_w = (rows(rwkv_mu), rows(decay_w0), bf(_block_diag2(decay_w2)), rows(iclr_a0),
              bf(_block_diag2(iclr_a2)), bf(gate_g2), rows(k_k), rows(k_a), rows(r_k))
    merge_w = (rows(ln_x_g), rows(ln_x_b), bd_mean, bf(w_branch), bf(w_out))
    sub_g = rows(subln_g)
    for l in range(depth):
        lam_init = 0.8 - 0.6 * math.exp(-0.3 * l)
        x2 = _ffn(x2, *ffn1, l)
        q, k, vt, gts, r, vv, kk, lw0, k0, a0, lw1, k1, a1, gate, bonus = _proj(
            x2, seq, *proj_w, qg, kg, bd_mean, *prep_w, bd_sum, l)
        yf, yb = _scan(r, vv, kk, lw0, k0, a0, lw1, k1, a1, batch, seq)
        ob = _attn(q, k, vt, diff_lambda, sub_g, qg, kg, lam_init, batch, seq, l)
        x2 = _merge(yf, yb, gate, bonus, ob, gts, x2, *merge_w, l)
        x2 = _ffn(x2, *ffn2, l)
    return x2.reshape(batch, seq, d)
```

```python
import functools
import math

import jax
import jax.numpy as jnp
from jax import lax
from jax.experimental import pallas as pl
from jax.experimental.pallas import tpu as pltpu

F32 = jnp.float32
BF16 = jnp.bfloat16

D_MODEL = 1024
D_A = 512
HEAD_A = 64
LORA = 64
LORA_GATE = 128
C_RWKV = 3 * D_A + 2 * LORA + 2 * LORA + LORA_GATE
D_B = 512
HEAD_B = 64
N_HEADS_B = D_B // (2 * HEAD_B)
D_FF = 2816
DECAY_SCALE = math.exp(-0.5)
GN_EPS = 64e-5
NORM_EPS = 1e-6
LOG2E = math.log2(math.e)
F32_ZERO_EXP2 = 150.0
ROUNDING_MARGIN = 1.02

LANES = 128
SUBLANES = 8
VMEM_LIMIT = 56 * 1024 * 1024

TOKEN_TILE = 512
MXU_WIDTH = 256
FF_SPLITS = (0, 6 * MXU_WIDTH, D_FF)
CHUNK = 64
SCAN_CHUNKS_PER_STEP = 2
ATTN_TQ = 256
V_ROWS = 2 * HEAD_B + 16

_NT = (((1,), (1,)), ((), ()))


def _mm(a, b):
    return jnp.dot(a.astype(BF16), b.astype(BF16), preferred_element_type=F32)


def _mm_nt(a, b):
    return lax.dot_general(a.astype(BF16), b.astype(BF16), _NT, preferred_element_type=F32)


def _split_bf16(x, n):
    parts = []
    for _ in range(n - 1):
        hi = x.astype(BF16)
        parts.append(hi)
        x = x - hi.astype(F32)
    parts.append(x.astype(BF16))
    return parts


def _mm_split_lhs(x, w, n=2):
    return sum(jnp.dot(p, w, preferred_element_type=F32) for p in _split_bf16(x, n))


def _rms_norm(x, g):
    return x * lax.rsqrt(jnp.mean(x * x, axis=-1, keepdims=True) + NORM_EPS) * g


def _const_spec(shape):
    return pl.BlockSpec(shape, lambda *_: (0,) * len(shape), pipeline_mode=pl.Buffered(1))


def _layer_spec(shape, layer, *lead):
    index = (layer,) + lead + (0,) * len(shape)
    return pl.BlockSpec((None,) * (1 + len(lead)) + shape, lambda *_: index, pipeline_mode=pl.Buffered(1))


def _params(n_axes):
    return pltpu.CompilerParams(dimension_semantics=("arbitrary",) * n_axes,
                                vmem_limit_bytes=VMEM_LIMIT)


def _ffn_math(x, g_ref, win_ref, wout_ref):
    h = _rms_norm(x, g_ref[...]).astype(BF16)
    acc = jnp.zeros(x.shape, F32)
    for lo, hi in zip(FF_SPLITS[:-1], FF_SPLITS[1:]):
        gate = jnp.dot(h, win_ref[:, lo:hi], preferred_element_type=F32)
        up = jnp.dot(h, win_ref[:, D_FF + lo:D_FF + hi], preferred_element_type=F32)
        act = (gate * jax.nn.sigmoid(gate) * up).astype(BF16)
        acc = acc + jnp.dot(act, wout_ref[lo:hi, :], preferred_element_type=F32)
    return x + 0.5 * acc


def _ffn_body(x_ref, g_ref, win_ref, wout_ref, o_ref):
    o_ref[...] = _ffn_math(x_ref[...], g_ref, win_ref, wout_ref)


def _ffn(x2, g, w_in, w_out, layer):
    t = x2.shape[0]
    tm = min(TOKEN_TILE, t)
    return pl.pallas_call(
        _ffn_body,
        grid=(t // tm,),
        in_specs=[pl.BlockSpec((tm, D_MODEL), lambda i: (i, 0)),
                  _layer_spec((1, D_MODEL), layer),
                  _layer_spec((D_MODEL, 2 * D_FF), layer),
                  _layer_spec((D_FF, D_MODEL), layer)],
        out_specs=pl.BlockSpec((tm, D_MODEL), lambda i: (i, 0)),
        out_shape=jax.ShapeDtypeStruct((t, D_MODEL), F32),
        compiler_params=_params(1),
        name="ffn",
    )(x2, g, w_in, w_out)


_Q0 = C_RWKV
_K0 = C_RWKV + D_B
_V0 = C_RWKV + 2 * D_B
_G0 = C_RWKV + 3 * D_B


def _proj_body(x_ref, xp_ref, xn_ref, g_ref, w_ref, qg_ref, kg_ref, bdm_ref, mu_ref, w0_ref, w2_ref, a0_ref,
               a2_ref, g2_ref, kk_ref, ka_ref, rk_ref, bds_ref,
               q_ref, k_ref, vt_ref, gt_ref, r_out, v_out, kk_out, lw0_out, k0_out, a0_out, lw1_out, k1_out,
               a1_out, gate_out, bonus_out, *, seq):
    i = pl.program_id(0)
    g = g_ref[...]
    h = _rms_norm(x_ref[...], g).astype(BF16)
    tm = h.shape[0]

    bd = bdm_ref[...]
    q = jnp.dot(h, w_ref[:, _Q0:_K0], preferred_element_type=F32)
    q_ref[...] = (q * lax.rsqrt(_mm(q * q, bd) + NORM_EPS) * qg_ref[...]).astype(BF16)
    k = jnp.dot(h, w_ref[:, _K0:_V0], preferred_element_type=F32)
    k_ref[...] = (k * lax.rsqrt(_mm(k * k, bd) + NORM_EPS) * kg_ref[...]).astype(BF16)
    vt = jnp.dot(h, w_ref[:, _V0:_G0], preferred_element_type=F32).T.astype(BF16)
    hw = 2 * HEAD_B
    ones = jnp.ones((V_ROWS - hw, vt.shape[1]), BF16)
    vt_ref[0] = jnp.concatenate([part for hd in range(N_HEADS_B)
                                 for part in (vt[hd * hw:(hd + 1) * hw], ones)], axis=0)
    gt_ref[...] = jax.nn.sigmoid(jnp.dot(h, w_ref[:, _G0:], preferred_element_type=F32)).astype(BF16)

    w_rwkv = w_ref[:, :C_RWKV]
    p = jnp.dot(h, w_rwkv, preferred_element_type=F32)
    edge = jnp.concatenate([xp_ref[...], xn_ref[...]], axis=0)
    p_edge = jnp.dot(_rms_norm(edge, g).astype(BF16), w_rwkv, preferred_element_type=F32)
    keep_prev = jnp.where((i * tm) % seq == 0, 0.0, 1.0)
    keep_next = jnp.where(((i + 1) * tm) % seq == 0, 0.0, 1.0)
    row = lax.broadcasted_iota(jnp.int32, p.shape, 0)
    prev = jnp.where(row == 0, keep_prev * p_edge[SUBLANES - 1:SUBLANES, :], pltpu.roll(p, 1, 0))
    nxt = jnp.where(row == tm - 1, keep_next * p_edge[SUBLANES:SUBLANES + 1, :], pltpu.roll(p, tm - 1, 0))
    p = p + mu_ref[...] * (0.5 * (prev + nxt) - p)

    r = p[:, :D_A]
    k = p[:, D_A:2 * D_A]
    v = p[:, 2 * D_A:3 * D_A]
    dw = p[:, 3 * D_A:3 * D_A + 2 * LORA]
    da = p[:, 3 * D_A + 2 * LORA:3 * D_A + 4 * LORA]
    dg = p[:, 3 * D_A + 4 * LORA:]
    bd = bds_ref[...]

    logw = -DECAY_SCALE * jax.nn.sigmoid(w0_ref[...] + _mm(jnp.tanh(dw), w2_ref[...]))
    iclr = jax.nn.sigmoid(a0_ref[...] + _mm(da, a2_ref[...]))
    gate_out[...] = _mm(jax.nn.sigmoid(dg), g2_ref[...])

    kk = k * kk_ref[...]
    norm = jnp.sqrt(_mm(kk * kk, bd))
    kk_out[...] = (kk / jnp.maximum(norm, 1e-12)).astype(kk_out.dtype)

    ka = ka_ref[...]
    a_f = iclr[:, :D_A]
    a_b = iclr[:, D_A:]
    k_f = k * (1.0 + (a_f - 1.0) * ka)
    k_b = k * (1.0 + (a_b - 1.0) * ka)
    bonus = _mm_split_lhs(r * rk_ref[...] * (k_f + k_b), bd)
    r_out[...] = r.astype(r_out.dtype)
    v_out[...] = v.astype(v_out.dtype)
    lw0_out[...] = logw[:, :D_A]
    k0_out[...] = k_f.astype(k0_out.dtype)
    a0_out[...] = a_f.astype(a0_out.dtype)
    lw1_out[...] = logw[:, D_A:]
    k1_out[...] = k_b.astype(k1_out.dtype)
    a1_out[...] = a_b.astype(a1_out.dtype)
    bonus_out[...] = bonus * v


def _proj(x2, seq, g, w, qg, kg, bd_mean, mu, w0, w2cat, a0, a2cat, g2, k_k, k_a, r_k, bd_sum, layer):
    t = x2.shape[0]
    tm = TOKEN_TILE
    c_in = w.shape[2]
    nb = tm // SUBLANES
    last = t // SUBLANES - 1
    row = lambda width: pl.BlockSpec((tm, width), lambda i: (i, 0))
    wide = jax.ShapeDtypeStruct((t, D_A), F32)
    half = jax.ShapeDtypeStruct((t, D_A), BF16)
    return pl.pallas_call(
        functools.partial(_proj_body, seq=seq),
        grid=(t // tm,),
        in_specs=[row(D_MODEL),
                  pl.BlockSpec((SUBLANES, D_MODEL), lambda i: (jnp.maximum(i * nb - 1, 0), 0)),
                  pl.BlockSpec((SUBLANES, D_MODEL), lambda i: (jnp.minimum((i + 1) * nb, last), 0)),
                  _layer_spec((1, D_MODEL), layer), _layer_spec((D_MODEL, c_in), layer),
                  _layer_spec((1, D_B), layer), _layer_spec((1, D_B), layer), _const_spec((D_B, D_B)),
                  _layer_spec((1, C_RWKV), layer),
                  _layer_spec((1, 2 * D_A), layer), _layer_spec((2 * LORA, 2 * D_A), layer),
                  _layer_spec((1, 2 * D_A), layer), _layer_spec((2 * LORA, 2 * D_A), layer),
                  _layer_spec((LORA_GATE, D_A), layer),
                  _layer_spec((1, D_A), layer), _layer_spec((1, D_A), layer), _layer_spec((1, D_A), layer),
                  _const_spec((D_A, D_A))],
        out_specs=[row(D_B), row(D_B),
                   pl.BlockSpec((1, N_HEADS_B * V_ROWS, tm), lambda i: (i, 0, 0)), row(2 * D_MODEL)]
                  + [row(D_A)] * 11,
        out_shape=[jax.ShapeDtypeStruct((t, D_B), BF16),
                   jax.ShapeDtypeStruct((t, D_B), BF16),
                   jax.ShapeDtypeStruct((t // tm, N_HEADS_B * V_ROWS, tm), BF16),
                   jax.ShapeDtypeStruct((t, 2 * D_MODEL), BF16),
                   half, half, half, wide, half, half, wide, half, half, wide, wide],
        compiler_params=_params(1),
        name="proj",
    )(x2, x2, x2, g, w, qg, kg, bd_mean, mu, w0, w2cat, a0, a2cat, g2, k_k, k_a, r_k, bd_sum)


def _scan_body(rf, vf, kkf, lwf, kf, af, rb, vb, kkb, lwb, kb, ab, yf_ref, yb_ref, h_ref):
    @pl.when(pl.program_id(1) == 0)
    def _():
        h_ref[...] = jnp.zeros(h_ref.shape, F32)

    L = CHUNK
    n_sub = rf.shape[0] // L
    n_pairs = D_A // LANES
    ti = lax.broadcasted_iota(jnp.int32, (L, L), 0)
    tj = lax.broadcasted_iota(jnp.int32, (L, L), 1)
    pi = lax.broadcasted_iota(jnp.int32, (L, LANES), 0)
    lane = lax.broadcasted_iota(jnp.int32, (L, LANES), 1)
    pj = lane & (HEAD_A - 1)
    lane_lo = lane < HEAD_A
    eye = jnp.where(pj == pi, 1.0, 0.0)
    blk = lambda s: (pi // s) == (pj // s)
    row2 = lax.broadcasted_iota(jnp.int32, (2 * L, LANES), 0)
    col2 = lax.broadcasted_iota(jnp.int32, (2 * L, LANES), 1)
    same_head = (row2 // HEAD_A) == (col2 // HEAD_A)
    diag2 = row2 == col2
    zeros_l = jnp.zeros((L, LANES), F32)
    zeros_2l = jnp.zeros((2 * L, LANES), F32)

    def stack(x):
        return jnp.concatenate([jnp.where(lane_lo, x, 0.0), jnp.where(lane_lo, 0.0, x)], axis=0)

    def mmp(x, y):
        return _mm(x, stack(y))

    cat = jnp.concatenate
    chains = []
    for reverse, refs in ((False, (rf, vf, kkf, lwf, kf, af)), (True, (rb, vb, kkb, lwb, kb, ab))):
        tri = ((tj >= ti) if reverse else (tj <= ti)).astype(BF16)
        strict = (pj > pi) if reverse else (pj < pi)
        incl = (pj >= pi) if reverse else (pj <= pi)
        steps = []
        for sub in (range(n_sub - 1, -1, -1) if reverse else range(n_sub)):
            rows = slice(sub * L, (sub + 1) * L)
            r, v, kk, lw, k, a = (ref[rows, :].astype(F32) for ref in refs)
            G = sum(jnp.dot(tri, part, preferred_element_type=F32) for part in _split_bf16(lw, 3))
            g_end = G[0:1] if reverse else G[L - 1:L]
            gam_end = jnp.exp(g_end)
            inv_gam = jnp.exp(-G)
            rel_end = jnp.exp(g_end - G)
            b = kk * a
            full = dict(At=-kk * jnp.exp(G - lw), Bt=b * inv_gam, Kt=k * inv_gam, Rt=r * jnp.exp(G),
                        Kh=k * rel_end, Bh=b * rel_end, v=v)
            pairs = []
            for p in range(n_pairs):
                sl = slice(p * LANES, (p + 1) * LANES)
                c = {name: val[:, sl] for name, val in full.items()}
                c.update(strict=strict, incl=incl, gam_end=gam_end[:, sl], rows=rows)
                pairs.append(c)
            steps.append(pairs)
        chains.append(steps)
    flat = [c for steps in chains for pairs in steps for c in pairs]

    for c in flat:
        a4 = _mm_nt(cat([c["At"], c["Rt"]], axis=0), cat([stack(c["Bt"]), stack(c["Kt"])], axis=0))
        c["A_ab"] = jnp.where(c["strict"], a4[:L, :LANES], 0.0)
        c["A_ak"] = jnp.where(c["strict"], a4[:L, LANES:], 0.0)
        c["A_rb"] = jnp.where(c["incl"], a4[L:, :LANES], 0.0)
        c["A_rk"] = jnp.where(c["incl"], a4[L:, LANES:], 0.0)
    for c in flat:
        c["A0"] = jnp.where(blk(8), c["A_ab"], 0.0)
        c["A2"] = mmp(c["A0"], c["A0"])
    for c in flat:
        t1 = eye + c["A0"]
        both = mmp(cat([c["A2"], t1], axis=0), c["A2"])
        c["A4"] = both[:L]
        c["T"] = t1 + both[L:]
    for c in flat:
        c["T"] = c["T"] + mmp(c["T"], c["A4"])
    s = 8
    while s < L:
        outer = blk(2 * s) & jnp.logical_not(blk(s))
        for c in flat:
            c["TO"] = mmp(c["T"], jnp.where(outer, c["A_ab"], 0.0))
        for c in flat:
            c["T"] = c["T"] + mmp(c["TO"], c["T"])
        s *= 2
    for c in flat:
        c["AkV"] = mmp(c["A_ak"], c["v"])
    for c in flat:
        wu = _mm(c["T"], cat([stack(c["At"]), stack(c["AkV"])], axis=1))
        c["W"], c["U0"] = wu[:, :LANES], wu[:, LANES:]
    for c in flat:
        kb_t = cat([c["Kh"], c["Bh"]], axis=0).T
        mn = _mm(kb_t, cat([cat([zeros_l, c["v"]], axis=1), cat([c["W"], c["U0"]], axis=1)], axis=0))
        c["M"] = jnp.where(same_head, mn[:, :LANES], 0.0) + jnp.where(diag2, c["gam_end"], 0.0)
        c["N"] = jnp.where(same_head, mn[:, LANES:], 0.0)
    for c in flat:
        qy = _mm(cat([c["A_rb"], c["A_rk"]], axis=1),
                 cat([cat([stack(c["W"]), stack(c["U0"])], axis=1),
                      cat([zeros_2l, stack(c["v"])], axis=1)], axis=0))
        c["Q"] = c["Rt"] + qy[:, :LANES]
        c["Y0"] = qy[:, LANES:]
    for d, (steps, y_ref) in enumerate(zip(chains, (yf_ref, yb_ref))):
        for pairs in steps:
            ys = []
            for p, c in enumerate(pairs):
                qm_h = _mm(cat([c["Q"], c["M"]], axis=0), h_ref[d * n_pairs + p])
                ys.append(qm_h[:L] + c["Y0"])
                h_ref[d * n_pairs + p] = qm_h[L:] + c["N"]
            y_ref[pairs[0]["rows"], :] = cat(ys, axis=1)


def _scan(r, v, kk, lw0, k0, a0, lw1, k1, a1, batch, seq):
    t = r.shape[0]
    rows = SCAN_CHUNKS_PER_STEP * CHUNK
    nc = seq // rows
    fwd = pl.BlockSpec((rows, D_A), lambda b, c: (b * nc + c, 0))
    bwd = pl.BlockSpec((rows, D_A), lambda b, c: (b * nc + nc - 1 - c, 0))
    out = jax.ShapeDtypeStruct((t, D_A), F32)
    return pl.pallas_call(
        _scan_body,
        grid=(batch, nc),
        in_specs=[fwd] * 6 + [bwd] * 6,
        out_specs=[fwd, bwd],
        out_shape=[out, out],
        scratch_shapes=[pltpu.VMEM((2 * (D_A // LANES), LANES, LANES), F32)],
        compiler_params=_params(2),
        name="rwkv_scan",
    )(r, v, kk, lw0, k0, a0, r, v, kk, lw1, k1, a1)


def _bias_features(slope, n, width, key_side):
    pos = lax.broadcasted_iota(jnp.int32, (n, width), 0).astype(F32)
    lane = lax.broadcasted_iota(jnp.int32, (n, width), 1)
    hi, mid, lo = (t.astype(F32) for t in _split_bf16(slope * pos, 3))
    terms = jnp.where(lane % 3 == 0, hi, jnp.where(lane % 3 == 1, mid, lo))
    if key_side:
        return jnp.where(lane < 3, -1.0, jnp.where(lane < 6, terms, 0.0))
    return jnp.where(lane < 3, terms, jnp.where(lane < 6, 1.0, 0.0))


def _attn_body(q_ref, k_ref, vt_ref, lv_ref, sg_ref, qg_ref, kg_ref, o_ref, qs_ref, feat_ref, src_ref,
               s_ref, mx_ref, p_ref, al_ref, m_ref, acc_ref, *, lam_init):
    tq = src_ref.shape[1]
    n_kv, _, tk = vt_ref.shape
    seq = n_kv * tk
    h = pl.program_id(1)
    i = pl.program_id(2)
    slope_bits = (127 - 2 * (h + 1)) << 23
    slope = (lax.bitcast_convert_type(jnp.full((1, 1), slope_bits, jnp.int32), jnp.float32).astype(F32)
             * LOG2E)
    bound =(HEAD_B * ROUNDING_MARGIN * jnp.max(jnp.abs(qg_ref[...]), axis=-1, keepdims=True)
             * jnp.max(jnp.abs(kg_ref[...]), axis=-1, keepdims=True))
    reach = jnp.minimum((F32_ZERO_EXP2 + 2.0 * bound) / slope + 1.0, float(seq)).astype(jnp.int32)[0, 0]

    n_st = tk // tq
    hw = 2 * HEAD_B

    @pl.when(i == 0)
    def _():
        kv_pos = lax.broadcasted_iota(jnp.int32, (tk, tq), 0)
        q_pos = lax.broadcasted_iota(jnp.int32, (tk, tq), 1)
        src_ref[...] = slope * (q_pos - kv_pos).astype(F32)
        key_feat = _bias_features(slope, tk, hw, key_side=True)
        feat_ref[0] = key_feat.astype(BF16)
        feat_ref[1] = (-key_feat).astype(BF16)
        feat_ref[2] = jnp.zeros((tk, hw), BF16)
        q_feat = _bias_features(slope, tq, hw, key_side=False).T.astype(BF16)
        for st in range(n_st):
            qs_ref[st, hw:, :tq] = q_feat
            qs_ref[st, hw:, tq:] = q_feat

    chan =lax.broadcasted_iota(jnp.int32, (hw, tq), 0)
    for st in range(n_st):
        q_t = q_ref[st * tq:(st + 1) * tq, :].astype(F32).T.astype(BF16)
        zero = jnp.zeros_like(q_t)
        qs_ref[st, :hw, :tq] = jnp.where(chan < HEAD_B, q_t, zero)
        qs_ref[st, :hw, tq:] = jnp.where(chan < HEAD_B, zero, q_t)
    q0 = i * tk
    j_lo = jnp.maximum(q0 - reach, 0) // tk
    j_hi = jnp.minimum(q0 + tk - 1 + reach, seq - 1) // tk + 1
    n_left = i - j_lo
    n_off = n_left + (j_hi - i - 1)

    def scores(j, side, st):
        kb = k_ref[pl.ds(pl.multiple_of(j * tk, tk), tk), :]
        return jnp.dot(jnp.concatenate([kb, feat_ref[side]], axis=1), qs_ref[st],
                       preferred_element_type=F32)

    def softmax_update(s, s_max, cst, st):
        m_old = m_ref[st]
        m_new = jnp.maximum(m_old, s_max + cst)
        alpha = jnp.exp2(m_old - m_new)
        p = jnp.exp2(s - (m_new - cst))
        m_ref[st] = m_new
        return alpha, p.astype(BF16)

    def block(n):
        j = jnp.clip(j_lo + n + (n >= n_left).astype(jnp.int32), 0, n_kv - 1)
        j = jnp.where(n < 0, i, j)
        return j, (j > i).astype(jnp.int32), jnp.logical_and(n >= 0, n < n_off)

    def qk_stage(n, slot):
        j, side, _ = block(n)
        for st in range(n_st):
            s = scores(j, side, st)
            s_ref[slot, st] = s
            mx_ref[slot, st] = jnp.max(s, axis=0, keepdims=True)

    def softmax_stage(n, slot):
        j, _, valid = block(n)
        for st in range(n_st):
            dist0 = jnp.abs(q0 + st * tq - j * tk).astype(F32)
            cst = jnp.where(valid, -slope * dist0, -1e30)
            alpha, p = softmax_update(s_ref[slot, st], mx_ref[slot, st], cst, st)
            al_ref[slot, st] = alpha
            p_ref[slot, st] = p

    def pv_stage(n, slot):
        j, _, _ = block(n)
        for st in range(n_st):
            acc_ref[st] = al_ref[slot, st] * acc_ref[st] + jnp.dot(vt_ref[j], p_ref[slot, st],
                                                                    preferred_element_type=F32)

    m_ref[...] = jnp.full(m_ref.shape, -1e30, F32)
    acc_ref[...] = jnp.zeros(acc_ref.shape, F32)
    diag_scores = [scores(i, 2, st) for st in range(n_st)]
    qk_stage(0, 0)
    src = src_ref[...]
    for st in range(n_st):
        bias = jnp.abs(src + slope * float(st * tq))
        s = diag_scores[st] - jnp.concatenate([bias, bias], axis=1)
        alpha, p = softmax_update(s, jnp.max(s, axis=0, keepdims=True), jnp.zeros((1, 1), F32), st)
        al_ref[1, st] = alpha
        p_ref[1, st] = p

    def pipelined_pair(t, carry):
        n = 2 * t
        qk_stage(n + 1, 1)
        softmax_stage(n, 0)
        pv_stage(n - 1, 1)
        qk_stage(n + 2, 0)
        softmax_stage(n + 1, 1)
        pv_stage(n, 0)
        return carry

    n_pairs = n_off // 2
    lax.fori_loop(0, n_pairs, pipelined_pair, 0)
    last = 2 * n_pairs

    @pl.when(n_off % 2 == 1)
    def _():
        softmax_stage(last, 0)
        pv_stage(last - 1, 1)
        pv_stage(last, 0)

    @pl.when(n_off % 2 == 0)
    def _():
        pv_stage(last - 1, 1)

    lv = lv_ref[...]
    lam = (jnp.exp(jnp.sum(lv[0:1] * lv[1:2], axis=-1, keepdims=True))
           - jnp.exp(jnp.sum(lv[2:3] * lv[3:4], axis=-1, keepdims=True)) + lam_init)
    for st in range(n_st):
        acc = acc_ref[st, :hw, :]
        l = acc_ref[st, hw:hw + 1, :]
        o_t = acc[:, :tq] / l[:, :tq] - lam * (acc[:, tq:] / l[:, tq:])
        o_t = o_t * lax.rsqrt(jnp.mean(o_t * o_t, axis=0, keepdims=True) + NORM_EPS)
        o_ref[st * tq:(st + 1) * tq, :] = (o_t.T * sg_ref[...] * (1.0 - lam_init)).astype(o_ref.dtype)


def _attn(q, k, vt, lam_vecs, sub_g, qg, kg, lam_init, batch, seq, layer):
    t = q.shape[0]
    tk = vt.shape[2]
    tq = ATTN_TQ
    n_st = tk // tq
    nq = seq // tk
    n_kv = seq // tk
    hw = 2 * HEAD_B
    return pl.pallas_call(
        functools.partial(_attn_body, lam_init=lam_init),
        grid=(batch, N_HEADS_B, nq),
        in_specs=[pl.BlockSpec((tk, hw), lambda b, h, i: (b * nq + i, h)),
                  pl.BlockSpec((seq, hw), lambda b, h, i: (b, h)),
                  pl.BlockSpec((n_kv, V_ROWS, tk), lambda b, h, i: (b, h, 0)),
                  _layer_spec((4, HEAD_B), layer), _layer_spec((1, hw), layer),
                  _layer_spec((1, D_B), layer), _layer_spec((1, D_B), layer)],
        out_specs=pl.BlockSpec((tk, hw), lambda b, h, i: (b * nq + i, h)),
        out_shape=jax.ShapeDtypeStruct((t, D_B), BF16),
        scratch_shapes=[pltpu.VMEM((n_st, 2 * hw, 2 * tq), BF16),
                        pltpu.VMEM((3, tk, hw), BF16),
                        pltpu.VMEM((tk, tq), F32),
                        pltpu.VMEM((2, n_st, tk, 2 * tq), F32),
                        pltpu.VMEM((2, n_st, 1, 2 * tq), F32),
                        pltpu.VMEM((2, n_st, tk, 2 * tq), BF16),
                        pltpu.VMEM((2, n_st, 1, 2 * tq), F32),
                        pltpu.VMEM((n_st, 1, 2 * tq), F32),
                        pltpu.VMEM((n_st, V_ROWS, 2 * tq), F32)],
        compiler_params=_params(3),
        name="diff_attn",
    )(q, k, vt, lam_vecs, sub_g, qg, kg)


def _merge_body(yf_ref, yb_ref, gate_ref, bonus_ref, ob_ref, gt_ref, x_ref, lng_ref, lnb_ref, bd_ref,
                wb0_ref, wb1_ref, wo_ref, g2_ref, win_ref, wout_ref, o_ref):
    bd = bd_ref[...]
    y = yf_ref[...] + yb_ref[...]
    yc = y - _mm_split_lhs(y, bd)
    var = _mm(yc * yc, bd)
    yn = yc * lax.rsqrt(var + GN_EPS) * lng_ref[...] + lnb_ref[...]
    oa = (yn + bonus_ref[...]) * gate_ref[...]
    gts = gt_ref[...].astype(F32)
    merged = (gts[:, :D_MODEL] * _mm(oa, wb0_ref[...])
              + gts[:, D_MODEL:] * jnp.dot(ob_ref[...], wb1_ref[...], preferred_element_type=F32))
    o_ref[...] = _ffn_math(x_ref[...] + _mm(merged, wo_ref[...]), g2_ref, win_ref, wout_ref)


def _merge(yf, yb, gate, bonus, ob, gts, x2, ln_g, ln_b, bd_mean, wb, wo, g2, w_in, w_out, layer):
    t = x2.shape[0]
    tm = min(TOKEN_TILE, t)
    row = lambda width: pl.BlockSpec((tm, width), lambda i: (i, 0))
    return pl.pallas_call(
        _merge_body,
        grid=(t // tm,),
        in_specs=[row(D_A), row(D_A), row(D_A), row(D_A), row(D_B), row(2 * D_MODEL), row(D_MODEL),
                  _layer_spec((1, D_A), layer), _layer_spec((1, D_A), layer), _const_spec((D_A, D_A)),
                  _layer_spec((D_A, D_MODEL), layer, 0), _layer_spec((D_B, D_MODEL), layer, 1),
                  _layer_spec((D_MODEL, D_MODEL), layer),
                  _layer_spec((1, D_MODEL), layer), _layer_spec((D_MODEL, 2 * D_FF), layer),
                  _layer_spec((D_FF, D_MODEL), layer)],
        out_specs=row(D_MODEL),
        out_shape=jax.ShapeDtypeStruct((t, D_MODEL), F32),
        compiler_params=_params(1),
        name="merge",
    )(yf, yb, gate, bonus, ob, gts, x2, ln_g, ln_b, bd_mean, wb, wb, wo, g2, w_in, w_out)


def _block_diag2(w):
    z = jnp.zeros_like(w[:, 0])
    return jnp.concatenate([jnp.concatenate([w[:, 0], z], axis=2), jnp.concatenate([z, w[:, 1]], axis=2)],
                           axis=1)


def kernel(x, norm_ffn1, ffn1_in, ffn1_out, norm_mix, w_in, rwkv_mu, decay_w0, decay_w2, iclr_a0, iclr_a2,
           gate_g2, k_k, k_a, r_k, ln_x_g, ln_x_b, q_gain, k_gain, diff_lambda, subln_g, w_branch, w_out,
           norm_ffn2, ffn2_in, ffn2_out):
    batch, seq, d = x.shape
    depth = norm_ffn1.shape[0]
    assert seq % TOKEN_TILE == 0 and d == D_MODEL
    x2 = x.reshape(batch * seq, d)
    head_id = jnp.arange(D_A) // HEAD_A
    same = head_id[:, None] == head_id[None, :]
    bd_sum = same.astype(BF16)
    bd_mean = (same.astype(F32) / HEAD_A).astype(BF16)
    rows = lambda a: a.reshape(depth, 1, -1)
    bf = lambda a: a.astype(BF16)
    n_qk = D_B // HEAD_B
    qg = rows(jnp.tile(q_gain, (1, n_qk))) * (HEAD_B ** -0.5 * LOG2E)
    kg = rows(jnp.tile(k_gain, (1, n_qk)))
    ffn1 = (rows(norm_ffn1), bf(ffn1_in), bf(ffn1_out))
    ffn2 = (rows(norm_ffn2), bf(ffn2_in), bf(ffn2_out))
    proj_w = (rows(norm_mix), bf(w_in))
    prep_w = (rows(rwkv_mu), rows(decay_w0), bf(_block_diag2(decay_w2)), rows(iclr_a0),
              bf(_block_diag2(iclr_a2)), bf(gate_g2), rows(k_k), rows(k_a), rows(r_k))
    merge_w = (rows(ln_x_g), rows(ln_x_b), bd_mean, bf(w_branch), bf(w_out))
    sub_g = rows(subln_g)
    for l in range(depth):
        lam_init = 0.8 - 0.6 * math.exp(-0.3 * l)
        x2 = _ffn(x2, *ffn1, l)
        q, k, vt, gts, r, vv, kk, lw0, k0, a0, lw1, k1, a1, gate, bonus = _proj(
            x2, seq, *proj_w, qg, kg, bd_mean, *prep_w, bd_sum, l)
        yf, yb = _scan(r, vv, kk, lw0, k0, a0, lw1, k1, a1, batch, seq)
        ob = _attn(q, k, vt, diff_lambda, sub_g, qg, kg, lam_init, batch, seq, l)
        x2 = _merge(yf, yb, gate, bonus, ob, gts, x2, *merge_w, *ffn2, l)
    return x2.reshape(batch, seq, d)
```

```python
import functools
import math

import jax
import jax.numpy as jnp
from jax import lax
from jax.experimental import pallas as pl
from jax.experimental.pallas import tpu as pltpu

F32 = jnp.float32
BF16 = jnp.bfloat16

D_MODEL = 1024
D_A = 512
HEAD_A = 64
LORA = 64
LORA_GATE = 128
C_RWKV = 3 * D_A + 2 * LORA + 2 * LORA + LORA_GATE
D_B = 512
HEAD_B = 64
N_HEADS_B = D_B // (2 * HEAD_B)
D_FF = 2816
DECAY_SCALE = math.exp(-0.5)
GN_EPS = 64e-5
NORM_EPS = 1e-6
LOG2E = math.log2(math.e)
F32_ZERO_EXP2 = 150.0
ROUNDING_MARGIN = 1.02

LANES = 128
SUBLANES = 8
VMEM_LIMIT = 56 * 1024 * 1024

TOKEN_TILE = 512
MXU_WIDTH = 256
FF_SPLITS = (0, 6 * MXU_WIDTH, D_FF)
CHUNK = 64
SCAN_CHUNKS_PER_STEP = 2
ATTN_TQ = 256
V_ROWS = 2 * HEAD_B + 16

_NT = (((1,), (1,)), ((), ()))


def _mm(a, b):
    return jnp.dot(a.astype(BF16), b.astype(BF16), preferred_element_type=F32)


def _mm_nt(a, b):
    return lax.dot_general(a.astype(BF16), b.astype(BF16), _NT, preferred_element_type=F32)


def _split_bf16(x, n):
    parts = []
    for _ in range(n - 1):
        hi = x.astype(BF16)
        parts.append(hi)
        x = x - hi.astype(F32)
    parts.append(x.astype(BF16))
    return parts


def _mm_split_lhs(x, w, n=2):
    return sum(jnp.dot(p, w, preferred_element_type=F32) for p in _split_bf16(x, n))


def _rms_norm(x, g):
    return x * lax.rsqrt(jnp.mean(x * x, axis=-1, keepdims=True) + NORM_EPS) * g


def _const_spec(shape):
    return pl.BlockSpec(shape, lambda *_: (0,) * len(shape), pipeline_mode=pl.Buffered(1))


def _layer_spec(shape, layer, *lead):
    index = (layer,) + lead + (0,) * len(shape)
    return pl.BlockSpec((None,) * (1 + len(lead)) + shape, lambda *_: index, pipeline_mode=pl.Buffered(1))


def _params(n_axes):
    return pltpu.CompilerParams(dimension_semantics=("arbitrary",) * n_axes,
                                vmem_limit_bytes=VMEM_LIMIT)


def _ffn_math(x, g_ref, win_ref, wout_ref):
    h = _rms_norm(x, g_ref[...]).astype(BF16)
    acc = jnp.zeros(x.shape, F32)
    for lo, hi in zip(FF_SPLITS[:-1], FF_SPLITS[1:]):
        gate = jnp.dot(h, win_ref[:, lo:hi], preferred_element_type=F32)
        up = jnp.dot(h, win_ref[:, D_FF + lo:D_FF + hi], preferred_element_type=F32)
        act = (gate * jax.nn.sigmoid(gate) * up).astype(BF16)
        acc = acc + jnp.dot(act, wout_ref[lo:hi, :], preferred_element_type=F32)
    return x + 0.5 * acc


def _ffn_body(x_ref, g_ref, win_ref, wout_ref, o_ref):
    o_ref[...] = _ffn_math(x_ref[...], g_ref, win_ref, wout_ref)


def _ffn(x2, g, w_in, w_out, layer):
    t = x2.shape[0]
    tm = min(TOKEN_TILE, t)
    return pl.pallas_call(
        _ffn_body,
        grid=(t // tm,),
        in_specs=[pl.BlockSpec((tm, D_MODEL), lambda i: (i, 0)),
                  _layer_spec((1, D_MODEL), layer),
                  _layer_spec((D_MODEL, 2 * D_FF), layer),
                  _layer_spec((D_FF, D_MODEL), layer)],
        out_specs=pl.BlockSpec((tm, D_MODEL), lambda i: (i, 0)),
        out_shape=jax.ShapeDtypeStruct((t, D_MODEL), F32),
        compiler_params=_params(1),
        name="ffn",
    )(x2, g, w_in, w_out)


_Q0 = C_RWKV
_K0 = C_RWKV + D_B
_V0 = C_RWKV + 2 * D_B
_G0 = C_RWKV + 3 * D_B


def _proj_body(x_ref, xp_ref, xn_ref, g_ref, w_ref, qg_ref, kg_ref, bdm_ref, mu_ref, w0_ref, w2_ref, a0_ref,
               a2_ref, g2_ref, kk_ref, ka_ref, rk_ref, bds_ref,
               q_ref, k_ref, vt_ref, gt_ref, r_out, v_out, kk_out, lw0_out, k0_out, a0_out, lw1_out, k1_out,
               a1_out, gate_out, bonus_out, *, seq):
    i = pl.program_id(0)
    g = g_ref[...]
    h = _rms_norm(x_ref[...], g).astype(BF16)
    tm = h.shape[0]

    bd = bdm_ref[...]
    q = jnp.dot(h, w_ref[:, _Q0:_K0], preferred_element_type=F32)
    q_ref[...] = (q * lax.rsqrt(_mm(q * q, bd) + NORM_EPS) * qg_ref[...]).astype(BF16)
    k = jnp.dot(h, w_ref[:, _K0:_V0], preferred_element_type=F32)
    k_ref[...] = (k * lax.rsqrt(_mm(k * k, bd) + NORM_EPS) * kg_ref[...]).astype(BF16)
    vt = jnp.dot(h, w_ref[:, _V0:_G0], preferred_element_type=F32).T.astype(BF16)
    hw = 2 * HEAD_B
    ones = jnp.ones((V_ROWS - hw, vt.shape[1]), BF16)
    vt_ref[0] = jnp.concatenate([part for hd in range(N_HEADS_B)
                                 for part in (vt[hd * hw:(hd + 1) * hw], ones)], axis=0)
    gt_ref[...] = jax.nn.sigmoid(jnp.dot(h, w_ref[:, _G0:], preferred_element_type=F32)).astype(BF16)

    w_rwkv = w_ref[:, :C_RWKV]
    p = jnp.dot(h, w_rwkv, preferred_element_type=F32)
    edge = jnp.concatenate([xp_ref[...], xn_ref[...]], axis=0)
    p_edge = jnp.dot(_rms_norm(edge, g).astype(BF16), w_rwkv, preferred_element_type=F32)
    keep_prev = jnp.where((i * tm) % seq == 0, 0.0, 1.0)
    keep_next = jnp.where(((i + 1) * tm) % seq == 0, 0.0, 1.0)
    row = lax.broadcasted_iota(jnp.int32, p.shape, 0)
    prev = jnp.where(row == 0, keep_prev * p_edge[SUBLANES - 1:SUBLANES, :], pltpu.roll(p, 1, 0))
    nxt = jnp.where(row == tm - 1, keep_next * p_edge[SUBLANES:SUBLANES + 1, :], pltpu.roll(p, tm - 1, 0))
    p = p + mu_ref[...] * (0.5 * (prev + nxt) - p)

    r = p[:, :D_A]
    k = p[:, D_A:2 * D_A]
    v = p[:, 2 * D_A:3 * D_A]
    dw = p[:, 3 * D_A:3 * D_A + 2 * LORA]
    da = p[:, 3 * D_A + 2 * LORA:3 * D_A + 4 * LORA]
    dg = p[:, 3 * D_A + 4 * LORA:]
    bd = bds_ref[...]

    logw = -DECAY_SCALE * jax.nn.sigmoid(w0_ref[...] + _mm(jnp.tanh(dw), w2_ref[...]))
    iclr = jax.nn.sigmoid(a0_ref[...] + _mm(da, a2_ref[...]))
    gate_out[...] = _mm(jax.nn.sigmoid(dg), g2_ref[...])

    kk = k * kk_ref[...]
    norm = jnp.sqrt(_mm(kk * kk, bd))
    kk_out[...] = (kk / jnp.maximum(norm, 1e-12)).astype(kk_out.dtype)

    ka = ka_ref[...]
    a_f = iclr[:, :D_A]
    a_b = iclr[:, D_A:]
    k_f = k * (1.0 + (a_f - 1.0) * ka)
    k_b = k * (1.0 + (a_b - 1.0) * ka)
    bonus = _mm_split_lhs(r * rk_ref[...] * (k_f + k_b), bd)
    r_out[...] = r.astype(r_out.dtype)
    v_out[...] = v.astype(v_out.dtype)
    lw0_out[...] = logw[:, :D_A]
    k0_out[...] = k_f.astype(k0_out.dtype)
    a0_out[...] = a_f.astype(a0_out.dtype)
    lw1_out[...] = logw[:, D_A:]
    k1_out[...] = k_b.astype(k1_out.dtype)
    a1_out[...] = a_b.astype(a1_out.dtype)
    bonus_out[...] = bonus * v


def _proj(x2, seq, g, w, qg, kg, bd_mean, mu, w0, w2cat, a0, a2cat, g2, k_k, k_a, r_k, bd_sum, layer):
    t = x2.shape[0]
    tm = TOKEN_TILE
    c_in = w.shape[2]
    nb = tm // SUBLANES
    last = t // SUBLANES - 1
    row = lambda width: pl.BlockSpec((tm, width), lambda i: (i, 0))
    wide = jax.ShapeDtypeStruct((t, D_A), F32)
    half = jax.ShapeDtypeStruct((t, D_A), BF16)
    return pl.pallas_call(
        functools.partial(_proj_body, seq=seq),
        grid=(t // tm,),
        in_specs=[row(D_MODEL),
                  pl.BlockSpec((SUBLANES, D_MODEL), lambda i: (jnp.maximum(i * nb - 1, 0), 0)),
                  pl.BlockSpec((SUBLANES, D_MODEL), lambda i: (jnp.minimum((i + 1) * nb, last), 0)),
                  _layer_spec((1, D_MODEL), layer), _layer_spec((D_MODEL, c_in), layer),
                  _layer_spec((1, D_B), layer), _layer_spec((1, D_B), layer), _const_spec((D_B, D_B)),
                  _layer_spec((1, C_RWKV), layer),
                  _layer_spec((1, 2 * D_A), layer), _layer_spec((2 * LORA, 2 * D_A), layer),
                  _layer_spec((1, 2 * D_A), layer), _layer_spec((2 * LORA, 2 * D_A), layer),
                  _layer_spec((LORA_GATE, D_A), layer),
                  _layer_spec((1, D_A), layer), _layer_spec((1, D_A), layer), _layer_spec((1, D_A), layer),
                  _const_spec((D_A, D_A))],
        out_specs=[row(D_B), row(D_B),
                   pl.BlockSpec((1, N_HEADS_B * V_ROWS, tm), lambda i: (i, 0, 0)), row(2 * D_MODEL)]
                  + [row(D_A)] * 11,
        out_shape=[jax.ShapeDtypeStruct((t, D_B), BF16),
                   jax.ShapeDtypeStruct((t, D_B), BF16),
                   jax.ShapeDtypeStruct((t // tm, N_HEADS_B * V_ROWS, tm), BF16),
                   jax.ShapeDtypeStruct((t, 2 * D_MODEL), BF16),
                   half, half, half, wide, half, half, wide, half, half, wide, wide],
        compiler_params=_params(1),
        name="proj",
    )(x2, x2, x2, g, w, qg, kg, bd_mean, mu, w0, w2cat, a0, a2cat, g2, k_k, k_a, r_k, bd_sum)


def _scan_step(rf, vf, kkf, lwf, kf, af, rb, vb, kkb, lwb, kb, ab, yf_ref, yb_ref, h_ref):
    L = CHUNK
    n_sub = rf.shape[0] // L
    n_pairs = D_A // LANES
    ti = lax.broadcasted_iota(jnp.int32, (L, L), 0)
    tj = lax.broadcasted_iota(jnp.int32, (L, L), 1)
    pi = lax.broadcasted_iota(jnp.int32, (L, LANES), 0)
    lane = lax.broadcasted_iota(jnp.int32, (L, LANES), 1)
    pj = lane & (HEAD_A - 1)
    lane_lo = lane < HEAD_A
    eye = jnp.where(pj == pi, 1.0, 0.0)
    blk = lambda s: (pi // s) == (pj // s)
    row2 = lax.broadcasted_iota(jnp.int32, (2 * L, LANES), 0)
    col2 = lax.broadcasted_iota(jnp.int32, (2 * L, LANES), 1)
    same_head = (row2 // HEAD_A) == (col2 // HEAD_A)
    diag2 = row2 == col2
    zeros_l = jnp.zeros((L, LANES), F32)
    zeros_2l = jnp.zeros((2 * L, LANES), F32)

    def stack(x):
        return jnp.concatenate([jnp.where(lane_lo, x, 0.0), jnp.where(lane_lo, 0.0, x)], axis=0)

    def mmp(x, y):
        return _mm(x, stack(y))

    cat = jnp.concatenate
    chains = []
    for reverse, refs in ((False, (rf, vf, kkf, lwf, kf, af)), (True, (rb, vb, kkb, lwb, kb, ab))):
        tri = ((tj >= ti) if reverse else (tj <= ti)).astype(BF16)
        strict = (pj > pi) if reverse else (pj < pi)
        incl = (pj >= pi) if reverse else (pj <= pi)
        steps = []
        for sub in (range(n_sub - 1, -1, -1) if reverse else range(n_sub)):
            rows = slice(sub * L, (sub + 1) * L)
            r, v, kk, lw, k, a = (ref[rows, :].astype(F32) for ref in refs)
            G = sum(jnp.dot(tri, part, preferred_element_type=F32) for part in _split_bf16(lw, 3))
            g_end = G[0:1] if reverse else G[L - 1:L]
            gam_end = jnp.exp(g_end)
            inv_gam = jnp.exp(-G)
            rel_end = jnp.exp(g_end - G)
            b = kk * a
            full = dict(At=-kk * jnp.exp(G - lw), Bt=b * inv_gam, Kt=k * inv_gam, Rt=r * jnp.exp(G),
                        Kh=k * rel_end, Bh=b * rel_end, v=v)
            pairs = []
            for p in range(n_pairs):
                sl = slice(p * LANES, (p + 1) * LANES)
                c = {name: val[:, sl] for name, val in full.items()}
                c.update(strict=strict, incl=incl, gam_end=gam_end[:, sl], rows=rows)
                pairs.append(c)
            steps.append(pairs)
        chains.append(steps)
    flat = [c for steps in chains for pairs in steps for c in pairs]

    for c in flat:
        a4 = _mm_nt(cat([c["At"], c["Rt"]], axis=0), cat([stack(c["Bt"]), stack(c["Kt"])], axis=0))
        c["A_ab"] = jnp.where(c["strict"], a4[:L, :LANES], 0.0)
        c["A_ak"] = jnp.where(c["strict"], a4[:L, LANES:], 0.0)
        c["A_rb"] = jnp.where(c["incl"], a4[L:, :LANES], 0.0)
        c["A_rk"] = jnp.where(c["incl"], a4[L:, LANES:], 0.0)
    for c in flat:
        c["A0"] = jnp.where(blk(8), c["A_ab"], 0.0)
        c["A2"] = mmp(c["A0"], c["A0"])
    for c in flat:
        t1 = eye + c["A0"]
        both = mmp(cat([c["A2"], t1], axis=0), c["A2"])
        c["A4"] = both[:L]
        c["T"] = t1 + both[L:]
    for c in flat:
        c["T"] = c["T"] + mmp(c["T"], c["A4"])
    s = 8
    while s < L:
        outer = blk(2 * s) & jnp.logical_not(blk(s))
        for c in flat:
            c["TO"] = mmp(c["T"], jnp.where(outer, c["A_ab"], 0.0))
        for c in flat:
            c["T"] = c["T"] + mmp(c["TO"], c["T"])
        s *= 2
    for c in flat:
        c["AkV"] = mmp(c["A_ak"], c["v"])
    for c in flat:
        wu = _mm(c["T"], cat([stack(c["At"]), stack(c["AkV"])], axis=1))
        c["W"], c["U0"] = wu[:, :LANES], wu[:, LANES:]
    for c in flat:
        kb_t = cat([c["Kh"], c["Bh"]], axis=0).T
        mn = _mm(kb_t, cat([cat([zeros_l, c["v"]], axis=1), cat([c["W"], c["U0"]], axis=1)], axis=0))
        c["M"] = jnp.where(same_head, mn[:, :LANES], 0.0) + jnp.where(diag2, c["gam_end"], 0.0)
        c["N"] = jnp.where(same_head, mn[:, LANES:], 0.0)
    for c in flat:
        qy = _mm(cat([c["A_rb"], c["A_rk"]], axis=1),
                 cat([cat([stack(c["W"]), stack(c["U0"])], axis=1),
                      cat([zeros_2l, stack(c["v"])], axis=1)], axis=0))
        c["Q"] = c["Rt"] + qy[:, :LANES]
        c["Y0"] = qy[:, LANES:]
    for d, (steps, y_ref) in enumerate(zip(chains, (yf_ref, yb_ref))):
        for pairs in steps:
            ys = []
            for p, c in enumerate(pairs):
                qm_h = _mm(cat([c["Q"], c["M"]], axis=0), h_ref[d * n_pairs + p])
                ys.append(qm_h[:L] + c["Y0"])
                h_ref[d * n_pairs + p] = qm_h[L:] + c["N"]
            y_ref[pairs[0]["rows"], :] = cat(ys, axis=1)


def _bias_features(slope, n, width, key_side):
    pos = lax.broadcasted_iota(jnp.int32, (n, width), 0).astype(F32)
    lane = lax.broadcasted_iota(jnp.int32, (n, width), 1)
    hi, mid, lo = (t.astype(F32) for t in _split_bf16(slope * pos, 3))
    terms = jnp.where(lane % 3 == 0, hi, jnp.where(lane % 3 == 1, mid, lo))
    if key_side:
        return jnp.where(lane < 3, -1.0, jnp.where(lane < 6, terms, 0.0))
    return jnp.where(lane < 3, terms, jnp.where(lane < 6, 1.0, 0.0))


def _attn_body(q_ref, k_ref, vt_ref, lv_ref, sg_ref, qg_ref, kg_ref, o_ref, qs_ref, feat_ref, src_ref,
               s_ref, mx_ref, p_ref, al_ref, m_ref, acc_ref, *, lam_init, before_loop):
    tq = src_ref.shape[1]
    n_kv, _, tk = vt_ref.shape
    seq = n_kv * tk
    h = pl.program_id(1)
    i = pl.program_id(2)
    slope_bits = (127 - 2 * (h + 1)) << 23
    slope = (lax.bitcast_convert_type(jnp.full((1, 1), slope_bits, jnp.int32), jnp.float32).astype(F32)
             * LOG2E)
    bound =(HEAD_B * ROUNDING_MARGIN * jnp.max(jnp.abs(qg_ref[...]), axis=-1, keepdims=True)
             * jnp.max(jnp.abs(kg_ref[...]), axis=-1, keepdims=True))
    reach = jnp.minimum((F32_ZERO_EXP2 + 2.0 * bound) / slope + 1.0, float(seq)).astype(jnp.int32)[0, 0]

    n_st = tk // tq
    hw = 2 * HEAD_B

    @pl.when(i == 0)
    def _():
        kv_pos = lax.broadcasted_iota(jnp.int32, (tk, tq), 0)
        q_pos = lax.broadcasted_iota(jnp.int32, (tk, tq), 1)
        src_ref[...] = slope * (q_pos - kv_pos).astype(F32)
        key_feat = _bias_features(slope, tk, hw, key_side=True)
        feat_ref[0] = key_feat.astype(BF16)
        feat_ref[1] = (-key_feat).astype(BF16)
        feat_ref[2] = jnp.zeros((tk, hw), BF16)
        q_feat = _bias_features(slope, tq, hw, key_side=False).T.astype(BF16)
        for st in range(n_st):
            qs_ref[st, hw:, :tq] = q_feat
            qs_ref[st, hw:, tq:] = q_feat

    chan =lax.broadcasted_iota(jnp.int32, (hw, tq), 0)
    for st in range(n_st):
        q_t = q_ref[st * tq:(st + 1) * tq, :].astype(F32).T.astype(BF16)
        zero = jnp.zeros_like(q_t)
        qs_ref[st, :hw, :tq] = jnp.where(chan < HEAD_B, q_t, zero)
        qs_ref[st, :hw, tq:] = jnp.where(chan < HEAD_B, zero, q_t)
    q0 = i * tk
    j_lo = jnp.maximum(q0 - reach, 0) // tk
    j_hi = jnp.minimum(q0 + tk - 1 + reach, seq - 1) // tk + 1
    n_left = i - j_lo
    n_off = n_left + (j_hi - i - 1)

    def scores(j, side, st):
        kb = k_ref[pl.ds(pl.multiple_of(j * tk, tk), tk), :]
        return jnp.dot(jnp.concatenate([kb, feat_ref[side]], axis=1), qs_ref[st],
                       preferred_element_type=F32)

    def softmax_update(s, s_max, cst, st):
        m_old = m_ref[st]
        m_new = jnp.maximum(m_old, s_max + cst)
        alpha = jnp.exp2(m_old - m_new)
        p = jnp.exp2(s - (m_new - cst))
        m_ref[st] = m_new
        return alpha, p.astype(BF16)

    def block(n):
        j = jnp.clip(j_lo + n + (n >= n_left).astype(jnp.int32), 0, n_kv - 1)
        j = jnp.where(n < 0, i, j)
        return j, (j > i).astype(jnp.int32), jnp.logical_and(n >= 0, n < n_off)

    def qk_stage(n, slot):
        j, side, _ = block(n)
        for st in range(n_st):
            s = scores(j, side, st)
            s_ref[slot, st] = s
            mx_ref[slot, st] = jnp.max(s, axis=0, keepdims=True)

    def softmax_stage(n, slot):
        j, _, valid = block(n)
        for st in range(n_st):
            dist0 = jnp.abs(q0 + st * tq - j * tk).astype(F32)
            cst = jnp.where(valid, -slope * dist0, -1e30)
            alpha, p = softmax_update(s_ref[slot, st], mx_ref[slot, st], cst, st)
            al_ref[slot, st] = alpha
            p_ref[slot, st] = p

    def pv_stage(n, slot):
        j, _, _ = block(n)
        for st in range(n_st):
            acc_ref[st] = al_ref[slot, st] * acc_ref[st] + jnp.dot(vt_ref[j], p_ref[slot, st],
                                                                    preferred_element_type=F32)

    m_ref[...] = jnp.full(m_ref.shape, -1e30, F32)
    acc_ref[...] = jnp.zeros(acc_ref.shape, F32)
    diag_scores = [scores(i, 2, st) for st in range(n_st)]
    qk_stage(0, 0)
    src = src_ref[...]
    for st in range(n_st):
        bias = jnp.abs(src + slope * float(st * tq))
        s = diag_scores[st] - jnp.concatenate([bias, bias], axis=1)
        alpha, p = softmax_update(s, jnp.max(s, axis=0, keepdims=True), jnp.zeros((1, 1), F32), st)
        al_ref[1, st] = alpha
        p_ref[1, st] = p

    before_loop()

    def pipelined_pair(t, carry):
        n = 2 * t
        qk_stage(n + 1, 1)
        softmax_stage(n, 0)
        pv_stage(n - 1, 1)
        qk_stage(n + 2, 0)
        softmax_stage(n + 1, 1)
        pv_stage(n, 0)
        return carry

    n_pairs = n_off // 2
    lax.fori_loop(0, n_pairs, pipelined_pair, 0)
    last = 2 * n_pairs

    @pl.when(n_off % 2 == 1)
    def _():
        softmax_stage(last, 0)
        pv_stage(last - 1, 1)
        pv_stage(last, 0)

    @pl.when(n_off % 2 == 0)
    def _():
        pv_stage(last - 1, 1)

    lv = lv_ref[...]
    lam = (jnp.exp(jnp.sum(lv[0:1] * lv[1:2], axis=-1, keepdims=True))
           - jnp.exp(jnp.sum(lv[2:3] * lv[3:4], axis=-1, keepdims=True)) + lam_init)
    for st in range(n_st):
        acc = acc_ref[st, :hw, :]
        l = acc_ref[st, hw:hw + 1, :]
        o_t = acc[:, :tq] / l[:, :tq] - lam * (acc[:, tq:] / l[:, tq:])
        o_t = o_t * lax.rsqrt(jnp.mean(o_t * o_t, axis=0, keepdims=True) + NORM_EPS)
        o_ref[st * tq:(st + 1) * tq, :] = (o_t.T * sg_ref[...] * (1.0 - lam_init)).astype(o_ref.dtype)


def _mixers_body(q_ref, k_ref, vt_ref, lv_ref, sg_ref, qg_ref, kg_ref,
                 rf, vf, kkf, lwf, kf, af, rb, vb, kkb, lwb, kb, ab,
                 o_ref, yf_ref, yb_ref, *scratch, lam_init, n_scan):
    *attn_scratch, h_ref = scratch
    step = (pl.program_id(0) * pl.num_programs(1) + pl.program_id(1)) * pl.num_programs(2) + pl.program_id(2)

    @pl.when(step % n_scan == 0)
    def _():
        h_ref[...] = jnp.zeros(h_ref.shape, F32)

    _attn_body(q_ref, k_ref, vt_ref, lv_ref, sg_ref, qg_ref, kg_ref, o_ref, *attn_scratch, lam_init=lam_init,
               before_loop=lambda: _scan_step(rf, vf, kkf, lwf, kf, af, rb, vb, kkb, lwb, kb, ab,
                                              yf_ref, yb_ref, h_ref))


def _mixers(q, k, vt, lam_vecs, sub_g, qg, kg, r, v, kk, lw0, k0, a0, lw1, k1, a1, lam_init, batch, seq, layer):
    t = q.shape[0]
    tk = vt.shape[2]
    tq = ATTN_TQ
    n_st = tk // tq
    nq = seq // tk
    n_kv = seq // tk
    hw = 2 * HEAD_B
    rows = SCAN_CHUNKS_PER_STEP * CHUNK
    n_scan = seq // rows
    assert N_HEADS_B * nq == n_scan
    step = lambda b, h, i: (b * N_HEADS_B + h) * nq + i
    fwd = pl.BlockSpec((rows, D_A), lambda b, h, i: (step(b, h, i), 0))
    bwd = pl.BlockSpec((rows, D_A), lambda b, h, i: (
        (step(b, h, i) // n_scan) * n_scan + n_scan - 1 - step(b, h, i) % n_scan, 0))
    y = jax.ShapeDtypeStruct((t, D_A), F32)
    return pl.pallas_call(
        functools.partial(_mixers_body, lam_init=lam_init, n_scan=n_scan),
        grid=(batch, N_HEADS_B, nq),
        in_specs=[pl.BlockSpec((tk, hw), lambda b, h, i: (b * nq + i, h)),
                  pl.BlockSpec((seq, hw), lambda b, h, i: (b, h)),
                  pl.BlockSpec((n_kv, V_ROWS, tk), lambda b, h, i: (b, h, 0)),
                  _layer_spec((4, HEAD_B), layer), _layer_spec((1, hw), layer),
                  _layer_spec((1, D_B), layer), _layer_spec((1, D_B), layer)] + [fwd] * 6 + [bwd] * 6,
        out_specs=[pl.BlockSpec((tk, hw), lambda b, h, i: (b * nq + i, h)), fwd, bwd],
        out_shape=[jax.ShapeDtypeStruct((t, D_B), BF16), y, y],
        scratch_shapes=[pltpu.VMEM((n_st, 2 * hw, 2 * tq), BF16),
                        pltpu.VMEM((3, tk, hw), BF16),
                        pltpu.VMEM((tk, tq), F32),
                        pltpu.VMEM((2, n_st, tk, 2 * tq), F32),
                        pltpu.VMEM((2, n_st, 1, 2 * tq), F32),
                        pltpu.VMEM((2, n_st, tk, 2 * tq), BF16),
                        pltpu.VMEM((2, n_st, 1, 2 * tq), F32),
                        pltpu.VMEM((n_st, 1, 2 * tq), F32),
                        pltpu.VMEM((n_st, V_ROWS, 2 * tq), F32),
                        pltpu.VMEM((2 * (D_A // LANES), LANES, LANES), F32)],
        compiler_params=_params(3),
        name="mixers",
    )(q, k, vt, lam_vecs, sub_g, qg, kg, r, v, kk, lw0, k0, a0, r, v, kk, lw1, k1, a1)


def _merge_body(yf_ref, yb_ref, gate_ref, bonus_ref, ob_ref, gt_ref, x_ref, lng_ref, lnb_ref, bd_ref,
                wb0_ref, wb1_ref, wo_ref, g2_ref, win_ref, wout_ref, o_ref):
    bd = bd_ref[...]
    y = yf_ref[...] + yb_ref[...]
    yc = y - _mm_split_lhs(y, bd)
    var = _mm(yc * yc, bd)
    yn = yc * lax.rsqrt(var + GN_EPS) * lng_ref[...] + lnb_ref[...]
    oa = (yn + bonus_ref[...]) * gate_ref[...]
    gts = gt_ref[...].astype(F32)
    merged = (gts[:, :D_MODEL] * _mm(oa, wb0_ref[...])
              + gts[:, D_MODEL:] * jnp.dot(ob_ref[...], wb1_ref[...], preferred_element_type=F32))
    o_ref[...] = _ffn_math(x_ref[...] + _mm(merged, wo_ref[...]), g2_ref, win_ref, wout_ref)


def _merge(yf, yb, gate, bonus, ob, gts, x2, ln_g, ln_b, bd_mean, wb, wo, g2, w_in, w_out, layer):
    t = x2.shape[0]
    tm = min(TOKEN_TILE, t)
    row = lambda width: pl.BlockSpec((tm, width), lambda i: (i, 0))
    return pl.pallas_call(
        _merge_body,
        grid=(t // tm,),
        in_specs=[row(D_A), row(D_A), row(D_A), row(D_A), row(D_B), row(2 * D_MODEL), row(D_MODEL),
                  _layer_spec((1, D_A), layer), _layer_spec((1, D_A), layer), _const_spec((D_A, D_A)),
                  _layer_spec((D_A, D_MODEL), layer, 0), _layer_spec((D_B, D_MODEL), layer, 1),
                  _layer_spec((D_MODEL, D_MODEL), layer),
                  _layer_spec((1, D_MODEL), layer), _layer_spec((D_MODEL, 2 * D_FF), layer),
                  _layer_spec((D_FF, D_MODEL), layer)],
        out_specs=row(D_MODEL),
        out_shape=jax.ShapeDtypeStruct((t, D_MODEL), F32),
        compiler_params=_params(1),
        name="merge",
    )(yf, yb, gate, bonus, ob, gts, x2, ln_g, ln_b, bd_mean, wb, wb, wo, g2, w_in, w_out)


def _block_diag2(w):
    z = jnp.zeros_like(w[:, 0])
    return jnp.concatenate([jnp.concatenate([w[:, 0], z], axis=2), jnp.concatenate([z, w[:, 1]], axis=2)],
                           axis=1)


def kernel(x, norm_ffn1, ffn1_in, ffn1_out, norm_mix, w_in, rwkv_mu, decay_w0, decay_w2, iclr_a0, iclr_a2,
           gate_g2, k_k, k_a, r_k, ln_x_g, ln_x_b, q_gain, k_gain, diff_lambda, subln_g, w_branch, w_out,
           norm_ffn2, ffn2_in, ffn2_out):
    batch, seq, d = x.shape
    depth = norm_ffn1.shape[0]
    assert seq % TOKEN_TILE == 0 and d == D_MODEL
    x2 = x.reshape(batch * seq, d)
    head_id = jnp.arange(D_A) // HEAD_A
    same = head_id[:, None] == head_id[None, :]
    bd_sum = same.astype(BF16)
    bd_mean = (same.astype(F32) / HEAD_A).astype(BF16)
    rows = lambda a: a.reshape(depth, 1, -1)
    bf = lambda a: a.astype(BF16)
    n_qk = D_B // HEAD_B
    qg = rows(jnp.tile(q_gain, (1, n_qk))) * (HEAD_B ** -0.5 * LOG2E)
    kg = rows(jnp.tile(k_gain, (1, n_qk)))
    ffn1 = (rows(norm_ffn1), bf(ffn1_in), bf(ffn1_out))
    ffn2 = (rows(norm_ffn2), bf(ffn2_in), bf(ffn2_out))
    proj_w = (rows(norm_mix), bf(w_in))
    prep_w = (rows(rwkv_mu), rows(decay_w0), bf(_block_diag2(decay_w2)), rows(iclr_a0),
              bf(_block_diag2(iclr_a2)), bf(gate_g2), rows(k_k), rows(k_a), rows(r_k))
    merge_w = (rows(ln_x_g), rows(ln_x_b), bd_mean, bf(w_branch), bf(w_out))
    sub_g = rows(subln_g)
    for l in range(depth):
        lam_init = 0.8 - 0.6 * math.exp(-0.3 * l)
        x2 = _ffn(x2, *ffn1, l)
        q, k, vt, gts, r, vv, kk, lw0, k0, a0, lw1, k1, a1, gate, bonus = _proj(
            x2, seq, *proj_w, qg, kg, bd_mean, *prep_w, bd_sum, l)
        ob, yf, yb = _mixers(q, k, vt, diff_lambda, sub_g, qg, kg, r, vv, kk, lw0, k0, a0, lw1, k1, a1,
                             lam_init, batch, seq, l)
        x2 = _merge(yf, yb, gate, bonus, ob, gts, x2, *merge_w, *ffn2, l)
    return x2.reshape(batch, seq, d)
```

```python
import functools
import math

import jax
import jax.numpy as jnp
from jax import lax
from jax.experimental import pallas as pl
from jax.experimental.pallas import tpu as pltpu

F32 = jnp.float32
BF16 = jnp.bfloat16

D_MODEL = 1024
D_A = 512
HEAD_A = 64
LORA = 64
LORA_GATE = 128
C_RWKV = 3 * D_A + 2 * LORA + 2 * LORA + LORA_GATE
D_B = 512
HEAD_B = 64
N_HEADS_B = D_B // (2 * HEAD_B)
D_FF = 2816
DECAY_SCALE = math.exp(-0.5)
GN_EPS = 64e-5
NORM_EPS = 1e-6
LOG2E = math.log2(math.e)
F32_ZERO_EXP2 = 150.0
ROUNDING_MARGIN = 1.02

LANES = 128
SUBLANES = 8
VMEM_LIMIT = 56 * 1024 * 1024

TOKEN_TILE = 512
MXU_WIDTH = 256
FF_SPLITS = (0, 6 * MXU_WIDTH, D_FF)
CHUNK = 64
SCAN_CHUNKS_PER_STEP = 2
ATTN_TQ = 256
V_ROWS = 2 * HEAD_B + 16

_NT = (((1,), (1,)), ((), ()))


def _mm(a, b):
    return jnp.dot(a.astype(BF16), b.astype(BF16), preferred_element_type=F32)


def _mm_nt(a, b):
    return lax.dot_general(a.astype(BF16), b.astype(BF16), _NT, preferred_element_type=F32)


def _split_bf16(x, n):
    parts = []
    for _ in range(n - 1):
        hi = x.astype(BF16)
        parts.append(hi)
        x = x - hi.astype(F32)
    parts.append(x.astype(BF16))
    return parts


def _mm_split_lhs(x, w, n=2):
    return sum(jnp.dot(p, w, preferred_element_type=F32) for p in _split_bf16(x, n))


def _rms_norm(x, g):
    return x * lax.rsqrt(jnp.mean(x * x, axis=-1, keepdims=True) + NORM_EPS) * g


def _const_spec(shape):
    return pl.BlockSpec(shape, lambda *_: (0,) * len(shape), pipeline_mode=pl.Buffered(1))


def _layer_spec(shape, layer, *lead):
    index = (layer,) + lead + (0,) * len(shape)
    return pl.BlockSpec((None,) * (1 + len(lead)) + shape, lambda *_: index, pipeline_mode=pl.Buffered(1))


def _params(n_axes):
    return pltpu.CompilerParams(dimension_semantics=("arbitrary",) * n_axes,
                                vmem_limit_bytes=VMEM_LIMIT)


def _ffn_math(x, g_ref, win_ref, wout_ref):
    h = _rms_norm(x, g_ref[...]).astype(BF16)
    acc = jnp.zeros(x.shape, F32)
    for lo, hi in zip(FF_SPLITS[:-1], FF_SPLITS[1:]):
        gate = jnp.dot(h, win_ref[:, lo:hi], preferred_element_type=F32)
        up = jnp.dot(h, win_ref[:, D_FF + lo:D_FF + hi], preferred_element_type=F32)
        act = (gate * jax.nn.sigmoid(gate) * up).astype(BF16)
        acc = acc + jnp.dot(act, wout_ref[lo:hi, :], preferred_element_type=F32)
    return x + 0.5 * acc


def _ffn_body(x_ref, g_ref, win_ref, wout_ref, o_ref):
    o_ref[...] = _ffn_math(x_ref[...], g_ref, win_ref, wout_ref)


def _ffn(x2, g, w_in, w_out, layer):
    t = x2.shape[0]
    tm = min(TOKEN_TILE, t)
    return pl.pallas_call(
        _ffn_body,
        grid=(t // tm,),
        in_specs=[pl.BlockSpec((tm, D_MODEL), lambda i: (i, 0)),
                  _layer_spec((1, D_MODEL), layer),
                  _layer_spec((D_MODEL, 2 * D_FF), layer),
                  _layer_spec((D_FF, D_MODEL), layer)],
        out_specs=pl.BlockSpec((tm, D_MODEL), lambda i: (i, 0)),
        out_shape=jax.ShapeDtypeStruct((t, D_MODEL), F32),
        compiler_params=_params(1),
        name="ffn",
    )(x2, g, w_in, w_out)


_Q0 = C_RWKV
_K0 = C_RWKV + D_B
_V0 = C_RWKV + 2 * D_B
_G0 = C_RWKV + 3 * D_B


def _proj_body(x_ref, xp_ref, xn_ref, g_ref, w_ref, qg_ref, kg_ref, bdm_ref, mu_ref, w0_ref, w2_ref, a0_ref,
               a2_ref, g2_ref, kk_ref, ka_ref, rk_ref, bds_ref,
               q_ref, k_ref, vt_ref, gt_ref, r_out, v_out, kk_out, lw0_out, k0_out, a0_out, lw1_out, k1_out,
               a1_out, gate_out, bonus_out, *, seq):
    i = pl.program_id(0)
    g = g_ref[...]
    h = _rms_norm(x_ref[...], g).astype(BF16)
    tm = h.shape[0]

    bd = bdm_ref[...]
    q = jnp.dot(h, w_ref[:, _Q0:_K0], preferred_element_type=F32)
    q_ref[...] = (q * lax.rsqrt(_mm(q * q, bd) + NORM_EPS) * qg_ref[...]).astype(BF16)
    k = jnp.dot(h, w_ref[:, _K0:_V0], preferred_element_type=F32)
    k_ref[...] = (k * lax.rsqrt(_mm(k * k, bd) + NORM_EPS) * kg_ref[...]).astype(BF16)
    vt = jnp.dot(h, w_ref[:, _V0:_G0], preferred_element_type=F32).T.astype(BF16)
    hw = 2 * HEAD_B
    ones = jnp.ones((V_ROWS - hw, vt.shape[1]), BF16)
    vt_ref[0] = jnp.concatenate([part for hd in range(N_HEADS_B)
                                 for part in (vt[hd * hw:(hd + 1) * hw], ones)], axis=0)
    gt_ref[...] = jax.nn.sigmoid(jnp.dot(h, w_ref[:, _G0:], preferred_element_type=F32)).astype(BF16)

    w_rwkv = w_ref[:, :C_RWKV]
    p = jnp.dot(h, w_rwkv, preferred_element_type=F32)
    edge = jnp.concatenate([xp_ref[...], xn_ref[...]], axis=0)
    p_edge = jnp.dot(_rms_norm(edge, g).astype(BF16), w_rwkv, preferred_element_type=F32)
    keep_prev = jnp.where((i * tm) % seq == 0, 0.0, 1.0)
    keep_next = jnp.where(((i + 1) * tm) % seq == 0, 0.0, 1.0)
    row = lax.broadcasted_iota(jnp.int32, p.shape, 0)
    prev = jnp.where(row == 0, keep_prev * p_edge[SUBLANES - 1:SUBLANES, :], pltpu.roll(p, 1, 0))
    nxt = jnp.where(row == tm - 1, keep_next * p_edge[SUBLANES:SUBLANES + 1, :], pltpu.roll(p, tm - 1, 0))
    p = p + mu_ref[...] * (0.5 * (prev + nxt) - p)

    r = p[:, :D_A]
    k = p[:, D_A:2 * D_A]
    v = p[:, 2 * D_A:3 * D_A]
    dw = p[:, 3 * D_A:3 * D_A + 2 * LORA]
    da = p[:, 3 * D_A + 2 * LORA:3 * D_A + 4 * LORA]
    dg = p[:, 3 * D_A + 4 * LORA:]
    bd = bds_ref[...]

    logw = -DECAY_SCALE * jax.nn.sigmoid(w0_ref[...] + _mm(jnp.tanh(dw), w2_ref[...]))
    iclr = jax.nn.sigmoid(a0_ref[...] + _mm(da, a2_ref[...]))
    gate_out[...] = _mm(jax.nn.sigmoid(dg), g2_ref[...])

    kk = k * kk_ref[...]
    norm = jnp.sqrt(_mm(kk * kk, bd))
    kk_out[...] = (kk / jnp.maximum(norm, 1e-12)).astype(kk_out.dtype)

    ka = ka_ref[...]
    a_f = iclr[:, :D_A]
    a_b = iclr[:, D_A:]
    k_f = k * (1.0 + (a_f - 1.0) * ka)
    k_b = k * (1.0 + (a_b - 1.0) * ka)
    bonus = _mm_split_lhs(r * rk_ref[...] * (k_f + k_b), bd)
    r_out[...] = r.astype(r_out.dtype)
    v_out[...] = v.astype(v_out.dtype)
    lw0_out[...] = logw[:, :D_A]
    k0_out[...] = k_f.astype(k0_out.dtype)
    a0_out[...] = a_f.astype(a0_out.dtype)
    lw1_out[...] = logw[:, D_A:]
    k1_out[...] = k_b.astype(k1_out.dtype)
    a1_out[...] = a_b.astype(a1_out.dtype)
    bonus_out[...] = bonus * v


def _proj(x2, seq, g, w, qg, kg, bd_mean, mu, w0, w2cat, a0, a2cat, g2, k_k, k_a, r_k, bd_sum, layer):
    t = x2.shape[0]
    tm = TOKEN_TILE
    c_in = w.shape[2]
    nb = tm // SUBLANES
    last = t // SUBLANES - 1
    row = lambda width: pl.BlockSpec((tm, width), lambda i: (i, 0))
    wide = jax.ShapeDtypeStruct((t, D_A), F32)
    half = jax.ShapeDtypeStruct((t, D_A), BF16)
    return pl.pallas_call(
        functools.partial(_proj_body, seq=seq),
        grid=(t // tm,),
        in_specs=[row(D_MODEL),
                  pl.BlockSpec((SUBLANES, D_MODEL), lambda i: (jnp.maximum(i * nb - 1, 0), 0)),
                  pl.BlockSpec((SUBLANES, D_MODEL), lambda i: (jnp.minimum((i + 1) * nb, last), 0)),
                  _layer_spec((1, D_MODEL), layer), _layer_spec((D_MODEL, c_in), layer),
                  _layer_spec((1, D_B), layer), _layer_spec((1, D_B), layer), _const_spec((D_B, D_B)),
                  _layer_spec((1, C_RWKV), layer),
                  _layer_spec((1, 2 * D_A), layer), _layer_spec((2 * LORA, 2 * D_A), layer),
                  _layer_spec((1, 2 * D_A), layer), _layer_spec((2 * LORA, 2 * D_A), layer),
                  _layer_spec((LORA_GATE, D_A), layer),
                  _layer_spec((1, D_A), layer), _layer_spec((1, D_A), layer), _layer_spec((1, D_A), layer),
                  _const_spec((D_A, D_A))],
        out_specs=[row(D_B), row(D_B),
                   pl.BlockSpec((1, N_HEADS_B * V_ROWS, tm), lambda i: (i, 0, 0)), row(2 * D_MODEL)]
                  + [row(D_A)] * 11,
        out_shape=[jax.ShapeDtypeStruct((t, D_B), BF16),
                   jax.ShapeDtypeStruct((t, D_B), BF16),
                   jax.ShapeDtypeStruct((t // tm, N_HEADS_B * V_ROWS, tm), BF16),
                   jax.ShapeDtypeStruct((t, 2 * D_MODEL), BF16),
                   half, half, half, wide, half, half, wide, half, half, wide, wide],
        compiler_params=_params(1),
        name="proj",
    )(x2, x2, x2, g, w, qg, kg, bd_mean, mu, w0, w2cat, a0, a2cat, g2, k_k, k_a, r_k, bd_sum)


def _scan_step(rf, vf, kkf, lwf, kf, af, rb, vb, kkb, lwb, kb, ab, yf_ref, yb_ref, h_ref):
    L = CHUNK
    n_sub = rf.shape[0] // L
    n_pairs = D_A // LANES
    ti = lax.broadcasted_iota(jnp.int32, (L, L), 0)
    tj = lax.broadcasted_iota(jnp.int32, (L, L), 1)
    pi = lax.broadcasted_iota(jnp.int32, (L, LANES), 0)
    lane = lax.broadcasted_iota(jnp.int32, (L, LANES), 1)
    pj = lane & (HEAD_A - 1)
    lane_lo = lane < HEAD_A
    eye = jnp.where(pj == pi, 1.0, 0.0)
    blk = lambda s: (pi // s) == (pj // s)
    row2 = lax.broadcasted_iota(jnp.int32, (2 * L, LANES), 0)
    col2 = lax.broadcasted_iota(jnp.int32, (2 * L, LANES), 1)
    same_head = (row2 // HEAD_A) == (col2 // HEAD_A)
    diag2 = row2 == col2
    zeros_l = jnp.zeros((L, LANES), F32)
    zeros_2l = jnp.zeros((2 * L, LANES), F32)

    def stack(x):
        return jnp.concatenate([jnp.where(lane_lo, x, 0.0), jnp.where(lane_lo, 0.0, x)], axis=0)

    def mmp(x, y):
        return _mm(x, stack(y))

    cat = jnp.concatenate
    chains = []
    for reverse, refs in ((False, (rf, vf, kkf, lwf, kf, af)), (True, (rb, vb, kkb, lwb, kb, ab))):
        tri = ((tj >= ti) if reverse else (tj <= ti)).astype(BF16)
        strict = (pj > pi) if reverse else (pj < pi)
        incl = (pj >= pi) if reverse else (pj <= pi)
        steps = []
        for sub in (range(n_sub - 1, -1, -1) if reverse else range(n_sub)):
            rows = slice(sub * L, (sub + 1) * L)
            r, v, kk, lw, k, a = (ref[rows, :].astype(F32) for ref in refs)
            G = sum(jnp.dot(tri, part, preferred_element_type=F32) for part in _split_bf16(lw, 3))
            g_end = G[0:1] if reverse else G[L - 1:L]
            gam_end = jnp.exp(g_end)
            inv_gam = jnp.exp(-G)
            rel_end = jnp.exp(g_end - G)
            b = kk * a
            full = dict(At=-kk * jnp.exp(G - lw), Bt=b * inv_gam, Kt=k * inv_gam, Rt=r * jnp.exp(G),
                        Kh=k * rel_end, Bh=b * rel_end, v=v)
            pairs = []
            for p in range(n_pairs):
                sl = slice(p * LANES, (p + 1) * LANES)
                c = {name: val[:, sl] for name, val in full.items()}
                c.update(strict=strict, incl=incl, gam_end=gam_end[:, sl], rows=rows)
                pairs.append(c)
            steps.append(pairs)
        chains.append(steps)
    flat = [c for steps in chains for pairs in steps for c in pairs]

    for c in flat:
        a4 = _mm_nt(cat([c["At"], c["Rt"]], axis=0), cat([stack(c["Bt"]), stack(c["Kt"])], axis=0))
        c["A_ab"] = jnp.where(c["strict"], a4[:L, :LANES], 0.0)
        c["A_ak"] = jnp.where(c["strict"], a4[:L, LANES:], 0.0)
        c["A_rb"] = jnp.where(c["incl"], a4[L:, :LANES], 0.0)
        c["A_rk"] = jnp.where(c["incl"], a4[L:, LANES:], 0.0)
    for c in flat:
        c["A0"] = jnp.where(blk(8), c["A_ab"], 0.0)
        c["A2"] = mmp(c["A0"], c["A0"])
    for c in flat:
        t1 = eye + c["A0"]
        both = mmp(cat([c["A2"], t1], axis=0), c["A2"])
        c["A4"] = both[:L]
        c["T"] = t1 + both[L:]
    for c in flat:
        c["T"] = c["T"] + mmp(c["T"], c["A4"])
    s = 8
    while s < L:
        outer = blk(2 * s) & jnp.logical_not(blk(s))
        for c in flat:
            c["TO"] = mmp(c["T"], jnp.where(outer, c["A_ab"], 0.0))
        for c in flat:
            c["T"] = c["T"] + mmp(c["TO"], c["T"])
        s *= 2
    for c in flat:
        c["AkV"] = mmp(c["A_ak"], c["v"])
    for c in flat:
        wu = _mm(c["T"], cat([stack(c["At"]), stack(c["AkV"])], axis=1))
        c["W"], c["U0"] = wu[:, :LANES], wu[:, LANES:]
    for c in flat:
        kb_t = cat([c["Kh"], c["Bh"]], axis=0).T
        mn = _mm(kb_t, cat([cat([zeros_l, c["v"]], axis=1), cat([c["W"], c["U0"]], axis=1)], axis=0))
        c["M"] = jnp.where(same_head, mn[:, :LANES], 0.0) + jnp.where(diag2, c["gam_end"], 0.0)
        c["N"] = jnp.where(same_head, mn[:, LANES:], 0.0)
    for c in flat:
        qy = _mm(cat([c["A_rb"], c["A_rk"]], axis=1),
                 cat([cat([stack(c["W"]), stack(c["U0"])], axis=1),
                      cat([zeros_2l, stack(c["v"])], axis=1)], axis=0))
        c["Q"] = c["Rt"] + qy[:, :LANES]
        c["Y0"] = qy[:, LANES:]
    for d, (steps, y_ref) in enumerate(zip(chains, (yf_ref, yb_ref))):
        for pairs in steps:
            ys = []
            for p, c in enumerate(pairs):
                qm_h = _mm(cat([c["Q"], c["M"]], axis=0), h_ref[d * n_pairs + p])
                ys.append(qm_h[:L] + c["Y0"])
                h_ref[d * n_pairs + p] = qm_h[L:] + c["N"]
            y_ref[pairs[0]["rows"], :] = cat(ys, axis=1)


def _bias_features(slope, n, width, key_side):
    pos = lax.broadcasted_iota(jnp.int32, (n, width), 0).astype(F32)
    lane = lax.broadcasted_iota(jnp.int32, (n, width), 1)
    hi, mid, lo = (t.astype(F32) for t in _split_bf16(slope * pos, 3))
    terms = jnp.where(lane % 3 == 0, hi, jnp.where(lane % 3 == 1, mid, lo))
    if key_side:
        return jnp.where(lane < 3, -1.0, jnp.where(lane < 6, terms, 0.0))
    return jnp.where(lane < 3, terms, jnp.where(lane < 6, 1.0, 0.0))


def _attn_body(q_ref, k_ref, vt_ref, lv_ref, sg_ref, qg_ref, kg_ref, o_ref, qs_ref, feat_ref, src_ref,
               s_ref, mx_ref, p_ref, al_ref, m_ref, acc_ref, *, lam_init):
    tq = src_ref.shape[1]
    n_kv, _, tk = vt_ref.shape
    seq = n_kv * tk
    h = pl.program_id(1)
    i = pl.program_id(2)
    slope_bits = (127 - 2 * (h + 1)) << 23
    slope = (lax.bitcast_convert_type(jnp.full((1, 1), slope_bits, jnp.int32), jnp.float32).astype(F32)
             * LOG2E)
    bound =(HEAD_B * ROUNDING_MARGIN * jnp.max(jnp.abs(qg_ref[...]), axis=-1, keepdims=True)
             * jnp.max(jnp.abs(kg_ref[...]), axis=-1, keepdims=True))
    reach = jnp.minimum((F32_ZERO_EXP2 + 2.0 * bound) / slope + 1.0, float(seq)).astype(jnp.int32)[0, 0]

    n_st = tk // tq
    hw = 2 * HEAD_B

    @pl.when(i == 0)
    def _():
        kv_pos = lax.broadcasted_iota(jnp.int32, (tk, tq), 0)
        q_pos = lax.broadcasted_iota(jnp.int32, (tk, tq), 1)
        src_ref[...] = slope * (q_pos - kv_pos).astype(F32)
        key_feat = _bias_features(slope, tk, hw, key_side=True)
        feat_ref[0] = key_feat.astype(BF16)
        feat_ref[1] = (-key_feat).astype(BF16)
        feat_ref[2] = jnp.zeros((tk, hw), BF16)
        q_feat = _bias_features(slope, tq, hw, key_side=False).T.astype(BF16)
        for st in range(n_st):
            qs_ref[st, hw:, :tq] = q_feat
            qs_ref[st, hw:, tq:] = q_feat

    chan =lax.broadcasted_iota(jnp.int32, (hw, tq), 0)
    for st in range(n_st):
        q_t = q_ref[st * tq:(st + 1) * tq, :].astype(F32).T.astype(BF16)
        zero = jnp.zeros_like(q_t)
        qs_ref[st, :hw, :tq] = jnp.where(chan < HEAD_B, q_t, zero)
        qs_ref[st, :hw, tq:] = jnp.where(chan < HEAD_B, zero, q_t)
    q0 = i * tk
    j_lo = jnp.maximum(q0 - reach, 0) // tk
    j_hi = jnp.minimum(q0 + tk - 1 + reach, seq - 1) // tk + 1
    n_left = i - j_lo
    n_off = n_left + (j_hi - i - 1)

    def scores(j, side, st):
        kb = k_ref[pl.ds(pl.multiple_of(j * tk, tk), tk), :]
        return jnp.dot(jnp.concatenate([kb, feat_ref[side]], axis=1), qs_ref[st],
                       preferred_element_type=F32)

    def softmax_update(s, s_max, cst, st):
        m_old = m_ref[st]
        m_new = jnp.maximum(m_old, s_max + cst)
        alpha = jnp.exp2(m_old - m_new)
        p = jnp.exp2(s - (m_new - cst))
        m_ref[st] = m_new
        return alpha, p.astype(BF16)

    def block(n):
        j = jnp.clip(j_lo + n + (n >= n_left).astype(jnp.int32), 0, n_kv - 1)
        j = jnp.where(n < 0, i, j)
        return j, (j > i).astype(jnp.int32), jnp.logical_and(n >= 0, n < n_off)

    def qk_stage(n, slot):
        j, side, _ = block(n)
        for st in range(n_st):
            s = scores(j, side, st)
            s_ref[slot, st] = s
            mx_ref[slot, st] = jnp.max(s, axis=0, keepdims=True)

    def softmax_stage(n, slot):
        j, _, valid = block(n)
        for st in range(n_st):
            dist0 = jnp.abs(q0 + st * tq - j * tk).astype(F32)
            cst = jnp.where(valid, -slope * dist0, -1e30)
            alpha, p = softmax_update(s_ref[slot, st], mx_ref[slot, st], cst, st)
            al_ref[slot, st] = alpha
            p_ref[slot, st] = p

    def pv_stage(n, slot):
        j, _, _ = block(n)
        for st in range(n_st):
            acc_ref[st] = al_ref[slot, st] * acc_ref[st] + jnp.dot(vt_ref[j], p_ref[slot, st],
                                                                    preferred_element_type=F32)

    m_ref[...] = jnp.full(m_ref.shape, -1e30, F32)
    acc_ref[...] = jnp.zeros(acc_ref.shape, F32)
    diag_scores = [scores(i, 2, st) for st in range(n_st)]
    qk_stage(0, 0)
    src = src_ref[...]
    for st in range(n_st):
        bias = jnp.abs(src + slope * float(st * tq))
        s = diag_scores[st] - jnp.concatenate([bias, bias], axis=1)
        alpha, p = softmax_update(s, jnp.max(s, axis=0, keepdims=True), jnp.zeros((1, 1), F32), st)
        al_ref[1, st] = alpha
        p_ref[1, st] = p

    def pipelined_pair(t, carry):
        n = 2 * t
        qk_stage(n + 1, 1)
        softmax_stage(n, 0)
        pv_stage(n - 1, 1)
        qk_stage(n + 2, 0)
        softmax_stage(n + 1, 1)
        pv_stage(n, 0)
        return carry

    n_pairs = n_off // 2
    lax.fori_loop(0, n_pairs, pipelined_pair, 0)
    last = 2 * n_pairs

    @pl.when(n_off % 2 == 1)
    def _():
        softmax_stage(last, 0)
        pv_stage(last - 1, 1)
        pv_stage(last, 0)

    @pl.when(n_off % 2 == 0)
    def _():
        pv_stage(last - 1, 1)

    lv = lv_ref[...]
    lam = (jnp.exp(jnp.sum(lv[0:1] * lv[1:2], axis=-1, keepdims=True))
           - jnp.exp(jnp.sum(lv[2:3] * lv[3:4], axis=-1, keepdims=True)) + lam_init)
    for st in range(n_st):
        acc = acc_ref[st, :hw, :]
        l = acc_ref[st, hw:hw + 1, :]
        o_t = acc[:, :tq] / l[:, :tq] - lam * (acc[:, tq:] / l[:, tq:])
        o_t = o_t * lax.rsqrt(jnp.mean(o_t * o_t, axis=0, keepdims=True) + NORM_EPS)
        o_ref[st * tq:(st + 1) * tq, :] = (o_t.T * sg_ref[...] * (1.0 - lam_init)).astype(o_ref.dtype)


def _mixers_body(q_ref, k_ref, vt_ref, lv_ref, sg_ref, qg_ref, kg_ref,
                 rf, vf, kkf, lwf, kf, af, rb, vb, kkb, lwb, kb, ab,
                 o_ref, yf_ref, yb_ref, *scratch, lam_init, n_scan):
    *attn_scratch, h_ref = scratch
    step = (pl.program_id(0) * pl.num_programs(1) + pl.program_id(1)) * pl.num_programs(2) + pl.program_id(2)

    @pl.when(step % n_scan == 0)
    def _():
        h_ref[...] = jnp.zeros(h_ref.shape, F32)

    _scan_step(rf, vf, kkf, lwf, kf, af, rb, vb, kkb, lwb, kb, ab, yf_ref, yb_ref, h_ref)
    _attn_body(q_ref, k_ref, vt_ref, lv_ref, sg_ref, qg_ref, kg_ref, o_ref, *attn_scratch, lam_init=lam_init)


def _mixers(q, k, vt, lam_vecs, sub_g, qg, kg, r, v, kk, lw0, k0, a0, lw1, k1, a1, lam_init, batch, seq, layer):
    t = q.shape[0]
    tk = vt.shape[2]
    tq = ATTN_TQ
    n_st = tk // tq
    nq = seq // tk
    n_kv = seq // tk
    hw = 2 * HEAD_B
    rows = SCAN_CHUNKS_PER_STEP * CHUNK
    n_scan = seq // rows
    assert N_HEADS_B * nq == n_scan
    step = lambda b, h, i: (b * N_HEADS_B + h) * nq + i
    fwd = pl.BlockSpec((rows, D_A), lambda b, h, i: (step(b, h, i), 0))
    bwd = pl.BlockSpec((rows, D_A), lambda b, h, i: (
        (step(b, h, i) // n_scan) * n_scan + n_scan - 1 - step(b, h, i) % n_scan, 0))
    y = jax.ShapeDtypeStruct((t, D_A), F32)
    return pl.pallas_call(
        functools.partial(_mixers_body, lam_init=lam_init, n_scan=n_scan),
        grid=(batch, N_HEADS_B, nq),
        in_specs=[pl.BlockSpec((tk, hw), lambda b, h, i: (b * nq + i, h)),
                  pl.BlockSpec((seq, hw), lambda b, h, i: (b, h)),
                  pl.BlockSpec((n_kv, V_ROWS, tk), lambda b, h, i: (b, h, 0)),
                  _layer_spec((4, HEAD_B), layer), _layer_spec((1, hw), layer),
                  _layer_spec((1, D_B), layer), _layer_spec((1, D_B), layer)] + [fwd] * 6 + [bwd] * 6,
        out_specs=[pl.BlockSpec((tk, hw), lambda b, h, i: (b * nq + i, h)), fwd, bwd],
        out_shape=[jax.ShapeDtypeStruct((t, D_B), BF16), y, y],
        scratch_shapes=[pltpu.VMEM((n_st, 2 * hw, 2 * tq), BF16),
                        pltpu.VMEM((3, tk, hw), BF16),
                        pltpu.VMEM((tk, tq), F32),
                        pltpu.VMEM((2, n_st, tk, 2 * tq), F32),
                        pltpu.VMEM((2, n_st, 1, 2 * tq), F32),
                        pltpu.VMEM((2, n_st, tk, 2 * tq), BF16),
                        pltpu.VMEM((2, n_st, 1, 2 * tq), F32),
                        pltpu.VMEM((n_st, 1, 2 * tq), F32),
                        pltpu.VMEM((n_st, V_ROWS, 2 * tq), F32),
                        pltpu.VMEM((2 * (D_A // LANES), LANES, LANES), F32)],
        compiler_params=_params(3),
        name="mixers",
    )(q, k, vt, lam_vecs, sub_g, qg, kg, r, v, kk, lw0, k0, a0, r, v, kk, lw1, k1, a1)


def _merge_body(yf_ref, yb_ref, gate_ref, bonus_ref, ob_ref, gt_ref, x_ref, lng_ref, lnb_ref, bd_ref,
                wb0_ref, wb1_ref, wo_ref, g2_ref, win_ref, wout_ref, o_ref):
    bd = bd_ref[...]
    y = yf_ref[...] + yb_ref[...]
    yc = y - _mm_split_lhs(y, bd)
    var = _mm(yc * yc, bd)
    yn = yc * lax.rsqrt(var + GN_EPS) * lng_ref[...] + lnb_ref[...]
    oa = (yn + bonus_ref[...]) * gate_ref[...]
    gts = gt_ref[...].astype(F32)
    merged = (gts[:, :D_MODEL] * _mm(oa, wb0_ref[...])
              + gts[:, D_MODEL:] * jnp.dot(ob_ref[...], wb1_ref[...], preferred_element_type=F32))
    o_ref[...] = _ffn_math(x_ref[...] + _mm(merged, wo_ref[...]), g2_ref, win_ref, wout_ref)


def _merge(yf, yb, gate, bonus, ob, gts, x2, ln_g, ln_b, bd_mean, wb, wo, g2, w_in, w_out, layer):
    t = x2.shape[0]
    tm = min(TOKEN_TILE, t)
    row = lambda width: pl.BlockSpec((tm, width), lambda i: (i, 0))
    return pl.pallas_call(
        _merge_body,
        grid=(t // tm,),
        in_specs=[row(D_A), row(D_A), row(D_A), row(D_A), row(D_B), row(2 * D_MODEL), row(D_MODEL),
                  _layer_spec((1, D_A), layer), _layer_spec((1, D_A), layer), _const_spec((D_A, D_A)),
                  _layer_spec((D_A, D_MODEL), layer, 0), _layer_spec((D_B, D_MODEL), layer, 1),
                  _layer_spec((D_MODEL, D_MODEL), layer),
                  _layer_spec((1, D_MODEL), layer), _layer_spec((D_MODEL, 2 * D_FF), layer),
                  _layer_spec((D_FF, D_MODEL), layer)],
        out_specs=row(D_MODEL),
        out_shape=jax.ShapeDtypeStruct((t, D_MODEL), F32),
        compiler_params=_params(1),
        name="merge",
    )(yf, yb, gate, bonus, ob, gts, x2, ln_g, ln_b, bd_mean, wb, wb, wo, g2, w_in, w_out)


def _block_diag2(w):
    z = jnp.zeros_like(w[:, 0])
    return jnp.concatenate([jnp.concatenate([w[:, 0], z], axis=2), jnp.concatenate([z, w[:, 1]], axis=2)],
                           axis=1)


def kernel(x, norm_ffn1, ffn1_in, ffn1_out, norm_mix, w_in, rwkv_mu, decay_w0, decay_w2, iclr_a0, iclr_a2,
           gate_g2, k_k, k_a, r_k, ln_x_g, ln_x_b, q_gain, k_gain, diff_lambda, subln_g, w_branch, w_out,
           norm_ffn2, ffn2_in, ffn2_out):
    batch, seq, d = x.shape
    depth = norm_ffn1.shape[0]
    assert seq % TOKEN_TILE == 0 and d == D_MODEL
    x2 = x.reshape(batch * seq, d)
    head_id = jnp.arange(D_A) // HEAD_A
    same = head_id[:, None] == head_id[None, :]
    bd_sum = same.astype(BF16)
    bd_mean = (same.astype(F32) / HEAD_A).astype(BF16)
    rows = lambda a: a.reshape(depth, 1, -1)
    bf = lambda a: a.astype(BF16)
    n_qk = D_B // HEAD_B
    qg = rows(jnp.tile(q_gain, (1, n_qk))) * (HEAD_B ** -0.5 * LOG2E)
    kg = rows(jnp.tile(k_gain, (1, n_qk)))
    ffn1 = (rows(norm_ffn1), bf(ffn1_in), bf(ffn1_out))
    ffn2 = (rows(norm_ffn2), bf(ffn2_in), bf(ffn2_out))
    proj_w = (rows(norm_mix), bf(w_in))
    prep_w = (rows(rwkv_mu), rows(decay_w0), bf(_block_diag2(decay_w2)), rows(iclr_a0),
              bf(_block_diag2(iclr_a2)), bf(gate_g2), rows(k_k), rows(k_a), rows(r_k))
    merge_w = (rows(ln_x_g), rows(ln_x_b), bd_mean, bf(w_branch), bf(w_out))
    sub_g = rows(subln_g)
    for l in range(depth):
        lam_init = 0.8 - 0.6 * math.exp(-0.3 * l)
        x2 = _ffn(x2, *ffn1, l)
        q, k, vt, gts, r, vv, kk, lw0, k0, a0, lw1, k1, a1, gate, bonus = _proj(
            x2, seq, *proj_w, qg, kg, bd_mean, *prep_w, bd_sum, l)
        ob, yf, yb = _mixers(q, k, vt, diff_lambda, sub_g, qg, kg, r, vv, kk, lw0, k0, a0, lw1, k1, a1,
                             lam_init, batch, seq, l)
        x2 = _merge(yf, yb, gate, bonus, ob, gts, x2, *merge_w, *ffn2, l)
    return x2.reshape(batch, seq, d)
```

```python
import functools
import math

import jax
import jax.numpy as jnp
from jax import lax
from jax.experimental import pallas as pl
from jax.experimental.pallas import tpu as pltpu

F32 = jnp.float32
BF16 = jnp.bfloat16

D_MODEL = 1024
D_A = 512
HEAD_A = 64
LORA = 64
LORA_GATE = 128
C_RWKV = 3 * D_A + 2 * LORA + 2 * LORA + LORA_GATE
D_B = 512
HEAD_B = 64
N_HEADS_B = D_B // (2 * HEAD_B)
D_FF = 2816
DECAY_SCALE = math.exp(-0.5)
GN_EPS = 64e-5
NORM_EPS = 1e-6
LOG2E = math.log2(math.e)
F32_ZERO_EXP2 = 150.0
ROUNDING_MARGIN = 1.02

LANES = 128
SUBLANES = 8
VMEM_LIMIT = 56 * 1024 * 1024

TOKEN_TILE = 512
MXU_WIDTH = 256
FF_SPLITS = (0, 6 * MXU_WIDTH, D_FF)
CHUNK = 64
SCAN_CHUNKS_PER_STEP = 2
ATTN_TQ = 128
V_ROWS = 2 * HEAD_B + 16

_NT = (((1,), (1,)), ((), ()))


def _mm(a, b):
    return jnp.dot(a.astype(BF16), b.astype(BF16), preferred_element_type=F32)


def _mm_nt(a, b):
    return lax.dot_general(a.astype(BF16), b.astype(BF16), _NT, preferred_element_type=F32)


def _split_bf16(x, n):
    parts = []
    for _ in range(n - 1):
        hi = x.astype(BF16)
        parts.append(hi)
        x = x - hi.astype(F32)
    parts.append(x.astype(BF16))
    return parts


def _mm_split_lhs(x, w, n=2):
    return sum(jnp.dot(p, w, preferred_element_type=F32) for p in _split_bf16(x, n))


def _rms_norm(x, g):
    return x * lax.rsqrt(jnp.mean(x * x, axis=-1, keepdims=True) + NORM_EPS) * g


def _const_spec(shape):
    return pl.BlockSpec(shape, lambda *_: (0,) * len(shape), pipeline_mode=pl.Buffered(1))


def _layer_spec(shape, layer, *lead):
    index = (layer,) + lead + (0,) * len(shape)
    return pl.BlockSpec((None,) * (1 + len(lead)) + shape, lambda *_: index, pipeline_mode=pl.Buffered(1))


def _params(n_axes):
    return pltpu.CompilerParams(dimension_semantics=("arbitrary",) * n_axes,
                                vmem_limit_bytes=VMEM_LIMIT)


def _ffn_math(x, g_ref, win_ref, wout_ref):
    h = _rms_norm(x, g_ref[...]).astype(BF16)
    acc = jnp.zeros(x.shape, F32)
    for lo, hi in zip(FF_SPLITS[:-1], FF_SPLITS[1:]):
        gate = jnp.dot(h, win_ref[:, lo:hi], preferred_element_type=F32)
        up = jnp.dot(h, win_ref[:, D_FF + lo:D_FF + hi], preferred_element_type=F32)
        act = (gate * jax.nn.sigmoid(gate) * up).astype(BF16)
        acc = acc + jnp.dot(act, wout_ref[lo:hi, :], preferred_element_type=F32)
    return x + 0.5 * acc


def _ffn_body(x_ref, g_ref, win_ref, wout_ref, o_ref):
    o_ref[...] = _ffn_math(x_ref[...], g_ref, win_ref, wout_ref)


def _ffn(x2, g, w_in, w_out, layer):
    t = x2.shape[0]
    tm = min(TOKEN_TILE, t)
    return pl.pallas_call(
        _ffn_body,
        grid=(t // tm,),
        in_specs=[pl.BlockSpec((tm, D_MODEL), lambda i: (i, 0)),
                  _layer_spec((1, D_MODEL), layer),
                  _layer_spec((D_MODEL, 2 * D_FF), layer),
                  _layer_spec((D_FF, D_MODEL), layer)],
        out_specs=pl.BlockSpec((tm, D_MODEL), lambda i: (i, 0)),
        out_shape=jax.ShapeDtypeStruct((t, D_MODEL), F32),
        compiler_params=_params(1),
        name="ffn",
    )(x2, g, w_in, w_out)


_Q0 = C_RWKV
_K0 = C_RWKV + D_B
_V0 = C_RWKV + 2 * D_B
_G0 = C_RWKV + 3 * D_B


def _proj_body(x_ref, xp_ref, xn_ref, g_ref, w_ref, qg_ref, kg_ref, bdm_ref, mu_ref, w0_ref, w2_ref, a0_ref,
               a2_ref, g2_ref, kk_ref, ka_ref, rk_ref, bds_ref,
               q_ref, k_ref, vt_ref, gt_ref, r_out, v_out, kk_out, lw0_out, k0_out, a0_out, lw1_out, k1_out,
               a1_out, gate_out, bonus_out, *, seq):
    i = pl.program_id(0)
    g = g_ref[...]
    h = _rms_norm(x_ref[...], g).astype(BF16)
    tm = h.shape[0]

    bd = bdm_ref[...]
    q = jnp.dot(h, w_ref[:, _Q0:_K0], preferred_element_type=F32)
    q_ref[...] = (q * lax.rsqrt(_mm(q * q, bd) + NORM_EPS) * qg_ref[...]).astype(BF16)
    k = jnp.dot(h, w_ref[:, _K0:_V0], preferred_element_type=F32)
    k_ref[...] = (k * lax.rsqrt(_mm(k * k, bd) + NORM_EPS) * kg_ref[...]).astype(BF16)
    vt = jnp.dot(h, w_ref[:, _V0:_G0], preferred_element_type=F32).T.astype(BF16)
    hw = 2 * HEAD_B
    ones = jnp.ones((V_ROWS - hw, vt.shape[1]), BF16)
    vt_ref[0] = jnp.concatenate([part for hd in range(N_HEADS_B)
                                 for part in (vt[hd * hw:(hd + 1) * hw], ones)], axis=0)
    gt_ref[...] = jax.nn.sigmoid(jnp.dot(h, w_ref[:, _G0:], preferred_element_type=F32)).astype(BF16)

    w_rwkv = w_ref[:, :C_RWKV]
    p = jnp.dot(h, w_rwkv, preferred_element_type=F32)
    edge = jnp.concatenate([xp_ref[...], xn_ref[...]], axis=0)
    p_edge = jnp.dot(_rms_norm(edge, g).astype(BF16), w_rwkv, preferred_element_type=F32)
    keep_prev = jnp.where((i * tm) % seq == 0, 0.0, 1.0)
    keep_next = jnp.where(((i + 1) * tm) % seq == 0, 0.0, 1.0)
    row = lax.broadcasted_iota(jnp.int32, p.shape, 0)
    prev = jnp.where(row == 0, keep_prev * p_edge[SUBLANES - 1:SUBLANES, :], pltpu.roll(p, 1, 0))
    nxt = jnp.where(row == tm - 1, keep_next * p_edge[SUBLANES:SUBLANES + 1, :], pltpu.roll(p, tm - 1, 0))
    p = p + mu_ref[...] * (0.5 * (prev + nxt) - p)

    r = p[:, :D_A]
    k = p[:, D_A:2 * D_A]
    v = p[:, 2 * D_A:3 * D_A]
    dw = p[:, 3 * D_A:3 * D_A + 2 * LORA]
    da = p[:, 3 * D_A + 2 * LORA:3 * D_A + 4 * LORA]
    dg = p[:, 3 * D_A + 4 * LORA:]
    bd = bds_ref[...]

    logw = -DECAY_SCALE * jax.nn.sigmoid(w0_ref[...] + _mm(jnp.tanh(dw), w2_ref[...]))
    iclr = jax.nn.sigmoid(a0_ref[...] + _mm(da, a2_ref[...]))
    gate_out[...] = _mm(jax.nn.sigmoid(dg), g2_ref[...])

    kk = k * kk_ref[...]
    norm = jnp.sqrt(_mm(kk * kk, bd))
    kk_out[...] = (kk / jnp.maximum(norm, 1e-12)).astype(kk_out.dtype)

    ka = ka_ref[...]
    a_f = iclr[:, :D_A]
    a_b = iclr[:, D_A:]
    k_f = k * (1.0 + (a_f - 1.0) * ka)
    k_b = k * (1.0 + (a_b - 1.0) * ka)
    bonus = _mm_split_lhs(r * rk_ref[...] * (k_f + k_b), bd)
    r_out[...] = r.astype(r_out.dtype)
    v_out[...] = v.astype(v_out.dtype)
    lw0_out[...] = logw[:, :D_A]
    k0_out[...] = k_f.astype(k0_out.dtype)
    a0_out[...] = a_f.astype(a0_out.dtype)
    lw1_out[...] = logw[:, D_A:]
    k1_out[...] = k_b.astype(k1_out.dtype)
    a1_out[...] = a_b.astype(a1_out.dtype)
    bonus_out[...] = bonus * v


def _proj(x2, seq, g, w, qg, kg, bd_mean, mu, w0, w2cat, a0, a2cat, g2, k_k, k_a, r_k, bd_sum, layer):
    t = x2.shape[0]
    tm = TOKEN_TILE
    c_in = w.shape[2]
    nb = tm // SUBLANES
    last = t // SUBLANES - 1
    row = lambda width: pl.BlockSpec((tm, width), lambda i: (i, 0))
    wide = jax.ShapeDtypeStruct((t, D_A), F32)
    half = jax.ShapeDtypeStruct((t, D_A), BF16)
    return pl.pallas_call(
        functools.partial(_proj_body, seq=seq),
        grid=(t // tm,),
        in_specs=[row(D_MODEL),
                  pl.BlockSpec((SUBLANES, D_MODEL), lambda i: (jnp.maximum(i * nb - 1, 0), 0)),
                  pl.BlockSpec((SUBLANES, D_MODEL), lambda i: (jnp.minimum((i + 1) * nb, last), 0)),
                  _layer_spec((1, D_MODEL), layer), _layer_spec((D_MODEL, c_in), layer),
                  _layer_spec((1, D_B), layer), _layer_spec((1, D_B), layer), _const_spec((D_B, D_B)),
                  _layer_spec((1, C_RWKV), layer),
                  _layer_spec((1, 2 * D_A), layer), _layer_spec((2 * LORA, 2 * D_A), layer),
                  _layer_spec((1, 2 * D_A), layer), _layer_spec((2 * LORA, 2 * D_A), layer),
                  _layer_spec((LORA_GATE, D_A), layer),
                  _layer_spec((1, D_A), layer), _layer_spec((1, D_A), layer), _layer_spec((1, D_A), layer),
                  _const_spec((D_A, D_A))],
        out_specs=[row(D_B), row(D_B),
                   pl.BlockSpec((1, N_HEADS_B * V_ROWS, tm), lambda i: (i, 0, 0)), row(2 * D_MODEL)]
                  + [row(D_A)] * 11,
        out_shape=[jax.ShapeDtypeStruct((t, D_B), BF16),
                   jax.ShapeDtypeStruct((t, D_B), BF16),
                   jax.ShapeDtypeStruct((t // tm, N_HEADS_B * V_ROWS, tm), BF16),
                   jax.ShapeDtypeStruct((t, 2 * D_MODEL), BF16),
                   half, half, half, wide, half, half, wide, half, half, wide, wide],
        compiler_params=_params(1),
        name="proj",
    )(x2, x2, x2, g, w, qg, kg, bd_mean, mu, w0, w2cat, a0, a2cat, g2, k_k, k_a, r_k, bd_sum)


def _scan_step(rf, vf, kkf, lwf, kf, af, rb, vb, kkb, lwb, kb, ab, yf_ref, yb_ref, h_ref):
    L = CHUNK
    n_sub = rf.shape[0] // L
    n_pairs = D_A // LANES
    ti = lax.broadcasted_iota(jnp.int32, (L, L), 0)
    tj = lax.broadcasted_iota(jnp.int32, (L, L), 1)
    pi = lax.broadcasted_iota(jnp.int32, (L, LANES), 0)
    lane = lax.broadcasted_iota(jnp.int32, (L, LANES), 1)
    pj = lane & (HEAD_A - 1)
    lane_lo = lane < HEAD_A
    eye = jnp.where(pj == pi, 1.0, 0.0)
    blk = lambda s: (pi // s) == (pj // s)
    row2 = lax.broadcasted_iota(jnp.int32, (2 * L, LANES), 0)
    col2 = lax.broadcasted_iota(jnp.int32, (2 * L, LANES), 1)
    same_head = (row2 // HEAD_A) == (col2 // HEAD_A)
    diag2 = row2 == col2
    zeros_l = jnp.zeros((L, LANES), F32)
    zeros_2l = jnp.zeros((2 * L, LANES), F32)

    def stack(x):
        return jnp.concatenate([jnp.where(lane_lo, x, 0.0), jnp.where(lane_lo, 0.0, x)], axis=0)

    def mmp(x, y):
        return _mm(x, stack(y))

    cat = jnp.concatenate
    chains = []
    for reverse, refs in ((False, (rf, vf, kkf, lwf, kf, af)), (True, (rb, vb, kkb, lwb, kb, ab))):
        tri = ((tj >= ti) if reverse else (tj <= ti)).astype(BF16)
        strict = (pj > pi) if reverse else (pj < pi)
        incl = (pj >= pi) if reverse else (pj <= pi)
        steps = []
        for sub in (range(n_sub - 1, -1, -1) if reverse else range(n_sub)):
            rows = slice(sub * L, (sub + 1) * L)
            r, v, kk, lw, k, a = (ref[rows, :].astype(F32) for ref in refs)
            G = sum(jnp.dot(tri, part, preferred_element_type=F32) for part in _split_bf16(lw, 3))
            g_end = G[0:1] if reverse else G[L - 1:L]
            gam_end = jnp.exp(g_end)
            inv_gam = jnp.exp(-G)
            rel_end = jnp.exp(g_end - G)
            b = kk * a
            full = dict(At=-kk * jnp.exp(G - lw), Bt=b * inv_gam, Kt=k * inv_gam, Rt=r * jnp.exp(G),
                        Kh=k * rel_end, Bh=b * rel_end, v=v)
            pairs = []
            for p in range(n_pairs):
                sl = slice(p * LANES, (p + 1) * LANES)
                c = {name: val[:, sl] for name, val in full.items()}
                c.update(strict=strict, incl=incl, gam_end=gam_end[:, sl], rows=rows)
                pairs.append(c)
            steps.append(pairs)
        chains.append(steps)
    flat = [c for steps in chains for pairs in steps for c in pairs]

    for c in flat:
        a4 = _mm_nt(cat([c["At"], c["Rt"]], axis=0), cat([stack(c["Bt"]), stack(c["Kt"])], axis=0))
        c["A_ab"] = jnp.where(c["strict"], a4[:L, :LANES], 0.0)
        c["A_ak"] = jnp.where(c["strict"], a4[:L, LANES:], 0.0)
        c["A_rb"] = jnp.where(c["incl"], a4[L:, :LANES], 0.0)
        c["A_rk"] = jnp.where(c["incl"], a4[L:, LANES:], 0.0)
    for c in flat:
        c["A0"] = jnp.where(blk(8), c["A_ab"], 0.0)
        c["A2"] = mmp(c["A0"], c["A0"])
    for c in flat:
        t1 = eye + c["A0"]
        both = mmp(cat([c["A2"], t1], axis=0), c["A2"])
        c["A4"] = both[:L]
        c["T"] = t1 + both[L:]
    for c in flat:
        c["T"] = c["T"] + mmp(c["T"], c["A4"])
    s = 8
    while s < L:
        outer = blk(2 * s) & jnp.logical_not(blk(s))
        for c in flat:
            c["TO"] = mmp(c["T"], jnp.where(outer, c["A_ab"], 0.0))
        for c in flat:
            c["T"] = c["T"] + mmp(c["TO"], c["T"])
        s *= 2
    for c in flat:
        c["AkV"] = mmp(c["A_ak"], c["v"])
    for c in flat:
        wu = _mm(c["T"], cat([stack(c["At"]), stack(c["AkV"])], axis=1))
        c["W"], c["U0"] = wu[:, :LANES], wu[:, LANES:]
    for c in flat:
        kb_t = cat([c["Kh"], c["Bh"]], axis=0).T
        mn = _mm(kb_t, cat([cat([zeros_l, c["v"]], axis=1), cat([c["W"], c["U0"]], axis=1)], axis=0))
        c["M"] = jnp.where(same_head, mn[:, :LANES], 0.0) + jnp.where(diag2, c["gam_end"], 0.0)
        c["N"] = jnp.where(same_head, mn[:, LANES:], 0.0)
    for c in flat:
        qy = _mm(cat([c["A_rb"], c["A_rk"]], axis=1),
                 cat([cat([stack(c["W"]), stack(c["U0"])], axis=1),
                      cat([zeros_2l, stack(c["v"])], axis=1)], axis=0))
        c["Q"] = c["Rt"] + qy[:, :LANES]
        c["Y0"] = qy[:, LANES:]
    for d, (steps, y_ref) in enumerate(zip(chains, (yf_ref, yb_ref))):
        for pairs in steps:
            ys = []
            for p, c in enumerate(pairs):
                qm_h = _mm(cat([c["Q"], c["M"]], axis=0), h_ref[d * n_pairs + p])
                ys.append(qm_h[:L] + c["Y0"])
                h_ref[d * n_pairs + p] = qm_h[L:] + c["N"]
            y_ref[pairs[0]["rows"], :] = cat(ys, axis=1)


def _bias_features(slope, n, width, key_side):
    pos = lax.broadcasted_iota(jnp.int32, (n, width), 0).astype(F32)
    lane = lax.broadcasted_iota(jnp.int32, (n, width), 1)
    hi, mid, lo = (t.astype(F32) for t in _split_bf16(slope * pos, 3))
    terms = jnp.where(lane % 3 == 0, hi, jnp.where(lane % 3 == 1, mid, lo))
    if key_side:
        return jnp.where(lane < 3, -1.0, jnp.where(lane < 6, terms, 0.0))
    return jnp.where(lane < 3, terms, jnp.where(lane < 6, 1.0, 0.0))


def _attn_body(q_ref, k_ref, vt_ref, lv_ref, sg_ref, qg_ref, kg_ref, o_ref, qs_ref, feat_ref, src_ref,
               s_ref, mx_ref, p_ref, al_ref, m_ref, acc_ref, *, lam_init, before_loop):
    tq = src_ref.shape[1]
    n_kv, _, tk = vt_ref.shape
    seq = n_kv * tk
    h = pl.program_id(1)
    i = pl.program_id(2)
    slope_bits = (127 - 2 * (h + 1)) << 23
    slope = (lax.bitcast_convert_type(jnp.full((1, 1), slope_bits, jnp.int32), jnp.float32).astype(F32)
             * LOG2E)
    bound =(HEAD_B * ROUNDING_MARGIN * jnp.max(jnp.abs(qg_ref[...]), axis=-1, keepdims=True)
             * jnp.max(jnp.abs(kg_ref[...]), axis=-1, keepdims=True))
    reach = jnp.minimum((F32_ZERO_EXP2 + 2.0 * bound) / slope + 1.0, float(seq)).astype(jnp.int32)[0, 0]

    n_st = tk // tq
    hw = 2 * HEAD_B

    @pl.when(i == 0)
    def _():
        kv_pos = lax.broadcasted_iota(jnp.int32, (tk, tq), 0)
        q_pos = lax.broadcasted_iota(jnp.int32, (tk, tq), 1)
        src_ref[...] = slope * (q_pos - kv_pos).astype(F32)
        key_feat = _bias_features(slope, tk, hw, key_side=True)
        feat_ref[0] = key_feat.astype(BF16)
        feat_ref[1] = (-key_feat).astype(BF16)
        feat_ref[2] = jnp.zeros((tk, hw), BF16)
        q_feat = _bias_features(slope, tq, hw, key_side=False).T.astype(BF16)
        for st in range(n_st):
            qs_ref[st, hw:, :tq] = q_feat
            qs_ref[st, hw:, tq:] = q_feat

    chan =lax.broadcasted_iota(jnp.int32, (hw, tq), 0)
    for st in range(n_st):
        q_t = q_ref[st * tq:(st + 1) * tq, :].astype(F32).T.astype(BF16)
        zero = jnp.zeros_like(q_t)
        qs_ref[st, :hw, :tq] = jnp.where(chan < HEAD_B, q_t, zero)
        qs_ref[st, :hw, tq:] = jnp.where(chan < HEAD_B, zero, q_t)
    q0 = i * tk
    j_lo = jnp.maximum(q0 - reach, 0) // tk
    j_hi = jnp.minimum(q0 + tk - 1 + reach, seq - 1) // tk + 1
    n_left = i - j_lo
    n_off = n_left + (j_hi - i - 1)

    def scores(j, side, st):
        kb = k_ref[pl.ds(pl.multiple_of(j * tk, tk), tk), :]
        return jnp.dot(jnp.concatenate([kb, feat_ref[side]], axis=1), qs_ref[st],
                       preferred_element_type=F32)

    def softmax_update(s, s_max, cst, st):
        m_old = m_ref[st]
        m_new = jnp.maximum(m_old, s_max + cst)
        alpha = jnp.exp2(m_old - m_new)
        p = jnp.exp2(s - (m_new - cst))
        m_ref[st] = m_new
        return alpha, p.astype(BF16)

    def block(n):
        j = jnp.clip(j_lo + n + (n >= n_left).astype(jnp.int32), 0, n_kv - 1)
        j = jnp.where(n < 0, i, j)
        return j, (j > i).astype(jnp.int32), jnp.logical_and(n >= 0, n < n_off)

    def qk_stage(n, slot):
        j, side, _ = block(n)
        for st in range(n_st):
            s = scores(j, side, st)
            s_ref[slot, st] = s
            mx_ref[slot, st] = jnp.max(s, axis=0, keepdims=True)

    def softmax_stage(n, slot):
        j, _, valid = block(n)
        for st in range(n_st):
            dist0 = jnp.abs(q0 + st * tq - j * tk).astype(F32)
            cst = jnp.where(valid, -slope * dist0, -1e30)
            alpha, p = softmax_update(s_ref[slot, st], mx_ref[slot, st], cst, st)
            al_ref[slot, st] = alpha
            p_ref[slot, st] = p

    def pv_stage(n, slot):
        j, _, _ = block(n)
        for st in range(n_st):
            acc_ref[st] = al_ref[slot, st] * acc_ref[st] + jnp.dot(vt_ref[j], p_ref[slot, st],
                                                                    preferred_element_type=F32)

    m_ref[...] = jnp.full(m_ref.shape, -1e30, F32)
    acc_ref[...] = jnp.zeros(acc_ref.shape, F32)
    diag_scores = [scores(i, 2, st) for st in range(n_st)]
    qk_stage(0, 0)
    src = src_ref[...]
    for st in range(n_st):
        bias = jnp.abs(src + slope * float(st * tq))
        s = diag_scores[st] - jnp.concatenate([bias, bias], axis=1)
        alpha, p = softmax_update(s, jnp.max(s, axis=0, keepdims=True), jnp.zeros((1, 1), F32), st)
        al_ref[1, st] = alpha
        p_ref[1, st] = p

    before_loop()

    def pipelined_pair(t, carry):
        n = 2 * t
        qk_stage(n + 1, 1)
        softmax_stage(n, 0)
        pv_stage(n - 1, 1)
        qk_stage(n + 2, 0)
        softmax_stage(n + 1, 1)
        pv_stage(n, 0)
        return carry

    n_pairs = n_off // 2
    lax.fori_loop(0, n_pairs, pipelined_pair, 0)
    last = 2 * n_pairs

    @pl.when(n_off % 2 == 1)
    def _():
        softmax_stage(last, 0)
        pv_stage(last - 1, 1)
        pv_stage(last, 0)

    @pl.when(n_off % 2 == 0)
    def _():
        pv_stage(last - 1, 1)

    lv = lv_ref[...]
    lam = (jnp.exp(jnp.sum(lv[0:1] * lv[1:2], axis=-1, keepdims=True))
           - jnp.exp(jnp.sum(lv[2:3] * lv[3:4], axis=-1, keepdims=True)) + lam_init)
    for st in range(n_st):
        acc = acc_ref[st, :hw, :]
        l = acc_ref[st, hw:hw + 1, :]
        o_t = acc[:, :tq] / l[:, :tq] - lam * (acc[:, tq:] / l[:, tq:])
        o_t = o_t * lax.rsqrt(jnp.mean(o_t * o_t, axis=0, keepdims=True) + NORM_EPS)
        o_ref[st * tq:(st + 1) * tq, :] = (o_t.T * sg_ref[...] * (1.0 - lam_init)).astype(o_ref.dtype)


def _mixers_body(q_ref, k_ref, vt_ref, lv_ref, sg_ref, qg_ref, kg_ref,
                 rf, vf, kkf, lwf, kf, af, rb, vb, kkb, lwb, kb, ab,
                 o_ref, yf_ref, yb_ref, *scratch, lam_init, n_scan):
    *attn_scratch, h_ref = scratch
    step = (pl.program_id(0) * pl.num_programs(1) + pl.program_id(1)) * pl.num_programs(2) + pl.program_id(2)

    @pl.when(step % n_scan == 0)
    def _():
        h_ref[...] = jnp.zeros(h_ref.shape, F32)

    _attn_body(q_ref, k_ref, vt_ref, lv_ref, sg_ref, qg_ref, kg_ref, o_ref, *attn_scratch, lam_init=lam_init,
               before_loop=lambda: _scan_step(rf, vf, kkf, lwf, kf, af, rb, vb, kkb, lwb, kb, ab,
                                              yf_ref, yb_ref, h_ref))


def _mixers(q, k, vt, lam_vecs, sub_g, qg, kg, r, v, kk, lw0, k0, a0, lw1, k1, a1, lam_init, batch, seq, layer):
    t = q.shape[0]
    tk = vt.shape[2]
    tq = ATTN_TQ
    n_st = tk // tq
    nq = seq // tk
    n_kv = seq // tk
    hw = 2 * HEAD_B
    rows = SCAN_CHUNKS_PER_STEP * CHUNK
    n_scan = seq // rows
    assert N_HEADS_B * nq == n_scan
    step = lambda b, h, i: (b * N_HEADS_B + h) * nq + i
    fwd = pl.BlockSpec((rows, D_A), lambda b, h, i: (step(b, h, i), 0))
    bwd = pl.BlockSpec((rows, D_A), lambda b, h, i: (
        (step(b, h, i) // n_scan) * n_scan + n_scan - 1 - step(b, h, i) % n_scan, 0))
    y = jax.ShapeDtypeStruct((t, D_A), F32)
    return pl.pallas_call(
        functools.partial(_mixers_body, lam_init=lam_init, n_scan=n_scan),
        grid=(batch, N_HEADS_B, nq),
        in_specs=[pl.BlockSpec((tk, hw), lambda b, h, i: (b * nq + i, h)),
                  pl.BlockSpec((seq, hw), lambda b, h, i: (b, h)),
                  pl.BlockSpec((n_kv, V_ROWS, tk), lambda b, h, i: (b, h, 0)),
                  _layer_spec((4, HEAD_B), layer), _layer_spec((1, hw), layer),
                  _layer_spec((1, D_B), layer), _layer_spec((1, D_B), layer)] + [fwd] * 6 + [bwd] * 6,
        out_specs=[pl.BlockSpec((tk, hw), lambda b, h, i: (b * nq + i, h)), fwd, bwd],
        out_shape=[jax.ShapeDtypeStruct((t, D_B), BF16), y, y],
        scratch_shapes=[pltpu.VMEM((n_st, 2 * hw, 2 * tq), BF16),
                        pltpu.VMEM((3, tk, hw), BF16),
                        pltpu.VMEM((tk, tq), F32),
                        pltpu.VMEM((2, n_st, tk, 2 * tq), F32),
                        pltpu.VMEM((2, n_st, 1, 2 * tq), F32),
                        pltpu.VMEM((2, n_st, tk, 2 * tq), BF16),
                        pltpu.VMEM((2, n_st, 1, 2 * tq), F32),
                        pltpu.VMEM((n_st, 1, 2 * tq), F32),
                        pltpu.VMEM((n_st, V_ROWS, 2 * tq), F32),
                        pltpu.VMEM((2 * (D_A // LANES), LANES, LANES), F32)],
        compiler_params=_params(3),
        name="mixers",
    )(q, k, vt, lam_vecs, sub_g, qg, kg, r, v, kk, lw0, k0, a0, r, v, kk, lw1, k1, a1)


def _merge_body(yf_ref, yb_ref, gate_ref, bonus_ref, ob_ref, gt_ref, x_ref, lng_ref, lnb_ref, bd_ref,
                wb0_ref, wb1_ref, wo_ref, g2_ref, win_ref, wout_ref, o_ref):
    bd = bd_ref[...]
    y = yf_ref[...] + yb_ref[...]
    yc = y - _mm_split_lhs(y, bd)
    var = _mm(yc * yc, bd)
    yn = yc * lax.rsqrt(var + GN_EPS) * lng_ref[...] + lnb_ref[...]
    oa = (yn + bonus_ref[...]) * gate_ref[...]
    gts = gt_ref[...].astype(F32)
    merged = (gts[:, :D_MODEL] * _mm(oa, wb0_ref[...])
              + gts[:, D_MODEL:] * jnp.dot(ob_ref[...], wb1_ref[...], preferred_element_type=F32))
    o_ref[...] = _ffn_math(x_ref[...] + _mm(merged, wo_ref[...]), g2_ref, win_ref, wout_ref)


def _merge(yf, yb, gate, bonus, ob, gts, x2, ln_g, ln_b, bd_mean, wb, wo, g2, w_in, w_out, layer):
    t = x2.shape[0]
    tm = min(TOKEN_TILE, t)
    row = lambda width: pl.BlockSpec((tm, width), lambda i: (i, 0))
    return pl.pallas_call(
        _merge_body,
        grid=(t // tm,),
        in_specs=[row(D_A), row(D_A), row(D_A), row(D_A), row(D_B), row(2 * D_MODEL), row(D_MODEL),
                  _layer_spec((1, D_A), layer), _layer_spec((1, D_A), layer), _const_spec((D_A, D_A)),
                  _layer_spec((D_A, D_MODEL), layer, 0), _layer_spec((D_B, D_MODEL), layer, 1),
                  _layer_spec((D_MODEL, D_MODEL), layer),
                  _layer_spec((1, D_MODEL), layer), _layer_spec((D_MODEL, 2 * D_FF), layer),
                  _layer_spec((D_FF, D_MODEL), layer)],
        out_specs=row(D_MODEL),
        out_shape=jax.ShapeDtypeStruct((t, D_MODEL), F32),
        compiler_params=_params(1),
        name="merge",
    )(yf, yb, gate, bonus, ob, gts, x2, ln_g, ln_b, bd_mean, wb, wb, wo, g2, w_in, w_out)


def _block_diag2(w):
    z = jnp.zeros_like(w[:, 0])
    return jnp.concatenate([jnp.concatenate([w[:, 0], z], axis=2), jnp.concatenate([z, w[:, 1]], axis=2)],
                           axis=1)


def kernel(x, norm_ffn1, ffn1_in, ffn1_out, norm_mix, w_in, rwkv_mu, decay_w0, decay_w2, iclr_a0, iclr_a2,
           gate_g2, k_k, k_a, r_k, ln_x_g, ln_x_b, q_gain, k_gain, diff_lambda, subln_g, w_branch, w_out,
           norm_ffn2, ffn2_in, ffn2_out):
    batch, seq, d = x.shape
    depth = norm_ffn1.shape[0]
    assert seq % TOKEN_TILE == 0 and d == D_MODEL
    x2 = x.reshape(batch * seq, d)
    head_id = jnp.arange(D_A) // HEAD_A
    same = head_id[:, None] == head_id[None, :]
    bd_sum = same.astype(BF16)
    bd_mean = (same.astype(F32) / HEAD_A).astype(BF16)
    rows = lambda a: a.reshape(depth, 1, -1)
    bf = lambda a: a.astype(BF16)
    n_qk = D_B // HEAD_B
    qg = rows(jnp.tile(q_gain, (1, n_qk))) * (HEAD_B ** -0.5 * LOG2E)
    kg = rows(jnp.tile(k_gain, (1, n_qk)))
    ffn1 = (rows(norm_ffn1), bf(ffn1_in), bf(ffn1_out))
    ffn2 = (rows(norm_ffn2), bf(ffn2_in), bf(ffn2_out))
    proj_w = (rows(norm_mix), bf(w_in))
    prep_w = (rows(rwkv_mu), rows(decay_w0), bf(_block_diag2(decay_w2)), rows(iclr_a0),
              bf(_block_diag2(iclr_a2)), bf(gate_g2), rows(k_k), rows(k_a), rows(r_k))
    merge_w = (rows(ln_x_g), rows(ln_x_b), bd_mean, bf(w_branch), bf(w_out))
    sub_g = rows(subln_g)
    for l in range(depth):
        lam_init = 0.8 - 0.6 * math.exp(-0.3 * l)
        x2 = _ffn(x2, *ffn1, l)
        q, k, vt, gts, r, vv, kk, lw0, k0, a0, lw1, k1, a1, gate, bonus = _proj(
            x2, seq, *proj_w, qg, kg, bd_mean, *prep_w, bd_sum, l)
        ob, yf, yb = _mixers(q, k, vt, diff_lambda, sub_g, qg, kg, r, vv, kk, lw0, k0, a0, lw1, k1, a1,
                             lam_init, batch, seq, l)
        x2 = _merge(yf, yb, gate, bonus, ob, gts, x2, *merge_w, *ffn2, l)
    return x2.reshape(batch, seq, d)
```

```python
import functools
import math

import jax
import jax.numpy as jnp
from jax import lax
from jax.experimental import pallas as pl
from jax.experimental.pallas import tpu as pltpu

F32 = jnp.float32
BF16 = jnp.bfloat16

D_MODEL = 1024
D_A = 512
HEAD_A = 64
LORA = 64
LORA_GATE = 128
C_RWKV = 3 * D_A + 2 * LORA + 2 * LORA + LORA_GATE
D_B = 512
HEAD_B = 64
N_HEADS_B = D_B // (2 * HEAD_B)
D_FF = 2816
DECAY_SCALE = math.exp(-0.5)
GN_EPS = 64e-5
NORM_EPS = 1e-6
LOG2E = math.log2(math.e)
F32_ZERO_EXP2 = 150.0
ROUNDING_MARGIN = 1.02

LANES = 128
SUBLANES = 8
VMEM_LIMIT = 56 * 1024 * 1024

TOKEN_TILE = 512
MXU_WIDTH = 256
FF_SPLITS = (0, 6 * MXU_WIDTH, D_FF)
CHUNK = 64
SCAN_CHUNKS_PER_STEP = 2
ATTN_TQ = 512
V_ROWS = 2 * HEAD_B + 16

_NT = (((1,), (1,)), ((), ()))


def _mm(a, b):
    return jnp.dot(a.astype(BF16), b.astype(BF16), preferred_element_type=F32)


def _mm_nt(a, b):
    return lax.dot_general(a.astype(BF16), b.astype(BF16), _NT, preferred_element_type=F32)


def _split_bf16(x, n):
    parts = []
    for _ in range(n - 1):
        hi = x.astype(BF16)
        parts.append(hi)
        x = x - hi.astype(F32)
    parts.append(x.astype(BF16))
    return parts


def _mm_split_lhs(x, w, n=2):
    return sum(jnp.dot(p, w, preferred_element_type=F32) for p in _split_bf16(x, n))


def _rms_norm(x, g):
    return x * lax.rsqrt(jnp.mean(x * x, axis=-1, keepdims=True) + NORM_EPS) * g


def _const_spec(shape):
    return pl.BlockSpec(shape, lambda *_: (0,) * len(shape), pipeline_mode=pl.Buffered(1))


def _layer_spec(shape, layer, *lead):
    index = (layer,) + lead + (0,) * len(shape)
    return pl.BlockSpec((None,) * (1 + len(lead)) + shape, lambda *_: index, pipeline_mode=pl.Buffered(1))


def _params(n_axes):
    return pltpu.CompilerParams(dimension_semantics=("arbitrary",) * n_axes,
                                vmem_limit_bytes=VMEM_LIMIT)


def _ffn_math(x, g_ref, win_ref, wout_ref):
    h = _rms_norm(x, g_ref[...]).astype(BF16)
    acc = jnp.zeros(x.shape, F32)
    for lo, hi in zip(FF_SPLITS[:-1], FF_SPLITS[1:]):
        gate = jnp.dot(h, win_ref[:, lo:hi], preferred_element_type=F32)
        up = jnp.dot(h, win_ref[:, D_FF + lo:D_FF + hi], preferred_element_type=F32)
        act = (gate * jax.nn.sigmoid(gate) * up).astype(BF16)
        acc = acc + jnp.dot(act, wout_ref[lo:hi, :], preferred_element_type=F32)
    return x + 0.5 * acc


def _ffn_body(x_ref, g_ref, win_ref, wout_ref, o_ref):
    o_ref[...] = _ffn_math(x_ref[...], g_ref, win_ref, wout_ref)


def _ffn(x2, g, w_in, w_out, layer):
    t = x2.shape[0]
    tm = min(TOKEN_TILE, t)
    return pl.pallas_call(
        _ffn_body,
        grid=(t // tm,),
        in_specs=[pl.BlockSpec((tm, D_MODEL), lambda i: (i, 0)),
                  _layer_spec((1, D_MODEL), layer),
                  _layer_spec((D_MODEL, 2 * D_FF), layer),
                  _layer_spec((D_FF, D_MODEL), layer)],
        out_specs=pl.BlockSpec((tm, D_MODEL), lambda i: (i, 0)),
        out_shape=jax.ShapeDtypeStruct((t, D_MODEL), F32),
        compiler_params=_params(1),
        name="ffn",
    )(x2, g, w_in, w_out)


_Q0 = C_RWKV
_K0 = C_RWKV + D_B
_V0 = C_RWKV + 2 * D_B
_G0 = C_RWKV + 3 * D_B


def _proj_body(x_ref, xp_ref, xn_ref, g_ref, w_ref, qg_ref, kg_ref, bdm_ref, mu_ref, w0_ref, w2_ref, a0_ref,
               a2_ref, g2_ref, kk_ref, ka_ref, rk_ref, bds_ref,
               q_ref, k_ref, vt_ref, gt_ref, r_out, v_out, kk_out, lw0_out, k0_out, a0_out, lw1_out, k1_out,
               a1_out, gate_out, bonus_out, *, seq):
    i = pl.program_id(0)
    g = g_ref[...]
    h = _rms_norm(x_ref[...], g).astype(BF16)
    tm = h.shape[0]

    bd = bdm_ref[...]
    q = jnp.dot(h, w_ref[:, _Q0:_K0], preferred_element_type=F32)
    q_ref[...] = (q * lax.rsqrt(_mm(q * q, bd) + NORM_EPS) * qg_ref[...]).astype(BF16)
    k = jnp.dot(h, w_ref[:, _K0:_V0], preferred_element_type=F32)
    k_ref[...] = (k * lax.rsqrt(_mm(k * k, bd) + NORM_EPS) * kg_ref[...]).astype(BF16)
    vt = jnp.dot(h, w_ref[:, _V0:_G0], preferred_element_type=F32).T.astype(BF16)
    hw = 2 * HEAD_B
    ones = jnp.ones((V_ROWS - hw, vt.shape[1]), BF16)
    vt_ref[0] = jnp.concatenate([part for hd in range(N_HEADS_B)
                                 for part in (vt[hd * hw:(hd + 1) * hw], ones)], axis=0)
    gt_ref[...] = jax.nn.sigmoid(jnp.dot(h, w_ref[:, _G0:], preferred_element_type=F32)).astype(BF16)

    w_rwkv = w_ref[:, :C_RWKV]
    p = jnp.dot(h, w_rwkv, preferred_element_type=F32)
    edge = jnp.concatenate([xp_ref[...], xn_ref[...]], axis=0)
    p_edge = jnp.dot(_rms_norm(edge, g).astype(BF16), w_rwkv, preferred_element_type=F32)
    keep_prev = jnp.where((i * tm) % seq == 0, 0.0, 1.0)
    keep_next = jnp.where(((i + 1) * tm) % seq == 0, 0.0, 1.0)
    row = lax.broadcasted_iota(jnp.int32, p.shape, 0)
    prev = jnp.where(row == 0, keep_prev * p_edge[SUBLANES - 1:SUBLANES, :], pltpu.roll(p, 1, 0))
    nxt = jnp.where(row == tm - 1, keep_next * p_edge[SUBLANES:SUBLANES + 1, :], pltpu.roll(p, tm - 1, 0))
    p = p + mu_ref[...] * (0.5 * (prev + nxt) - p)

    r = p[:, :D_A]
    k = p[:, D_A:2 * D_A]
    v = p[:, 2 * D_A:3 * D_A]
    dw = p[:, 3 * D_A:3 * D_A + 2 * LORA]
    da = p[:, 3 * D_A + 2 * LORA:3 * D_A + 4 * LORA]
    dg = p[:, 3 * D_A + 4 * LORA:]
    bd = bds_ref[...]

    logw = -DECAY_SCALE * jax.nn.sigmoid(w0_ref[...] + _mm(jnp.tanh(dw), w2_ref[...]))
    iclr = jax.nn.sigmoid(a0_ref[...] + _mm(da, a2_ref[...]))
    gate_out[...] = _mm(jax.nn.sigmoid(dg), g2_ref[...])

    kk = k * kk_ref[...]
    norm = jnp.sqrt(_mm(kk * kk, bd))
    kk_out[...] = (kk / jnp.maximum(norm, 1e-12)).astype(kk_out.dtype)

    ka = ka_ref[...]
    a_f = iclr[:, :D_A]
    a_b = iclr[:, D_A:]
    k_f = k * (1.0 + (a_f - 1.0) * ka)
    k_b = k * (1.0 + (a_b - 1.0) * ka)
    bonus = _mm_split_lhs(r * rk_ref[...] * (k_f + k_b), bd)
    r_out[...] = r.astype(r_out.dtype)
    v_out[...] = v.astype(v_out.dtype)
    lw0_out[...] = logw[:, :D_A]
    k0_out[...] = k_f.astype(k0_out.dtype)
    a0_out[...] = a_f.astype(a0_out.dtype)
    lw1_out[...] = logw[:, D_A:]
    k1_out[...] = k_b.astype(k1_out.dtype)
    a1_out[...] = a_b.astype(a1_out.dtype)
    bonus_out[...] = bonus * v


def _proj(x2, seq, g, w, qg, kg, bd_mean, mu, w0, w2cat, a0, a2cat, g2, k_k, k_a, r_k, bd_sum, layer):
    t = x2.shape[0]
    tm = TOKEN_TILE
    c_in = w.shape[2]
    nb = tm // SUBLANES
    last = t // SUBLANES - 1
    row = lambda width: pl.BlockSpec((tm, width), lambda i: (i, 0))
    wide = jax.ShapeDtypeStruct((t, D_A), F32)
    half = jax.ShapeDtypeStruct((t, D_A), BF16)
    return pl.pallas_call(
        functools.partial(_proj_body, seq=seq),
        grid=(t // tm,),
        in_specs=[row(D_MODEL),
                  pl.BlockSpec((SUBLANES, D_MODEL), lambda i: (jnp.maximum(i * nb - 1, 0), 0)),
                  pl.BlockSpec((SUBLANES, D_MODEL), lambda i: (jnp.minimum((i + 1) * nb, last), 0)),
                  _layer_spec((1, D_MODEL), layer), _layer_spec((D_MODEL, c_in), layer),
                  _layer_spec((1, D_B), layer), _layer_spec((1, D_B), layer), _const_spec((D_B, D_B)),
                  _layer_spec((1, C_RWKV), layer),
                  _layer_spec((1, 2 * D_A), layer), _layer_spec((2 * LORA, 2 * D_A), layer),
                  _layer_spec((1, 2 * D_A), layer), _layer_spec((2 * LORA, 2 * D_A), layer),
                  _layer_spec((LORA_GATE, D_A), layer),
                  _layer_spec((1, D_A), layer), _layer_spec((1, D_A), layer), _layer_spec((1, D_A), layer),
                  _const_spec((D_A, D_A))],
        out_specs=[row(D_B), row(D_B),
                   pl.BlockSpec((1, N_HEADS_B * V_ROWS, tm), lambda i: (i, 0, 0)), row(2 * D_MODEL)]
                  + [row(D_A)] * 11,
        out_shape=[jax.ShapeDtypeStruct((t, D_B), BF16),
                   jax.ShapeDtypeStruct((t, D_B), BF16),
                   jax.ShapeDtypeStruct((t // tm, N_HEADS_B * V_ROWS, tm), BF16),
                   jax.ShapeDtypeStruct((t, 2 * D_MODEL), BF16),
                   half, half, half, wide, half, half, wide, half, half, wide, wide],
        compiler_params=_params(1),
        name="proj",
    )(x2, x2, x2, g, w, qg, kg, bd_mean, mu, w0, w2cat, a0, a2cat, g2, k_k, k_a, r_k, bd_sum)


def _scan_step(rf, vf, kkf, lwf, kf, af, rb, vb, kkb, lwb, kb, ab, yf_ref, yb_ref, h_ref):
    L = CHUNK
    n_sub = rf.shape[0] // L
    n_pairs = D_A // LANES
    ti = lax.broadcasted_iota(jnp.int32, (L, L), 0)
    tj = lax.broadcasted_iota(jnp.int32, (L, L), 1)
    pi = lax.broadcasted_iota(jnp.int32, (L, LANES), 0)
    lane = lax.broadcasted_iota(jnp.int32, (L, LANES), 1)
    pj = lane & (HEAD_A - 1)
    lane_lo = lane < HEAD_A
    eye = jnp.where(pj == pi, 1.0, 0.0)
    blk = lambda s: (pi // s) == (pj // s)
    row2 = lax.broadcasted_iota(jnp.int32, (2 * L, LANES), 0)
    col2 = lax.broadcasted_iota(jnp.int32, (2 * L, LANES), 1)
    same_head = (row2 // HEAD_A) == (col2 // HEAD_A)
    diag2 = row2 == col2
    zeros_l = jnp.zeros((L, LANES), F32)
    zeros_2l = jnp.zeros((2 * L, LANES), F32)

    def stack(x):
        return jnp.concatenate([jnp.where(lane_lo, x, 0.0), jnp.where(lane_lo, 0.0, x)], axis=0)

    def mmp(x, y):
        return _mm(x, stack(y))

    cat = jnp.concatenate
    chains = []
    for reverse, refs in ((False, (rf, vf, kkf, lwf, kf, af)), (True, (rb, vb, kkb, lwb, kb, ab))):
        tri = ((tj >= ti) if reverse else (tj <= ti)).astype(BF16)
        strict = (pj > pi) if reverse else (pj < pi)
        incl = (pj >= pi) if reverse else (pj <= pi)
        steps = []
        for sub in (range(n_sub - 1, -1, -1) if reverse else range(n_sub)):
            rows = slice(sub * L, (sub + 1) * L)
            r, v, kk, lw, k, a = (ref[rows, :].astype(F32) for ref in refs)
            G = sum(jnp.dot(tri, part, preferred_element_type=F32) for part in _split_bf16(lw, 3))
            g_end = G[0:1] if reverse else G[L - 1:L]
            gam_end = jnp.exp(g_end)
            inv_gam = jnp.exp(-G)
            rel_end = jnp.exp(g_end - G)
            b = kk * a
            full = dict(At=-kk * jnp.exp(G - lw), Bt=b * inv_gam, Kt=k * inv_gam, Rt=r * jnp.exp(G),
                        Kh=k * rel_end, Bh=b * rel_end, v=v)
            pairs = []
            for p in range(n_pairs):
                sl = slice(p * LANES, (p + 1) * LANES)
                c = {name: val[:, sl] for name, val in full.items()}
                c.update(strict=strict, incl=incl, gam_end=gam_end[:, sl], rows=rows)
                pairs.append(c)
            steps.append(pairs)
        chains.append(steps)
    flat = [c for steps in chains for pairs in steps for c in pairs]

    for c in flat:
        a4 = _mm_nt(cat([c["At"], c["Rt"]], axis=0), cat([stack(c["Bt"]), stack(c["Kt"])], axis=0))
        c["A_ab"] = jnp.where(c["strict"], a4[:L, :LANES], 0.0)
        c["A_ak"] = jnp.where(c["strict"], a4[:L, LANES:], 0.0)
        c["A_rb"] = jnp.where(c["incl"], a4[L:, :LANES], 0.0)
        c["A_rk"] = jnp.where(c["incl"], a4[L:, LANES:], 0.0)
    for c in flat:
        c["A0"] = jnp.where(blk(8), c["A_ab"], 0.0)
        c["A2"] = mmp(c["A0"], c["A0"])
    for c in flat:
        t1 = eye + c["A0"]
        both = mmp(cat([c["A2"], t1], axis=0), c["A2"])
        c["A4"] = both[:L]
        c["T"] = t1 + both[L:]
    for c in flat:
        c["T"] = c["T"] + mmp(c["T"], c["A4"])
    s = 8
    while s < L:
        outer = blk(2 * s) & jnp.logical_not(blk(s))
        for c in flat:
            c["TO"] = mmp(c["T"], jnp.where(outer, c["A_ab"], 0.0))
        for c in flat:
            c["T"] = c["T"] + mmp(c["TO"], c["T"])
        s *= 2
    for c in flat:
        c["AkV"] = mmp(c["A_ak"], c["v"])
    for c in flat:
        wu = _mm(c["T"], cat([stack(c["At"]), stack(c["AkV"])], axis=1))
        c["W"], c["U0"] = wu[:, :LANES], wu[:, LANES:]
    for c in flat:
        kb_t = cat([c["Kh"], c["Bh"]], axis=0).T
        mn = _mm(kb_t, cat([cat([zeros_l, c["v"]], axis=1), cat([c["W"], c["U0"]], axis=1)], axis=0))
        c["M"] = jnp.where(same_head, mn[:, :LANES], 0.0) + jnp.where(diag2, c["gam_end"], 0.0)
        c["N"] = jnp.where(same_head, mn[:, LANES:], 0.0)
    for c in flat:
        qy = _mm(cat([c["A_rb"], c["A_rk"]], axis=1),
                 cat([cat([stack(c["W"]), stack(c["U0"])], axis=1),
                      cat([zeros_2l, stack(c["v"])], axis=1)], axis=0))
        c["Q"] = c["Rt"] + qy[:, :LANES]
        c["Y0"] = qy[:, LANES:]
    for d, (steps, y_ref) in enumerate(zip(chains, (yf_ref, yb_ref))):
        for pairs in steps:
            ys = []
            for p, c in enumerate(pairs):
                qm_h = _mm(cat([c["Q"], c["M"]], axis=0), h_ref[d * n_pairs + p])
                ys.append(qm_h[:L] + c["Y0"])
                h_ref[d * n_pairs + p] = qm_h[L:] + c["N"]
            y_ref[pairs[0]["rows"], :] = cat(ys, axis=1)


def _bias_features(slope, n, width, key_side):
    pos = lax.broadcasted_iota(jnp.int32, (n, width), 0).astype(F32)
    lane = lax.broadcasted_iota(jnp.int32, (n, width), 1)
    hi, mid, lo = (t.astype(F32) for t in _split_bf16(slope * pos, 3))
    terms = jnp.where(lane % 3 == 0, hi, jnp.where(lane % 3 == 1, mid, lo))
    if key_side:
        return jnp.where(lane < 3, -1.0, jnp.where(lane < 6, terms, 0.0))
    return jnp.where(lane < 3, terms, jnp.where(lane < 6, 1.0, 0.0))


def _attn_body(q_ref, k_ref, vt_ref, lv_ref, sg_ref, qg_ref, kg_ref, o_ref, qs_ref, feat_ref, src_ref,
               s_ref, mx_ref, p_ref, al_ref, m_ref, acc_ref, *, lam_init, before_loop):
    tq = src_ref.shape[1]
    n_kv, _, tk = vt_ref.shape
    seq = n_kv * tk
    h = pl.program_id(1)
    i = pl.program_id(2)
    slope_bits = (127 - 2 * (h + 1)) << 23
    slope = (lax.bitcast_convert_type(jnp.full((1, 1), slope_bits, jnp.int32), jnp.float32).astype(F32)
             * LOG2E)
    bound =(HEAD_B * ROUNDING_MARGIN * jnp.max(jnp.abs(qg_ref[...]), axis=-1, keepdims=True)
             * jnp.max(jnp.abs(kg_ref[...]), axis=-1, keepdims=True))
    reach = jnp.minimum((F32_ZERO_EXP2 + 2.0 * bound) / slope + 1.0, float(seq)).astype(jnp.int32)[0, 0]

    n_st = tk // tq
    hw = 2 * HEAD_B

    @pl.when(i == 0)
    def _():
        kv_pos = lax.broadcasted_iota(jnp.int32, (tk, tq), 0)
        q_pos = lax.broadcasted_iota(jnp.int32, (tk, tq), 1)
        src_ref[...] = slope * (q_pos - kv_pos).astype(F32)
        key_feat = _bias_features(slope, tk, hw, key_side=True)
        feat_ref[0] = key_feat.astype(BF16)
        feat_ref[1] = (-key_feat).astype(BF16)
        feat_ref[2] = jnp.zeros((tk, hw), BF16)
        q_feat = _bias_features(slope, tq, hw, key_side=False).T.astype(BF16)
        for st in range(n_st):
            qs_ref[st, hw:, :tq] = q_feat
            qs_ref[st, hw:, tq:] = q_feat

    chan =lax.broadcasted_iota(jnp.int32, (hw, tq), 0)
    for st in range(n_st):
        q_t = q_ref[st * tq:(st + 1) * tq, :].astype(F32).T.astype(BF16)
        zero = jnp.zeros_like(q_t)
        qs_ref[st, :hw, :tq] = jnp.where(chan < HEAD_B, q_t, zero)
        qs_ref[st, :hw, tq:] = jnp.where(chan < HEAD_B, zero, q_t)
    q0 = i * tk
    j_lo = jnp.maximum(q0 - reach, 0) // tk
    j_hi = jnp.minimum(q0 + tk - 1 + reach, seq - 1) // tk + 1
    n_left = i - j_lo
    n_off = n_left + (j_hi - i - 1)

    def scores(j, side, st):
        kb = k_ref[pl.ds(pl.multiple_of(j * tk, tk), tk), :]
        return jnp.dot(jnp.concatenate([kb, feat_ref[side]], axis=1), qs_ref[st],
                       preferred_element_type=F32)

    def softmax_update(s, s_max, cst, st):
        m_old = m_ref[st]
        m_new = jnp.maximum(m_old, s_max + cst)
        alpha = jnp.exp2(m_old - m_new)
        p = jnp.exp2(s - (m_new - cst))
        m_ref[st] = m_new
        return alpha, p.astype(BF16)

    def block(n):
        j = jnp.clip(j_lo + n + (n >= n_left).astype(jnp.int32), 0, n_kv - 1)
        j = jnp.where(n < 0, i, j)
        return j, (j > i).astype(jnp.int32), jnp.logical_and(n >= 0, n < n_off)

    def qk_stage(n, slot):
        j, side, _ = block(n)
        for st in range(n_st):
            s = scores(j, side, st)
            s_ref[slot, st] = s
            mx_ref[slot, st] = jnp.max(s, axis=0, keepdims=True)

    def softmax_stage(n, slot):
        j, _, valid = block(n)
        for st in range(n_st):
            dist0 = jnp.abs(q0 + st * tq - j * tk).astype(F32)
            cst = jnp.where(valid, -slope * dist0, -1e30)
            alpha, p = softmax_update(s_ref[slot, st], mx_ref[slot, st], cst, st)
            al_ref[slot, st] = alpha
            p_ref[slot, st] = p

    def pv_stage(n, slot):
        j, _, _ = block(n)
        for st in range(n_st):
            acc_ref[st] = al_ref[slot, st] * acc_ref[st] + jnp.dot(vt_ref[j], p_ref[slot, st],
                                                                    preferred_element_type=F32)

    m_ref[...] = jnp.full(m_ref.shape, -1e30, F32)
    acc_ref[...] = jnp.zeros(acc_ref.shape, F32)
    diag_scores = [scores(i, 2, st) for st in range(n_st)]
    qk_stage(0, 0)
    src = src_ref[...]
    for st in range(n_st):
        bias = jnp.abs(src + slope * float(st * tq))
        s = diag_scores[st] - jnp.concatenate([bias, bias], axis=1)
        alpha, p = softmax_update(s, jnp.max(s, axis=0, keepdims=True), jnp.zeros((1, 1), F32), st)
        al_ref[1, st] = alpha
        p_ref[1, st] = p

    before_loop()

    def pipelined_pair(t, carry):
        n = 2 * t
        qk_stage(n + 1, 1)
        softmax_stage(n, 0)
        pv_stage(n - 1, 1)
        qk_stage(n + 2, 0)
        softmax_stage(n + 1, 1)
        pv_stage(n, 0)
        return carry

    n_pairs = n_off // 2
    lax.fori_loop(0, n_pairs, pipelined_pair, 0)
    last = 2 * n_pairs

    @pl.when(n_off % 2 == 1)
    def _():
        softmax_stage(last, 0)
        pv_stage(last - 1, 1)
        pv_stage(last, 0)

    @pl.when(n_off % 2 == 0)
    def _():
        pv_stage(last - 1, 1)

    lv = lv_ref[...]
    lam = (jnp.exp(jnp.sum(lv[0:1] * lv[1:2], axis=-1, keepdims=True))
           - jnp.exp(jnp.sum(lv[2:3] * lv[3:4], axis=-1, keepdims=True)) + lam_init)
    for st in range(n_st):
        acc = acc_ref[st, :hw, :]
        l = acc_ref[st, hw:hw + 1, :]
        o_t = acc[:, :tq] / l[:, :tq] - lam * (acc[:, tq:] / l[:, tq:])
        o_t = o_t * lax.rsqrt(jnp.mean(o_t * o_t, axis=0, keepdims=True) + NORM_EPS)
        o_ref[st * tq:(st + 1) * tq, :] = (o_t.T * sg_ref[...] * (1.0 - lam_init)).astype(o_ref.dtype)


def _mixers_body(q_ref, k_ref, vt_ref, lv_ref, sg_ref, qg_ref, kg_ref,
                 rf, vf, kkf, lwf, kf, af, rb, vb, kkb, lwb, kb, ab,
                 o_ref, yf_ref, yb_ref, *scratch, lam_init, n_scan):
    *attn_scratch, h_ref = scratch
    step = (pl.program_id(0) * pl.num_programs(1) + pl.program_id(1)) * pl.num_programs(2) + pl.program_id(2)

    @pl.when(step % n_scan == 0)
    def _():
        h_ref[...] = jnp.zeros(h_ref.shape, F32)

    _attn_body(q_ref, k_ref, vt_ref, lv_ref, sg_ref, qg_ref, kg_ref, o_ref, *attn_scratch, lam_init=lam_init,
               before_loop=lambda: _scan_step(rf, vf, kkf, lwf, kf, af, rb, vb, kkb, lwb, kb, ab,
                                              yf_ref, yb_ref, h_ref))


def _mixers(q, k, vt, lam_vecs, sub_g, qg, kg, r, v, kk, lw0, k0, a0, lw1, k1, a1, lam_init, batch, seq, layer):
    t = q.shape[0]
    tk = vt.shape[2]
    tq = ATTN_TQ
    n_st = tk // tq
    nq = seq // tk
    n_kv = seq // tk
    hw = 2 * HEAD_B
    rows = SCAN_CHUNKS_PER_STEP * CHUNK
    n_scan = seq // rows
    assert N_HEADS_B * nq == n_scan
    step = lambda b, h, i: (b * N_HEADS_B + h) * nq + i
    fwd = pl.BlockSpec((rows, D_A), lambda b, h, i: (step(b, h, i), 0))
    bwd = pl.BlockSpec((rows, D_A), lambda b, h, i: (
        (step(b, h, i) // n_scan) * n_scan + n_scan - 1 - step(b, h, i) % n_scan, 0))
    y = jax.ShapeDtypeStruct((t, D_A), F32)
    return pl.pallas_call(
        functools.partial(_mixers_body, lam_init=lam_init, n_scan=n_scan),
        grid=(batch, N_HEADS_B, nq),
        in_specs=[pl.BlockSpec((tk, hw), lambda b, h, i: (b * nq + i, h)),
                  pl.BlockSpec((seq, hw), lambda b, h, i: (b, h)),
                  pl.BlockSpec((n_kv, V_ROWS, tk), lambda b, h, i: (b, h, 0)),
                  _layer_spec((4, HEAD_B), layer), _layer_spec((1, hw), layer),
                  _layer_spec((1, D_B), layer), _layer_spec((1, D_B), layer)] + [fwd] * 6 + [bwd] * 6,
        out_specs=[pl.BlockSpec((tk, hw), lambda b, h, i: (b * nq + i, h)), fwd, bwd],
        out_shape=[jax.ShapeDtypeStruct((t, D_B), BF16), y, y],
        scratch_shapes=[pltpu.VMEM((n_st, 2 * hw, 2 * tq), BF16),
                        pltpu.VMEM((3, tk, hw), BF16),
                        pltpu.VMEM((tk, tq), F32),
                        pltpu.VMEM((2, n_st, tk, 2 * tq), F32),
                        pltpu.VMEM((2, n_st, 1, 2 * tq), F32),
                        pltpu.VMEM((2, n_st, tk, 2 * tq), BF16),
                        pltpu.VMEM((2, n_st, 1, 2 * tq), F32),
                        pltpu.VMEM((n_st, 1, 2 * tq), F32),
                        pltpu.VMEM((n_st, V_ROWS, 2 * tq), F32),
                        pltpu.VMEM((2 * (D_A // LANES), LANES, LANES), F32)],
        compiler_params=_params(3),
        name="mixers",
    )(q, k, vt, lam_vecs, sub_g, qg, kg, r, v, kk, lw0, k0, a0, r, v, kk, lw1, k1, a1)


def _merge_body(yf_ref, yb_ref, gate_ref, bonus_ref, ob_ref, gt_ref, x_ref, lng_ref, lnb_ref, bd_ref,
                wb0_ref, wb1_ref, wo_ref, g2_ref, win_ref, wout_ref, o_ref):
    bd = bd_ref[...]
    y = yf_ref[...] + yb_ref[...]
    yc = y - _mm_split_lhs(y, bd)
    var = _mm(yc * yc, bd)
    yn = yc * lax.rsqrt(var + GN_EPS) * lng_ref[...] + lnb_ref[...]
    oa = (yn + bonus_ref[...]) * gate_ref[...]
    gts = gt_ref[...].astype(F32)
    merged = (gts[:, :D_MODEL] * _mm(oa, wb0_ref[...])
              + gts[:, D_MODEL:] * jnp.dot(ob_ref[...], wb1_ref[...], preferred_element_type=F32))
    o_ref[...] = _ffn_math(x_ref[...] + _mm(merged, wo_ref[...]), g2_ref, win_ref, wout_ref)


def _merge(yf, yb, gate, bonus, ob, gts, x2, ln_g, ln_b, bd_mean, wb, wo, g2, w_in, w_out, layer):
    t = x2.shape[0]
    tm = min(TOKEN_TILE, t)
    row = lambda width: pl.BlockSpec((tm, width), lambda i: (i, 0))
    return pl.pallas_call(
        _merge_body,
        grid=(t // tm,),
        in_specs=[row(D_A), row(D_A), row(D_A), row(D_A), row(D_B), row(2 * D_MODEL), row(D_MODEL),
                  _layer_spec((1, D_A), layer), _layer_spec((1, D_A), layer), _const_spec((D_A, D_A)),
                  _layer_spec((D_A, D_MODEL), layer, 0), _layer_spec((D_B, D_MODEL), layer, 1),
                  _layer_spec((D_MODEL, D_MODEL), layer),
                  _layer_spec((1, D_MODEL), layer), _layer_spec((D_MODEL, 2 * D_FF), layer),
                  _layer_spec((D_FF, D_MODEL), layer)],
        out_specs=row(D_MODEL),
        out_shape=jax.ShapeDtypeStruct((t, D_MODEL), F32),
        compiler_params=_params(1),
        name="merge",
    )(yf, yb, gate, bonus, ob, gts, x2, ln_g, ln_b, bd_mean, wb, wb, wo, g2, w_in, w_out)


def _block_diag2(w):
    z = jnp.zeros_like(w[:, 0])
    return jnp.concatenate([jnp.concatenate([w[:, 0], z], axis=2), jnp.concatenate([z, w[:, 1]], axis=2)],
                           axis=1)


def kernel(x, norm_ffn1, ffn1_in, ffn1_out, norm_mix, w_in, rwkv_mu, decay_w0, decay_w2, iclr_a0, iclr_a2,
           gate_g2, k_k, k_a, r_k, ln_x_g, ln_x_b, q_gain, k_gain, diff_lambda, subln_g, w_branch, w_out,
           norm_ffn2, ffn2_in, ffn2_out):
    batch, seq, d = x.shape
    depth = norm_ffn1.shape[0]
    assert seq % TOKEN_TILE == 0 and d == D_MODEL
    x2 = x.reshape(batch * seq, d)
    head_id = jnp.arange(D_A) // HEAD_A
    same = head_id[:, None] == head_id[None, :]
    bd_sum = same.astype(BF16)
    bd_mean = (same.astype(F32) / HEAD_A).astype(BF16)
    rows = lambda a: a.reshape(depth, 1, -1)
    bf = lambda a: a.astype(BF16)
    n_qk = D_B // HEAD_B
    qg = rows(jnp.tile(q_gain, (1, n_qk))) * (HEAD_B ** -0.5 * LOG2E)
    kg = rows(jnp.tile(k_gain, (1, n_qk)))
    ffn1 = (rows(norm_ffn1), bf(ffn1_in), bf(ffn1_out))
    ffn2 = (rows(norm_ffn2), bf(ffn2_in), bf(ffn2_out))
    proj_w = (rows(norm_mix), bf(w_in))
    prep_w = (rows(rwkv_mu), rows(decay_w0), bf(_block_diag2(decay_w2)), rows(iclr_a0),
              bf(_block_diag2(iclr_a2)), bf(gate_g2), rows(k_k), rows(k_a), rows(r_k))
    merge_w = (rows(ln_x_g), rows(ln_x_b), bd_mean, bf(w_branch), bf(w_out))
    sub_g = rows(subln_g)
    for l in range(depth):
        lam_init = 0.8 - 0.6 * math.exp(-0.3 * l)
        x2 = _ffn(x2, *ffn1, l)
        q, k, vt, gts, r, vv, kk, lw0, k0, a0, lw1, k1, a1, gate, bonus = _proj(
            x2, seq, *proj_w, qg, kg, bd_mean, *prep_w, bd_sum, l)
        ob, yf, yb = _mixers(q, k, vt, diff_lambda, sub_g, qg, kg, r, vv, kk, lw0, k0, a0, lw1, k1, a1,
                             lam_init, batch, seq, l)
        x2 = _merge(yf, yb, gate, bonus, ob, gts, x2, *merge_w, *ffn2, l)
    return x2.reshape(batch, seq, d)
```

```python
import functools
import math

import jax
import jax.numpy as jnp
from jax import lax
from jax.experimental import pallas as pl
from jax.experimental.pallas import tpu as pltpu

F32 = jnp.float32
BF16 = jnp.bfloat16

D_MODEL = 1024
D_A = 512
HEAD_A = 64
LORA = 64
LORA_GATE = 128
C_RWKV = 3 * D_A + 2 * LORA + 2 * LORA + LORA_GATE
D_B = 512
HEAD_B = 64
N_HEADS_B = D_B // (2 * HEAD_B)
D_FF = 2816
N_BRANCH = 2
DECAY_SCALE = math.exp(-0.5)
GN_EPS = 64e-5
NORM_EPS = 1e-6
LOG2E = math.log2(math.e)
F32_ZERO_EXP2 = 150.0
ROUNDING_MARGIN = 1.02

LANES = 128
SUBLANES = 8
BF16_ROWS = 16
VMEM_LIMIT = 56 * 1024 * 1024

TOKEN_TILE = 512
MXU_WIDTH = 256
FF_SPLITS = (0, 6 * MXU_WIDTH, D_FF)
CHUNK = 64
SCAN_CHUNKS_PER_STEP = 2
ATTN_TQ = 256
V_ROWS = 2 * HEAD_B + 16

_NT = (((1,), (1,)), ((), ()))


def _mm(a, b):
    return jnp.dot(a.astype(BF16), b.astype(BF16), preferred_element_type=F32)


def _mm_nt(a, b):
    return lax.dot_general(a.astype(BF16), b.astype(BF16), _NT, preferred_element_type=F32)


def _split_bf16(x, n):
    parts = []
    for _ in range(n - 1):
        hi = x.astype(BF16)
        parts.append(hi)
        x = x - hi.astype(F32)
    parts.append(x.astype(BF16))
    return parts


def _mm_split_lhs(x, w, n=2):
    return sum(jnp.dot(p, w, preferred_element_type=F32) for p in _split_bf16(x, n))


def _rms_norm(x, g):
    return x * lax.rsqrt(jnp.mean(x * x, axis=-1, keepdims=True) + NORM_EPS) * g


def _const_spec(shape):
    return pl.BlockSpec(shape, lambda *_: (0,) * len(shape), pipeline_mode=pl.Buffered(1))


def _layer_spec(shape, layer, *lead):
    index = (layer,) + lead + (0,) * len(shape)
    return pl.BlockSpec((None,) * (1 + len(lead)) + shape, lambda *_: index, pipeline_mode=pl.Buffered(1))


def _weight_spec(shape, *lead):
    return _layer_spec(shape, 0, *lead)


def _params(n_axes):
    return pltpu.CompilerParams(dimension_semantics=("arbitrary",) * n_axes,
                                vmem_limit_bytes=VMEM_LIMIT)


def _ffn_math(x, g_ref, win_ref, wout_ref):
    h = _rms_norm(x, g_ref[...]).astype(BF16)
    acc = jnp.zeros(x.shape, F32)
    for lo, hi in zip(FF_SPLITS[:-1], FF_SPLITS[1:]):
        gate = jnp.dot(h, win_ref[:, lo:hi], preferred_element_type=F32)
        up = jnp.dot(h, win_ref[:, D_FF + lo:D_FF + hi], preferred_element_type=F32)
        act = (gate * jax.nn.sigmoid(gate) * up).astype(BF16)
        acc = acc + jnp.dot(act, wout_ref[lo:hi, :], preferred_element_type=F32)
    return x + 0.5 * acc


def _ffn_body(x_ref, g_ref, win_ref, wout_ref, o_ref):
    o_ref[...] = _ffn_math(x_ref[...], g_ref, win_ref, wout_ref)


def _ffn(x2, g, w_in, w_out, layer):
    t = x2.shape[0]
    tm = min(TOKEN_TILE, t)
    return pl.pallas_call(
        _ffn_body,
        grid=(t // tm,),
        in_specs=[pl.BlockSpec((tm, D_MODEL), lambda i: (i, 0)),
                  _layer_spec((1, D_MODEL), layer),
                  _weight_spec((D_MODEL, 2 * D_FF)),
                  _weight_spec((D_FF, D_MODEL))],
        out_specs=pl.BlockSpec((tm, D_MODEL), lambda i: (i, 0)),
        out_shape=jax.ShapeDtypeStruct((t, D_MODEL), F32),
        compiler_params=_params(1),
        name="ffn",
    )(x2, g, w_in, w_out)


_Q0 = C_RWKV
_K0 = C_RWKV + D_B
_V0 = C_RWKV + 2 * D_B
_G0 = C_RWKV + 3 * D_B


def _proj_body(x_ref, xp_ref, xn_ref, g_ref, w_ref, qg_ref, kg_ref, bdm_ref, mu_ref, w0_ref, w2_ref, a0_ref,
               a2_ref, g2_ref, kk_ref, ka_ref, rk_ref, bds_ref,
               q_ref, k_ref, vt_ref, gt_ref, r_out, v_out, kk_out, lw0_out, k0_out, a0_out, lw1_out, k1_out,
               a1_out, gate_out, bonus_out, *, seq):
    i = pl.program_id(0)
    g = g_ref[...]
    h = _rms_norm(x_ref[...], g).astype(BF16)
    tm = h.shape[0]

    bd = bdm_ref[...]
    q = jnp.dot(h, w_ref[:, _Q0:_K0], preferred_element_type=F32)
    q_ref[...] = (q * lax.rsqrt(_mm(q * q, bd) + NORM_EPS) * qg_ref[...]).astype(BF16)
    k = jnp.dot(h, w_ref[:, _K0:_V0], preferred_element_type=F32)
    k_ref[...] = (k * lax.rsqrt(_mm(k * k, bd) + NORM_EPS) * kg_ref[...]).astype(BF16)
    vt = jnp.dot(h, w_ref[:, _V0:_G0], preferred_element_type=F32).T.astype(BF16)
    hw = 2 * HEAD_B
    ones = jnp.ones((V_ROWS - hw, vt.shape[1]), BF16)
    vt_ref[0] = jnp.concatenate([part for hd in range(N_HEADS_B)
                                 for part in (vt[hd * hw:(hd + 1) * hw], ones)], axis=0)
    gt_ref[...] = jax.nn.sigmoid(jnp.dot(h, w_ref[:, _G0:], preferred_element_type=F32)).astype(BF16)

    w_rwkv = w_ref[:, :C_RWKV]
    p = jnp.dot(h, w_rwkv, preferred_element_type=F32)
    edge = jnp.concatenate([xp_ref[...], xn_ref[...]], axis=0)
    p_edge = jnp.dot(_rms_norm(edge, g).astype(BF16), w_rwkv, preferred_element_type=F32)
    keep_prev = jnp.where((i * tm) % seq == 0, 0.0, 1.0)
    keep_next = jnp.where(((i + 1) * tm) % seq == 0, 0.0, 1.0)
    row = lax.broadcasted_iota(jnp.int32, p.shape, 0)
    prev = jnp.where(row == 0, keep_prev * p_edge[SUBLANES - 1:SUBLANES, :], pltpu.roll(p, 1, 0))
    nxt = jnp.where(row == tm - 1, keep_next * p_edge[SUBLANES:SUBLANES + 1, :], pltpu.roll(p, tm - 1, 0))
    p = p + mu_ref[...] * (0.5 * (prev + nxt) - p)

    r = p[:, :D_A]
    k = p[:, D_A:2 * D_A]
    v = p[:, 2 * D_A:3 * D_A]
    dw = p[:, 3 * D_A:3 * D_A + 2 * LORA]
    da = p[:, 3 * D_A + 2 * LORA:3 * D_A + 4 * LORA]
    dg = p[:, 3 * D_A + 4 * LORA:]
    bd = bds_ref[...]

    logw = -DECAY_SCALE * jax.nn.sigmoid(w0_ref[...] + _mm(jnp.tanh(dw), w2_ref[...]))
    iclr = jax.nn.sigmoid(a0_ref[...] + _mm(da, a2_ref[...]))
    gate_out[...] = _mm(jax.nn.sigmoid(dg), g2_ref[...])

    kk = k * kk_ref[...]
    norm = jnp.sqrt(_mm(kk * kk, bd))
    kk_out[...] = (kk / jnp.maximum(norm, 1e-12)).astype(kk_out.dtype)

    ka = ka_ref[...]
    a_f = iclr[:, :D_A]
    a_b = iclr[:, D_A:]
    k_f = k * (1.0 + (a_f - 1.0) * ka)
    k_b = k * (1.0 + (a_b - 1.0) * ka)
    bonus = _mm_split_lhs(r * rk_ref[...] * (k_f + k_b), bd)
    r_out[...] = r.astype(r_out.dtype)
    v_out[...] = v.astype(v_out.dtype)
    lw0_out[...] = logw[:, :D_A]
    k0_out[...] = k_f.astype(k0_out.dtype)
    a0_out[...] = a_f.astype(a0_out.dtype)
    lw1_out[...] = logw[:, D_A:]
    k1_out[...] = k_b.astype(k1_out.dtype)
    a1_out[...] = a_b.astype(a1_out.dtype)
    bonus_out[...] = bonus * v


def _proj(x2, seq, g, w, qg, kg, bd_mean, mu, w0, w2cat, a0, a2cat, g2, k_k, k_a, r_k, bd_sum, layer):
    t = x2.shape[0]
    tm = TOKEN_TILE
    c_in = w.shape[2]
    nb = tm // SUBLANES
    last = t // SUBLANES - 1
    row = lambda width: pl.BlockSpec((tm, width), lambda i: (i, 0))
    wide = jax.ShapeDtypeStruct((t, D_A), F32)
    half = jax.ShapeDtypeStruct((t, D_A), BF16)
    return pl.pallas_call(
        functools.partial(_proj_body, seq=seq),
        grid=(t // tm,),
        in_specs=[row(D_MODEL),
                  pl.BlockSpec((SUBLANES, D_MODEL), lambda i: (jnp.maximum(i * nb - 1, 0), 0)),
                  pl.BlockSpec((SUBLANES, D_MODEL), lambda i: (jnp.minimum((i + 1) * nb, last), 0)),
                  _layer_spec((1, D_MODEL), layer), _weight_spec((D_MODEL, c_in)),
                  _layer_spec((1, D_B), layer), _layer_spec((1, D_B), layer), _const_spec((D_B, D_B)),
                  _layer_spec((1, C_RWKV), layer),
                  _layer_spec((1, 2 * D_A), layer), _layer_spec((2 * LORA, 2 * D_A), layer),
                  _layer_spec((1, 2 * D_A), layer), _layer_spec((2 * LORA, 2 * D_A), layer),
                  _layer_spec((LORA_GATE, D_A), layer),
                  _layer_spec((1, D_A), layer), _layer_spec((1, D_A), layer), _layer_spec((1, D_A), layer),
                  _const_spec((D_A, D_A))],
        out_specs=[row(D_B), row(D_B),
                   pl.BlockSpec((1, N_HEADS_B * V_ROWS, tm), lambda i: (i, 0, 0)), row(2 * D_MODEL)]
                  + [row(D_A)] * 11,
        out_shape=[jax.ShapeDtypeStruct((t, D_B), BF16),
                   jax.ShapeDtypeStruct((t, D_B), BF16),
                   jax.ShapeDtypeStruct((t // tm, N_HEADS_B * V_ROWS, tm), BF16),
                   jax.ShapeDtypeStruct((t, 2 * D_MODEL), BF16),
                   half, half, half, wide, half, half, wide, half, half, wide, wide],
        compiler_params=_params(1),
        name="proj",
    )(x2, x2, x2, g, w, qg, kg, bd_mean, mu, w0, w2cat, a0, a2cat, g2, k_k, k_a, r_k, bd_sum)


def _scan_step(rf, vf, kkf, lwf, kf, af, rb, vb, kkb, lwb, kb, ab, yf_ref, yb_ref, h_ref):
    L = CHUNK
    n_sub = rf.shape[0] // L
    n_pairs = D_A // LANES
    ti = lax.broadcasted_iota(jnp.int32, (L, L), 0)
    tj = lax.broadcasted_iota(jnp.int32, (L, L), 1)
    pi = lax.broadcasted_iota(jnp.int32, (L, LANES), 0)
    lane = lax.broadcasted_iota(jnp.int32, (L, LANES), 1)
    pj = lane & (HEAD_A - 1)
    lane_lo = lane < HEAD_A
    eye = jnp.where(pj == pi, 1.0, 0.0)
    blk = lambda s: (pi // s) == (pj // s)
    row2 = lax.broadcasted_iota(jnp.int32, (2 * L, LANES), 0)
    col2 = lax.broadcasted_iota(jnp.int32, (2 * L, LANES), 1)
    same_head = (row2 // HEAD_A) == (col2 // HEAD_A)
    diag2 = row2 == col2
    zeros_l = jnp.zeros((L, LANES), F32)
    zeros_2l = jnp.zeros((2 * L, LANES), F32)

    def stack(x):
        return jnp.concatenate([jnp.where(lane_lo, x, 0.0), jnp.where(lane_lo, 0.0, x)], axis=0)

    def mmp(x, y):
        return _mm(x, stack(y))

    cat = jnp.concatenate
    chains = []
    for reverse, refs in ((False, (rf, vf, kkf, lwf, kf, af)), (True, (rb, vb, kkb, lwb, kb, ab))):
        tri = ((tj >= ti) if reverse else (tj <= ti)).astype(BF16)
        strict = (pj > pi) if reverse else (pj < pi)
        incl = (pj >= pi) if reverse else (pj <= pi)
        steps = []
        for sub in (range(n_sub - 1, -1, -1) if reverse else range(n_sub)):
            rows = slice(sub * L, (sub + 1) * L)
            r, v, kk, lw, k, a = (ref[rows, :].astype(F32) for ref in refs)
            G = sum(jnp.dot(tri, part, preferred_element_type=F32) for part in _split_bf16(lw, 3))
            g_end = G[0:1] if reverse else G[L - 1:L]
            gam_end = jnp.exp(g_end)
            inv_gam = jnp.exp(-G)
            rel_end = jnp.exp(g_end - G)
            b = kk * a
            full = dict(At=-kk * jnp.exp(G - lw), Bt=b * inv_gam, Kt=k * inv_gam, Rt=r * jnp.exp(G),
                        Kh=k * rel_end, Bh=b * rel_end, v=v)
            pairs = []
            for p in range(n_pairs):
                sl = slice(p * LANES, (p + 1) * LANES)
                c = {name: val[:, sl] for name, val in full.items()}
                c.update(strict=strict, incl=incl, gam_end=gam_end[:, sl], rows=rows)
                pairs.append(c)
            steps.append(pairs)
        chains.append(steps)
    flat = [c for steps in chains for pairs in steps for c in pairs]

    for c in flat:
        a4 = _mm_nt(cat([c["At"], c["Rt"]], axis=0), cat([stack(c["Bt"]), stack(c["Kt"])], axis=0))
        c["A_ab"] = jnp.where(c["strict"], a4[:L, :LANES], 0.0)
        c["A_ak"] = jnp.where(c["strict"], a4[:L, LANES:], 0.0)
        c["A_rb"] = jnp.where(c["incl"], a4[L:, :LANES], 0.0)
        c["A_rk"] = jnp.where(c["incl"], a4[L:, LANES:], 0.0)
    for c in flat:
        c["A0"] = jnp.where(blk(8), c["A_ab"], 0.0)
        c["A2"] = mmp(c["A0"], c["A0"])
    for c in flat:
        t1 = eye + c["A0"]
        both = mmp(cat([c["A2"], t1], axis=0), c["A2"])
        c["A4"] = both[:L]
        c["T"] = t1 + both[L:]
    for c in flat:
        c["T"] = c["T"] + mmp(c["T"], c["A4"])
    s = 8
    while s < L:
        outer = blk(2 * s) & jnp.logical_not(blk(s))
        for c in flat:
            c["TO"] = mmp(c["T"], jnp.where(outer, c["A_ab"], 0.0))
        for c in flat:
            c["T"] = c["T"] + mmp(c["TO"], c["T"])
        s *= 2
    for c in flat:
        c["AkV"] = mmp(c["A_ak"], c["v"])
    for c in flat:
        wu = _mm(c["T"], cat([stack(c["At"]), stack(c["AkV"])], axis=1))
        c["W"], c["U0"] = wu[:, :LANES], wu[:, LANES:]
    for c in flat:
        kb_t = cat([c["Kh"], c["Bh"]], axis=0).T
        mn = _mm(kb_t, cat([cat([zeros_l, c["v"]], axis=1), cat([c["W"], c["U0"]], axis=1)], axis=0))
        c["M"] = jnp.where(same_head, mn[:, :LANES], 0.0) + jnp.where(diag2, c["gam_end"], 0.0)
        c["N"] = jnp.where(same_head, mn[:, LANES:], 0.0)
    for c in flat:
        qy = _mm(cat([c["A_rb"], c["A_rk"]], axis=1),
                 cat([cat([stack(c["W"]), stack(c["U0"])], axis=1),
                      cat([zeros_2l, stack(c["v"])], axis=1)], axis=0))
        c["Q"] = c["Rt"] + qy[:, :LANES]
        c["Y0"] = qy[:, LANES:]
    for d, (steps, y_ref) in enumerate(zip(chains, (yf_ref, yb_ref))):
        for pairs in steps:
            ys = []
            for p, c in enumerate(pairs):
                qm_h = _mm(cat([c["Q"], c["M"]], axis=0), h_ref[d * n_pairs + p])
                ys.append(qm_h[:L] + c["Y0"])
                h_ref[d * n_pairs + p] = qm_h[L:] + c["N"]
            y_ref[pairs[0]["rows"], :] = cat(ys, axis=1)


def _bias_features(slope, n, width, key_side):
    pos = lax.broadcasted_iota(jnp.int32, (n, width), 0).astype(F32)
    lane = lax.broadcasted_iota(jnp.int32, (n, width), 1)
    hi, mid, lo = (t.astype(F32) for t in _split_bf16(slope * pos, 3))
    terms = jnp.where(lane % 3 == 0, hi, jnp.where(lane % 3 == 1, mid, lo))
    if key_side:
        return jnp.where(lane < 3, -1.0, jnp.where(lane < 6, terms, 0.0))
    return jnp.where(lane < 3, terms, jnp.where(lane < 6, 1.0, 0.0))


def _attn_body(q_ref, k_ref, vt_ref, lv_ref, sg_ref, qg_ref, kg_ref, o_ref, qs_ref, feat_ref, src_ref,
               s_ref, mx_ref, p_ref, al_ref, m_ref, acc_ref, *, lam_init, before_loop):
    tq = src_ref.shape[1]
    n_kv, _, tk = vt_ref.shape
    seq = n_kv * tk
    h = pl.program_id(1)
    i = pl.program_id(2)
    slope_bits = (127 - 2 * (h + 1)) << 23
    slope = (lax.bitcast_convert_type(jnp.full((1, 1), slope_bits, jnp.int32), jnp.float32).astype(F32)
             * LOG2E)
    bound =(HEAD_B * ROUNDING_MARGIN * jnp.max(jnp.abs(qg_ref[...]), axis=-1, keepdims=True)
             * jnp.max(jnp.abs(kg_ref[...]), axis=-1, keepdims=True))
    reach = jnp.minimum((F32_ZERO_EXP2 + 2.0 * bound) / slope + 1.0, float(seq)).astype(jnp.int32)[0, 0]

    n_st = tk // tq
    hw = 2 * HEAD_B

    @pl.when(i == 0)
    def _():
        kv_pos = lax.broadcasted_iota(jnp.int32, (tk, tq), 0)
        q_pos = lax.broadcasted_iota(jnp.int32, (tk, tq), 1)
        src_ref[...] = slope * (q_pos - kv_pos).astype(F32)
        key_feat = _bias_features(slope, tk, hw, key_side=True)
        feat_ref[0] = key_feat.astype(BF16)
        feat_ref[1] = (-key_feat).astype(BF16)
        feat_ref[2] = jnp.zeros((tk, hw), BF16)
        q_feat = _bias_features(slope, tq, hw, key_side=False).T.astype(BF16)
        for st in range(n_st):
            qs_ref[st, hw:, :tq] = q_feat
            qs_ref[st, hw:, tq:] = q_feat

    chan =lax.broadcasted_iota(jnp.int32, (hw, tq), 0)
    for st in range(n_st):
        q_t = q_ref[st * tq:(st + 1) * tq, :].astype(F32).T.astype(BF16)
        zero = jnp.zeros_like(q_t)
        qs_ref[st, :hw, :tq] = jnp.where(chan < HEAD_B, q_t, zero)
        qs_ref[st, :hw, tq:] = jnp.where(chan < HEAD_B, zero, q_t)
    q0 = i * tk
    j_lo = jnp.maximum(q0 - reach, 0) // tk
    j_hi = jnp.minimum(q0 + tk - 1 + reach, seq - 1) // tk + 1
    n_left = i - j_lo
    n_off = n_left + (j_hi - i - 1)

    def scores(j, side, st):
        kb = k_ref[pl.ds(pl.multiple_of(j * tk, tk), tk), :]
        return jnp.dot(jnp.concatenate([kb, feat_ref[side]], axis=1), qs_ref[st],
                       preferred_element_type=F32)

    def softmax_update(s, s_max, cst, st):
        m_old = m_ref[st]
        m_new = jnp.maximum(m_old, s_max + cst)
        alpha = jnp.exp2(m_old - m_new)
        p = jnp.exp2(s - (m_new - cst))
        m_ref[st] = m_new
        return alpha, p.astype(BF16)

    def block(n):
        j = jnp.clip(j_lo + n + (n >= n_left).astype(jnp.int32), 0, n_kv - 1)
        j = jnp.where(n < 0, i, j)
        return j, (j > i).astype(jnp.int32), jnp.logical_and(n >= 0, n < n_off)

    def qk_stage(n, slot):
        j, side, _ = block(n)
        for st in range(n_st):
            s = scores(j, side, st)
            s_ref[slot, st] = s
            mx_ref[slot, st] = jnp.max(s, axis=0, keepdims=True)

    def softmax_stage(n, slot):
        j, _, valid = block(n)
        for st in range(n_st):
            dist0 = jnp.abs(q0 + st * tq - j * tk).astype(F32)
            cst = jnp.where(valid, -slope * dist0, -1e30)
            alpha, p = softmax_update(s_ref[slot, st], mx_ref[slot, st], cst, st)
            al_ref[slot, st] = alpha
            p_ref[slot, st] = p

    def pv_stage(n, slot):
        j, _, _ = block(n)
        for st in range(n_st):
            acc_ref[st] = al_ref[slot, st] * acc_ref[st] + jnp.dot(vt_ref[j], p_ref[slot, st],
                                                                    preferred_element_type=F32)

    m_ref[...] = jnp.full(m_ref.shape, -1e30, F32)
    acc_ref[...] = jnp.zeros(acc_ref.shape, F32)
    diag_scores = [scores(i, 2, st) for st in range(n_st)]
    qk_stage(0, 0)
    src = src_ref[...]
    for st in range(n_st):
        bias = jnp.abs(src + slope * float(st * tq))
        s = diag_scores[st] - jnp.concatenate([bias, bias], axis=1)
        alpha, p = softmax_update(s, jnp.max(s, axis=0, keepdims=True), jnp.zeros((1, 1), F32), st)
        al_ref[1, st] = alpha
        p_ref[1, st] = p

    before_loop()

    def pipelined_pair(t, carry):
        n = 2 * t
        qk_stage(n + 1, 1)
        softmax_stage(n, 0)
        pv_stage(n - 1, 1)
        qk_stage(n + 2, 0)
        softmax_stage(n + 1, 1)
        pv_stage(n, 0)
        return carry

    n_pairs = n_off // 2
    lax.fori_loop(0, n_pairs, pipelined_pair, 0)
    last = 2 * n_pairs

    @pl.when(n_off % 2 == 1)
    def _():
        softmax_stage(last, 0)
        pv_stage(last - 1, 1)
        pv_stage(last, 0)

    @pl.when(n_off % 2 == 0)
    def _():
        pv_stage(last - 1, 1)

    lv = lv_ref[...]
    lam = (jnp.exp(jnp.sum(lv[0:1] * lv[1:2], axis=-1, keepdims=True))
           - jnp.exp(jnp.sum(lv[2:3] * lv[3:4], axis=-1, keepdims=True)) + lam_init)
    for st in range(n_st):
        acc = acc_ref[st, :hw, :]
        l = acc_ref[st, hw:hw + 1, :]
        o_t = acc[:, :tq] / l[:, :tq] - lam * (acc[:, tq:] / l[:, tq:])
        o_t = o_t * lax.rsqrt(jnp.mean(o_t * o_t, axis=0, keepdims=True) + NORM_EPS)
        o_ref[st * tq:(st + 1) * tq, :] = (o_t.T * sg_ref[...] * (1.0 - lam_init)).astype(o_ref.dtype)


def _mixers_body(q_ref, k_ref, vt_ref, lv_ref, sg_ref, qg_ref, kg_ref,
                 rf, vf, kkf, lwf, kf, af, rb, vb, kkb, lwb, kb, ab, *rest, lam_init, n_scan, n_cast):
    cast_in, (o_ref, yf_ref, yb_ref), rest = rest[:n_cast], rest[n_cast:n_cast + 3], rest[n_cast + 3:]
    cast_out, scratch = rest[:n_cast], rest[n_cast:]
    *attn_scratch, h_ref = scratch
    step = (pl.program_id(0) * pl.num_programs(1) + pl.program_id(1)) * pl.num_programs(2) + pl.program_id(2)

    @pl.when(step % n_scan == 0)
    def _():
        h_ref[...] = jnp.zeros(h_ref.shape, F32)

    def side_work():
        _scan_step(rf, vf, kkf, lwf, kf, af, rb, vb, kkb, lwb, kb, ab, yf_ref, yb_ref, h_ref)
        for src, dst in zip(cast_in, cast_out):
            dst[...] = src[...].astype(BF16)

    _attn_body(q_ref, k_ref, vt_ref, lv_ref, sg_ref, qg_ref, kg_ref, o_ref, *attn_scratch, lam_init=lam_init,
               before_loop=side_work)


def _mixers(q, k, vt, lam_vecs, sub_g, qg, kg, r, v, kk, lw0, k0, a0, lw1, k1, a1, lam_init, batch, seq, layer,
            next_weights):
    t = q.shape[0]
    tk = vt.shape[2]
    tq = ATTN_TQ
    n_st = tk // tq
    nq = seq // tk
    n_kv = seq // tk
    hw = 2 * HEAD_B
    rows = SCAN_CHUNKS_PER_STEP * CHUNK
    n_scan = seq // rows
    assert N_HEADS_B * nq == n_scan
    step = lambda b, h, i: (b * N_HEADS_B + h) * nq + i
    fwd = pl.BlockSpec((rows, D_A), lambda b, h, i: (step(b, h, i), 0))
    bwd = pl.BlockSpec((rows, D_A), lambda b, h, i: (
        (step(b, h, i) // n_scan) * n_scan + n_scan - 1 - step(b, h, i) % n_scan, 0))
    y = jax.ShapeDtypeStruct((t, D_A), F32)
    n_steps = batch * N_HEADS_B * nq
    cast_in, cast_out, cast_shapes = [], [], []
    for w in next_weights:
        _, w_rows, w_cols = w.shape
        rb = next(r for r in range(BF16_ROWS, w_rows + 1, BF16_ROWS) if w_rows % r == 0 and w_rows // r <= n_steps)
        blk = lambda b, h, i, last=w_rows // rb - 1: jnp.minimum(step(b, h, i), last)
        cast_in.append(pl.BlockSpec((None, rb, w_cols), lambda b, h, i, blk=blk: (layer + 1, blk(b, h, i), 0)))
        cast_out.append(pl.BlockSpec((rb, w_cols), lambda b, h, i, blk=blk: (blk(b, h, i), 0)))
        cast_shapes.append(jax.ShapeDtypeStruct((w_rows, w_cols), BF16))
    return pl.pallas_call(
        functools.partial(_mixers_body, lam_init=lam_init, n_scan=n_scan, n_cast=len(next_weights)),
        grid=(batch, N_HEADS_B, nq),
        in_specs=[pl.BlockSpec((tk, hw), lambda b, h, i: (b * nq + i, h)),
                  pl.BlockSpec((seq, hw), lambda b, h, i: (b, h)),
                  pl.BlockSpec((n_kv, V_ROWS, tk), lambda b, h, i: (b, h, 0)),
                  _layer_spec((4, HEAD_B), layer), _layer_spec((1, hw), layer),
                  _layer_spec((1, D_B), layer), _layer_spec((1, D_B), layer)] + [fwd] * 6 + [bwd] * 6 + cast_in,
        out_specs=[pl.BlockSpec((tk, hw), lambda b, h, i: (b * nq + i, h)), fwd, bwd] + cast_out,
        out_shape=[jax.ShapeDtypeStruct((t, D_B), BF16), y, y] + cast_shapes,
        scratch_shapes=[pltpu.VMEM((n_st, 2 * hw, 2 * tq), BF16),
                        pltpu.VMEM((3, tk, hw), BF16),
                        pltpu.VMEM((tk, tq), F32),
                        pltpu.VMEM((2, n_st, tk, 2 * tq), F32),
                        pltpu.VMEM((2, n_st, 1, 2 * tq), F32),
                        pltpu.VMEM((2, n_st, tk, 2 * tq), BF16),
                        pltpu.VMEM((2, n_st, 1, 2 * tq), F32),
                        pltpu.VMEM((n_st, 1, 2 * tq), F32),
                        pltpu.VMEM((n_st, V_ROWS, 2 * tq), F32),
                        pltpu.VMEM((2 * (D_A // LANES), LANES, LANES), F32)],
        compiler_params=_params(3),
        name="mixers",
    )(q, k, vt, lam_vecs, sub_g, qg, kg, r, v, kk, lw0, k0, a0, r, v, kk, lw1, k1, a1, *next_weights)


def _merge_body(yf_ref, yb_ref, gate_ref, bonus_ref, ob_ref, gt_ref, x_ref, lng_ref, lnb_ref, bd_ref,
                wb0_ref, wb1_ref, wo_ref, g2_ref, win_ref, wout_ref, o_ref):
    bd = bd_ref[...]
    y = yf_ref[...] + yb_ref[...]
    yc = y - _mm_split_lhs(y, bd)
    var = _mm(yc * yc, bd)
    yn = yc * lax.rsqrt(var + GN_EPS) * lng_ref[...] + lnb_ref[...]
    oa = (yn + bonus_ref[...]) * gate_ref[...]
    gts = gt_ref[...].astype(F32)
    merged = (gts[:, :D_MODEL] * _mm(oa, wb0_ref[...])
              + gts[:, D_MODEL:] * jnp.dot(ob_ref[...], wb1_ref[...], preferred_element_type=F32))
    o_ref[...] = _ffn_math(x_ref[...] + _mm(merged, wo_ref[...]), g2_ref, win_ref, wout_ref)


def _merge(yf, yb, gate, bonus, ob, gts, x2, ln_g, ln_b, bd_mean, wb, wo, g2, w_in, w_out, layer):
    t = x2.shape[0]
    tm = min(TOKEN_TILE, t)
    row = lambda width: pl.BlockSpec((tm, width), lambda i: (i, 0))
    return pl.pallas_call(
        _merge_body,
        grid=(t // tm,),
        in_specs=[row(D_A), row(D_A), row(D_A), row(D_A), row(D_B), row(2 * D_MODEL), row(D_MODEL),
                  _layer_spec((1, D_A), layer), _layer_spec((1, D_A), layer), _const_spec((D_A, D_A)),
                  _weight_spec((D_A, D_MODEL), 0), _weight_spec((D_B, D_MODEL), 1),
                  _weight_spec((D_MODEL, D_MODEL)),
                  _layer_spec((1, D_MODEL), layer), _weight_spec((D_MODEL, 2 * D_FF)),
                  _weight_spec((D_FF, D_MODEL))],
        out_specs=row(D_MODEL),
        out_shape=jax.ShapeDtypeStruct((t, D_MODEL), F32),
        compiler_params=_params(1),
        name="merge",
    )(yf, yb, gate, bonus, ob, gts, x2, ln_g, ln_b, bd_mean, wb, wb, wo, g2, w_in, w_out)


def _block_diag2(w):
    z = jnp.zeros_like(w[:, 0])
    return jnp.concatenate([jnp.concatenate([w[:, 0], z], axis=2), jnp.concatenate([z, w[:, 1]], axis=2)],
                           axis=1)


def kernel(x, norm_ffn1, ffn1_in, ffn1_out, norm_mix, w_in, rwkv_mu, decay_w0, decay_w2, iclr_a0, iclr_a2,
           gate_g2, k_k, k_a, r_k, ln_x_g, ln_x_b, q_gain, k_gain, diff_lambda, subln_g, w_branch, w_out,
           norm_ffn2, ffn2_in, ffn2_out):
    batch, seq, d = x.shape
    depth = norm_ffn1.shape[0]
    assert seq % TOKEN_TILE == 0 and d == D_MODEL
    x2 = x.reshape(batch * seq, d)
    head_id = jnp.arange(D_A) // HEAD_A
    same = head_id[:, None] == head_id[None, :]
    bd_sum = same.astype(BF16)
    bd_mean = (same.astype(F32) / HEAD_A).astype(BF16)
    rows = lambda a: a.reshape(depth, 1, -1)
    bf = lambda a: a.astype(BF16)
    n_qk = D_B // HEAD_B
    qg = rows(jnp.tile(q_gain, (1, n_qk))) * (HEAD_B ** -0.5 * LOG2E)
    kg = rows(jnp.tile(k_gain, (1, n_qk)))
    big = (ffn1_in, ffn1_out, w_in, w_branch.reshape(depth, N_BRANCH * D_A, D_MODEL), w_out, ffn2_in, ffn2_out)
    weights = [bf(w[:1]) for w in big]
    prep_w = (rows(rwkv_mu), rows(decay_w0), bf(_block_diag2(decay_w2)), rows(iclr_a0),
              bf(_block_diag2(iclr_a2)), bf(gate_g2), rows(k_k), rows(k_a), rows(r_k))
    sub_g = rows(subln_g)
    for l in range(depth):
        lam_init = 0.8 - 0.6 * math.exp(-0.3 * l)
        f1_in, f1_out, w_proj, w_br, w_o, f2_in, f2_out = weights
        x2 = _ffn(x2, rows(norm_ffn1), f1_in, f1_out, l)
        q, k, vt, gts, r, vv, kk, lw0, k0, a0, lw1, k1, a1, gate, bonus = _proj(
            x2, seq, rows(norm_mix), w_proj, qg, kg, bd_mean, *prep_w, bd_sum, l)
        ob, yf, yb, *converted = _mixers(q, k, vt, diff_lambda, sub_g, qg, kg, r, vv, kk, lw0, k0, a0, lw1, k1, a1,
                                        lam_init, batch, seq, l, big if l + 1 < depth else ())
        x2 = _merge(yf, yb, gate, bonus, ob, gts, x2, rows(ln_x_g), rows(ln_x_b), bd_mean,
                    w_br.reshape(1, N_BRANCH, D_A, D_MODEL), w_o, rows(norm_ffn2), f2_in, f2_out, l)
        weights = [w[None] for w in converted]
    return x2.reshape(batch, seq, d)
```

```python
import functools
import math

import jax
import jax.numpy as jnp
from jax import lax
from jax.experimental import pallas as pl
from jax.experimental.pallas import tpu as pltpu

F32 = jnp.float32
BF16 = jnp.bfloat16

D_MODEL = 1024
D_A = 512
HEAD_A = 64
LORA = 64
LORA_GATE = 128
C_RWKV = 3 * D_A + 2 * LORA + 2 * LORA + LORA_GATE
D_B = 512
HEAD_B = 64
N_HEADS_B = D_B // (2 * HEAD_B)
D_FF = 2816
N_BRANCH = 2
DECAY_SCALE = math.exp(-0.5)
GN_EPS = 64e-5
NORM_EPS = 1e-6
LOG2E = math.log2(math.e)
F32_ZERO_EXP2 = 150.0
ROUNDING_MARGIN = 1.02

LANES = 128
SUBLANES = 8
BF16_ROWS = 16
VMEM_LIMIT = 56 * 1024 * 1024

TOKEN_TILE = 512
MXU_WIDTH = 256
FF_SPLITS = (0, 6 * MXU_WIDTH, D_FF)
CHUNK = 64
SCAN_CHUNKS_PER_STEP = 2
ATTN_TQ = 256
V_ROWS = 2 * HEAD_B + 16

_NT = (((1,), (1,)), ((), ()))


def _mm(a, b):
    return jnp.dot(a.astype(BF16), b.astype(BF16), preferred_element_type=F32)


def _mm_nt(a, b):
    return lax.dot_general(a.astype(BF16), b.astype(BF16), _NT, preferred_element_type=F32)


def _split_bf16(x, n):
    parts = []
    for _ in range(n - 1):
        hi = x.astype(BF16)
        parts.append(hi)
        x = x - hi.astype(F32)
    parts.append(x.astype(BF16))
    return parts


def _mm_split_lhs(x, w, n=2):
    return sum(jnp.dot(p, w, preferred_element_type=F32) for p in _split_bf16(x, n))


def _rms_norm(x, g):
    return x * lax.rsqrt(jnp.mean(x * x, axis=-1, keepdims=True) + NORM_EPS) * g


def _const_spec(shape):
    return pl.BlockSpec(shape, lambda *_: (0,) * len(shape), pipeline_mode=pl.Buffered(1))


def _layer_spec(shape, layer, *lead):
    index = (layer,) + lead + (0,) * len(shape)
    return pl.BlockSpec((None,) * (1 + len(lead)) + shape, lambda *_: index, pipeline_mode=pl.Buffered(1))


def _weight_spec(shape, *lead):
    return _layer_spec(shape, 0, *lead)


def _conversion_specs(weights, src_layer, n_steps, step):
    ins, outs, shapes = [], [], []
    for w in weights:
        _, w_rows, w_cols = w.shape
        rb = next(r for r in range(BF16_ROWS, w_rows + 1, BF16_ROWS) if w_rows % r == 0 and w_rows // r <= n_steps)
        blk = lambda *ids, last=w_rows // rb - 1: jnp.minimum(step(*ids), last)
        ins.append(pl.BlockSpec((None, rb, w_cols), lambda *ids, blk=blk: (src_layer, blk(*ids), 0)))
        outs.append(pl.BlockSpec((None, rb, w_cols), lambda *ids, blk=blk: (0, blk(*ids), 0)))
        shapes.append(jax.ShapeDtypeStruct((1, w_rows, w_cols), BF16))
    return ins, outs, shapes


def _convert(srcs, dsts):
    for src, dst in zip(srcs, dsts):
        dst[...] = src[...].astype(BF16)


def _params(n_axes):
    return pltpu.CompilerParams(dimension_semantics=("arbitrary",) * n_axes,
                                vmem_limit_bytes=VMEM_LIMIT)


def _ffn_math(x, g_ref, win_ref, wout_ref):
    h = _rms_norm(x, g_ref[...]).astype(BF16)
    acc = jnp.zeros(x.shape, F32)
    for lo, hi in zip(FF_SPLITS[:-1], FF_SPLITS[1:]):
        gate = jnp.dot(h, win_ref[:, lo:hi], preferred_element_type=F32)
        up = jnp.dot(h, win_ref[:, D_FF + lo:D_FF + hi], preferred_element_type=F32)
        act = (gate * jax.nn.sigmoid(gate) * up).astype(BF16)
        acc = acc + jnp.dot(act, wout_ref[lo:hi, :], preferred_element_type=F32)
    return x + 0.5 * acc


def _ffn_body(x_ref, g_ref, win_ref, wout_ref, *rest):
    n_conv = len(rest) // 2
    _convert(rest[:n_conv], rest[n_conv + 1:])
    rest[n_conv][...] = _ffn_math(x_ref[...], g_ref, win_ref, wout_ref)


def _ffn(x2, g, w_in, w_out, layer, convert=()):
    t = x2.shape[0]
    tm = min(TOKEN_TILE, t)
    conv_in, conv_out, conv_shapes = _conversion_specs(convert, layer, t // tm, lambda i: i)
    return pl.pallas_call(
        _ffn_body,
        grid=(t // tm,),
        in_specs=[pl.BlockSpec((tm, D_MODEL), lambda i: (i, 0)),
                  _layer_spec((1, D_MODEL), layer),
                  _weight_spec((D_MODEL, 2 * D_FF)),
                  _weight_spec((D_FF, D_MODEL))] + conv_in,
        out_specs=[pl.BlockSpec((tm, D_MODEL), lambda i: (i, 0))] + conv_out,
        out_shape=[jax.ShapeDtypeStruct((t, D_MODEL), F32)] + conv_shapes,
        compiler_params=_params(1),
        name="ffn",
    )(x2, g, w_in, w_out, *convert)


_Q0 = C_RWKV
_K0 = C_RWKV + D_B
_V0 = C_RWKV + 2 * D_B
_G0 = C_RWKV + 3 * D_B


def _proj_body(x_ref, xp_ref, xn_ref, g_ref, w_ref, qg_ref, kg_ref, bdm_ref, mu_ref, w0_ref, w2_ref, a0_ref,
               a2_ref, g2_ref, kk_ref, ka_ref, rk_ref, bds_ref,
               q_ref, k_ref, vt_ref, gt_ref, r_out, v_out, kk_out, lw0_out, k0_out, a0_out, lw1_out, k1_out,
               a1_out, gate_out, bonus_out, *, seq):
    i = pl.program_id(0)
    g = g_ref[...]
    h = _rms_norm(x_ref[...], g).astype(BF16)
    tm = h.shape[0]

    bd = bdm_ref[...]
    q = jnp.dot(h, w_ref[:, _Q0:_K0], preferred_element_type=F32)
    q_ref[...] = (q * lax.rsqrt(_mm(q * q, bd) + NORM_EPS) * qg_ref[...]).astype(BF16)
    k = jnp.dot(h, w_ref[:, _K0:_V0], preferred_element_type=F32)
    k_ref[...] = (k * lax.rsqrt(_mm(k * k, bd) + NORM_EPS) * kg_ref[...]).astype(BF16)
    vt = jnp.dot(h, w_ref[:, _V0:_G0], preferred_element_type=F32).T.astype(BF16)
    hw = 2 * HEAD_B
    ones = jnp.ones((V_ROWS - hw, vt.shape[1]), BF16)
    vt_ref[0] = jnp.concatenate([part for hd in range(N_HEADS_B)
                                 for part in (vt[hd * hw:(hd + 1) * hw], ones)], axis=0)
    gt_ref[...] = jax.nn.sigmoid(jnp.dot(h, w_ref[:, _G0:], preferred_element_type=F32)).astype(BF16)

    w_rwkv = w_ref[:, :C_RWKV]
    p = jnp.dot(h, w_rwkv, preferred_element_type=F32)
    edge = jnp.concatenate([xp_ref[...], xn_ref[...]], axis=0)
    p_edge = jnp.dot(_rms_norm(edge, g).astype(BF16), w_rwkv, preferred_element_type=F32)
    keep_prev = jnp.where((i * tm) % seq == 0, 0.0, 1.0)
    keep_next = jnp.where(((i + 1) * tm) % seq == 0, 0.0, 1.0)
    row = lax.broadcasted_iota(jnp.int32, p.shape, 0)
    prev = jnp.where(row == 0, keep_prev * p_edge[SUBLANES - 1:SUBLANES, :], pltpu.roll(p, 1, 0))
    nxt = jnp.where(row == tm - 1, keep_next * p_edge[SUBLANES:SUBLANES + 1, :], pltpu.roll(p, tm - 1, 0))
    p = p + mu_ref[...] * (0.5 * (prev + nxt) - p)

    r = p[:, :D_A]
    k = p[:, D_A:2 * D_A]
    v = p[:, 2 * D_A:3 * D_A]
    dw = p[:, 3 * D_A:3 * D_A + 2 * LORA]
    da = p[:, 3 * D_A + 2 * LORA:3 * D_A + 4 * LORA]
    dg = p[:, 3 * D_A + 4 * LORA:]
    bd = bds_ref[...]

    logw = -DECAY_SCALE * jax.nn.sigmoid(w0_ref[...] + _mm(jnp.tanh(dw), w2_ref[...]))
    iclr = jax.nn.sigmoid(a0_ref[...] + _mm(da, a2_ref[...]))
    gate_out[...] = _mm(jax.nn.sigmoid(dg), g2_ref[...])

    kk = k * kk_ref[...]
    norm = jnp.sqrt(_mm(kk * kk, bd))
    kk_out[...] = (kk / jnp.maximum(norm, 1e-12)).astype(kk_out.dtype)

    ka = ka_ref[...]
    a_f = iclr[:, :D_A]
    a_b = iclr[:, D_A:]
    k_f = k * (1.0 + (a_f - 1.0) * ka)
    k_b = k * (1.0 + (a_b - 1.0) * ka)
    bonus = _mm_split_lhs(r * rk_ref[...] * (k_f + k_b), bd)
    r_out[...] = r.astype(r_out.dtype)
    v_out[...] = v.astype(v_out.dtype)
    lw0_out[...] = logw[:, :D_A]
    k0_out[...] = k_f.astype(k0_out.dtype)
    a0_out[...] = a_f.astype(a0_out.dtype)
    lw1_out[...] = logw[:, D_A:]
    k1_out[...] = k_b.astype(k1_out.dtype)
    a1_out[...] = a_b.astype(a1_out.dtype)
    bonus_out[...] = bonus * v


def _proj(x2, seq, g, w, qg, kg, bd_mean, mu, w0, w2cat, a0, a2cat, g2, k_k, k_a, r_k, bd_sum, layer):
    t = x2.shape[0]
    tm = TOKEN_TILE
    c_in = w.shape[2]
    nb = tm // SUBLANES
    last = t // SUBLANES - 1
    row = lambda width: pl.BlockSpec((tm, width), lambda i: (i, 0))
    wide = jax.ShapeDtypeStruct((t, D_A), F32)
    half = jax.ShapeDtypeStruct((t, D_A), BF16)
    return pl.pallas_call(
        functools.partial(_proj_body, seq=seq),
        grid=(t // tm,),
        in_specs=[row(D_MODEL),
                  pl.BlockSpec((SUBLANES, D_MODEL), lambda i: (jnp.maximum(i * nb - 1, 0), 0)),
                  pl.BlockSpec((SUBLANES, D_MODEL), lambda i: (jnp.minimum((i + 1) * nb, last), 0)),
                  _layer_spec((1, D_MODEL), layer), _weight_spec((D_MODEL, c_in)),
                  _layer_spec((1, D_B), layer), _layer_spec((1, D_B), layer), _const_spec((D_B, D_B)),
                  _layer_spec((1, C_RWKV), layer),
                  _layer_spec((1, 2 * D_A), layer), _layer_spec((2 * LORA, 2 * D_A), layer),
                  _layer_spec((1, 2 * D_A), layer), _layer_spec((2 * LORA, 2 * D_A), layer),
                  _layer_spec((LORA_GATE, D_A), layer),
                  _layer_spec((1, D_A), layer), _layer_spec((1, D_A), layer), _layer_spec((1, D_A), layer),
                  _const_spec((D_A, D_A))],
        out_specs=[row(D_B), row(D_B),
                   pl.BlockSpec((1, N_HEADS_B * V_ROWS, tm), lambda i: (i, 0, 0)), row(2 * D_MODEL)]
                  + [row(D_A)] * 11,
        out_shape=[jax.ShapeDtypeStruct((t, D_B), BF16),
                   jax.ShapeDtypeStruct((t, D_B), BF16),
                   jax.ShapeDtypeStruct((t // tm, N_HEADS_B * V_ROWS, tm), BF16),
                   jax.ShapeDtypeStruct((t, 2 * D_MODEL), BF16),
                   half, half, half, wide, half, half, wide, half, half, wide, wide],
        compiler_params=_params(1),
        name="proj",
    )(x2, x2, x2, g, w, qg, kg, bd_mean, mu, w0, w2cat, a0, a2cat, g2, k_k, k_a, r_k, bd_sum)


def _scan_step(rf, vf, kkf, lwf, kf, af, rb, vb, kkb, lwb, kb, ab, yf_ref, yb_ref, h_ref):
    L = CHUNK
    n_sub = rf.shape[0] // L
    n_pairs = D_A // LANES
    ti = lax.broadcasted_iota(jnp.int32, (L, L), 0)
    tj = lax.broadcasted_iota(jnp.int32, (L, L), 1)
    pi = lax.broadcasted_iota(jnp.int32, (L, LANES), 0)
    lane = lax.broadcasted_iota(jnp.int32, (L, LANES), 1)
    pj = lane & (HEAD_A - 1)
    lane_lo = lane < HEAD_A
    eye = jnp.where(pj == pi, 1.0, 0.0)
    blk = lambda s: (pi // s) == (pj // s)
    row2 = lax.broadcasted_iota(jnp.int32, (2 * L, LANES), 0)
    col2 = lax.broadcasted_iota(jnp.int32, (2 * L, LANES), 1)
    same_head = (row2 // HEAD_A) == (col2 // HEAD_A)
    diag2 = row2 == col2
    zeros_l = jnp.zeros((L, LANES), F32)
    zeros_2l = jnp.zeros((2 * L, LANES), F32)

    def stack(x):
        return jnp.concatenate([jnp.where(lane_lo, x, 0.0), jnp.where(lane_lo, 0.0, x)], axis=0)

    def mmp(x, y):
        return _mm(x, stack(y))

    cat = jnp.concatenate
    chains = []
    for reverse, refs in ((False, (rf, vf, kkf, lwf, kf, af)), (True, (rb, vb, kkb, lwb, kb, ab))):
        tri = ((tj >= ti) if reverse else (tj <= ti)).astype(BF16)
        strict = (pj > pi) if reverse else (pj < pi)
        incl = (pj >= pi) if reverse else (pj <= pi)
        steps = []
        for sub in (range(n_sub - 1, -1, -1) if reverse else range(n_sub)):
            rows = slice(sub * L, (sub + 1) * L)
            r, v, kk, lw, k, a = (ref[rows, :].astype(F32) for ref in refs)
            G = sum(jnp.dot(tri, part, preferred_element_type=F32) for part in _split_bf16(lw, 3))
            g_end = G[0:1] if reverse else G[L - 1:L]
            gam_end = jnp.exp(g_end)
            inv_gam = jnp.exp(-G)
            rel_end = jnp.exp(g_end - G)
            b = kk * a
            full = dict(At=-kk * jnp.exp(G - lw), Bt=b * inv_gam, Kt=k * inv_gam, Rt=r * jnp.exp(G),
                        Kh=k * rel_end, Bh=b * rel_end, v=v)
            pairs = []
            for p in range(n_pairs):
                sl = slice(p * LANES, (p + 1) * LANES)
                c = {name: val[:, sl] for name, val in full.items()}
                c.update(strict=strict, incl=incl, gam_end=gam_end[:, sl], rows=rows)
                pairs.append(c)
            steps.append(pairs)
        chains.append(steps)
    flat = [c for steps in chains for pairs in steps for c in pairs]

    for c in flat:
        a4 = _mm_nt(cat([c["At"], c["Rt"]], axis=0), cat([stack(c["Bt"]), stack(c["Kt"])], axis=0))
        c["A_ab"] = jnp.where(c["strict"], a4[:L, :LANES], 0.0)
        c["A_ak"] = jnp.where(c["strict"], a4[:L, LANES:], 0.0)
        c["A_rb"] = jnp.where(c["incl"], a4[L:, :LANES], 0.0)
        c["A_rk"] = jnp.where(c["incl"], a4[L:, LANES:], 0.0)
    for c in flat:
        c["A0"] = jnp.where(blk(8), c["A_ab"], 0.0)
        c["A2"] = mmp(c["A0"], c["A0"])
    for c in flat:
        t1 = eye + c["A0"]
        both = mmp(cat([c["A2"], t1], axis=0), c["A2"])
        c["A4"] = both[:L]
        c["T"] = t1 + both[L:]
    for c in flat:
        c["T"] = c["T"] + mmp(c["T"], c["A4"])
    s = 8
    while s < L:
        outer = blk(2 * s) & jnp.logical_not(blk(s))
        for c in flat:
            c["TO"] = mmp(c["T"], jnp.where(outer, c["A_ab"], 0.0))
        for c in flat:
            c["T"] = c["T"] + mmp(c["TO"], c["T"])
        s *= 2
    for c in flat:
        c["AkV"] = mmp(c["A_ak"], c["v"])
    for c in flat:
        wu = _mm(c["T"], cat([stack(c["At"]), stack(c["AkV"])], axis=1))
        c["W"], c["U0"] = wu[:, :LANES], wu[:, LANES:]
    for c in flat:
        kb_t = cat([c["Kh"], c["Bh"]], axis=0).T
        mn = _mm(kb_t, cat([cat([zeros_l, c["v"]], axis=1), cat([c["W"], c["U0"]], axis=1)], axis=0))
        c["M"] = jnp.where(same_head, mn[:, :LANES], 0.0) + jnp.where(diag2, c["gam_end"], 0.0)
        c["N"] = jnp.where(same_head, mn[:, LANES:], 0.0)
    for c in flat:
        qy = _mm(cat([c["A_rb"], c["A_rk"]], axis=1),
                 cat([cat([stack(c["W"]), stack(c["U0"])], axis=1),
                      cat([zeros_2l, stack(c["v"])], axis=1)], axis=0))
        c["Q"] = c["Rt"] + qy[:, :LANES]
        c["Y0"] = qy[:, LANES:]
    for d, (steps, y_ref) in enumerate(zip(chains, (yf_ref, yb_ref))):
        for pairs in steps:
            ys = []
            for p, c in enumerate(pairs):
                qm_h = _mm(cat([c["Q"], c["M"]], axis=0), h_ref[d * n_pairs + p])
                ys.append(qm_h[:L] + c["Y0"])
                h_ref[d * n_pairs + p] = qm_h[L:] + c["N"]
            y_ref[pairs[0]["rows"], :] = cat(ys, axis=1)


def _bias_features(slope, n, width, key_side):
    pos = lax.broadcasted_iota(jnp.int32, (n, width), 0).astype(F32)
    lane = lax.broadcasted_iota(jnp.int32, (n, width), 1)
    hi, mid, lo = (t.astype(F32) for t in _split_bf16(slope * pos, 3))
    terms = jnp.where(lane % 3 == 0, hi, jnp.where(lane % 3 == 1, mid, lo))
    if key_side:
        return jnp.where(lane < 3, -1.0, jnp.where(lane < 6, terms, 0.0))
    return jnp.where(lane < 3, terms, jnp.where(lane < 6, 1.0, 0.0))


def _attn_body(q_ref, k_ref, vt_ref, lv_ref, sg_ref, qg_ref, kg_ref, o_ref, qs_ref, feat_ref, src_ref,
               s_ref, mx_ref, p_ref, al_ref, m_ref, acc_ref, *, lam_init, before_loop):
    tq = src_ref.shape[1]
    n_kv, _, tk = vt_ref.shape
    seq = n_kv * tk
    h = pl.program_id(1)
    i = pl.program_id(2)
    slope_bits = (127 - 2 * (h + 1)) << 23
    slope = (lax.bitcast_convert_type(jnp.full((1, 1), slope_bits, jnp.int32), jnp.float32).astype(F32)
             * LOG2E)
    bound =(HEAD_B * ROUNDING_MARGIN * jnp.max(jnp.abs(qg_ref[...]), axis=-1, keepdims=True)
             * jnp.max(jnp.abs(kg_ref[...]), axis=-1, keepdims=True))
    reach = jnp.minimum((F32_ZERO_EXP2 + 2.0 * bound) / slope + 1.0, float(seq)).astype(jnp.int32)[0, 0]

    n_st = tk // tq
    hw = 2 * HEAD_B

    @pl.when(i == 0)
    def _():
        kv_pos = lax.broadcasted_iota(jnp.int32, (tk, tq), 0)
        q_pos = lax.broadcasted_iota(jnp.int32, (tk, tq), 1)
        src_ref[...] = slope * (q_pos - kv_pos).astype(F32)
        key_feat = _bias_features(slope, tk, hw, key_side=True)
        feat_ref[0] = key_feat.astype(BF16)
        feat_ref[1] = (-key_feat).astype(BF16)
        feat_ref[2] = jnp.zeros((tk, hw), BF16)
        q_feat = _bias_features(slope, tq, hw, key_side=False).T.astype(BF16)
        for st in range(n_st):
            qs_ref[st, hw:, :tq] = q_feat
            qs_ref[st, hw:, tq:] = q_feat

    chan =lax.broadcasted_iota(jnp.int32, (hw, tq), 0)
    for st in range(n_st):
        q_t = q_ref[st * tq:(st + 1) * tq, :].astype(F32).T.astype(BF16)
        zero = jnp.zeros_like(q_t)
        qs_ref[st, :hw, :tq] = jnp.where(chan < HEAD_B, q_t, zero)
        qs_ref[st, :hw, tq:] = jnp.where(chan < HEAD_B, zero, q_t)
    q0 = i * tk
    j_lo = jnp.maximum(q0 - reach, 0) // tk
    j_hi = jnp.minimum(q0 + tk - 1 + reach, seq - 1) // tk + 1
    n_left = i - j_lo
    n_off = n_left + (j_hi - i - 1)

    def scores(j, side, st):
        kb = k_ref[pl.ds(pl.multiple_of(j * tk, tk), tk), :]
        return jnp.dot(jnp.concatenate([kb, feat_ref[side]], axis=1), qs_ref[st],
                       preferred_element_type=F32)

    def softmax_update(s, s_max, cst, st):
        m_old = m_ref[st]
        m_new = jnp.maximum(m_old, s_max + cst)
        alpha = jnp.exp2(m_old - m_new)
        p = jnp.exp2(s - (m_new - cst))
        m_ref[st] = m_new
        return alpha, p.astype(BF16)

    def block(n):
        j = jnp.clip(j_lo + n + (n >= n_left).astype(jnp.int32), 0, n_kv - 1)
        j = jnp.where(n < 0, i, j)
        return j, (j > i).astype(jnp.int32), jnp.logical_and(n >= 0, n < n_off)

    def qk_stage(n, slot):
        j, side, _ = block(n)
        for st in range(n_st):
            s = scores(j, side, st)
            s_ref[slot, st] = s
            mx_ref[slot, st] = jnp.max(s, axis=0, keepdims=True)

    def softmax_stage(n, slot):
        j, _, valid = block(n)
        for st in range(n_st):
            dist0 = jnp.abs(q0 + st * tq - j * tk).astype(F32)
            cst = jnp.where(valid, -slope * dist0, -1e30)
            alpha, p = softmax_update(s_ref[slot, st], mx_ref[slot, st], cst, st)
            al_ref[slot, st] = alpha
            p_ref[slot, st] = p

    def pv_stage(n, slot):
        j, _, _ = block(n)
        for st in range(n_st):
            acc_ref[st] = al_ref[slot, st] * acc_ref[st] + jnp.dot(vt_ref[j], p_ref[slot, st],
                                                                    preferred_element_type=F32)

    m_ref[...] = jnp.full(m_ref.shape, -1e30, F32)
    acc_ref[...] = jnp.zeros(acc_ref.shape, F32)
    diag_scores = [scores(i, 2, st) for st in range(n_st)]
    qk_stage(0, 0)
    src = src_ref[...]
    for st in range(n_st):
        bias = jnp.abs(src + slope * float(st * tq))
        s = diag_scores[st] - jnp.concatenate([bias, bias], axis=1)
        alpha, p = softmax_update(s, jnp.max(s, axis=0, keepdims=True), jnp.zeros((1, 1), F32), st)
        al_ref[1, st] = alpha
        p_ref[1, st] = p

    before_loop()

    def pipelined_pair(t, carry):
        n = 2 * t
        qk_stage(n + 1, 1)
        softmax_stage(n, 0)
        pv_stage(n - 1, 1)
        qk_stage(n + 2, 0)
        softmax_stage(n + 1, 1)
        pv_stage(n, 0)
        return carry

    n_pairs = n_off // 2
    lax.fori_loop(0, n_pairs, pipelined_pair, 0)
    last = 2 * n_pairs

    @pl.when(n_off % 2 == 1)
    def _():
        softmax_stage(last, 0)
        pv_stage(last - 1, 1)
        pv_stage(last, 0)

    @pl.when(n_off % 2 == 0)
    def _():
        pv_stage(last - 1, 1)

    lv = lv_ref[...]
    lam = (jnp.exp(jnp.sum(lv[0:1] * lv[1:2], axis=-1, keepdims=True))
           - jnp.exp(jnp.sum(lv[2:3] * lv[3:4], axis=-1, keepdims=True)) + lam_init)
    for st in range(n_st):
        acc = acc_ref[st, :hw, :]
        l = acc_ref[st, hw:hw + 1, :]
        o_t = acc[:, :tq] / l[:, :tq] - lam * (acc[:, tq:] / l[:, tq:])
        o_t = o_t * lax.rsqrt(jnp.mean(o_t * o_t, axis=0, keepdims=True) + NORM_EPS)
        o_ref[st * tq:(st + 1) * tq, :] = (o_t.T * sg_ref[...] * (1.0 - lam_init)).astype(o_ref.dtype)


def _mixers_body(q_ref, k_ref, vt_ref, lv_ref, sg_ref, qg_ref, kg_ref,
                 rf, vf, kkf, lwf, kf, af, rb, vb, kkb, lwb, kb, ab, *rest, lam_init, n_scan, n_cast):
    cast_in, (o_ref, yf_ref, yb_ref), rest = rest[:n_cast], rest[n_cast:n_cast + 3], rest[n_cast + 3:]
    cast_out, scratch = rest[:n_cast], rest[n_cast:]
    *attn_scratch, h_ref = scratch
    step = (pl.program_id(0) * pl.num_programs(1) + pl.program_id(1)) * pl.num_programs(2) + pl.program_id(2)

    @pl.when(step % n_scan == 0)
    def _():
        h_ref[...] = jnp.zeros(h_ref.shape, F32)

    def side_work():
        _scan_step(rf, vf, kkf, lwf, kf, af, rb, vb, kkb, lwb, kb, ab, yf_ref, yb_ref, h_ref)
        _convert(cast_in, cast_out)

    _attn_body(q_ref, k_ref, vt_ref, lv_ref, sg_ref, qg_ref, kg_ref, o_ref, *attn_scratch, lam_init=lam_init,
               before_loop=side_work)


def _mixers(q, k, vt, lam_vecs, sub_g, qg, kg, r, v, kk, lw0, k0, a0, lw1, k1, a1, lam_init, batch, seq, layer,
            next_weights):
    t = q.shape[0]
    tk = vt.shape[2]
    tq = ATTN_TQ
    n_st = tk // tq
    nq = seq // tk
    n_kv = seq // tk
    hw = 2 * HEAD_B
    rows = SCAN_CHUNKS_PER_STEP * CHUNK
    n_scan = seq // rows
    assert N_HEADS_B * nq == n_scan
    step = lambda b, h, i: (b * N_HEADS_B + h) * nq + i
    fwd = pl.BlockSpec((rows, D_A), lambda b, h, i: (step(b, h, i), 0))
    bwd = pl.BlockSpec((rows, D_A), lambda b, h, i: (
        (step(b, h, i) // n_scan) * n_scan + n_scan - 1 - step(b, h, i) % n_scan, 0))
    y = jax.ShapeDtypeStruct((t, D_A), F32)
    cast_in, cast_out, cast_shapes = _conversion_specs(next_weights, layer + 1, batch * N_HEADS_B * nq, step)
    return pl.pallas_call(
        functools.partial(_mixers_body, lam_init=lam_init, n_scan=n_scan, n_cast=len(next_weights)),
        grid=(batch, N_HEADS_B, nq),
        in_specs=[pl.BlockSpec((tk, hw), lambda b, h, i: (b * nq + i, h)),
                  pl.BlockSpec((seq, hw), lambda b, h, i: (b, h)),
                  pl.BlockSpec((n_kv, V_ROWS, tk), lambda b, h, i: (b, h, 0)),
                  _layer_spec((4, HEAD_B), layer), _layer_spec((1, hw), layer),
                  _layer_spec((1, D_B), layer), _layer_spec((1, D_B), layer)] + [fwd] * 6 + [bwd] * 6 + cast_in,
        out_specs=[pl.BlockSpec((tk, hw), lambda b, h, i: (b * nq + i, h)), fwd, bwd] + cast_out,
        out_shape=[jax.ShapeDtypeStruct((t, D_B), BF16), y, y] + cast_shapes,
        scratch_shapes=[pltpu.VMEM((n_st, 2 * hw, 2 * tq), BF16),
                        pltpu.VMEM((3, tk, hw), BF16),
                        pltpu.VMEM((tk, tq), F32),
                        pltpu.VMEM((2, n_st, tk, 2 * tq), F32),
                        pltpu.VMEM((2, n_st, 1, 2 * tq), F32),
                        pltpu.VMEM((2, n_st, tk, 2 * tq), BF16),
                        pltpu.VMEM((2, n_st, 1, 2 * tq), F32),
                        pltpu.VMEM((n_st, 1, 2 * tq), F32),
                        pltpu.VMEM((n_st, V_ROWS, 2 * tq), F32),
                        pltpu.VMEM((2 * (D_A // LANES), LANES, LANES), F32)],
        compiler_params=_params(3),
        name="mixers",
    )(q, k, vt, lam_vecs, sub_g, qg, kg, r, v, kk, lw0, k0, a0, r, v, kk, lw1, k1, a1, *next_weights)


def _merge_body(yf_ref, yb_ref, gate_ref, bonus_ref, ob_ref, gt_ref, x_ref, lng_ref, lnb_ref, bd_ref,
                wb0_ref, wb1_ref, wo_ref, g2_ref, win_ref, wout_ref, o_ref):
    bd = bd_ref[...]
    y = yf_ref[...] + yb_ref[...]
    yc = y - _mm_split_lhs(y, bd)
    var = _mm(yc * yc, bd)
    yn = yc * lax.rsqrt(var + GN_EPS) * lng_ref[...] + lnb_ref[...]
    oa = (yn + bonus_ref[...]) * gate_ref[...]
    gts = gt_ref[...].astype(F32)
    merged = (gts[:, :D_MODEL] * _mm(oa, wb0_ref[...])
              + gts[:, D_MODEL:] * jnp.dot(ob_ref[...], wb1_ref[...], preferred_element_type=F32))
    o_ref[...] = _ffn_math(x_ref[...] + _mm(merged, wo_ref[...]), g2_ref, win_ref, wout_ref)


def _merge(yf, yb, gate, bonus, ob, gts, x2, ln_g, ln_b, bd_mean, wb, wo, g2, w_in, w_out, layer):
    t = x2.shape[0]
    tm = min(TOKEN_TILE, t)
    row = lambda width: pl.BlockSpec((tm, width), lambda i: (i, 0))
    branch = lambda g: pl.BlockSpec((None, D_A, D_MODEL), lambda *_: (0, g, 0), pipeline_mode=pl.Buffered(1))
    return pl.pallas_call(
        _merge_body,
        grid=(t // tm,),
        in_specs=[row(D_A), row(D_A), row(D_A), row(D_A), row(D_B), row(2 * D_MODEL), row(D_MODEL),
                  _layer_spec((1, D_A), layer), _layer_spec((1, D_A), layer), _const_spec((D_A, D_A)),
                  branch(0), branch(1),
                  _weight_spec((D_MODEL, D_MODEL)),
                  _layer_spec((1, D_MODEL), layer), _weight_spec((D_MODEL, 2 * D_FF)),
                  _weight_spec((D_FF, D_MODEL))],
        out_specs=row(D_MODEL),
        out_shape=jax.ShapeDtypeStruct((t, D_MODEL), F32),
        compiler_params=_params(1),
        name="merge",
    )(yf, yb, gate, bonus, ob, gts, x2, ln_g, ln_b, bd_mean, wb, wb, wo, g2, w_in, w_out)


def _block_diag2(w):
    z = jnp.zeros_like(w[:, 0])
    return jnp.concatenate([jnp.concatenate([w[:, 0], z], axis=2), jnp.concatenate([z, w[:, 1]], axis=2)],
                           axis=1)


def kernel(x, norm_ffn1, ffn1_in, ffn1_out, norm_mix, w_in, rwkv_mu, decay_w0, decay_w2, iclr_a0, iclr_a2,
           gate_g2, k_k, k_a, r_k, ln_x_g, ln_x_b, q_gain, k_gain, diff_lambda, subln_g, w_branch, w_out,
           norm_ffn2, ffn2_in, ffn2_out):
    batch, seq, d = x.shape
    depth = norm_ffn1.shape[0]
    assert seq % TOKEN_TILE == 0 and d == D_MODEL
    x2 = x.reshape(batch * seq, d)
    head_id = jnp.arange(D_A) // HEAD_A
    same = head_id[:, None] == head_id[None, :]
    bd_sum = same.astype(BF16)
    bd_mean = (same.astype(F32) / HEAD_A).astype(BF16)
    rows = lambda a: a.reshape(depth, 1, -1)
    bf = lambda a: a.astype(BF16)
    n_qk = D_B // HEAD_B
    qg = rows(jnp.tile(q_gain, (1, n_qk))) * (HEAD_B ** -0.5 * LOG2E)
    kg = rows(jnp.tile(k_gain, (1, n_qk)))
    big = (ffn1_in, ffn1_out, w_in, w_branch.reshape(depth, N_BRANCH * D_A, D_MODEL), w_out, ffn2_in, ffn2_out)
    weights = [bf(w[:1]) for w in big[:2]]
    prep_w = (rows(rwkv_mu), rows(decay_w0), bf(_block_diag2(decay_w2)), rows(iclr_a0),
              bf(_block_diag2(iclr_a2)), bf(gate_g2), rows(k_k), rows(k_a), rows(r_k))
    sub_g = rows(subln_g)
    for l in range(depth):
        lam_init = 0.8 - 0.6 * math.exp(-0.3 * l)
        x2, *converted = _ffn(x2, rows(norm_ffn1), weights[0], weights[1], l, big[2:] if l == 0 else ())
        f1_in, f1_out, w_proj, w_br, w_o, f2_in, f2_out = weights + converted
        q, k, vt, gts, r, vv, kk, lw0, k0, a0, lw1, k1, a1, gate, bonus = _proj(
            x2, seq, rows(norm_mix), w_proj, qg, kg, bd_mean, *prep_w, bd_sum, l)
        ob, yf, yb, *weights = _mixers(q, k, vt, diff_lambda, sub_g, qg, kg, r, vv, kk, lw0, k0, a0, lw1, k1, a1,
                                      lam_init, batch, seq, l, big if l + 1 < depth else ())
        x2 = _merge(yf, yb, gate, bonus, ob, gts, x2, rows(ln_x_g), rows(ln_x_b), bd_mean,
                    w_br, w_o, rows(norm_ffn2), f2_in, f2_out, l)
    return x2.reshape(batch, seq, d)
```

```python
import functools
import math

import jax
import jax.numpy as jnp
from jax import lax
from jax.experimental import pallas as pl
from jax.experimental.pallas import tpu as pltpu

F32 = jnp.float32
BF16 = jnp.bfloat16

D_MODEL = 1024
D_A = 512
HEAD_A = 64
LORA = 64
LORA_GATE = 128
C_RWKV = 3 * D_A + 2 * LORA + 2 * LORA + LORA_GATE
D_B = 512
HEAD_B = 64
N_HEADS_B = D_B // (2 * HEAD_B)
D_FF = 2816
N_BRANCH = 2
DECAY_SCALE = math.exp(-0.5)
GN_EPS = 64e-5
NORM_EPS = 1e-6
LOG2E = math.log2(math.e)
F32_ZERO_EXP2 = 150.0
ROUNDING_MARGIN = 1.02

LANES = 128
SUBLANES = 8
BF16_ROWS = 16
VMEM_LIMIT = 56 * 1024 * 1024

TOKEN_TILE = 512
MXU_WIDTH = 256
FF_SPLITS = (0, 6 * MXU_WIDTH, D_FF)
CHUNK = 64
SCAN_CHUNKS_PER_STEP = 2
ATTN_TQ = 256
V_ROWS = 2 * HEAD_B + 16

_NT = (((1,), (1,)), ((), ()))


def _mm(a, b):
    return jnp.dot(a.astype(BF16), b.astype(BF16), preferred_element_type=F32)


def _mm_nt(a, b):
    return lax.dot_general(a.astype(BF16), b.astype(BF16), _NT, preferred_element_type=F32)


def _split_bf16(x, n):
    parts = []
    for _ in range(n - 1):
        hi = x.astype(BF16)
        parts.append(hi)
        x = x - hi.astype(F32)
    parts.append(x.astype(BF16))
    return parts


def _group_sums(x, bd, passes=1):
    width = bd.shape[0]
    tiles = [x[:, c:c + width] for c in range(0, x.shape[1], width)]
    return jnp.concatenate([sum(jnp.dot(term, bd, preferred_element_type=F32) for term in _split_bf16(t, passes))
                            for t in tiles], axis=1)


def _rms_norm(x, g):
    return x * lax.rsqrt(jnp.mean(x * x, axis=-1, keepdims=True) + NORM_EPS) * g


def _const_spec(shape):
    return pl.BlockSpec(shape, lambda *_: (0,) * len(shape), pipeline_mode=pl.Buffered(1))


def _layer_spec(shape, layer, *lead):
    index = (layer,) + lead + (0,) * len(shape)
    return pl.BlockSpec((None,) * (1 + len(lead)) + shape, lambda *_: index, pipeline_mode=pl.Buffered(1))


def _weight_spec(shape, *lead):
    return _layer_spec(shape, 0, *lead)


def _conversion_specs(weights, src_layer, n_steps, step):
    ins, outs, shapes = [], [], []
    for w in weights:
        _, w_rows, w_cols = w.shape
        rb = next(r for r in range(BF16_ROWS, w_rows + 1, BF16_ROWS) if w_rows % r == 0 and w_rows // r <= n_steps)
        blk = lambda *ids, last=w_rows // rb - 1: jnp.minimum(step(*ids), last)
        ins.append(pl.BlockSpec((None, rb, w_cols), lambda *ids, blk=blk: (src_layer, blk(*ids), 0)))
        outs.append(pl.BlockSpec((None, rb, w_cols), lambda *ids, blk=blk: (0, blk(*ids), 0)))
        shapes.append(jax.ShapeDtypeStruct((1, w_rows, w_cols), BF16))
    return ins, outs, shapes


def _convert(srcs, dsts):
    for src, dst in zip(srcs, dsts):
        dst[...] = src[...].astype(BF16)


def _params(n_axes):
    return pltpu.CompilerParams(dimension_semantics=("arbitrary",) * n_axes,
                                vmem_limit_bytes=VMEM_LIMIT)


def _ffn_math(x, g_ref, win_ref, wout_ref):
    h = _rms_norm(x, g_ref[...]).astype(BF16)
    acc = jnp.zeros(x.shape, F32)
    for lo, hi in zip(FF_SPLITS[:-1], FF_SPLITS[1:]):
        gate = jnp.dot(h, win_ref[:, lo:hi], preferred_element_type=F32)
        up = jnp.dot(h, win_ref[:, D_FF + lo:D_FF + hi], preferred_element_type=F32)
        act = (gate * jax.nn.sigmoid(gate) * up).astype(BF16)
        acc = acc + jnp.dot(act, wout_ref[lo:hi, :], preferred_element_type=F32)
    return x + 0.5 * acc


def _ffn_body(x_ref, g_ref, win_ref, wout_ref, *rest):
    n_conv = len(rest) // 2
    _convert(rest[:n_conv], rest[n_conv + 1:])
    rest[n_conv][...] = _ffn_math(x_ref[...], g_ref, win_ref, wout_ref)


def _ffn(x2, g, w_in, w_out, layer, convert=()):
    t = x2.shape[0]
    tm = min(TOKEN_TILE, t)
    conv_in, conv_out, conv_shapes = _conversion_specs(convert, layer, t // tm, lambda i: i)
    return pl.pallas_call(
        _ffn_body,
        grid=(t // tm,),
        in_specs=[pl.BlockSpec((tm, D_MODEL), lambda i: (i, 0)),
                  _layer_spec((1, D_MODEL), layer),
                  _weight_spec((D_MODEL, 2 * D_FF)),
                  _weight_spec((D_FF, D_MODEL))] + conv_in,
        out_specs=[pl.BlockSpec((tm, D_MODEL), lambda i: (i, 0))] + conv_out,
        out_shape=[jax.ShapeDtypeStruct((t, D_MODEL), F32)] + conv_shapes,
        compiler_params=_params(1),
        name="ffn",
    )(x2, g, w_in, w_out, *convert)


_Q0 = C_RWKV
_K0 = C_RWKV + D_B
_V0 = C_RWKV + 2 * D_B
_G0 = C_RWKV + 3 * D_B


def _proj_body(x_ref, xp_ref, xn_ref, g_ref, w_ref, qg_ref, kg_ref, bdm_ref, mu_ref, w0_ref, w2_ref, a0_ref,
               a2_ref, g2_ref, kk_ref, ka_ref, rk_ref, bds_ref,
               q_ref, k_ref, vt_ref, gt_ref, r_out, v_out, kk_out, lw0_out, k0_out, a0_out, lw1_out, k1_out,
               a1_out, gate_out, bonus_out, *, seq):
    i = pl.program_id(0)
    g = g_ref[...]
    h = _rms_norm(x_ref[...], g).astype(BF16)
    tm = h.shape[0]

    bd = bdm_ref[...]
    q = jnp.dot(h, w_ref[:, _Q0:_K0], preferred_element_type=F32)
    q_ref[...] = (q * lax.rsqrt(_group_sums(q * q, bd) + NORM_EPS) * qg_ref[...]).astype(BF16)
    k = jnp.dot(h, w_ref[:, _K0:_V0], preferred_element_type=F32)
    k_ref[...] = (k * lax.rsqrt(_group_sums(k * k, bd) + NORM_EPS) * kg_ref[...]).astype(BF16)
    vt = jnp.dot(h, w_ref[:, _V0:_G0], preferred_element_type=F32).T.astype(BF16)
    hw = 2 * HEAD_B
    ones = jnp.ones((V_ROWS - hw, vt.shape[1]), BF16)
    vt_ref[0] = jnp.concatenate([part for hd in range(N_HEADS_B)
                                 for part in (vt[hd * hw:(hd + 1) * hw], ones)], axis=0)
    gt_ref[...] = jax.nn.sigmoid(jnp.dot(h, w_ref[:, _G0:], preferred_element_type=F32)).astype(BF16)

    w_rwkv = w_ref[:, :C_RWKV]
    p = jnp.dot(h, w_rwkv, preferred_element_type=F32)
    edge = jnp.concatenate([xp_ref[...], xn_ref[...]], axis=0)
    p_edge = jnp.dot(_rms_norm(edge, g).astype(BF16), w_rwkv, preferred_element_type=F32)
    keep_prev = jnp.where((i * tm) % seq == 0, 0.0, 1.0)
    keep_next = jnp.where(((i + 1) * tm) % seq == 0, 0.0, 1.0)
    row = lax.broadcasted_iota(jnp.int32, p.shape, 0)
    prev = jnp.where(row == 0, keep_prev * p_edge[SUBLANES - 1:SUBLANES, :], pltpu.roll(p, 1, 0))
    nxt = jnp.where(row == tm - 1, keep_next * p_edge[SUBLANES:SUBLANES + 1, :], pltpu.roll(p, tm - 1, 0))
    p = p + mu_ref[...] * (0.5 * (prev + nxt) - p)

    r = p[:, :D_A]
    k = p[:, D_A:2 * D_A]
    v = p[:, 2 * D_A:3 * D_A]
    dw = p[:, 3 * D_A:3 * D_A + 2 * LORA]
    da = p[:, 3 * D_A + 2 * LORA:3 * D_A + 4 * LORA]
    dg = p[:, 3 * D_A + 4 * LORA:]
    bd = bds_ref[...]

    logw = -DECAY_SCALE * jax.nn.sigmoid(w0_ref[...] + _mm(jnp.tanh(dw), w2_ref[...]))
    iclr = jax.nn.sigmoid(a0_ref[...] + _mm(da, a2_ref[...]))
    gate_out[...] = _mm(jax.nn.sigmoid(dg), g2_ref[...])

    kk = k * kk_ref[...]
    norm = jnp.sqrt(_group_sums(kk * kk, bd))
    kk_out[...] = (kk / jnp.maximum(norm, 1e-12)).astype(kk_out.dtype)

    ka = ka_ref[...]
    a_f = iclr[:, :D_A]
    a_b = iclr[:, D_A:]
    k_f = k * (1.0 + (a_f - 1.0) * ka)
    k_b = k * (1.0 + (a_b - 1.0) * ka)
    bonus = _group_sums(r * rk_ref[...] * (k_f + k_b), bd, passes=2)
    r_out[...] = r.astype(r_out.dtype)
    v_out[...] = v.astype(v_out.dtype)
    lw0_out[...] = logw[:, :D_A]
    k0_out[...] = k_f.astype(k0_out.dtype)
    a0_out[...] = a_f.astype(a0_out.dtype)
    lw1_out[...] = logw[:, D_A:]
    k1_out[...] = k_b.astype(k1_out.dtype)
    a1_out[...] = a_b.astype(a1_out.dtype)
    bonus_out[...] = bonus * v


def _proj(x2, seq, g, w, qg, kg, bd_mean, mu, w0, w2cat, a0, a2cat, g2, k_k, k_a, r_k, bd_sum, layer):
    t = x2.shape[0]
    tm = TOKEN_TILE
    c_in = w.shape[2]
    nb = tm // SUBLANES
    last = t // SUBLANES - 1
    row = lambda width: pl.BlockSpec((tm, width), lambda i: (i, 0))
    wide = jax.ShapeDtypeStruct((t, D_A), F32)
    half = jax.ShapeDtypeStruct((t, D_A), BF16)
    return pl.pallas_call(
        functools.partial(_proj_body, seq=seq),
        grid=(t // tm,),
        in_specs=[row(D_MODEL),
                  pl.BlockSpec((SUBLANES, D_MODEL), lambda i: (jnp.maximum(i * nb - 1, 0), 0)),
                  pl.BlockSpec((SUBLANES, D_MODEL), lambda i: (jnp.minimum((i + 1) * nb, last), 0)),
                  _layer_spec((1, D_MODEL), layer), _weight_spec((D_MODEL, c_in)),
                  _layer_spec((1, D_B), layer), _layer_spec((1, D_B), layer), _const_spec((MXU_WIDTH, MXU_WIDTH)),
                  _layer_spec((1, C_RWKV), layer),
                  _layer_spec((1, 2 * D_A), layer), _layer_spec((2 * LORA, 2 * D_A), layer),
                  _layer_spec((1, 2 * D_A), layer), _layer_spec((2 * LORA, 2 * D_A), layer),
                  _layer_spec((LORA_GATE, D_A), layer),
                  _layer_spec((1, D_A), layer), _layer_spec((1, D_A), layer), _layer_spec((1, D_A), layer),
                  _const_spec((MXU_WIDTH, MXU_WIDTH))],
        out_specs=[row(D_B), row(D_B),
                   pl.BlockSpec((1, N_HEADS_B * V_ROWS, tm), lambda i: (i, 0, 0)), row(2 * D_MODEL)]
                  + [row(D_A)] * 11,
        out_shape=[jax.ShapeDtypeStruct((t, D_B), BF16),
                   jax.ShapeDtypeStruct((t, D_B), BF16),
                   jax.ShapeDtypeStruct((t // tm, N_HEADS_B * V_ROWS, tm), BF16),
                   jax.ShapeDtypeStruct((t, 2 * D_MODEL), BF16),
                   half, half, half, wide, half, half, wide, half, half, wide, wide],
        compiler_params=_params(1),
        name="proj",
    )(x2, x2, x2, g, w, qg, kg, bd_mean, mu, w0, w2cat, a0, a2cat, g2, k_k, k_a, r_k, bd_sum)


def _scan_step(rf, vf, kkf, lwf, kf, af, rb, vb, kkb, lwb, kb, ab, yf_ref, yb_ref, h_ref):
    L = CHUNK
    n_sub = rf.shape[0] // L
    n_pairs = D_A // LANES
    ti = lax.broadcasted_iota(jnp.int32, (L, L), 0)
    tj = lax.broadcasted_iota(jnp.int32, (L, L), 1)
    pi = lax.broadcasted_iota(jnp.int32, (L, LANES), 0)
    lane = lax.broadcasted_iota(jnp.int32, (L, LANES), 1)
    pj = lane & (HEAD_A - 1)
    lane_lo = lane < HEAD_A
    eye = jnp.where(pj == pi, 1.0, 0.0)
    blk = lambda s: (pi // s) == (pj // s)
    row2 = lax.broadcasted_iota(jnp.int32, (2 * L, LANES), 0)
    col2 = lax.broadcasted_iota(jnp.int32, (2 * L, LANES), 1)
    same_head = (row2 // HEAD_A) == (col2 // HEAD_A)
    diag2 = row2 == col2
    zeros_l = jnp.zeros((L, LANES), F32)
    zeros_2l = jnp.zeros((2 * L, LANES), F32)

    def stack(x):
        return jnp.concatenate([jnp.where(lane_lo, x, 0.0), jnp.where(lane_lo, 0.0, x)], axis=0)

    def mmp(x, y):
        return _mm(x, stack(y))

    cat = jnp.concatenate
    chains = []
    for reverse, refs in ((False, (rf, vf, kkf, lwf, kf, af)), (True, (rb, vb, kkb, lwb, kb, ab))):
        tri = ((tj >= ti) if reverse else (tj <= ti)).astype(BF16)
        strict = (pj > pi) if reverse else (pj < pi)
        incl = (pj >= pi) if reverse else (pj <= pi)
        steps = []
        for sub in (range(n_sub - 1, -1, -1) if reverse else range(n_sub)):
            rows = slice(sub * L, (sub + 1) * L)
            r, v, kk, lw, k, a = (ref[rows, :].astype(F32) for ref in refs)
            G = sum(jnp.dot(tri, part, preferred_element_type=F32) for part in _split_bf16(lw, 3))
            g_end = G[0:1] if reverse else G[L - 1:L]
            gam_end = jnp.exp(g_end)
            inv_gam = jnp.exp(-G)
            rel_end = jnp.exp(g_end - G)
            b = kk * a
            full = dict(At=-kk * jnp.exp(G - lw), Bt=b * inv_gam, Kt=k * inv_gam, Rt=r * jnp.exp(G),
                        Kh=k * rel_end, Bh=b * rel_end, v=v)
            pairs = []
            for p in range(n_pairs):
                sl = slice(p * LANES, (p + 1) * LANES)
                c = {name: val[:, sl] for name, val in full.items()}
                c.update(strict=strict, incl=incl, gam_end=gam_end[:, sl], rows=rows)
                pairs.append(c)
            steps.append(pairs)
        chains.append(steps)
    flat = [c for steps in chains for pairs in steps for c in pairs]

    for c in flat:
        a4 = _mm_nt(cat([c["At"], c["Rt"]], axis=0), cat([stack(c["Bt"]), stack(c["Kt"])], axis=0))
        c["A_ab"] = jnp.where(c["strict"], a4[:L, :LANES], 0.0)
        c["A_ak"] = jnp.where(c["strict"], a4[:L, LANES:], 0.0)
        c["A_rb"] = jnp.where(c["incl"], a4[L:, :LANES], 0.0)
        c["A_rk"] = jnp.where(c["incl"], a4[L:, LANES:], 0.0)
    for c in flat:
        c["A0"] = jnp.where(blk(8), c["A_ab"], 0.0)
        c["A2"] = mmp(c["A0"], c["A0"])
    for c in flat:
        t1 = eye + c["A0"]
        both = mmp(cat([c["A2"], t1], axis=0), c["A2"])
        c["A4"] = both[:L]
        c["T"] = t1 + both[L:]
    for c in flat:
        c["T"] = c["T"] + mmp(c["T"], c["A4"])
    s = 8
    while s < L:
        outer = blk(2 * s) & jnp.logical_not(blk(s))
        for c in flat:
            c["TO"] = mmp(c["T"], jnp.where(outer, c["A_ab"], 0.0))
        for c in flat:
            c["T"] = c["T"] + mmp(c["TO"], c["T"])
        s *= 2
    for c in flat:
        c["AkV"] = mmp(c["A_ak"], c["v"])
    for c in flat:
        wu = _mm(c["T"], cat([stack(c["At"]), stack(c["AkV"])], axis=1))
        c["W"], c["U0"] = wu[:, :LANES], wu[:, LANES:]
    for c in flat:
        kb_t = cat([c["Kh"], c["Bh"]], axis=0).T
        mn = _mm(kb_t, cat([cat([zeros_l, c["v"]], axis=1), cat([c["W"], c["U0"]], axis=1)], axis=0))
        c["M"] = jnp.where(same_head, mn[:, :LANES], 0.0) + jnp.where(diag2, c["gam_end"], 0.0)
        c["N"] = jnp.where(same_head, mn[:, LANES:], 0.0)
    for c in flat:
        qy = _mm(cat([c["A_rb"], c["A_rk"]], axis=1),
                 cat([cat([stack(c["W"]), stack(c["U0"])], axis=1),
                      cat([zeros_2l, stack(c["v"])], axis=1)], axis=0))
        c["Q"] = c["Rt"] + qy[:, :LANES]
        c["Y0"] = qy[:, LANES:]
    for d, (steps, y_ref) in enumerate(zip(chains, (yf_ref, yb_ref))):
        for pairs in steps:
            ys = []
            for p, c in enumerate(pairs):
                qm_h = _mm(cat([c["Q"], c["M"]], axis=0), h_ref[d * n_pairs + p])
                ys.append(qm_h[:L] + c["Y0"])
                h_ref[d * n_pairs + p] = qm_h[L:] + c["N"]
            y_ref[pairs[0]["rows"], :] = cat(ys, axis=1)


def _bias_features(slope, n, width, key_side):
    pos = lax.broadcasted_iota(jnp.int32, (n, width), 0).astype(F32)
    lane = lax.broadcasted_iota(jnp.int32, (n, width), 1)
    hi, mid, lo = (t.astype(F32) for t in _split_bf16(slope * pos, 3))
    terms = jnp.where(lane % 3 == 0, hi, jnp.where(lane % 3 == 1, mid, lo))
    if key_side:
        return jnp.where(lane < 3, -1.0, jnp.where(lane < 6, terms, 0.0))
    return jnp.where(lane < 3, terms, jnp.where(lane < 6, 1.0, 0.0))


def _attn_body(q_ref, k_ref, vt_ref, lv_ref, sg_ref, qg_ref, kg_ref, o_ref, qs_ref, feat_ref, src_ref,
               s_ref, mx_ref, p_ref, al_ref, m_ref, acc_ref, *, lam_init, before_loop):
    tq = src_ref.shape[1]
    n_kv, _, tk = vt_ref.shape
    seq = n_kv * tk
    h = pl.program_id(1)
    i = pl.program_id(2)
    slope_bits = (127 - 2 * (h + 1)) << 23
    slope = (lax.bitcast_convert_type(jnp.full((1, 1), slope_bits, jnp.int32), jnp.float32).astype(F32)
             * LOG2E)
    bound =(HEAD_B * ROUNDING_MARGIN * jnp.max(jnp.abs(qg_ref[...]), axis=-1, keepdims=True)
             * jnp.max(jnp.abs(kg_ref[...]), axis=-1, keepdims=True))
    reach = jnp.minimum((F32_ZERO_EXP2 + 2.0 * bound) / slope + 1.0, float(seq)).astype(jnp.int32)[0, 0]

    n_st = tk // tq
    hw = 2 * HEAD_B

    @pl.when(i == 0)
    def _():
        kv_pos = lax.broadcasted_iota(jnp.int32, (tk, tq), 0)
        q_pos = lax.broadcasted_iota(jnp.int32, (tk, tq), 1)
        src_ref[...] = slope * (q_pos - kv_pos).astype(F32)
        key_feat = _bias_features(slope, tk, hw, key_side=True)
        feat_ref[0] = key_feat.astype(BF16)
        feat_ref[1] = (-key_feat).astype(BF16)
        feat_ref[2] = jnp.zeros((tk, hw), BF16)
        q_feat = _bias_features(slope, tq, hw, key_side=False).T.astype(BF16)
        for st in range(n_st):
            qs_ref[st, hw:, :tq] = q_feat
            qs_ref[st, hw:, tq:] = q_feat

    chan =lax.broadcasted_iota(jnp.int32, (hw, tq), 0)
    for st in range(n_st):
        q_t = q_ref[st * tq:(st + 1) * tq, :].astype(F32).T.astype(BF16)
        zero = jnp.zeros_like(q_t)
        qs_ref[st, :hw, :tq] = jnp.where(chan < HEAD_B, q_t, zero)
        qs_ref[st, :hw, tq:] = jnp.where(chan < HEAD_B, zero, q_t)
    q0 = i * tk
    j_lo = jnp.maximum(q0 - reach, 0) // tk
    j_hi = jnp.minimum(q0 + tk - 1 + reach, seq - 1) // tk + 1
    n_left = i - j_lo
    n_off = n_left + (j_hi - i - 1)

    def scores(j, side, st):
        kb = k_ref[pl.ds(pl.multiple_of(j * tk, tk), tk), :]
        return jnp.dot(jnp.concatenate([kb, feat_ref[side]], axis=1), qs_ref[st],
                       preferred_element_type=F32)

    def softmax_update(s, s_max, cst, st):
        m_old = m_ref[st]
        m_new = jnp.maximum(m_old, s_max + cst)
        alpha = jnp.exp2(m_old - m_new)
        p = jnp.exp2(s - (m_new - cst))
        m_ref[st] = m_new
        return alpha, p.astype(BF16)

    def block(n):
        j = jnp.clip(j_lo + n + (n >= n_left).astype(jnp.int32), 0, n_kv - 1)
        j = jnp.where(n < 0, i, j)
        return j, (j > i).astype(jnp.int32), jnp.logical_and(n >= 0, n < n_off)

    def qk_stage(n, slot):
        j, side, _ = block(n)
        for st in range(n_st):
            s = scores(j, side, st)
            s_ref[slot, st] = s
            mx_ref[slot, st] = jnp.max(s, axis=0, keepdims=True)

    def softmax_stage(n, slot):
        j, _, valid = block(n)
        for st in range(n_st):
            dist0 = jnp.abs(q0 + st * tq - j * tk).astype(F32)
            cst = jnp.where(valid, -slope * dist0, -1e30)
            alpha, p = softmax_update(s_ref[slot, st], mx_ref[slot, st], cst, st)
            al_ref[slot, st] = alpha
            p_ref[slot, st] = p

    def pv_stage(n, slot):
        j, _, _ = block(n)
        for st in range(n_st):
            acc_ref[st] = al_ref[slot, st] * acc_ref[st] + jnp.dot(vt_ref[j], p_ref[slot, st],
                                                                    preferred_element_type=F32)

    m_ref[...] = jnp.full(m_ref.shape, -1e30, F32)
    acc_ref[...] = jnp.zeros(acc_ref.shape, F32)
    diag_scores = [scores(i, 2, st) for st in range(n_st)]
    qk_stage(0, 0)
    src = src_ref[...]
    for st in range(n_st):
        bias = jnp.abs(src + slope * float(st * tq))
        s = diag_scores[st] - jnp.concatenate([bias, bias], axis=1)
        alpha, p = softmax_update(s, jnp.max(s, axis=0, keepdims=True), jnp.zeros((1, 1), F32), st)
        al_ref[1, st] = alpha
        p_ref[1, st] = p

    before_loop()

    def pipelined_pair(t, carry):
        n = 2 * t
        qk_stage(n + 1, 1)
        softmax_stage(n, 0)
        pv_stage(n - 1, 1)
        qk_stage(n + 2, 0)
        softmax_stage(n + 1, 1)
        pv_stage(n, 0)
        return carry

    n_pairs = n_off // 2
    lax.fori_loop(0, n_pairs, pipelined_pair, 0)
    last = 2 * n_pairs

    @pl.when(n_off % 2 == 1)
    def _():
        softmax_stage(last, 0)
        pv_stage(last - 1, 1)
        pv_stage(last, 0)

    @pl.when(n_off % 2 == 0)
    def _():
        pv_stage(last - 1, 1)

    lv = lv_ref[...]
    lam = (jnp.exp(jnp.sum(lv[0:1] * lv[1:2], axis=-1, keepdims=True))
           - jnp.exp(jnp.sum(lv[2:3] * lv[3:4], axis=-1, keepdims=True)) + lam_init)
    for st in range(n_st):
        acc = acc_ref[st, :hw, :]
        l = acc_ref[st, hw:hw + 1, :]
        o_t = acc[:, :tq] / l[:, :tq] - lam * (acc[:, tq:] / l[:, tq:])
        o_t = o_t * lax.rsqrt(jnp.mean(o_t * o_t, axis=0, keepdims=True) + NORM_EPS)
        o_ref[st * tq:(st + 1) * tq, :] = (o_t.T * sg_ref[...] * (1.0 - lam_init)).astype(o_ref.dtype)


def _mixers_body(q_ref, k_ref, vt_ref, lv_ref, sg_ref, qg_ref, kg_ref,
                 rf, vf, kkf, lwf, kf, af, rb, vb, kkb, lwb, kb, ab, *rest, lam_init, n_scan, n_cast):
    cast_in, (o_ref, yf_ref, yb_ref), rest = rest[:n_cast], rest[n_cast:n_cast + 3], rest[n_cast + 3:]
    cast_out, scratch = rest[:n_cast], rest[n_cast:]
    *attn_scratch, h_ref = scratch
    step = (pl.program_id(0) * pl.num_programs(1) + pl.program_id(1)) * pl.num_programs(2) + pl.program_id(2)

    @pl.when(step % n_scan == 0)
    def _():
        h_ref[...] = jnp.zeros(h_ref.shape, F32)

    def side_work():
        _scan_step(rf, vf, kkf, lwf, kf, af, rb, vb, kkb, lwb, kb, ab, yf_ref, yb_ref, h_ref)
        _convert(cast_in, cast_out)

    _attn_body(q_ref, k_ref, vt_ref, lv_ref, sg_ref, qg_ref, kg_ref, o_ref, *attn_scratch, lam_init=lam_init,
               before_loop=side_work)


def _mixers(q, k, vt, lam_vecs, sub_g, qg, kg, r, v, kk, lw0, k0, a0, lw1, k1, a1, lam_init, batch, seq, layer,
            next_weights):
    t = q.shape[0]
    tk = vt.shape[2]
    tq = ATTN_TQ
    n_st = tk // tq
    nq = seq // tk
    n_kv = seq // tk
    hw = 2 * HEAD_B
    rows = SCAN_CHUNKS_PER_STEP * CHUNK
    n_scan = seq // rows
    assert N_HEADS_B * nq == n_scan
    step = lambda b, h, i: (b * N_HEADS_B + h) * nq + i
    fwd = pl.BlockSpec((rows, D_A), lambda b, h, i: (step(b, h, i), 0))
    bwd = pl.BlockSpec((rows, D_A), lambda b, h, i: (
        (step(b, h, i) // n_scan) * n_scan + n_scan - 1 - step(b, h, i) % n_scan, 0))
    y = jax.ShapeDtypeStruct((t, D_A), F32)
    cast_in, cast_out, cast_shapes = _conversion_specs(next_weights, layer + 1, batch * N_HEADS_B * nq, step)
    return pl.pallas_call(
        functools.partial(_mixers_body, lam_init=lam_init, n_scan=n_scan, n_cast=len(next_weights)),
        grid=(batch, N_HEADS_B, nq),
        in_specs=[pl.BlockSpec((tk, hw), lambda b, h, i: (b * nq + i, h)),
                  pl.BlockSpec((seq, hw), lambda b, h, i: (b, h)),
                  pl.BlockSpec((n_kv, V_ROWS, tk), lambda b, h, i: (b, h, 0)),
                  _layer_spec((4, HEAD_B), layer), _layer_spec((1, hw), layer),
                  _layer_spec((1, D_B), layer), _layer_spec((1, D_B), layer)] + [fwd] * 6 + [bwd] * 6 + cast_in,
        out_specs=[pl.BlockSpec((tk, hw), lambda b, h, i: (b * nq + i, h)), fwd, bwd] + cast_out,
        out_shape=[jax.ShapeDtypeStruct((t, D_B), BF16), y, y] + cast_shapes,
        scratch_shapes=[pltpu.VMEM((n_st, 2 * hw, 2 * tq), BF16),
                        pltpu.VMEM((3, tk, hw), BF16),
                        pltpu.VMEM((tk, tq), F32),
                        pltpu.VMEM((2, n_st, tk, 2 * tq), F32),
                        pltpu.VMEM((2, n_st, 1, 2 * tq), F32),
                        pltpu.VMEM((2, n_st, tk, 2 * tq), BF16),
                        pltpu.VMEM((2, n_st, 1, 2 * tq), F32),
                        pltpu.VMEM((n_st, 1, 2 * tq), F32),
                        pltpu.VMEM((n_st, V_ROWS, 2 * tq), F32),
                        pltpu.VMEM((2 * (D_A // LANES), LANES, LANES), F32)],
        compiler_params=_params(3),
        name="mixers",
    )(q, k, vt, lam_vecs, sub_g, qg, kg, r, v, kk, lw0, k0, a0, r, v, kk, lw1, k1, a1, *next_weights)


def _merge_body(yf_ref, yb_ref, gate_ref, bonus_ref, ob_ref, gt_ref, x_ref, lng_ref, lnb_ref, bd_ref,
                wb0_ref, wb1_ref, wo_ref, g2_ref, win_ref, wout_ref, o_ref):
    bd = bd_ref[...]
    y = yf_ref[...] + yb_ref[...]
    yc = y - _group_sums(y, bd, passes=2)
    var = _group_sums(yc * yc, bd)
    yn = yc * lax.rsqrt(var + GN_EPS) * lng_ref[...] + lnb_ref[...]
    oa = (yn + bonus_ref[...]) * gate_ref[...]
    gts = gt_ref[...].astype(F32)
    merged = (gts[:, :D_MODEL] * _mm(oa, wb0_ref[...])
              + gts[:, D_MODEL:] * jnp.dot(ob_ref[...], wb1_ref[...], preferred_element_type=F32))
    o_ref[...] = _ffn_math(x_ref[...] + _mm(merged, wo_ref[...]), g2_ref, win_ref, wout_ref)


def _merge(yf, yb, gate, bonus, ob, gts, x2, ln_g, ln_b, bd_mean, wb, wo, g2, w_in, w_out, layer):
    t = x2.shape[0]
    tm = min(TOKEN_TILE, t)
    row = lambda width: pl.BlockSpec((tm, width), lambda i: (i, 0))
    branch = lambda g: pl.BlockSpec((None, D_A, D_MODEL), lambda *_: (0, g, 0), pipeline_mode=pl.Buffered(1))
    return pl.pallas_call(
        _merge_body,
        grid=(t // tm,),
        in_specs=[row(D_A), row(D_A), row(D_A), row(D_A), row(D_B), row(2 * D_MODEL), row(D_MODEL),
                  _layer_spec((1, D_A), layer), _layer_spec((1, D_A), layer), _const_spec((MXU_WIDTH, MXU_WIDTH)),
                  branch(0), branch(1),
                  _weight_spec((D_MODEL, D_MODEL)),
                  _layer_spec((1, D_MODEL), layer), _weight_spec((D_MODEL, 2 * D_FF)),
                  _weight_spec((D_FF, D_MODEL))],
        out_specs=row(D_MODEL),
        out_shape=jax.ShapeDtypeStruct((t, D_MODEL), F32),
        compiler_params=_params(1),
        name="merge",
    )(yf, yb, gate, bonus, ob, gts, x2, ln_g, ln_b, bd_mean, wb, wb, wo, g2, w_in, w_out)


def _block_diag2(w):
    z = jnp.zeros_like(w[:, 0])
    return jnp.concatenate([jnp.concatenate([w[:, 0], z], axis=2), jnp.concatenate([z, w[:, 1]], axis=2)],
                           axis=1)


def kernel(x, norm_ffn1, ffn1_in, ffn1_out, norm_mix, w_in, rwkv_mu, decay_w0, decay_w2, iclr_a0, iclr_a2,
           gate_g2, k_k, k_a, r_k, ln_x_g, ln_x_b, q_gain, k_gain, diff_lambda, subln_g, w_branch, w_out,
           norm_ffn2, ffn2_in, ffn2_out):
    batch, seq, d = x.shape
    depth = norm_ffn1.shape[0]
    assert seq % TOKEN_TILE == 0 and d == D_MODEL
    x2 = x.reshape(batch * seq, d)
    head_id = jnp.arange(MXU_WIDTH) // HEAD_A
    same = head_id[:, None] == head_id[None, :]
    bd_sum = same.astype(BF16)
    bd_mean = (same.astype(F32) / HEAD_A).astype(BF16)
    rows = lambda a: a.reshape(depth, 1, -1)
    bf = lambda a: a.astype(BF16)
    n_qk = D_B // HEAD_B
    qg = rows(jnp.tile(q_gain, (1, n_qk))) * (HEAD_B ** -0.5 * LOG2E)
    kg = rows(jnp.tile(k_gain, (1, n_qk)))
    big = (ffn1_in, ffn1_out, w_in, w_branch.reshape(depth, N_BRANCH * D_A, D_MODEL), w_out, ffn2_in, ffn2_out)
    weights = [bf(w[:1]) for w in big[:2]]
    prep_w = (rows(rwkv_mu), rows(decay_w0), bf(_block_diag2(decay_w2)), rows(iclr_a0),
              bf(_block_diag2(iclr_a2)), bf(gate_g2), rows(k_k), rows(k_a), rows(r_k))
    sub_g = rows(subln_g)
    for l in range(depth):
        lam_init = 0.8 - 0.6 * math.exp(-0.3 * l)
        x2, *converted = _ffn(x2, rows(norm_ffn1), weights[0], weights[1], l, big[2:] if l == 0 else ())
        f1_in, f1_out, w_proj, w_br, w_o, f2_in, f2_out = weights + converted
        q, k, vt, gts, r, vv, kk, lw0, k0, a0, lw1, k1, a1, gate, bonus = _proj(
            x2, seq, rows(norm_mix), w_proj, qg, kg, bd_mean, *prep_w, bd_sum, l)
        ob, yf, yb, *weights = _mixers(q, k, vt, diff_lambda, sub_g, qg, kg, r, vv, kk, lw0, k0, a0, lw1, k1, a1,
                                      lam_init, batch, seq, l, big if l + 1 < depth else ())
        x2 = _merge(yf, yb, gate, bonus, ob, gts, x2, rows(ln_x_g), rows(ln_x_b), bd_mean,
                    w_br, w_o, rows(norm_ffn2), f2_in, f2_out, l)
    return x2.reshape(batch, seq, d)
```

```python
import functools
import math

import jax
import jax.numpy as jnp
from jax import lax
from jax.experimental import pallas as pl
from jax.experimental.pallas import tpu as pltpu

F32 = jnp.float32
BF16 = jnp.bfloat16

D_MODEL = 1024
D_A = 512
HEAD_A = 64
LORA = 64
LORA_GATE = 128
C_RWKV = 3 * D_A + 2 * LORA + 2 * LORA + LORA_GATE
D_B = 512
HEAD_B = 64
N_HEADS_B = D_B // (2 * HEAD_B)
D_FF = 2816
N_BRANCH = 2
DECAY_SCALE = math.exp(-0.5)
GN_EPS = 64e-5
NORM_EPS = 1e-6
LOG2E = math.log2(math.e)
F32_ZERO_EXP2 = 150.0
ROUNDING_MARGIN = 1.02

LANES = 128
SUBLANES = 8
BF16_ROWS = 16
VMEM_LIMIT = 56 * 1024 * 1024

TOKEN_TILE = 512
MXU_WIDTH = 256
FF_SPLITS = (0, 6 * MXU_WIDTH, D_FF)
CHUNK = 64
SCAN_CHUNKS_PER_STEP = 2
ATTN_TQ = 256
V_ROWS = 2 * HEAD_B + 16

_NT = (((1,), (1,)), ((), ()))


def _mm(a, b):
    return jnp.dot(a.astype(BF16), b.astype(BF16), preferred_element_type=F32)


def _mm_nt(a, b):
    return lax.dot_general(a.astype(BF16), b.astype(BF16), _NT, preferred_element_type=F32)


def _split_bf16(x, n):
    parts = []
    for _ in range(n - 1):
        hi = x.astype(BF16)
        parts.append(hi)
        x = x - hi.astype(F32)
    parts.append(x.astype(BF16))
    return parts


def _group_sums(x, bd, passes=1):
    width = bd.shape[0]
    tiles = [x[:, c:c + width] for c in range(0, x.shape[1], width)]
    return jnp.concatenate([sum(jnp.dot(term, bd, preferred_element_type=F32) for term in _split_bf16(t, passes))
                            for t in tiles], axis=1)


def _rms_norm(x, g):
    return x * lax.rsqrt(jnp.mean(x * x, axis=-1, keepdims=True) + NORM_EPS) * g


def _const_spec(shape):
    return pl.BlockSpec(shape, lambda *_: (0,) * len(shape), pipeline_mode=pl.Buffered(1))


def _layer_spec(shape, layer, *lead):
    index = (layer,) + lead + (0,) * len(shape)
    return pl.BlockSpec((None,) * (1 + len(lead)) + shape, lambda *_: index, pipeline_mode=pl.Buffered(1))


def _weight_spec(shape, *lead):
    return _layer_spec(shape, 0, *lead)


def _conversion_specs(weights, src_layer, n_steps, step):
    ins, outs, shapes = [], [], []
    for w in weights:
        _, w_rows, w_cols = w.shape
        rb = next(r for r in range(BF16_ROWS, w_rows + 1, BF16_ROWS) if w_rows % r == 0 and w_rows // r <= n_steps)
        blk = lambda *ids, last=w_rows // rb - 1: jnp.minimum(step(*ids), last)
        ins.append(pl.BlockSpec((None, rb, w_cols), lambda *ids, blk=blk: (src_layer, blk(*ids), 0)))
        outs.append(pl.BlockSpec((None, rb, w_cols), lambda *ids, blk=blk: (0, blk(*ids), 0)))
        shapes.append(jax.ShapeDtypeStruct((1, w_rows, w_cols), BF16))
    return ins, outs, shapes


def _convert(srcs, dsts):
    for src, dst in zip(srcs, dsts):
        dst[...] = src[...].astype(BF16)


def _params(n_axes):
    return pltpu.CompilerParams(dimension_semantics=("arbitrary",) * n_axes,
                                vmem_limit_bytes=VMEM_LIMIT)


def _ffn_math(x, g_ref, win_ref, wout_ref):
    h = _rms_norm(x, g_ref[...]).astype(BF16)
    acc = jnp.zeros(x.shape, F32)
    for lo, hi in zip(FF_SPLITS[:-1], FF_SPLITS[1:]):
        gate = jnp.dot(h, win_ref[:, lo:hi], preferred_element_type=F32)
        up = jnp.dot(h, win_ref[:, D_FF + lo:D_FF + hi], preferred_element_type=F32)
        act = (gate * jax.nn.sigmoid(gate) * up).astype(BF16)
        acc = acc + jnp.dot(act, wout_ref[lo:hi, :], preferred_element_type=F32)
    return x + 0.5 * acc


def _ffn_body(x_ref, g_ref, win_ref, wout_ref, *rest):
    n_conv = len(rest) // 2
    _convert(rest[:n_conv], rest[n_conv + 1:])
    rest[n_conv][...] = _ffn_math(x_ref[...], g_ref, win_ref, wout_ref)


def _ffn(x2, g, w_in, w_out, layer, convert=()):
    t = x2.shape[0]
    tm = min(TOKEN_TILE, t)
    conv_in, conv_out, conv_shapes = _conversion_specs(convert, layer, t // tm, lambda i: i)
    return pl.pallas_call(
        _ffn_body,
        grid=(t // tm,),
        in_specs=[pl.BlockSpec((tm, D_MODEL), lambda i: (i, 0)),
                  _layer_spec((1, D_MODEL), layer),
                  _weight_spec((D_MODEL, 2 * D_FF)),
                  _weight_spec((D_FF, D_MODEL))] + conv_in,
        out_specs=[pl.BlockSpec((tm, D_MODEL), lambda i: (i, 0))] + conv_out,
        out_shape=[jax.ShapeDtypeStruct((t, D_MODEL), F32)] + conv_shapes,
        compiler_params=_params(1),
        name="ffn",
    )(x2, g, w_in, w_out, *convert)


_Q0 = C_RWKV
_K0 = C_RWKV + D_B
_V0 = C_RWKV + 2 * D_B
_G0 = C_RWKV + 3 * D_B


def _proj_body(x_ref, xp_ref, xn_ref, g_ref, w_ref, qg_ref, kg_ref, bdm_ref, mu_ref, w0_ref, w2_ref, a0_ref,
               a2_ref, g2_ref, kk_ref, ka_ref, rk_ref, bds_ref,
               q_ref, k_ref, vt_ref, gt_ref, r_out, v_out, kk_out, lw0_out, k0_out, a0_out, lw1_out, k1_out,
               a1_out, gate_out, bonus_out, *, seq):
    i = pl.program_id(0)
    g = g_ref[...]
    h = _rms_norm(x_ref[...], g).astype(BF16)
    tm = h.shape[0]

    bd = bdm_ref[...]
    q = jnp.dot(h, w_ref[:, _Q0:_K0], preferred_element_type=F32)
    q_ref[...] = (q * lax.rsqrt(_group_sums(q * q, bd) + NORM_EPS) * qg_ref[...]).astype(BF16)
    k = jnp.dot(h, w_ref[:, _K0:_V0], preferred_element_type=F32)
    k_ref[...] = (k * lax.rsqrt(_group_sums(k * k, bd) + NORM_EPS) * kg_ref[...]).astype(BF16)
    vt = jnp.dot(h, w_ref[:, _V0:_G0], preferred_element_type=F32).T.astype(BF16)
    hw = 2 * HEAD_B
    ones = jnp.ones((V_ROWS - hw, vt.shape[1]), BF16)
    vt_ref[0] = jnp.concatenate([part for hd in range(N_HEADS_B)
                                 for part in (vt[hd * hw:(hd + 1) * hw], ones)], axis=0)
    gt_ref[...] = jax.nn.sigmoid(jnp.dot(h, w_ref[:, _G0:], preferred_element_type=F32)).astype(BF16)

    w_rwkv = w_ref[:, :C_RWKV]
    p = jnp.dot(h, w_rwkv, preferred_element_type=F32)
    edge = jnp.concatenate([xp_ref[...], xn_ref[...]], axis=0)
    p_edge = jnp.dot(_rms_norm(edge, g).astype(BF16), w_rwkv, preferred_element_type=F32)
    keep_prev = jnp.where((i * tm) % seq == 0, 0.0, 1.0)
    keep_next = jnp.where(((i + 1) * tm) % seq == 0, 0.0, 1.0)
    row = lax.broadcasted_iota(jnp.int32, p.shape, 0)
    prev = jnp.where(row == 0, keep_prev * p_edge[SUBLANES - 1:SUBLANES, :], pltpu.roll(p, 1, 0))
    nxt = jnp.where(row == tm - 1, keep_next * p_edge[SUBLANES:SUBLANES + 1, :], pltpu.roll(p, tm - 1, 0))
    p = p + mu_ref[...] * (0.5 * (prev + nxt) - p)

    r = p[:, :D_A]
    k = p[:, D_A:2 * D_A]
    v = p[:, 2 * D_A:3 * D_A]
    dw = p[:, 3 * D_A:3 * D_A + 2 * LORA]
    da = p[:, 3 * D_A + 2 * LORA:3 * D_A + 4 * LORA]
    dg = p[:, 3 * D_A + 4 * LORA:]
    bd = bds_ref[...]

    logw = -DECAY_SCALE * jax.nn.sigmoid(w0_ref[...] + _mm(jnp.tanh(dw), w2_ref[...]))
    iclr = jax.nn.sigmoid(a0_ref[...] + _mm(da, a2_ref[...]))
    gate_out[...] = _mm(jax.nn.sigmoid(dg), g2_ref[...])

    kk = k * kk_ref[...]
    norm = jnp.sqrt(_group_sums(kk * kk, bd))
    kk_out[...] = (kk / jnp.maximum(norm, 1e-12)).astype(kk_out.dtype)

    ka = ka_ref[...]
    a_f = iclr[:, :D_A]
    a_b = iclr[:, D_A:]
    k_f = k * (1.0 + (a_f - 1.0) * ka)
    k_b = k * (1.0 + (a_b - 1.0) * ka)
    bonus = _group_sums(r * rk_ref[...] * (k_f + k_b), bd, passes=2)
    r_out[...] = r.astype(r_out.dtype)
    v_out[...] = v.astype(v_out.dtype)
    lw0_out[...] = logw[:, :D_A]
    k0_out[...] = k_f.astype(k0_out.dtype)
    a0_out[...] = a_f.astype(a0_out.dtype)
    lw1_out[...] = logw[:, D_A:]
    k1_out[...] = k_b.astype(k1_out.dtype)
    a1_out[...] = a_b.astype(a1_out.dtype)
    bonus_out[...] = bonus * v


def _proj(x2, seq, g, w, qg, kg, bd_mean, mu, w0, w2cat, a0, a2cat, g2, k_k, k_a, r_k, bd_sum, layer):
    t = x2.shape[0]
    tm = TOKEN_TILE
    c_in = w.shape[2]
    nb = tm // SUBLANES
    last = t // SUBLANES - 1
    row = lambda width: pl.BlockSpec((tm, width), lambda i: (i, 0))
    wide = jax.ShapeDtypeStruct((t, D_A), F32)
    half = jax.ShapeDtypeStruct((t, D_A), BF16)
    return pl.pallas_call(
        functools.partial(_proj_body, seq=seq),
        grid=(t // tm,),
        in_specs=[row(D_MODEL),
                  pl.BlockSpec((SUBLANES, D_MODEL), lambda i: (jnp.maximum(i * nb - 1, 0), 0)),
                  pl.BlockSpec((SUBLANES, D_MODEL), lambda i: (jnp.minimum((i + 1) * nb, last), 0)),
                  _layer_spec((1, D_MODEL), layer), _weight_spec((D_MODEL, c_in)),
                  _layer_spec((1, D_B), layer), _layer_spec((1, D_B), layer), _const_spec((MXU_WIDTH, MXU_WIDTH)),
                  _layer_spec((1, C_RWKV), layer),
                  _layer_spec((1, 2 * D_A), layer), _layer_spec((2 * LORA, 2 * D_A), layer),
                  _layer_spec((1, 2 * D_A), layer), _layer_spec((2 * LORA, 2 * D_A), layer),
                  _layer_spec((LORA_GATE, D_A), layer),
                  _layer_spec((1, D_A), layer), _layer_spec((1, D_A), layer), _layer_spec((1, D_A), layer),
                  _const_spec((MXU_WIDTH, MXU_WIDTH))],
        out_specs=[row(D_B), row(D_B),
                   pl.BlockSpec((1, N_HEADS_B * V_ROWS, tm), lambda i: (i, 0, 0)), row(2 * D_MODEL)]
                  + [row(D_A)] * 11,
        out_shape=[jax.ShapeDtypeStruct((t, D_B), BF16),
                   jax.ShapeDtypeStruct((t, D_B), BF16),
                   jax.ShapeDtypeStruct((t // tm, N_HEADS_B * V_ROWS, tm), BF16),
                   jax.ShapeDtypeStruct((t, 2 * D_MODEL), BF16),
                   half, half, half, wide, half, half, wide, half, half, wide, wide],
        compiler_params=_params(1),
        name="proj",
    )(x2, x2, x2, g, w, qg, kg, bd_mean, mu, w0, w2cat, a0, a2cat, g2, k_k, k_a, r_k, bd_sum)


def _scan_step(rf, vf, kkf, lwf, kf, af, rb, vb, kkb, lwb, kb, ab, yf_ref, yb_ref, h_ref):
    L = CHUNK
    n_sub = rf.shape[0] // L
    n_pairs = D_A // LANES
    ti = lax.broadcasted_iota(jnp.int32, (L, L), 0)
    tj = lax.broadcasted_iota(jnp.int32, (L, L), 1)
    pi = lax.broadcasted_iota(jnp.int32, (L, LANES), 0)
    lane = lax.broadcasted_iota(jnp.int32, (L, LANES), 1)
    pj = lane & (HEAD_A - 1)
    lane_lo = lane < HEAD_A
    eye = jnp.where(pj == pi, 1.0, 0.0)
    blk = lambda s: (pi // s) == (pj // s)
    row2 = lax.broadcasted_iota(jnp.int32, (2 * L, LANES), 0)
    col2 = lax.broadcasted_iota(jnp.int32, (2 * L, LANES), 1)
    same_head = (row2 // HEAD_A) == (col2 // HEAD_A)
    diag2 = row2 == col2
    zeros_l = jnp.zeros((L, LANES), F32)
    zeros_2l = jnp.zeros((2 * L, LANES), F32)

    def stack(x):
        return jnp.concatenate([jnp.where(lane_lo, x, 0.0), jnp.where(lane_lo, 0.0, x)], axis=0)

    def mmp(x, y):
        return _mm(x, stack(y))

    cat = jnp.concatenate
    chains = []
    for reverse, refs in ((False, (rf, vf, kkf, lwf, kf, af)), (True, (rb, vb, kkb, lwb, kb, ab))):
        tri = ((tj >= ti) if reverse else (tj <= ti)).astype(BF16)
        strict = (pj > pi) if reverse else (pj < pi)
        incl = (pj >= pi) if reverse else (pj <= pi)
        steps = []
        for sub in (range(n_sub - 1, -1, -1) if reverse else range(n_sub)):
            rows = slice(sub * L, (sub + 1) * L)
            r, v, kk, lw, k, a = (ref[rows, :].astype(F32) for ref in refs)
            G = sum(jnp.dot(tri, part, preferred_element_type=F32) for part in _split_bf16(lw, 2))
            g_end = G[0:1] if reverse else G[L - 1:L]
            gam_end = jnp.exp(g_end)
            inv_gam = jnp.exp(-G)
            rel_end = jnp.exp(g_end - G)
            b = kk * a
            full = dict(At=-kk * jnp.exp(G - lw), Bt=b * inv_gam, Kt=k * inv_gam, Rt=r * jnp.exp(G),
                        Kh=k * rel_end, Bh=b * rel_end, v=v)
            pairs = []
            for p in range(n_pairs):
                sl = slice(p * LANES, (p + 1) * LANES)
                c = {name: val[:, sl] for name, val in full.items()}
                c.update(strict=strict, incl=incl, gam_end=gam_end[:, sl], rows=rows)
                pairs.append(c)
            steps.append(pairs)
        chains.append(steps)
    flat = [c for steps in chains for pairs in steps for c in pairs]

    for c in flat:
        a4 = _mm_nt(cat([c["At"], c["Rt"]], axis=0), cat([stack(c["Bt"]), stack(c["Kt"])], axis=0))
        c["A_ab"] = jnp.where(c["strict"], a4[:L, :LANES], 0.0)
        c["A_ak"] = jnp.where(c["strict"], a4[:L, LANES:], 0.0)
        c["A_rb"] = jnp.where(c["incl"], a4[L:, :LANES], 0.0)
        c["A_rk"] = jnp.where(c["incl"], a4[L:, LANES:], 0.0)
    for c in flat:
        c["A0"] = jnp.where(blk(8), c["A_ab"], 0.0)
        c["A2"] = mmp(c["A0"], c["A0"])
    for c in flat:
        t1 = eye + c["A0"]
        both = mmp(cat([c["A2"], t1], axis=0), c["A2"])
        c["A4"] = both[:L]
        c["T"] = t1 + both[L:]
    for c in flat:
        c["T"] = c["T"] + mmp(c["T"], c["A4"])
    s = 8
    while s < L:
        outer = blk(2 * s) & jnp.logical_not(blk(s))
        for c in flat:
            c["TO"] = mmp(c["T"], jnp.where(outer, c["A_ab"], 0.0))
        for c in flat:
            c["T"] = c["T"] + mmp(c["TO"], c["T"])
        s *= 2
    for c in flat:
        c["AkV"] = mmp(c["A_ak"], c["v"])
    for c in flat:
        wu = _mm(c["T"], cat([stack(c["At"]), stack(c["AkV"])], axis=1))
        c["W"], c["U0"] = wu[:, :LANES], wu[:, LANES:]
    for c in flat:
        kb_t = cat([c["Kh"], c["Bh"]], axis=0).T
        mn = _mm(kb_t, cat([cat([zeros_l, c["v"]], axis=1), cat([c["W"], c["U0"]], axis=1)], axis=0))
        c["M"] = jnp.where(same_head, mn[:, :LANES], 0.0) + jnp.where(diag2, c["gam_end"], 0.0)
        c["N"] = jnp.where(same_head, mn[:, LANES:], 0.0)
    for c in flat:
        qy = _mm(cat([c["A_rb"], c["A_rk"]], axis=1),
                 cat([cat([stack(c["W"]), stack(c["U0"])], axis=1),
                      cat([zeros_2l, stack(c["v"])], axis=1)], axis=0))
        c["Q"] = c["Rt"] + qy[:, :LANES]
        c["Y0"] = qy[:, LANES:]
    for d, (steps, y_ref) in enumerate(zip(chains, (yf_ref, yb_ref))):
        for pairs in steps:
            ys = []
            for p, c in enumerate(pairs):
                qm_h = _mm(cat([c["Q"], c["M"]], axis=0), h_ref[d * n_pairs + p])
                ys.append(qm_h[:L] + c["Y0"])
                h_ref[d * n_pairs + p] = qm_h[L:] + c["N"]
            y_ref[pairs[0]["rows"], :] = cat(ys, axis=1)


def _bias_features(slope, n, width, key_side):
    pos = lax.broadcasted_iota(jnp.int32, (n, width), 0).astype(F32)
    lane = lax.broadcasted_iota(jnp.int32, (n, width), 1)
    hi, mid, lo = (t.astype(F32) for t in _split_bf16(slope * pos, 3))
    terms = jnp.where(lane % 3 == 0, hi, jnp.where(lane % 3 == 1, mid, lo))
    if key_side:
        return jnp.where(lane < 3, -1.0, jnp.where(lane < 6, terms, 0.0))
    return jnp.where(lane < 3, terms, jnp.where(lane < 6, 1.0, 0.0))


def _attn_body(q_ref, k_ref, vt_ref, lv_ref, sg_ref, qg_ref, kg_ref, o_ref, qs_ref, feat_ref, src_ref,
               s_ref, mx_ref, p_ref, al_ref, m_ref, acc_ref, *, lam_init, before_loop):
    tq = src_ref.shape[1]
    n_kv, _, tk = vt_ref.shape
    seq = n_kv * tk
    h = pl.program_id(1)
    i = pl.program_id(2)
    slope_bits = (127 - 2 * (h + 1)) << 23
    slope = (lax.bitcast_convert_type(jnp.full((1, 1), slope_bits, jnp.int32), jnp.float32).astype(F32)
             * LOG2E)
    bound =(HEAD_B * ROUNDING_MARGIN * jnp.max(jnp.abs(qg_ref[...]), axis=-1, keepdims=True)
             * jnp.max(jnp.abs(kg_ref[...]), axis=-1, keepdims=True))
    reach = jnp.minimum((F32_ZERO_EXP2 + 2.0 * bound) / slope + 1.0, float(seq)).astype(jnp.int32)[0, 0]

    n_st = tk // tq
    hw = 2 * HEAD_B

    @pl.when(i == 0)
    def _():
        kv_pos = lax.broadcasted_iota(jnp.int32, (tk, tq), 0)
        q_pos = lax.broadcasted_iota(jnp.int32, (tk, tq), 1)
        src_ref[...] = slope * (q_pos - kv_pos).astype(F32)
        key_feat = _bias_features(slope, tk, hw, key_side=True)
        feat_ref[0] = key_feat.astype(BF16)
        feat_ref[1] = (-key_feat).astype(BF16)
        feat_ref[2] = jnp.zeros((tk, hw), BF16)
        q_feat = _bias_features(slope, tq, hw, key_side=False).T.astype(BF16)
        for st in range(n_st):
            qs_ref[st, hw:, :tq] = q_feat
            qs_ref[st, hw:, tq:] = q_feat

    chan =lax.broadcasted_iota(jnp.int32, (hw, tq), 0)
    for st in range(n_st):
        q_t = q_ref[st * tq:(st + 1) * tq, :].astype(F32).T.astype(BF16)
        zero = jnp.zeros_like(q_t)
        qs_ref[st, :hw, :tq] = jnp.where(chan < HEAD_B, q_t, zero)
        qs_ref[st, :hw, tq:] = jnp.where(chan < HEAD_B, zero, q_t)
    q0 = i * tk
    j_lo = jnp.maximum(q0 - reach, 0) // tk
    j_hi = jnp.minimum(q0 + tk - 1 + reach, seq - 1) // tk + 1
    n_left = i - j_lo
    n_off = n_left + (j_hi - i - 1)

    def scores(j, side, st):
        kb = k_ref[pl.ds(pl.multiple_of(j * tk, tk), tk), :]
        return jnp.dot(jnp.concatenate([kb, feat_ref[side]], axis=1), qs_ref[st],
                       preferred_element_type=F32)

    def softmax_update(s, s_max, cst, st):
        m_old = m_ref[st]
        m_new = jnp.maximum(m_old, s_max + cst)
        alpha = jnp.exp2(m_old - m_new)
        p = jnp.exp2(s - (m_new - cst))
        m_ref[st] = m_new
        return alpha, p.astype(BF16)

    def block(n):
        j = jnp.clip(j_lo + n + (n >= n_left).astype(jnp.int32), 0, n_kv - 1)
        j = jnp.where(n < 0, i, j)
        return j, (j > i).astype(jnp.int32), jnp.logical_and(n >= 0, n < n_off)

    def qk_stage(n, slot):
        j, side, _ = block(n)
        for st in range(n_st):
            s = scores(j, side, st)
            s_ref[slot, st] = s
            mx_ref[slot, st] = jnp.max(s, axis=0, keepdims=True)

    def softmax_stage(n, slot):
        j, _, valid = block(n)
        for st in range(n_st):
            dist0 = jnp.abs(q0 + st * tq - j * tk).astype(F32)
            cst = jnp.where(valid, -slope * dist0, -1e30)
            alpha, p = softmax_update(s_ref[slot, st], mx_ref[slot, st], cst, st)
            al_ref[slot, st] = alpha
            p_ref[slot, st] = p

    def pv_stage(n, slot):
        j, _, _ = block(n)
        for st in range(n_st):
            acc_ref[st] = al_ref[slot, st] * acc_ref[st] + jnp.dot(vt_ref[j], p_ref[slot, st],
                                                                    preferred_element_type=F32)

    m_ref[...] = jnp.full(m_ref.shape, -1e30, F32)
    acc_ref[...] = jnp.zeros(acc_ref.shape, F32)
    diag_scores = [scores(i, 2, st) for st in range(n_st)]
    qk_stage(0, 0)
    src = src_ref[...]
    for st in range(n_st):
        bias = jnp.abs(src + slope * float(st * tq))
        s = diag_scores[st] - jnp.concatenate([bias, bias], axis=1)
        alpha, p = softmax_update(s, jnp.max(s, axis=0, keepdims=True), jnp.zeros((1, 1), F32), st)
        al_ref[1, st] = alpha
        p_ref[1, st] = p

    before_loop()

    def pipelined_pair(t, carry):
        n = 2 * t
        qk_stage(n + 1, 1)
        softmax_stage(n, 0)
        pv_stage(n - 1, 1)
        qk_stage(n + 2, 0)
        softmax_stage(n + 1, 1)
        pv_stage(n, 0)
        return carry

    n_pairs = n_off // 2
    lax.fori_loop(0, n_pairs, pipelined_pair, 0)
    last = 2 * n_pairs

    @pl.when(n_off % 2 == 1)
    def _():
        softmax_stage(last, 0)
        pv_stage(last - 1, 1)
        pv_stage(last, 0)

    @pl.when(n_off % 2 == 0)
    def _():
        pv_stage(last - 1, 1)

    lv = lv_ref[...]
    lam = (jnp.exp(jnp.sum(lv[0:1] * lv[1:2], axis=-1, keepdims=True))
           - jnp.exp(jnp.sum(lv[2:3] * lv[3:4], axis=-1, keepdims=True)) + lam_init)
    for st in range(n_st):
        acc = acc_ref[st, :hw, :]
        l = acc_ref[st, hw:hw + 1, :]
        o_t = acc[:, :tq] / l[:, :tq] - lam * (acc[:, tq:] / l[:, tq:])
        o_t = o_t * lax.rsqrt(jnp.mean(o_t * o_t, axis=0, keepdims=True) + NORM_EPS)
        o_ref[st * tq:(st + 1) * tq, :] = (o_t.T * sg_ref[...] * (1.0 - lam_init)).astype(o_ref.dtype)


def _mixers_body(q_ref, k_ref, vt_ref, lv_ref, sg_ref, qg_ref, kg_ref,
                 rf, vf, kkf, lwf, kf, af, rb, vb, kkb, lwb, kb, ab, *rest, lam_init, n_scan, n_cast):
    cast_in, (o_ref, yf_ref, yb_ref), rest = rest[:n_cast], rest[n_cast:n_cast + 3], rest[n_cast + 3:]
    cast_out, scratch = rest[:n_cast], rest[n_cast:]
    *attn_scratch, h_ref = scratch
    step = (pl.program_id(0) * pl.num_programs(1) + pl.program_id(1)) * pl.num_programs(2) + pl.program_id(2)

    @pl.when(step % n_scan == 0)
    def _():
        h_ref[...] = jnp.zeros(h_ref.shape, F32)

    def side_work():
        _scan_step(rf, vf, kkf, lwf, kf, af, rb, vb, kkb, lwb, kb, ab, yf_ref, yb_ref, h_ref)
        _convert(cast_in, cast_out)

    _attn_body(q_ref, k_ref, vt_ref, lv_ref, sg_ref, qg_ref, kg_ref, o_ref, *attn_scratch, lam_init=lam_init,
               before_loop=side_work)


def _mixers(q, k, vt, lam_vecs, sub_g, qg, kg, r, v, kk, lw0, k0, a0, lw1, k1, a1, lam_init, batch, seq, layer,
            next_weights):
    t = q.shape[0]
    tk = vt.shape[2]
    tq = ATTN_TQ
    n_st = tk // tq
    nq = seq // tk
    n_kv = seq // tk
    hw = 2 * HEAD_B
    rows = SCAN_CHUNKS_PER_STEP * CHUNK
    n_scan = seq // rows
    assert N_HEADS_B * nq == n_scan
    step = lambda b, h, i: (b * N_HEADS_B + h) * nq + i
    fwd = pl.BlockSpec((rows, D_A), lambda b, h, i: (step(b, h, i), 0))
    bwd = pl.BlockSpec((rows, D_A), lambda b, h, i: (
        (step(b, h, i) // n_scan) * n_scan + n_scan - 1 - step(b, h, i) % n_scan, 0))
    y = jax.ShapeDtypeStruct((t, D_A), F32)
    cast_in, cast_out, cast_shapes = _conversion_specs(next_weights, layer + 1, batch * N_HEADS_B * nq, step)
    return pl.pallas_call(
        functools.partial(_mixers_body, lam_init=lam_init, n_scan=n_scan, n_cast=len(next_weights)),
        grid=(batch, N_HEADS_B, nq),
        in_specs=[pl.BlockSpec((tk, hw), lambda b, h, i: (b * nq + i, h)),
                  pl.BlockSpec((seq, hw), lambda b, h, i: (b, h)),
                  pl.BlockSpec((n_kv, V_ROWS, tk), lambda b, h, i: (b, h, 0)),
                  _layer_spec((4, HEAD_B), layer), _layer_spec((1, hw), layer),
                  _layer_spec((1, D_B), layer), _layer_spec((1, D_B), layer)] + [fwd] * 6 + [bwd] * 6 + cast_in,
        out_specs=[pl.BlockSpec((tk, hw), lambda b, h, i: (b * nq + i, h)), fwd, bwd] + cast_out,
        out_shape=[jax.ShapeDtypeStruct((t, D_B), BF16), y, y] + cast_shapes,
        scratch_shapes=[pltpu.VMEM((n_st, 2 * hw, 2 * tq), BF16),
                        pltpu.VMEM((3, tk, hw), BF16),
                        pltpu.VMEM((tk, tq), F32),
                        pltpu.VMEM((2, n_st, tk, 2 * tq), F32),
                        pltpu.VMEM((2, n_st, 1, 2 * tq), F32),
                        pltpu.VMEM((2, n_st, tk, 2 * tq), BF16),
                        pltpu.VMEM((2, n_st, 1, 2 * tq), F32),
                        pltpu.VMEM((n_st, 1, 2 * tq), F32),
                        pltpu.VMEM((n_st, V_ROWS, 2 * tq), F32),
                        pltpu.VMEM((2 * (D_A // LANES), LANES, LANES), F32)],
        compiler_params=_params(3),
        name="mixers",
    )(q, k, vt, lam_vecs, sub_g, qg, kg, r, v, kk, lw0, k0, a0, r, v, kk, lw1, k1, a1, *next_weights)


def _merge_body(yf_ref, yb_ref, gate_ref, bonus_ref, ob_ref, gt_ref, x_ref, lng_ref, lnb_ref, bd_ref,
                wb0_ref, wb1_ref, wo_ref, g2_ref, win_ref, wout_ref, o_ref):
    bd = bd_ref[...]
    y = yf_ref[...] + yb_ref[...]
    yc = y - _group_sums(y, bd, passes=2)
    var = _group_sums(yc * yc, bd)
    yn = yc * lax.rsqrt(var + GN_EPS) * lng_ref[...] + lnb_ref[...]
    oa = (yn + bonus_ref[...]) * gate_ref[...]
    gts = gt_ref[...].astype(F32)
    merged = (gts[:, :D_MODEL] * _mm(oa, wb0_ref[...])
              + gts[:, D_MODEL:] * jnp.dot(ob_ref[...], wb1_ref[...], preferred_element_type=F32))
    o_ref[...] = _ffn_math(x_ref[...] + _mm(merged, wo_ref[...]), g2_ref, win_ref, wout_ref)


def _merge(yf, yb, gate, bonus, ob, gts, x2, ln_g, ln_b, bd_mean, wb, wo, g2, w_in, w_out, layer):
    t = x2.shape[0]
    tm = min(TOKEN_TILE, t)
    row = lambda width: pl.BlockSpec((tm, width), lambda i: (i, 0))
    branch = lambda g: pl.BlockSpec((None, D_A, D_MODEL), lambda *_: (0, g, 0), pipeline_mode=pl.Buffered(1))
    return pl.pallas_call(
        _merge_body,
        grid=(t // tm,),
        in_specs=[row(D_A), row(D_A), row(D_A), row(D_A), row(D_B), row(2 * D_MODEL), row(D_MODEL),
                  _layer_spec((1, D_A), layer), _layer_spec((1, D_A), layer), _const_spec((MXU_WIDTH, MXU_WIDTH)),
                  branch(0), branch(1),
                  _weight_spec((D_MODEL, D_MODEL)),
                  _layer_spec((1, D_MODEL), layer), _weight_spec((D_MODEL, 2 * D_FF)),
                  _weight_spec((D_FF, D_MODEL))],
        out_specs=row(D_MODEL),
        out_shape=jax.ShapeDtypeStruct((t, D_MODEL), F32),
        compiler_params=_params(1),
        name="merge",
    )(yf, yb, gate, bonus, ob, gts, x2, ln_g, ln_b, bd_mean, wb, wb, wo, g2, w_in, w_out)


def _block_diag2(w):
    z = jnp.zeros_like(w[:, 0])
    return jnp.concatenate([jnp.concatenate([w[:, 0], z], axis=2), jnp.concatenate([z, w[:, 1]], axis=2)],
                           axis=1)


def kernel(x, norm_ffn1, ffn1_in, ffn1_out, norm_mix, w_in, rwkv_mu, decay_w0, decay_w2, iclr_a0, iclr_a2,
           gate_g2, k_k, k_a, r_k, ln_x_g, ln_x_b, q_gain, k_gain, diff_lambda, subln_g, w_branch, w_out,
           norm_ffn2, ffn2_in, ffn2_out):
    batch, seq, d = x.shape
    depth = norm_ffn1.shape[0]
    assert seq % TOKEN_TILE == 0 and d == D_MODEL
    x2 = x.reshape(batch * seq, d)
    head_id = jnp.arange(MXU_WIDTH) // HEAD_A
    same = head_id[:, None] == head_id[None, :]
    bd_sum = same.astype(BF16)
    bd_mean = (same.astype(F32) / HEAD_A).astype(BF16)
    rows = lambda a: a.reshape(depth, 1, -1)
    bf = lambda a: a.astype(BF16)
    n_qk = D_B // HEAD_B
    qg = rows(jnp.tile(q_gain, (1, n_qk))) * (HEAD_B ** -0.5 * LOG2E)
    kg = rows(jnp.tile(k_gain, (1, n_qk)))
    big = (ffn1_in, ffn1_out, w_in, w_branch.reshape(depth, N_BRANCH * D_A, D_MODEL), w_out, ffn2_in, ffn2_out)
    weights = [bf(w[:1]) for w in big[:2]]
    prep_w = (rows(rwkv_mu), rows(decay_w0), bf(_block_diag2(decay_w2)), rows(iclr_a0),
              bf(_block_diag2(iclr_a2)), bf(gate_g2), rows(k_k), rows(k_a), rows(r_k))
    sub_g = rows(subln_g)
    for l in range(depth):
        lam_init = 0.8 - 0.6 * math.exp(-0.3 * l)
        x2, *converted = _ffn(x2, rows(norm_ffn1), weights[0], weights[1], l, big[2:] if l == 0 else ())
        f1_in, f1_out, w_proj, w_br, w_o, f2_in, f2_out = weights + converted
        q, k, vt, gts, r, vv, kk, lw0, k0, a0, lw1, k1, a1, gate, bonus = _proj(
            x2, seq, rows(norm_mix), w_proj, qg, kg, bd_mean, *prep_w, bd_sum, l)
        ob, yf, yb, *weights = _mixers(q, k, vt, diff_lambda, sub_g, qg, kg, r, vv, kk, lw0, k0, a0, lw1, k1, a1,
                                      lam_init, batch, seq, l, big if l + 1 < depth else ())
        x2 = _merge(yf, yb, gate, bonus, ob, gts, x2, rows(ln_x_g), rows(ln_x_b), bd_mean,
                    w_br, w_o, rows(norm_ffn2), f2_in, f2_out, l)
    return x2.reshape(batch, seq, d)
```

```python
import functools
import math

import jax
import jax.numpy as jnp
from jax import lax
from jax.experimental import pallas as pl
from jax.experimental.pallas import tpu as pltpu

F32 = jnp.float32
BF16 = jnp.bfloat16

D_MODEL = 1024
D_A = 512
HEAD_A = 64
LORA = 64
LORA_GATE = 128
C_RWKV = 3 * D_A + 2 * LORA + 2 * LORA + LORA_GATE
D_B = 512
HEAD_B = 64
N_HEADS_B = D_B // (2 * HEAD_B)
D_FF = 2816
N_BRANCH = 2
DECAY_SCALE = math.exp(-0.5)
GN_EPS = 64e-5
NORM_EPS = 1e-6
LOG2E = math.log2(math.e)
F32_ZERO_EXP2 = 150.0
ROUNDING_MARGIN = 1.02

LANES = 128
SUBLANES = 8
BF16_ROWS = 16
VMEM_LIMIT = 56 * 1024 * 1024

TOKEN_TILE = 512
MXU_WIDTH = 256
FF_SPLITS = (0, 6 * MXU_WIDTH, D_FF)
CHUNK = 64
SCAN_CHUNKS_PER_STEP = 2
ATTN_TQ = 256
V_ROWS = 2 * HEAD_B + 16

_NT = (((1,), (1,)), ((), ()))


def _mm(a, b):
    return jnp.dot(a.astype(BF16), b.astype(BF16), preferred_element_type=F32)


def _mm_nt(a, b):
    return lax.dot_general(a.astype(BF16), b.astype(BF16), _NT, preferred_element_type=F32)


def _split_bf16(x, n):
    parts = []
    for _ in range(n - 1):
        hi = x.astype(BF16)
        parts.append(hi)
        x = x - hi.astype(F32)
    parts.append(x.astype(BF16))
    return parts


def _group_sums(x, bd, passes=1):
    width = bd.shape[0]
    tiles = [x[:, c:c + width] for c in range(0, x.shape[1], width)]
    return jnp.concatenate([sum(jnp.dot(term, bd, preferred_element_type=F32) for term in _split_bf16(t, passes))
                            for t in tiles], axis=1)


def _rms_norm(x, g):
    return x * lax.rsqrt(jnp.mean(x * x, axis=-1, keepdims=True) + NORM_EPS) * g


def _const_spec(shape):
    return pl.BlockSpec(shape, lambda *_: (0,) * len(shape), pipeline_mode=pl.Buffered(1))


def _layer_spec(shape, layer, *lead):
    index = (layer,) + lead + (0,) * len(shape)
    return pl.BlockSpec((None,) * (1 + len(lead)) + shape, lambda *_: index, pipeline_mode=pl.Buffered(1))


def _weight_spec(shape, *lead):
    return _layer_spec(shape, 0, *lead)


def _conversion_specs(weights, src_layer, n_steps, step):
    ins, outs, shapes = [], [], []
    for w in weights:
        _, w_rows, w_cols = w.shape
        rb = next(r for r in range(BF16_ROWS, w_rows + 1, BF16_ROWS) if w_rows % r == 0 and w_rows // r <= n_steps)
        blk = lambda *ids, last=w_rows // rb - 1: jnp.minimum(step(*ids), last)
        ins.append(pl.BlockSpec((None, rb, w_cols), lambda *ids, blk=blk: (src_layer, blk(*ids), 0)))
        outs.append(pl.BlockSpec((None, rb, w_cols), lambda *ids, blk=blk: (0, blk(*ids), 0)))
        shapes.append(jax.ShapeDtypeStruct((1, w_rows, w_cols), BF16))
    return ins, outs, shapes


def _convert(srcs, dsts):
    for src, dst in zip(srcs, dsts):
        dst[...] = src[...].astype(BF16)


def _params(n_axes):
    return pltpu.CompilerParams(dimension_semantics=("arbitrary",) * n_axes,
                                vmem_limit_bytes=VMEM_LIMIT)


def _ffn_math(x, g_ref, win_ref, wout_ref):
    h = _rms_norm(x, g_ref[...]).astype(BF16)
    acc = jnp.zeros(x.shape, F32)
    for lo, hi in zip(FF_SPLITS[:-1], FF_SPLITS[1:]):
        gate = jnp.dot(h, win_ref[:, lo:hi], preferred_element_type=F32)
        up = jnp.dot(h, win_ref[:, D_FF + lo:D_FF + hi], preferred_element_type=F32)
        act = (gate * jax.nn.sigmoid(gate) * up).astype(BF16)
        acc = acc + jnp.dot(act, wout_ref[lo:hi, :], preferred_element_type=F32)
    return x + 0.5 * acc


def _ffn_body(x_ref, g_ref, win_ref, wout_ref, *rest):
    n_conv = len(rest) // 2
    _convert(rest[:n_conv], rest[n_conv + 1:])
    rest[n_conv][...] = _ffn_math(x_ref[...], g_ref, win_ref, wout_ref)


def _ffn(x2, g, w_in, w_out, layer, convert=()):
    t = x2.shape[0]
    tm = min(TOKEN_TILE, t)
    conv_in, conv_out, conv_shapes = _conversion_specs(convert, layer, t // tm, lambda i: i)
    return pl.pallas_call(
        _ffn_body,
        grid=(t // tm,),
        in_specs=[pl.BlockSpec((tm, D_MODEL), lambda i: (i, 0)),
                  _layer_spec((1, D_MODEL), layer),
                  _weight_spec((D_MODEL, 2 * D_FF)),
                  _weight_spec((D_FF, D_MODEL))] + conv_in,
        out_specs=[pl.BlockSpec((tm, D_MODEL), lambda i: (i, 0))] + conv_out,
        out_shape=[jax.ShapeDtypeStruct((t, D_MODEL), F32)] + conv_shapes,
        compiler_params=_params(1),
        name="ffn",
    )(x2, g, w_in, w_out, *convert)


_Q0 = C_RWKV
_K0 = C_RWKV + D_B
_V0 = C_RWKV + 2 * D_B
_G0 = C_RWKV + 3 * D_B


def _proj_body(x_ref, xp_ref, xn_ref, g_ref, w_ref, qg_ref, kg_ref, bdm_ref, mu_ref, w0_ref, w2_ref, a0_ref,
               a2_ref, g2_ref, kk_ref, ka_ref, rk_ref, bds_ref,
               q_ref, k_ref, vt_ref, gt_ref, r_out, v_out, kk_out, lw0_out, k0_out, a0_out, lw1_out, k1_out,
               a1_out, gate_out, bonus_out, *, seq):
    i = pl.program_id(0)
    g = g_ref[...]
    h = _rms_norm(x_ref[...], g).astype(BF16)
    tm = h.shape[0]

    bd = bdm_ref[...]
    q = jnp.dot(h, w_ref[:, _Q0:_K0], preferred_element_type=F32)
    q_ref[...] = (q * lax.rsqrt(_group_sums(q * q, bd) + NORM_EPS) * qg_ref[...]).astype(BF16)
    k = jnp.dot(h, w_ref[:, _K0:_V0], preferred_element_type=F32)
    k_ref[...] = (k * lax.rsqrt(_group_sums(k * k, bd) + NORM_EPS) * kg_ref[...]).astype(BF16)
    vt = jnp.dot(h, w_ref[:, _V0:_G0], preferred_element_type=F32).T.astype(BF16)
    hw = 2 * HEAD_B
    ones = jnp.ones((V_ROWS - hw, vt.shape[1]), BF16)
    vt_ref[0] = jnp.concatenate([part for hd in range(N_HEADS_B)
                                 for part in (vt[hd * hw:(hd + 1) * hw], ones)], axis=0)
    gt_ref[...] = jax.nn.sigmoid(jnp.dot(h, w_ref[:, _G0:], preferred_element_type=F32)).astype(BF16)

    w_rwkv = w_ref[:, :C_RWKV]
    p = jnp.dot(h, w_rwkv, preferred_element_type=F32)
    edge = jnp.concatenate([xp_ref[...], xn_ref[...]], axis=0)
    p_edge = jnp.dot(_rms_norm(edge, g).astype(BF16), w_rwkv, preferred_element_type=F32)
    keep_prev = jnp.where((i * tm) % seq == 0, 0.0, 1.0)
    keep_next = jnp.where(((i + 1) * tm) % seq == 0, 0.0, 1.0)
    row = lax.broadcasted_iota(jnp.int32, p.shape, 0)
    prev = jnp.where(row == 0, keep_prev * p_edge[SUBLANES - 1:SUBLANES, :], pltpu.roll(p, 1, 0))
    nxt = jnp.where(row == tm - 1, keep_next * p_edge[SUBLANES:SUBLANES + 1, :], pltpu.roll(p, tm - 1, 0))
    p = p + mu_ref[...] * (0.5 * (prev + nxt) - p)

    r = p[:, :D_A]
    k = p[:, D_A:2 * D_A]
    v = p[:, 2 * D_A:3 * D_A]
    dw = p[:, 3 * D_A:3 * D_A + 2 * LORA]
    da = p[:, 3 * D_A + 2 * LORA:3 * D_A + 4 * LORA]
    dg = p[:, 3 * D_A + 4 * LORA:]
    bd = bds_ref[...]

    logw = -DECAY_SCALE * jax.nn.sigmoid(w0_ref[...] + _mm(jnp.tanh(dw), w2_ref[...]))
    iclr = jax.nn.sigmoid(a0_ref[...] + _mm(da, a2_ref[...]))
    gate_out[...] = _mm(jax.nn.sigmoid(dg), g2_ref[...])

    kk = k * kk_ref[...]
    norm = jnp.sqrt(_group_sums(kk * kk, bd))
    kk_out[...] = (kk / jnp.maximum(norm, 1e-12)).astype(kk_out.dtype)

    ka = ka_ref[...]
    a_f = iclr[:, :D_A]
    a_b = iclr[:, D_A:]
    k_f = k * (1.0 + (a_f - 1.0) * ka)
    k_b = k * (1.0 + (a_b - 1.0) * ka)
    bonus = _group_sums(r * rk_ref[...] * (k_f + k_b), bd)
    r_out[...] = r.astype(r_out.dtype)
    v_out[...] = v.astype(v_out.dtype)
    lw0_out[...] = logw[:, :D_A]
    k0_out[...] = k_f.astype(k0_out.dtype)
    a0_out[...] = a_f.astype(a0_out.dtype)
    lw1_out[...] = logw[:, D_A:]
    k1_out[...] = k_b.astype(k1_out.dtype)
    a1_out[...] = a_b.astype(a1_out.dtype)
    bonus_out[...] = bonus * v


def _proj(x2, seq, g, w, qg, kg, bd_mean, mu, w0, w2cat, a0, a2cat, g2, k_k, k_a, r_k, bd_sum, layer):
    t = x2.shape[0]
    tm = TOKEN_TILE
    c_in = w.shape[2]
    nb = tm // SUBLANES
    last = t // SUBLANES - 1
    row = lambda width: pl.BlockSpec((tm, width), lambda i: (i, 0))
    wide = jax.ShapeDtypeStruct((t, D_A), F32)
    half = jax.ShapeDtypeStruct((t, D_A), BF16)
    return pl.pallas_call(
        functools.partial(_proj_body, seq=seq),
        grid=(t // tm,),
        in_specs=[row(D_MODEL),
                  pl.BlockSpec((SUBLANES, D_MODEL), lambda i: (jnp.maximum(i * nb - 1, 0), 0)),
                  pl.BlockSpec((SUBLANES, D_MODEL), lambda i: (jnp.minimum((i + 1) * nb, last), 0)),
                  _layer_spec((1, D_MODEL), layer), _weight_spec((D_MODEL, c_in)),
                  _layer_spec((1, D_B), layer), _layer_spec((1, D_B), layer), _const_spec((MXU_WIDTH, MXU_WIDTH)),
                  _layer_spec((1, C_RWKV), layer),
                  _layer_spec((1, 2 * D_A), layer), _layer_spec((2 * LORA, 2 * D_A), layer),
                  _layer_spec((1, 2 * D_A), layer), _layer_spec((2 * LORA, 2 * D_A), layer),
                  _layer_spec((LORA_GATE, D_A), layer),
                  _layer_spec((1, D_A), layer), _layer_spec((1, D_A), layer), _layer_spec((1, D_A), layer),
                  _const_spec((MXU_WIDTH, MXU_WIDTH))],
        out_specs=[row(D_B), row(D_B),
                   pl.BlockSpec((1, N_HEADS_B * V_ROWS, tm), lambda i: (i, 0, 0)), row(2 * D_MODEL)]
                  + [row(D_A)] * 11,
        out_shape=[jax.ShapeDtypeStruct((t, D_B), BF16),
                   jax.ShapeDtypeStruct((t, D_B), BF16),
                   jax.ShapeDtypeStruct((t // tm, N_HEADS_B * V_ROWS, tm), BF16),
                   jax.ShapeDtypeStruct((t, 2 * D_MODEL), BF16),
                   half, half, half, wide, half, half, wide, half, half, wide, wide],
        compiler_params=_params(1),
        name="proj",
    )(x2, x2, x2, g, w, qg, kg, bd_mean, mu, w0, w2cat, a0, a2cat, g2, k_k, k_a, r_k, bd_sum)


def _scan_step(rf, vf, kkf, lwf, kf, af, rb, vb, kkb, lwb, kb, ab, yf_ref, yb_ref, h_ref):
    L = CHUNK
    n_sub = rf.shape[0] // L
    n_pairs = D_A // LANES
    ti = lax.broadcasted_iota(jnp.int32, (L, L), 0)
    tj = lax.broadcasted_iota(jnp.int32, (L, L), 1)
    pi = lax.broadcasted_iota(jnp.int32, (L, LANES), 0)
    lane = lax.broadcasted_iota(jnp.int32, (L, LANES), 1)
    pj = lane & (HEAD_A - 1)
    lane_lo = lane < HEAD_A
    eye = jnp.where(pj == pi, 1.0, 0.0)
    blk = lambda s: (pi // s) == (pj // s)
    row2 = lax.broadcasted_iota(jnp.int32, (2 * L, LANES), 0)
    col2 = lax.broadcasted_iota(jnp.int32, (2 * L, LANES), 1)
    same_head = (row2 // HEAD_A) == (col2 // HEAD_A)
    diag2 = row2 == col2
    zeros_l = jnp.zeros((L, LANES), F32)
    zeros_2l = jnp.zeros((2 * L, LANES), F32)

    def stack(x):
        return jnp.concatenate([jnp.where(lane_lo, x, 0.0), jnp.where(lane_lo, 0.0, x)], axis=0)

    def mmp(x, y):
        return _mm(x, stack(y))

    cat = jnp.concatenate
    chains = []
    for reverse, refs in ((False, (rf, vf, kkf, lwf, kf, af)), (True, (rb, vb, kkb, lwb, kb, ab))):
        tri = ((tj >= ti) if reverse else (tj <= ti)).astype(BF16)
        strict = (pj > pi) if reverse else (pj < pi)
        incl = (pj >= pi) if reverse else (pj <= pi)
        steps = []
        for sub in (range(n_sub - 1, -1, -1) if reverse else range(n_sub)):
            rows = slice(sub * L, (sub + 1) * L)
            r, v, kk, lw, k, a = (ref[rows, :].astype(F32) for ref in refs)
            G = sum(jnp.dot(tri, part, preferred_element_type=F32) for part in _split_bf16(lw, 2))
            g_end = G[0:1] if reverse else G[L - 1:L]
            gam_end = jnp.exp(g_end)
            inv_gam = jnp.exp(-G)
            rel_end = jnp.exp(g_end - G)
            b = kk * a
            full = dict(At=-kk * jnp.exp(G - lw), Bt=b * inv_gam, Kt=k * inv_gam, Rt=r * jnp.exp(G),
                        Kh=k * rel_end, Bh=b * rel_end, v=v)
            pairs = []
            for p in range(n_pairs):
                sl = slice(p * LANES, (p + 1) * LANES)
                c = {name: val[:, sl] for name, val in full.items()}
                c.update(strict=strict, incl=incl, gam_end=gam_end[:, sl], rows=rows)
                pairs.append(c)
            steps.append(pairs)
        chains.append(steps)
    flat = [c for steps in chains for pairs in steps for c in pairs]

    for c in flat:
        a4 = _mm_nt(cat([c["At"], c["Rt"]], axis=0), cat([stack(c["Bt"]), stack(c["Kt"])], axis=0))
        c["A_ab"] = jnp.where(c["strict"], a4[:L, :LANES], 0.0)
        c["A_ak"] = jnp.where(c["strict"], a4[:L, LANES:], 0.0)
        c["A_rb"] = jnp.where(c["incl"], a4[L:, :LANES], 0.0)
        c["A_rk"] = jnp.where(c["incl"], a4[L:, LANES:], 0.0)
    for c in flat:
        c["A0"] = jnp.where(blk(8), c["A_ab"], 0.0)
        c["A2"] = mmp(c["A0"], c["A0"])
    for c in flat:
        t1 = eye + c["A0"]
        both = mmp(cat([c["A2"], t1], axis=0), c["A2"])
        c["A4"] = both[:L]
        c["T"] = t1 + both[L:]
    for c in flat:
        c["T"] = c["T"] + mmp(c["T"], c["A4"])
    s = 8
    while s < L:
        outer = blk(2 * s) & jnp.logical_not(blk(s))
        for c in flat:
            c["TO"] = mmp(c["T"], jnp.where(outer, c["A_ab"], 0.0))
        for c in flat:
            c["T"] = c["T"] + mmp(c["TO"], c["T"])
        s *= 2
    for c in flat:
        c["AkV"] = mmp(c["A_ak"], c["v"])
    for c in flat:
        wu = _mm(c["T"], cat([stack(c["At"]), stack(c["AkV"])], axis=1))
        c["W"], c["U0"] = wu[:, :LANES], wu[:, LANES:]
    for c in flat:
        kb_t = cat([c["Kh"], c["Bh"]], axis=0).T
        mn = _mm(kb_t, cat([cat([zeros_l, c["v"]], axis=1), cat([c["W"], c["U0"]], axis=1)], axis=0))
        c["M"] = jnp.where(same_head, mn[:, :LANES], 0.0) + jnp.where(diag2, c["gam_end"], 0.0)
        c["N"] = jnp.where(same_head, mn[:, LANES:], 0.0)
    for c in flat:
        qy = _mm(cat([c["A_rb"], c["A_rk"]], axis=1),
                 cat([cat([stack(c["W"]), stack(c["U0"])], axis=1),
                      cat([zeros_2l, stack(c["v"])], axis=1)], axis=0))
        c["Q"] = c["Rt"] + qy[:, :LANES]
        c["Y0"] = qy[:, LANES:]
    for d, (steps, y_ref) in enumerate(zip(chains, (yf_ref, yb_ref))):
        for pairs in steps:
            ys = []
            for p, c in enumerate(pairs):
                qm_h = _mm(cat([c["Q"], c["M"]], axis=0), h_ref[d * n_pairs + p])
                ys.append(qm_h[:L] + c["Y0"])
                h_ref[d * n_pairs + p] = qm_h[L:] + c["N"]
            y_ref[pairs[0]["rows"], :] = cat(ys, axis=1)


def _bias_features(slope, n, width, key_side):
    pos = lax.broadcasted_iota(jnp.int32, (n, width), 0).astype(F32)
    lane = lax.broadcasted_iota(jnp.int32, (n, width), 1)
    hi, mid, lo = (t.astype(F32) for t in _split_bf16(slope * pos, 3))
    terms = jnp.where(lane % 3 == 0, hi, jnp.where(lane % 3 == 1, mid, lo))
    if key_side:
        return jnp.where(lane < 3, -1.0, jnp.where(lane < 6, terms, 0.0))
    return jnp.where(lane < 3, terms, jnp.where(lane < 6, 1.0, 0.0))


def _attn_body(q_ref, k_ref, vt_ref, lv_ref, sg_ref, qg_ref, kg_ref, o_ref, qs_ref, feat_ref, src_ref,
               s_ref, mx_ref, p_ref, al_ref, m_ref, acc_ref, *, lam_init, before_loop):
    tq = src_ref.shape[1]
    n_kv, _, tk = vt_ref.shape
    seq = n_kv * tk
    h = pl.program_id(1)
    i = pl.program_id(2)
    slope_bits = (127 - 2 * (h + 1)) << 23
    slope = (lax.bitcast_convert_type(jnp.full((1, 1), slope_bits, jnp.int32), jnp.float32).astype(F32)
             * LOG2E)
    bound =(HEAD_B * ROUNDING_MARGIN * jnp.max(jnp.abs(qg_ref[...]), axis=-1, keepdims=True)
             * jnp.max(jnp.abs(kg_ref[...]), axis=-1, keepdims=True))
    reach = jnp.minimum((F32_ZERO_EXP2 + 2.0 * bound) / slope + 1.0, float(seq)).astype(jnp.int32)[0, 0]

    n_st = tk // tq
    hw = 2 * HEAD_B

    @pl.when(i == 0)
    def _():
        kv_pos = lax.broadcasted_iota(jnp.int32, (tk, tq), 0)
        q_pos = lax.broadcasted_iota(jnp.int32, (tk, tq), 1)
        src_ref[...] = slope * (q_pos - kv_pos).astype(F32)
        key_feat = _bias_features(slope, tk, hw, key_side=True)
        feat_ref[0] = key_feat.astype(BF16)
        feat_ref[1] = (-key_feat).astype(BF16)
        feat_ref[2] = jnp.zeros((tk, hw), BF16)
        q_feat = _bias_features(slope, tq, hw, key_side=False).T.astype(BF16)
        for st in range(n_st):
            qs_ref[st, hw:, :tq] = q_feat
            qs_ref[st, hw:, tq:] = q_feat

    chan =lax.broadcasted_iota(jnp.int32, (hw, tq), 0)
    for st in range(n_st):
        q_t = q_ref[st * tq:(st + 1) * tq, :].astype(F32).T.astype(BF16)
        zero = jnp.zeros_like(q_t)
        qs_ref[st, :hw, :tq] = jnp.where(chan < HEAD_B, q_t, zero)
        qs_ref[st, :hw, tq:] = jnp.where(chan < HEAD_B, zero, q_t)
    q0 = i * tk
    j_lo = jnp.maximum(q0 - reach, 0) // tk
    j_hi = jnp.minimum(q0 + tk - 1 + reach, seq - 1) // tk + 1
    n_left = i - j_lo
    n_off = n_left + (j_hi - i - 1)

    def scores(j, side, st):
        kb = k_ref[pl.ds(pl.multiple_of(j * tk, tk), tk), :]
        return jnp.dot(jnp.concatenate([kb, feat_ref[side]], axis=1), qs_ref[st],
                       preferred_element_type=F32)

    def softmax_update(s, s_max, cst, st):
        m_old = m_ref[st]
        m_new = jnp.maximum(m_old, s_max + cst)
        alpha = jnp.exp2(m_old - m_new)
        p = jnp.exp2(s - (m_new - cst))
        m_ref[st] = m_new
        return alpha, p.astype(BF16)

    def block(n):
        j = jnp.clip(j_lo + n + (n >= n_left).astype(jnp.int32), 0, n_kv - 1)
        j = jnp.where(n < 0, i, j)
        return j, (j > i).astype(jnp.int32), jnp.logical_and(n >= 0, n < n_off)

    def qk_stage(n, slot):
        j, side, _ = block(n)
        for st in range(n_st):
            s = scores(j, side, st)
            s_ref[slot, st] = s
            mx_ref[slot, st] = jnp.max(s, axis=0, keepdims=True)

    def softmax_stage(n, slot):
        j, _, valid = block(n)
        for st in range(n_st):
            dist0 = jnp.abs(q0 + st * tq - j * tk).astype(F32)
            cst = jnp.where(valid, -slope * dist0, -1e30)
            alpha, p = softmax_update(s_ref[slot, st], mx_ref[slot, st], cst, st)
            al_ref[slot, st] = alpha
            p_ref[slot, st] = p

    def pv_stage(n, slot):
        j, _, _ = block(n)
        for st in range(n_st):
            acc_ref[st] = al_ref[slot, st] * acc_ref[st] + jnp.dot(vt_ref[j], p_ref[slot, st],
                                                                    preferred_element_type=F32)

    m_ref[...] = jnp.full(m_ref.shape, -1e30, F32)
    acc_ref[...] = jnp.zeros(acc_ref.shape, F32)
    diag_scores = [scores(i, 2, st) for st in range(n_st)]
    qk_stage(0, 0)
    src = src_ref[...]
    for st in range(n_st):
        bias = jnp.abs(src + slope * float(st * tq))
        s = diag_scores[st] - jnp.concatenate([bias, bias], axis=1)
        alpha, p = softmax_update(s, jnp.max(s, axis=0, keepdims=True), jnp.zeros((1, 1), F32), st)
        al_ref[1, st] = alpha
        p_ref[1, st] = p

    before_loop()

    def pipelined_pair(t, carry):
        n = 2 * t
        qk_stage(n + 1, 1)
        softmax_stage(n, 0)
        pv_stage(n - 1, 1)
        qk_stage(n + 2, 0)
        softmax_stage(n + 1, 1)
        pv_stage(n, 0)
        return carry

    n_pairs = n_off // 2
    lax.fori_loop(0, n_pairs, pipelined_pair, 0)
    last = 2 * n_pairs

    @pl.when(n_off % 2 == 1)
    def _():
        softmax_stage(last, 0)
        pv_stage(last - 1, 1)
        pv_stage(last, 0)

    @pl.when(n_off % 2 == 0)
    def _():
        pv_stage(last - 1, 1)

    lv = lv_ref[...]
    lam = (jnp.exp(jnp.sum(lv[0:1] * lv[1:2], axis=-1, keepdims=True))
           - jnp.exp(jnp.sum(lv[2:3] * lv[3:4], axis=-1, keepdims=True)) + lam_init)
    for st in range(n_st):
        acc = acc_ref[st, :hw, :]
        l = acc_ref[st, hw:hw + 1, :]
        o_t = acc[:, :tq] / l[:, :tq] - lam * (acc[:, tq:] / l[:, tq:])
        o_t = o_t * lax.rsqrt(jnp.mean(o_t * o_t, axis=0, keepdims=True) + NORM_EPS)
        o_ref[st * tq:(st + 1) * tq, :] = (o_t.T * sg_ref[...] * (1.0 - lam_init)).astype(o_ref.dtype)


def _mixers_body(q_ref, k_ref, vt_ref, lv_ref, sg_ref, qg_ref, kg_ref,
                 rf, vf, kkf, lwf, kf, af, rb, vb, kkb, lwb, kb, ab, *rest, lam_init, n_scan, n_cast):
    cast_in, (o_ref, yf_ref, yb_ref), rest = rest[:n_cast], rest[n_cast:n_cast + 3], rest[n_cast + 3:]
    cast_out, scratch = rest[:n_cast], rest[n_cast:]
    *attn_scratch, h_ref = scratch
    step = (pl.program_id(0) * pl.num_programs(1) + pl.program_id(1)) * pl.num_programs(2) + pl.program_id(2)

    @pl.when(step % n_scan == 0)
    def _():
        h_ref[...] = jnp.zeros(h_ref.shape, F32)

    def side_work():
        _scan_step(rf, vf, kkf, lwf, kf, af, rb, vb, kkb, lwb, kb, ab, yf_ref, yb_ref, h_ref)
        _convert(cast_in, cast_out)

    _attn_body(q_ref, k_ref, vt_ref, lv_ref, sg_ref, qg_ref, kg_ref, o_ref, *attn_scratch, lam_init=lam_init,
               before_loop=side_work)


def _mixers(q, k, vt, lam_vecs, sub_g, qg, kg, r, v, kk, lw0, k0, a0, lw1, k1, a1, lam_init, batch, seq, layer,
            next_weights):
    t = q.shape[0]
    tk = vt.shape[2]
    tq = ATTN_TQ
    n_st = tk // tq
    nq = seq // tk
    n_kv = seq // tk
    hw = 2 * HEAD_B
    rows = SCAN_CHUNKS_PER_STEP * CHUNK
    n_scan = seq // rows
    assert N_HEADS_B * nq == n_scan
    step = lambda b, h, i: (b * N_HEADS_B + h) * nq + i
    fwd = pl.BlockSpec((rows, D_A), lambda b, h, i: (step(b, h, i), 0))
    bwd = pl.BlockSpec((rows, D_A), lambda b, h, i: (
        (step(b, h, i) // n_scan) * n_scan + n_scan - 1 - step(b, h, i) % n_scan, 0))
    y = jax.ShapeDtypeStruct((t, D_A), F32)
    cast_in, cast_out, cast_shapes = _conversion_specs(next_weights, layer + 1, batch * N_HEADS_B * nq, step)
    return pl.pallas_call(
        functools.partial(_mixers_body, lam_init=lam_init, n_scan=n_scan, n_cast=len(next_weights)),
        grid=(batch, N_HEADS_B, nq),
        in_specs=[pl.BlockSpec((tk, hw), lambda b, h, i: (b * nq + i, h)),
                  pl.BlockSpec((seq, hw), lambda b, h, i: (b, h)),
                  pl.BlockSpec((n_kv, V_ROWS, tk), lambda b, h, i: (b, h, 0)),
                  _layer_spec((4, HEAD_B), layer), _layer_spec((1, hw), layer),
                  _layer_spec((1, D_B), layer), _layer_spec((1, D_B), layer)] + [fwd] * 6 + [bwd] * 6 + cast_in,
        out_specs=[pl.BlockSpec((tk, hw), lambda b, h, i: (b * nq + i, h)), fwd, bwd] + cast_out,
        out_shape=[jax.ShapeDtypeStruct((t, D_B), BF16), y, y] + cast_shapes,
        scratch_shapes=[pltpu.VMEM((n_st, 2 * hw, 2 * tq), BF16),
                        pltpu.VMEM((3, tk, hw), BF16),
                        pltpu.VMEM((tk, tq), F32),
                        pltpu.VMEM((2, n_st, tk, 2 * tq), F32),
                        pltpu.VMEM((2, n_st, 1, 2 * tq), F32),
                        pltpu.VMEM((2, n_st, tk, 2 * tq), BF16),
                        pltpu.VMEM((2, n_st, 1, 2 * tq), F32),
                        pltpu.VMEM((n_st, 1, 2 * tq), F32),
                        pltpu.VMEM((n_st, V_ROWS, 2 * tq), F32),
                        pltpu.VMEM((2 * (D_A // LANES), LANES, LANES), F32)],
        compiler_params=_params(3),
        name="mixers",
    )(q, k, vt, lam_vecs, sub_g, qg, kg, r, v, kk, lw0, k0, a0, r, v, kk, lw1, k1, a1, *next_weights)


def _merge_body(yf_ref, yb_ref, gate_ref, bonus_ref, ob_ref, gt_ref, x_ref, lng_ref, lnb_ref, bd_ref,
                wb0_ref, wb1_ref, wo_ref, g2_ref, win_ref, wout_ref, o_ref):
    bd = bd_ref[...]
    y = yf_ref[...] + yb_ref[...]
    yc = y - _group_sums(y, bd, passes=2)
    var = _group_sums(yc * yc, bd)
    yn = yc * lax.rsqrt(var + GN_EPS) * lng_ref[...] + lnb_ref[...]
    oa = (yn + bonus_ref[...]) * gate_ref[...]
    gts = gt_ref[...].astype(F32)
    merged = (gts[:, :D_MODEL] * _mm(oa, wb0_ref[...])
              + gts[:, D_MODEL:] * jnp.dot(ob_ref[...], wb1_ref[...], preferred_element_type=F32))
    o_ref[...] = _ffn_math(x_ref[...] + _mm(merged, wo_ref[...]), g2_ref, win_ref, wout_ref)


def _merge(yf, yb, gate, bonus, ob, gts, x2, ln_g, ln_b, bd_mean, wb, wo, g2, w_in, w_out, layer):
    t = x2.shape[0]
    tm = min(TOKEN_TILE, t)
    row = lambda width: pl.BlockSpec((tm, width), lambda i: (i, 0))
    branch = lambda g: pl.BlockSpec((None, D_A, D_MODEL), lambda *_: (0, g, 0), pipeline_mode=pl.Buffered(1))
    return pl.pallas_call(
        _merge_body,
        grid=(t // tm,),
        in_specs=[row(D_A), row(D_A), row(D_A), row(D_A), row(D_B), row(2 * D_MODEL), row(D_MODEL),
                  _layer_spec((1, D_A), layer), _layer_spec((1, D_A), layer), _const_spec((MXU_WIDTH, MXU_WIDTH)),
                  branch(0), branch(1),
                  _weight_spec((D_MODEL, D_MODEL)),
                  _layer_spec((1, D_MODEL), layer), _weight_spec((D_MODEL, 2 * D_FF)),
                  _weight_spec((D_FF, D_MODEL))],
        out_specs=row(D_MODEL),
        out_shape=jax.ShapeDtypeStruct((t, D_MODEL), F32),
        compiler_params=_params(1),
        name="merge",
    )(yf, yb, gate, bonus, ob, gts, x2, ln_g, ln_b, bd_mean, wb, wb, wo, g2, w_in, w_out)


def _block_diag2(w):
    z = jnp.zeros_like(w[:, 0])
    return jnp.concatenate([jnp.concatenate([w[:, 0], z], axis=2), jnp.concatenate([z, w[:, 1]], axis=2)],
                           axis=1)


def kernel(x, norm_ffn1, ffn1_in, ffn1_out, norm_mix, w_in, rwkv_mu, decay_w0, decay_w2, iclr_a0, iclr_a2,
           gate_g2, k_k, k_a, r_k, ln_x_g, ln_x_b, q_gain, k_gain, diff_lambda, subln_g, w_branch, w_out,
           norm_ffn2, ffn2_in, ffn2_out):
    batch, seq, d = x.shape
    depth = norm_ffn1.shape[0]
    assert seq % TOKEN_TILE == 0 and d == D_MODEL
    x2 = x.reshape(batch * seq, d)
    head_id = jnp.arange(MXU_WIDTH) // HEAD_A
    same = head_id[:, None] == head_id[None, :]
    bd_sum = same.astype(BF16)
    bd_mean = (same.astype(F32) / HEAD_A).astype(BF16)
    rows = lambda a: a.reshape(depth, 1, -1)
    bf = lambda a: a.astype(BF16)
    n_qk = D_B // HEAD_B
    qg = rows(jnp.tile(q_gain, (1, n_qk))) * (HEAD_B ** -0.5 * LOG2E)
    kg = rows(jnp.tile(k_gain, (1, n_qk)))
    big = (ffn1_in, ffn1_out, w_in, w_branch.reshape(depth, N_BRANCH * D_A, D_MODEL), w_out, ffn2_in, ffn2_out)
    weights = [bf(w[:1]) for w in big[:2]]
    prep_w = (rows(rwkv_mu), rows(decay_w0), bf(_block_diag2(decay_w2)), rows(iclr_a0),
              bf(_block_diag2(iclr_a2)), bf(gate_g2), rows(k_k), rows(k_a), rows(r_k))
    sub_g = rows(subln_g)
    for l in range(depth):
        lam_init = 0.8 - 0.6 * math.exp(-0.3 * l)
        x2, *converted = _ffn(x2, rows(norm_ffn1), weights[0], weights[1], l, big[2:] if l == 0 else ())
        f1_in, f1_out, w_proj, w_br, w_o, f2_in, f2_out = weights + converted
        q, k, vt, gts, r, vv, kk, lw0, k0, a0, lw1, k1, a1, gate, bonus = _proj(
            x2, seq, rows(norm_mix), w_proj, qg, kg, bd_mean, *prep_w, bd_sum, l)
        ob, yf, yb, *weights = _mixers(q, k, vt, diff_lambda, sub_g, qg, kg, r, vv, kk, lw0, k0, a0, lw1, k1, a1,
                                      lam_init, batch, seq, l, big if l + 1 < depth else ())
        x2 = _merge(yf, yb, gate, bonus, ob, gts, x2, rows(ln_x_g), rows(ln_x_b), bd_mean,
                    w_br, w_o, rows(norm_ffn2), f2_in, f2_out, l)
    return x2.reshape(batch, seq, d)
```

```python
import functools
import math

import jax
import jax.numpy as jnp
from jax import lax
from jax.experimental import pallas as pl
from jax.experimental.pallas import tpu as pltpu

F32 = jnp.float32
BF16 = jnp.bfloat16

D_MODEL = 1024
D_A = 512
HEAD_A = 64
LORA = 64
LORA_GATE = 128
C_RWKV = 3 * D_A + 2 * LORA + 2 * LORA + LORA_GATE
D_B = 512
HEAD_B = 64
N_HEADS_B = D_B // (2 * HEAD_B)
D_FF = 2816
N_BRANCH = 2
DECAY_SCALE = math.exp(-0.5)
GN_EPS = 64e-5
NORM_EPS = 1e-6
LOG2E = math.log2(math.e)
F32_ZERO_EXP2 = 150.0
ROUNDING_MARGIN = 1.02

LANES = 128
SUBLANES = 8
BF16_ROWS = 16
VMEM_LIMIT = 56 * 1024 * 1024

TOKEN_TILE = 512
MXU_WIDTH = 256
FF_SPLITS = (0, 6 * MXU_WIDTH, D_FF)
CHUNK = 64
SCAN_CHUNKS_PER_STEP = 2
ATTN_TQ = 256
V_ROWS = 2 * HEAD_B + 16

_NT = (((1,), (1,)), ((), ()))


def _mm(a, b):
    return jnp.dot(a.astype(BF16), b.astype(BF16), preferred_element_type=F32)


def _mm_nt(a, b):
    return lax.dot_general(a.astype(BF16), b.astype(BF16), _NT, preferred_element_type=F32)


def _split_bf16(x, n):
    parts = []
    for _ in range(n - 1):
        hi = x.astype(BF16)
        parts.append(hi)
        x = x - hi.astype(F32)
    parts.append(x.astype(BF16))
    return parts


def _group_sums(x, bd, passes=1):
    width = bd.shape[0]
    tiles = [x[:, c:c + width] for c in range(0, x.shape[1], width)]
    return jnp.concatenate([sum(jnp.dot(term, bd, preferred_element_type=F32) for term in _split_bf16(t, passes))
                            for t in tiles], axis=1)


def _rms_norm(x, g):
    return x * lax.rsqrt(jnp.mean(x * x, axis=-1, keepdims=True) + NORM_EPS) * g


def _const_spec(shape):
    return pl.BlockSpec(shape, lambda *_: (0,) * len(shape), pipeline_mode=pl.Buffered(1))


def _layer_spec(shape, layer, *lead):
    index = (layer,) + lead + (0,) * len(shape)
    return pl.BlockSpec((None,) * (1 + len(lead)) + shape, lambda *_: index, pipeline_mode=pl.Buffered(1))


def _weight_spec(shape, *lead):
    return _layer_spec(shape, 0, *lead)


def _conversion_specs(weights, src_layer, n_steps, step):
    ins, outs, shapes = [], [], []
    for w in weights:
        _, w_rows, w_cols = w.shape
        rb = next(r for r in range(BF16_ROWS, w_rows + 1, BF16_ROWS) if w_rows % r == 0 and w_rows // r <= n_steps)
        blk = lambda *ids, last=w_rows // rb - 1: jnp.minimum(step(*ids), last)
        ins.append(pl.BlockSpec((None, rb, w_cols), lambda *ids, blk=blk: (src_layer, blk(*ids), 0)))
        outs.append(pl.BlockSpec((None, rb, w_cols), lambda *ids, blk=blk: (0, blk(*ids), 0)))
        shapes.append(jax.ShapeDtypeStruct((1, w_rows, w_cols), BF16))
    return ins, outs, shapes


def _convert(srcs, dsts):
    for src, dst in zip(srcs, dsts):
        dst[...] = src[...].astype(BF16)


def _params(n_axes):
    return pltpu.CompilerParams(dimension_semantics=("arbitrary",) * n_axes,
                                vmem_limit_bytes=VMEM_LIMIT)


def _ffn_math(x, g_ref, win_ref, wout_ref):
    h = _rms_norm(x, g_ref[...]).astype(BF16)
    acc = jnp.zeros(x.shape, F32)
    for lo, hi in zip(FF_SPLITS[:-1], FF_SPLITS[1:]):
        gate = jnp.dot(h, win_ref[:, lo:hi], preferred_element_type=F32)
        up = jnp.dot(h, win_ref[:, D_FF + lo:D_FF + hi], preferred_element_type=F32)
        act = (gate * jax.nn.sigmoid(gate) * up).astype(BF16)
        acc = acc + jnp.dot(act, wout_ref[lo:hi, :], preferred_element_type=F32)
    return x + 0.5 * acc


def _ffn_body(x_ref, g_ref, win_ref, wout_ref, *rest):
    n_conv = len(rest) // 2
    _convert(rest[:n_conv], rest[n_conv + 1:])
    rest[n_conv][...] = _ffn_math(x_ref[...], g_ref, win_ref, wout_ref)


def _ffn(x2, g, w_in, w_out, layer, convert=()):
    t = x2.shape[0]
    tm = min(TOKEN_TILE, t)
    conv_in, conv_out, conv_shapes = _conversion_specs(convert, layer, t // tm, lambda i: i)
    return pl.pallas_call(
        _ffn_body,
        grid=(t // tm,),
        in_specs=[pl.BlockSpec((tm, D_MODEL), lambda i: (i, 0)),
                  _layer_spec((1, D_MODEL), layer),
                  _weight_spec((D_MODEL, 2 * D_FF)),
                  _weight_spec((D_FF, D_MODEL))] + conv_in,
        out_specs=[pl.BlockSpec((tm, D_MODEL), lambda i: (i, 0))] + conv_out,
        out_shape=[jax.ShapeDtypeStruct((t, D_MODEL), F32)] + conv_shapes,
        compiler_params=_params(1),
        name="ffn",
    )(x2, g, w_in, w_out, *convert)


_Q0 = C_RWKV
_K0 = C_RWKV + D_B
_V0 = C_RWKV + 2 * D_B
_G0 = C_RWKV + 3 * D_B


def _proj_body(x_ref, xp_ref, xn_ref, g_ref, w_ref, qg_ref, kg_ref, bdm_ref, mu_ref, w0_ref, w2_ref, a0_ref,
               a2_ref, g2_ref, kk_ref, ka_ref, rk_ref, bds_ref,
               q_ref, k_ref, vt_ref, gt_ref, r_out, v_out, kk_out, lw0_out, k0_out, a0_out, lw1_out, k1_out,
               a1_out, gate_out, bonus_out, *, seq):
    i = pl.program_id(0)
    g = g_ref[...]
    h = _rms_norm(x_ref[...], g).astype(BF16)
    tm = h.shape[0]

    bd = bdm_ref[...]
    q = jnp.dot(h, w_ref[:, _Q0:_K0], preferred_element_type=F32)
    q_ref[...] = (q * lax.rsqrt(_group_sums(q * q, bd) + NORM_EPS) * qg_ref[...]).astype(BF16)
    k = jnp.dot(h, w_ref[:, _K0:_V0], preferred_element_type=F32)
    k_ref[...] = (k * lax.rsqrt(_group_sums(k * k, bd) + NORM_EPS) * kg_ref[...]).astype(BF16)
    vt = jnp.dot(h, w_ref[:, _V0:_G0], preferred_element_type=F32).T.astype(BF16)
    hw = 2 * HEAD_B
    ones = jnp.ones((V_ROWS - hw, vt.shape[1]), BF16)
    vt_ref[0] = jnp.concatenate([part for hd in range(N_HEADS_B)
                                 for part in (vt[hd * hw:(hd + 1) * hw], ones)], axis=0)
    gt_ref[...] = jax.nn.sigmoid(jnp.dot(h, w_ref[:, _G0:], preferred_element_type=F32)).astype(BF16)

    w_rwkv = w_ref[:, :C_RWKV]
    p = jnp.dot(h, w_rwkv, preferred_element_type=F32)
    edge = jnp.concatenate([xp_ref[...], xn_ref[...]], axis=0)
    p_edge = jnp.dot(_rms_norm(edge, g).astype(BF16), w_rwkv, preferred_element_type=F32)
    keep_prev = jnp.where((i * tm) % seq == 0, 0.0, 1.0)
    keep_next = jnp.where(((i + 1) * tm) % seq == 0, 0.0, 1.0)
    row = lax.broadcasted_iota(jnp.int32, p.shape, 0)
    prev = jnp.where(row == 0, keep_prev * p_edge[SUBLANES - 1:SUBLANES, :], pltpu.roll(p, 1, 0))
    nxt = jnp.where(row == tm - 1, keep_next * p_edge[SUBLANES:SUBLANES + 1, :], pltpu.roll(p, tm - 1, 0))
    p = p + mu_ref[...] * (0.5 * (prev + nxt) - p)

    r = p[:, :D_A]
    k = p[:, D_A:2 * D_A]
    v = p[:, 2 * D_A:3 * D_A]
    dw = p[:, 3 * D_A:3 * D_A + 2 * LORA]
    da = p[:, 3 * D_A + 2 * LORA:3 * D_A + 4 * LORA]
    dg = p[:, 3 * D_A + 4 * LORA:]
    bd = bds_ref[...]

    logw = -DECAY_SCALE * jax.nn.sigmoid(w0_ref[...] + _mm(jnp.tanh(dw), w2_ref[...]))
    iclr = jax.nn.sigmoid(a0_ref[...] + _mm(da, a2_ref[...]))
    gate_out[...] = _mm(jax.nn.sigmoid(dg), g2_ref[...])

    kk = k * kk_ref[...]
    norm = jnp.sqrt(_group_sums(kk * kk, bd))
    kk_out[...] = (kk / jnp.maximum(norm, 1e-12)).astype(kk_out.dtype)

    ka = ka_ref[...]
    a_f = iclr[:, :D_A]
    a_b = iclr[:, D_A:]
    k_f = k * (1.0 + (a_f - 1.0) * ka)
    k_b = k * (1.0 + (a_b - 1.0) * ka)
    bonus = _group_sums(r * rk_ref[...] * (k_f + k_b), bd)
    r_out[...] = r.astype(r_out.dtype)
    v_out[...] = v.astype(v_out.dtype)
    lw0_out[...] = logw[:, :D_A]
    k0_out[...] = k_f.astype(k0_out.dtype)
    a0_out[...] = a_f.astype(a0_out.dtype)
    lw1_out[...] = logw[:, D_A:]
    k1_out[...] = k_b.astype(k1_out.dtype)
    a1_out[...] = a_b.astype(a1_out.dtype)
    bonus_out[...] = bonus * v


def _proj(x2, seq, g, w, qg, kg, bd_mean, mu, w0, w2cat, a0, a2cat, g2, k_k, k_a, r_k, bd_sum, layer):
    t = x2.shape[0]
    tm = TOKEN_TILE
    c_in = w.shape[2]
    nb = tm // SUBLANES
    last = t // SUBLANES - 1
    row = lambda width: pl.BlockSpec((tm, width), lambda i: (i, 0))
    wide = jax.ShapeDtypeStruct((t, D_A), F32)
    half = jax.ShapeDtypeStruct((t, D_A), BF16)
    return pl.pallas_call(
        functools.partial(_proj_body, seq=seq),
        grid=(t // tm,),
        in_specs=[row(D_MODEL),
                  pl.BlockSpec((SUBLANES, D_MODEL), lambda i: (jnp.maximum(i * nb - 1, 0), 0)),
                  pl.BlockSpec((SUBLANES, D_MODEL), lambda i: (jnp.minimum((i + 1) * nb, last), 0)),
                  _layer_spec((1, D_MODEL), layer), _weight_spec((D_MODEL, c_in)),
                  _layer_spec((1, D_B), layer), _layer_spec((1, D_B), layer), _const_spec((MXU_WIDTH, MXU_WIDTH)),
                  _layer_spec((1, C_RWKV), layer),
                  _layer_spec((1, 2 * D_A), layer), _layer_spec((2 * LORA, 2 * D_A), layer),
                  _layer_spec((1, 2 * D_A), layer), _layer_spec((2 * LORA, 2 * D_A), layer),
                  _layer_spec((LORA_GATE, D_A), layer),
                  _layer_spec((1, D_A), layer), _layer_spec((1, D_A), layer), _layer_spec((1, D_A), layer),
                  _const_spec((MXU_WIDTH, MXU_WIDTH))],
        out_specs=[row(D_B), row(D_B),
                   pl.BlockSpec((1, N_HEADS_B * V_ROWS, tm), lambda i: (i, 0, 0)), row(2 * D_MODEL)]
                  + [row(D_A)] * 11,
        out_shape=[jax.ShapeDtypeStruct((t, D_B), BF16),
                   jax.ShapeDtypeStruct((t, D_B), BF16),
                   jax.ShapeDtypeStruct((t // tm, N_HEADS_B * V_ROWS, tm), BF16),
                   jax.ShapeDtypeStruct((t, 2 * D_MODEL), BF16),
                   half, half, half, wide, half, half, wide, half, half, wide, wide],
        compiler_params=_params(1),
        name="proj",
    )(x2, x2, x2, g, w, qg, kg, bd_mean, mu, w0, w2cat, a0, a2cat, g2, k_k, k_a, r_k, bd_sum)


def _scan_step(rf, vf, kkf, lwf, kf, af, rb, vb, kkb, lwb, kb, ab, yf_ref, yb_ref, h_ref):
    L = CHUNK
    n_sub = rf.shape[0] // L
    n_pairs = D_A // LANES
    ti = lax.broadcasted_iota(jnp.int32, (L, L), 0)
    tj = lax.broadcasted_iota(jnp.int32, (L, L), 1)
    pi = lax.broadcasted_iota(jnp.int32, (L, LANES), 0)
    lane = lax.broadcasted_iota(jnp.int32, (L, LANES), 1)
    pj = lane & (HEAD_A - 1)
    lane_lo = lane < HEAD_A
    eye = jnp.where(pj == pi, 1.0, 0.0)
    blk = lambda s: (pi // s) == (pj // s)
    row2 = lax.broadcasted_iota(jnp.int32, (2 * L, LANES), 0)
    col2 = lax.broadcasted_iota(jnp.int32, (2 * L, LANES), 1)
    same_head = (row2 // HEAD_A) == (col2 // HEAD_A)
    diag2 = row2 == col2
    zeros_l = jnp.zeros((L, LANES), F32)
    zeros_2l = jnp.zeros((2 * L, LANES), F32)

    def stack(x):
        return jnp.concatenate([jnp.where(lane_lo, x, 0.0), jnp.where(lane_lo, 0.0, x)], axis=0)

    def mmp(x, y):
        return _mm(x, stack(y))

    cat = jnp.concatenate
    chains = []
    for reverse, refs in ((False, (rf, vf, kkf, lwf, kf, af)), (True, (rb, vb, kkb, lwb, kb, ab))):
        tri = ((tj >= ti) if reverse else (tj <= ti)).astype(BF16)
        strict = (pj > pi) if reverse else (pj < pi)
        incl = (pj >= pi) if reverse else (pj <= pi)
        steps = []
        for sub in (range(n_sub - 1, -1, -1) if reverse else range(n_sub)):
            rows = slice(sub * L, (sub + 1) * L)
            r, v, kk, lw, k, a = (ref[rows, :].astype(F32) for ref in refs)
            G = sum(jnp.dot(tri, part, preferred_element_type=F32) for part in _split_bf16(lw, 2))
            g_end = G[0:1] if reverse else G[L - 1:L]
            gam_end = jnp.exp(g_end)
            inv_gam = jnp.exp(-G)
            rel_end = jnp.exp(g_end - G)
            b = kk * a
            full = dict(At=-kk * jnp.exp(G - lw), Bt=b * inv_gam, Kt=k * inv_gam, Rt=r * jnp.exp(G),
                        Kh=k * rel_end, Bh=b * rel_end, v=v)
            pairs = []
            for p in range(n_pairs):
                sl = slice(p * LANES, (p + 1) * LANES)
                c = {name: val[:, sl] for name, val in full.items()}
                c.update(strict=strict, incl=incl, gam_end=gam_end[:, sl], rows=rows)
                pairs.append(c)
            steps.append(pairs)
        chains.append(steps)
    flat = [c for steps in chains for pairs in steps for c in pairs]

    for c in flat:
        a4 = _mm_nt(cat([c["At"], c["Rt"]], axis=0), cat([stack(c["Bt"]), stack(c["Kt"])], axis=0))
        c["A_ab"] = jnp.where(c["strict"], a4[:L, :LANES], 0.0)
        c["A_ak"] = jnp.where(c["strict"], a4[:L, LANES:], 0.0)
        c["A_rb"] = jnp.where(c["incl"], a4[L:, :LANES], 0.0)
        c["A_rk"] = jnp.where(c["incl"], a4[L:, LANES:], 0.0)
    for c in flat:
        c["A0"] = jnp.where(blk(8), c["A_ab"], 0.0)
        c["A2"] = mmp(c["A0"], c["A0"])
    for c in flat:
        t1 = eye + c["A0"]
        both = mmp(cat([c["A2"], t1], axis=0), c["A2"])
        c["A4"] = both[:L]
        c["T"] = t1 + both[L:]
    for c in flat:
        c["T"] = c["T"] + mmp(c["T"], c["A4"])
    s = 8
    while s < L:
        outer = blk(2 * s) & jnp.logical_not(blk(s))
        for c in flat:
            c["TO"] = mmp(c["T"], jnp.where(outer, c["A_ab"], 0.0))
        for c in flat:
            c["T"] = c["T"] + mmp(c["TO"], c["T"])
        s *= 2
    for c in flat:
        c["AkV"] = mmp(c["A_ak"], c["v"])
    for c in flat:
        wu = _mm(c["T"], cat([stack(c["At"]), stack(c["AkV"])], axis=1))
        c["W"], c["U0"] = wu[:, :LANES], wu[:, LANES:]
    for c in flat:
        kb_t = cat([c["Kh"], c["Bh"]], axis=0).T
        mn = _mm(kb_t, cat([cat([zeros_l, c["v"]], axis=1), cat([c["W"], c["U0"]], axis=1)], axis=0))
        c["M"] = jnp.where(same_head, mn[:, :LANES], 0.0) + jnp.where(diag2, c["gam_end"], 0.0)
        c["N"] = jnp.where(same_head, mn[:, LANES:], 0.0)
    for c in flat:
        qy = _mm(cat([c["A_rb"], c["A_rk"]], axis=1),
                 cat([cat([stack(c["W"]), stack(c["U0"])], axis=1),
                      cat([zeros_2l, stack(c["v"])], axis=1)], axis=0))
        c["Q"] = c["Rt"] + qy[:, :LANES]
        c["Y0"] = qy[:, LANES:]
    for d, (steps, y_ref) in enumerate(zip(chains, (yf_ref, yb_ref))):
        for pairs in steps:
            ys = []
            for p, c in enumerate(pairs):
                qm_h = _mm(cat([c["Q"], c["M"]], axis=0), h_ref[d * n_pairs + p])
                ys.append(qm_h[:L] + c["Y0"])
                h_ref[d * n_pairs + p] = qm_h[L:] + c["N"]
            y_ref[pairs[0]["rows"], :] = cat(ys, axis=1)


def _bias_features(slope, n, width, key_side):
    pos = lax.broadcasted_iota(jnp.int32, (n, width), 0).astype(F32)
    lane = lax.broadcasted_iota(jnp.int32, (n, width), 1)
    hi, mid, lo = (t.astype(F32) for t in _split_bf16(slope * pos, 3))
    terms = jnp.where(lane % 3 == 0, hi, jnp.where(lane % 3 == 1, mid, lo))
    if key_side:
        return jnp.where(lane < 3, -1.0, jnp.where(lane < 6, terms, 0.0))
    return jnp.where(lane < 3, terms, jnp.where(lane < 6, 1.0, 0.0))


def _attn_body(q_ref, k_ref, vt_ref, lv_ref, sg_ref, qg_ref, kg_ref, o_ref, qs_ref, feat_ref, src_ref,
               s_ref, mx_ref, p_ref, al_ref, m_ref, acc_ref, *, lam_init, before_loop):
    tq = src_ref.shape[1]
    n_kv, _, tk = vt_ref.shape
    seq = n_kv * tk
    h = pl.program_id(1)
    i = pl.program_id(2)
    slope_bits = (127 - 2 * (h + 1)) << 23
    slope = (lax.bitcast_convert_type(jnp.full((1, 1), slope_bits, jnp.int32), jnp.float32).astype(F32)
             * LOG2E)
    bound =(HEAD_B * ROUNDING_MARGIN * jnp.max(jnp.abs(qg_ref[...]), axis=-1, keepdims=True)
             * jnp.max(jnp.abs(kg_ref[...]), axis=-1, keepdims=True))
    reach = jnp.minimum((F32_ZERO_EXP2 + 2.0 * bound) / slope + 1.0, float(seq)).astype(jnp.int32)[0, 0]

    n_st = tk // tq
    hw = 2 * HEAD_B

    @pl.when(i == 0)
    def _():
        kv_pos = lax.broadcasted_iota(jnp.int32, (tk, tq), 0)
        q_pos = lax.broadcasted_iota(jnp.int32, (tk, tq), 1)
        src_ref[...] = slope * (q_pos - kv_pos).astype(F32)
        key_feat = _bias_features(slope, tk, hw, key_side=True)
        feat_ref[0] = key_feat.astype(BF16)
        feat_ref[1] = (-key_feat).astype(BF16)
        feat_ref[2] = jnp.zeros((tk, hw), BF16)
        q_feat = _bias_features(slope, tq, hw, key_side=False).T.astype(BF16)
        for st in range(n_st):
            qs_ref[st, hw:, :tq] = q_feat
            qs_ref[st, hw:, tq:] = q_feat

    chan =lax.broadcasted_iota(jnp.int32, (hw, tq), 0)
    for st in range(n_st):
        q_t = q_ref[st * tq:(st + 1) * tq, :].astype(F32).T.astype(BF16)
        zero = jnp.zeros_like(q_t)
        qs_ref[st, :hw, :tq] = jnp.where(chan < HEAD_B, q_t, zero)
        qs_ref[st, :hw, tq:] = jnp.where(chan < HEAD_B, zero, q_t)
    q0 = i * tk
    j_lo = jnp.maximum(q0 - reach, 0) // tk
    j_hi = jnp.minimum(q0 + tk - 1 + reach, seq - 1) // tk + 1
    n_left = i - j_lo
    n_off = n_left + (j_hi - i - 1)

    def scores(j, side, st):
        kb = k_ref[pl.ds(pl.multiple_of(j * tk, tk), tk), :]
        return jnp.dot(jnp.concatenate([kb, feat_ref[side]], axis=1), qs_ref[st],
                       preferred_element_type=F32)

    def softmax_update(s, s_max, cst, st):
        m_old = m_ref[st]
        m_new = jnp.maximum(m_old, s_max + cst)
        alpha = jnp.exp2(m_old - m_new)
        p = jnp.exp2(s - (m_new - cst))
        m_ref[st] = m_new
        return alpha, p.astype(BF16)

    def block(n):
        j = jnp.clip(j_lo + n + (n >= n_left).astype(jnp.int32), 0, n_kv - 1)
        j = jnp.where(n < 0, i, j)
        return j, (j > i).astype(jnp.int32), jnp.logical_and(n >= 0, n < n_off)

    def qk_stage(n, slot):
        j, side, _ = block(n)
        for st in range(n_st):
            s = scores(j, side, st)
            s_ref[slot, st] = s
            mx_ref[slot, st] = jnp.max(s, axis=0, keepdims=True)

    def softmax_stage(n, slot):
        j, _, valid = block(n)
        for st in range(n_st):
            dist0 = jnp.abs(q0 + st * tq - j * tk).astype(F32)
            cst = jnp.where(valid, -slope * dist0, -1e30)
            alpha, p = softmax_update(s_ref[slot, st], mx_ref[slot, st], cst, st)
            al_ref[slot, st] = alpha
            p_ref[slot, st] = p

    def pv_stage(n, slot):
        j, _, _ = block(n)
        for st in range(n_st):
            acc_ref[st] = al_ref[slot, st] * acc_ref[st] + jnp.dot(vt_ref[j], p_ref[slot, st],
                                                                    preferred_element_type=F32)

    m_ref[...] = jnp.full(m_ref.shape, -1e30, F32)
    acc_ref[...] = jnp.zeros(acc_ref.shape, F32)
    diag_scores = [scores(i, 2, st) for st in range(n_st)]
    qk_stage(0, 0)
    src = src_ref[...]
    for st in range(n_st):
        bias = jnp.abs(src + slope * float(st * tq))
        s = diag_scores[st] - jnp.concatenate([bias, bias], axis=1)
        alpha, p = softmax_update(s, jnp.max(s, axis=0, keepdims=True), jnp.zeros((1, 1), F32), st)
        al_ref[1, st] = alpha
        p_ref[1, st] = p

    before_loop()

    def pipelined_pair(t, carry):
        n = 2 * t
        qk_stage(n + 1, 1)
        softmax_stage(n, 0)
        pv_stage(n - 1, 1)
        qk_stage(n + 2, 0)
        softmax_stage(n + 1, 1)
        pv_stage(n, 0)
        return carry

    n_pairs = n_off // 2
    lax.fori_loop(0, n_pairs, pipelined_pair, 0)
    last = 2 * n_pairs

    @pl.when(n_off % 2 == 1)
    def _():
        softmax_stage(last, 0)
        pv_stage(last - 1, 1)
        pv_stage(last, 0)

    @pl.when(n_off % 2 == 0)
    def _():
        pv_stage(last - 1, 1)

    lv = lv_ref[...]
    lam = (jnp.exp(jnp.sum(lv[0:1] * lv[1:2], axis=-1, keepdims=True))
           - jnp.exp(jnp.sum(lv[2:3] * lv[3:4], axis=-1, keepdims=True)) + lam_init)
    for st in range(n_st):
        acc = acc_ref[st, :hw, :]
        l = acc_ref[st, hw:hw + 1, :]
        o_t = acc[:, :tq] / l[:, :tq] - lam * (acc[:, tq:] / l[:, tq:])
        o_t = o_t * lax.rsqrt(jnp.mean(o_t * o_t, axis=0, keepdims=True) + NORM_EPS)
        o_ref[st * tq:(st + 1) * tq, :] = (o_t.T * sg_ref[...] * (1.0 - lam_init)).astype(o_ref.dtype)


def _mixers_body(q_ref, k_ref, vt_ref, lv_ref, sg_ref, qg_ref, kg_ref,
                 rf, vf, kkf, lwf, kf, af, rb, vb, kkb, lwb, kb, ab, *rest, lam_init, n_scan, n_cast):
    cast_in, (o_ref, yf_ref, yb_ref), rest = rest[:n_cast], rest[n_cast:n_cast + 3], rest[n_cast + 3:]
    cast_out, scratch = rest[:n_cast], rest[n_cast:]
    *attn_scratch, h_ref = scratch
    step = (pl.program_id(0) * pl.num_programs(1) + pl.program_id(1)) * pl.num_programs(2) + pl.program_id(2)

    @pl.when(step % n_scan == 0)
    def _():
        h_ref[...] = jnp.zeros(h_ref.shape, F32)

    def side_work():
        _scan_step(rf, vf, kkf, lwf, kf, af, rb, vb, kkb, lwb, kb, ab, yf_ref, yb_ref, h_ref)
        _convert(cast_in, cast_out)

    _attn_body(q_ref, k_ref, vt_ref, lv_ref, sg_ref, qg_ref, kg_ref, o_ref, *attn_scratch, lam_init=lam_init,
               before_loop=side_work)


def _mixers(q, k, vt, lam_vecs, sub_g, qg, kg, r, v, kk, lw0, k0, a0, lw1, k1, a1, lam_init, batch, seq, layer,
            next_weights):
    t = q.shape[0]
    tk = vt.shape[2]
    tq = ATTN_TQ
    n_st = tk // tq
    nq = seq // tk
    n_kv = seq // tk
    hw = 2 * HEAD_B
    rows = SCAN_CHUNKS_PER_STEP * CHUNK
    n_scan = seq // rows
    assert N_HEADS_B * nq == n_scan
    step = lambda b, h, i: (b * N_HEADS_B + h) * nq + i
    fwd = pl.BlockSpec((rows, D_A), lambda b, h, i: (step(b, h, i), 0))
    bwd = pl.BlockSpec((rows, D_A), lambda b, h, i: (
        (step(b, h, i) // n_scan) * n_scan + n_scan - 1 - step(b, h, i) % n_scan, 0))
    y = jax.ShapeDtypeStruct((t, D_A), F32)
    cast_in, cast_out, cast_shapes = _conversion_specs(next_weights, layer + 1, batch * N_HEADS_B * nq, step)
    return pl.pallas_call(
        functools.partial(_mixers_body, lam_init=lam_init, n_scan=n_scan, n_cast=len(next_weights)),
        grid=(batch, N_HEADS_B, nq),
        in_specs=[pl.BlockSpec((tk, hw), lambda b, h, i: (b * nq + i, h)),
                  pl.BlockSpec((seq, hw), lambda b, h, i: (b, h)),
                  pl.BlockSpec((n_kv, V_ROWS, tk), lambda b, h, i: (b, h, 0)),
                  _layer_spec((4, HEAD_B), layer), _layer_spec((1, hw), layer),
                  _layer_spec((1, D_B), layer), _layer_spec((1, D_B), layer)] + [fwd] * 6 + [bwd] * 6 + cast_in,
        out_specs=[pl.BlockSpec((tk, hw), lambda b, h, i: (b * nq + i, h)), fwd, bwd] + cast_out,
        out_shape=[jax.ShapeDtypeStruct((t, D_B), BF16), y, y] + cast_shapes,
        scratch_shapes=[pltpu.VMEM((n_st, 2 * hw, 2 * tq), BF16),
                        pltpu.VMEM((3, tk, hw), BF16),
                        pltpu.VMEM((tk, tq), F32),
                        pltpu.VMEM((2, n_st, tk, 2 * tq), F32),
                        pltpu.VMEM((2, n_st, 1, 2 * tq), F32),
                        pltpu.VMEM((2, n_st, tk, 2 * tq), BF16),
                        pltpu.VMEM((2, n_st, 1, 2 * tq), F32),
                        pltpu.VMEM((n_st, 1, 2 * tq), F32),
                        pltpu.VMEM((n_st, V_ROWS, 2 * tq), F32),
                        pltpu.VMEM((2 * (D_A // LANES), LANES, LANES), F32)],
        compiler_params=_params(3),
        name="mixers",
    )(q, k, vt, lam_vecs, sub_g, qg, kg, r, v, kk, lw0, k0, a0, r, v, kk, lw1, k1, a1, *next_weights)


def _merge_body(yf_ref, yb_ref, gate_ref, bonus_ref, ob_ref, gt_ref, x_ref, lng_ref, lnb_ref, bd_ref,
                wb0_ref, wb1_ref, wo_ref, g2_ref, win_ref, wout_ref, o_ref):
    bd = bd_ref[...]
    y = yf_ref[...] + yb_ref[...]
    yc = y - _group_sums(y, bd)
    var = _group_sums(yc * yc, bd)
    yn = yc * lax.rsqrt(var + GN_EPS) * lng_ref[...] + lnb_ref[...]
    oa = (yn + bonus_ref[...]) * gate_ref[...]
    gts = gt_ref[...].astype(F32)
    merged = (gts[:, :D_MODEL] * _mm(oa, wb0_ref[...])
              + gts[:, D_MODEL:] * jnp.dot(ob_ref[...], wb1_ref[...], preferred_element_type=F32))
    o_ref[...] = _ffn_math(x_ref[...] + _mm(merged, wo_ref[...]), g2_ref, win_ref, wout_ref)


def _merge(yf, yb, gate, bonus, ob, gts, x2, ln_g, ln_b, bd_mean, wb, wo, g2, w_in, w_out, layer):
    t = x2.shape[0]
    tm = min(TOKEN_TILE, t)
    row = lambda width: pl.BlockSpec((tm, width), lambda i: (i, 0))
    branch = lambda g: pl.BlockSpec((None, D_A, D_MODEL), lambda *_: (0, g, 0), pipeline_mode=pl.Buffered(1))
    return pl.pallas_call(
        _merge_body,
        grid=(t // tm,),
        in_specs=[row(D_A), row(D_A), row(D_A), row(D_A), row(D_B), row(2 * D_MODEL), row(D_MODEL),
                  _layer_spec((1, D_A), layer), _layer_spec((1, D_A), layer), _const_spec((MXU_WIDTH, MXU_WIDTH)),
                  branch(0), branch(1),
                  _weight_spec((D_MODEL, D_MODEL)),
                  _layer_spec((1, D_MODEL), layer), _weight_spec((D_MODEL, 2 * D_FF)),
                  _weight_spec((D_FF, D_MODEL))],
        out_specs=row(D_MODEL),
        out_shape=jax.ShapeDtypeStruct((t, D_MODEL), F32),
        compiler_params=_params(1),
        name="merge",
    )(yf, yb, gate, bonus, ob, gts, x2, ln_g, ln_b, bd_mean, wb, wb, wo, g2, w_in, w_out)


def _block_diag2(w):
    z = jnp.zeros_like(w[:, 0])
    return jnp.concatenate([jnp.concatenate([w[:, 0], z], axis=2), jnp.concatenate([z, w[:, 1]], axis=2)],
                           axis=1)


def kernel(x, norm_ffn1, ffn1_in, ffn1_out, norm_mix, w_in, rwkv_mu, decay_w0, decay_w2, iclr_a0, iclr_a2,
           gate_g2, k_k, k_a, r_k, ln_x_g, ln_x_b, q_gain, k_gain, diff_lambda, subln_g, w_branch, w_out,
           norm_ffn2, ffn2_in, ffn2_out):
    batch, seq, d = x.shape
    depth = norm_ffn1.shape[0]
    assert seq % TOKEN_TILE == 0 and d == D_MODEL
    x2 = x.reshape(batch * seq, d)
    head_id = jnp.arange(MXU_WIDTH) // HEAD_A
    same = head_id[:, None] == head_id[None, :]
    bd_sum = same.astype(BF16)
    bd_mean = (same.astype(F32) / HEAD_A).astype(BF16)
    rows = lambda a: a.reshape(depth, 1, -1)
    bf = lambda a: a.astype(BF16)
    n_qk = D_B // HEAD_B
    qg = rows(jnp.tile(q_gain, (1, n_qk))) * (HEAD_B ** -0.5 * LOG2E)
    kg = rows(jnp.tile(k_gain, (1, n_qk)))
    big = (ffn1_in, ffn1_out, w_in, w_branch.reshape(depth, N_BRANCH * D_A, D_MODEL), w_out, ffn2_in, ffn2_out)
    weights = [bf(w[:1]) for w in big[:2]]
    prep_w = (rows(rwkv_mu), rows(decay_w0), bf(_block_diag2(decay_w2)), rows(iclr_a0),
              bf(_block_diag2(iclr_a2)), bf(gate_g2), rows(k_k), rows(k_a), rows(r_k))
    sub_g = rows(subln_g)
    for l in range(depth):
        lam_init = 0.8 - 0.6 * math.exp(-0.3 * l)
        x2, *converted = _ffn(x2, rows(norm_ffn1), weights[0], weights[1], l, big[2:] if l == 0 else ())
        f1_in, f1_out, w_proj, w_br, w_o, f2_in, f2_out = weights + converted
        q, k, vt, gts, r, vv, kk, lw0, k0, a0, lw1, k1, a1, gate, bonus = _proj(
            x2, seq, rows(norm_mix), w_proj, qg, kg, bd_mean, *prep_w, bd_sum, l)
        ob, yf, yb, *weights = _mixers(q, k, vt, diff_lambda, sub_g, qg, kg, r, vv, kk, lw0, k0, a0, lw1, k1, a1,
                                      lam_init, batch, seq, l, big if l + 1 < depth else ())
        x2 = _merge(yf, yb, gate, bonus, ob, gts, x2, rows(ln_x_g), rows(ln_x_b), bd_mean,
                    w_br, w_o, rows(norm_ffn2), f2_in, f2_out, l)
    return x2.reshape(batch, seq, d)
```
